```python
import math
import jax, jax.numpy as jnp
from jax import lax
import numpy as np

D_MODEL = 2048
BATCH = 1
SEQ = 16384
DEPTH = 2

HEAD_DIM = 64
MIX_WIDTH = D_MODEL
A_WIDTH = MIX_WIDTH // 4
B_WIDTH = MIX_WIDTH // 4
C_WIDTH = MIX_WIDTH // 2
A_HEADS = A_WIDTH // HEAD_DIM
B_HEADS = B_WIDTH // HEAD_DIM
C_VDIM = 2 * HEAD_DIM
C_HEADS = C_WIDTH // C_VDIM
IN_WIDTH = 3 * A_WIDTH + 3 * B_WIDTH + 3 * C_WIDTH
A_PATTERNS = ((128, 1), (512, 4), (2048, 16))
A_BLOCK = 128
MOBA_BLOCK = 256
MOBA_TOPK = 3
QBLOCK = 128
ROPE_THETA = 10000.0
N_EXPERTS = 16
N_GROUPS = 4
EXPERTS_PER_GROUP = N_EXPERTS // N_GROUPS
TOP_K = 2
D_EXPERT = D_MODEL // 2
RMS_EPS = 1e-6

kernel_name = "hybrid_dilated_moba_diffattn_groupmoe"


def _rms_norm(x, g):
    xf = x.astype(jnp.float32)
    y = xf * lax.rsqrt(jnp.mean(xf * xf, axis=-1, keepdims=True) + RMS_EPS)
    return (y * g.astype(jnp.float32)).astype(x.dtype)


def _rope_tables(positions):
    inv = ROPE_THETA ** (-jnp.arange(0, HEAD_DIM, 2, dtype=jnp.float32) / HEAD_DIM)
    ang = positions.astype(jnp.float32)[..., None] * inv
    return jnp.cos(ang)[:, :, None, :], jnp.sin(ang)[:, :, None, :]


def _rope(x, cos, sin):
    xf = x.astype(jnp.float32)
    x1, x2 = xf[..., : HEAD_DIM // 2], xf[..., HEAD_DIM // 2:]
    return jnp.concatenate([x1 * cos - x2 * sin, x2 * cos + x1 * sin], axis=-1).astype(x.dtype)


def _banded_window_attention(q, k, v, span):
    L, hd = q.shape[-2], q.shape[-1]
    lead = q.shape[:-2]
    nb = -(-L // A_BLOCK)
    pad = nb * A_BLOCK - L
    qb = jnp.pad(q, [(0, 0)] * len(lead) + [(0, pad), (0, 0)]).reshape(*lead, nb, A_BLOCK, hd)

    def band(t):
        tb = jnp.pad(t, [(0, 0)] * len(lead) + [(A_BLOCK, pad), (0, 0)])
        tb = tb.reshape(*lead, nb + 1, A_BLOCK, t.shape[-1])
        return jnp.concatenate([tb[..., :-1, :, :], tb[..., 1:, :, :]], axis=-2)

    kb, vb = band(k), band(v)
    s = jnp.einsum('...nqd,...nkd->...nqk', qb, kb).astype(jnp.float32) * (hd ** -0.5)
    dist = A_BLOCK + jnp.arange(A_BLOCK)[:, None] - jnp.arange(2 * A_BLOCK)[None, :]
    kpos = (jnp.arange(nb)[:, None, None] - 1) * A_BLOCK + jnp.arange(2 * A_BLOCK)[None, None, :]
    mask = (dist >= 0) & (dist <= span) & (kpos >= 0)
    s = jnp.where(mask, s, -jnp.inf)
    m = jnp.max(s, axis=-1, keepdims=True)
    p = jnp.exp(s - m)
    den = jnp.sum(p, axis=-1)
    o = jnp.einsum('...nqk,...nkd->...nqd', p.astype(v.dtype), vb).astype(jnp.float32) / den[..., None]
    lse = m[..., 0] + jnp.log(den)
    o = o.reshape(*lead, nb * A_BLOCK, v.shape[-1])[..., :L, :]
    lse = lse.reshape(*lead, nb * A_BLOCK)[..., :L]
    return o, lse


def _dilated_attention(q, k, v):
    B, H, S, hd = q.shape
    outs, lses = [], []
    for window, dil in A_PATTERNS:
        def to_sub(t):
            return t.reshape(B, H, S // dil, dil, t.shape[-1]).swapaxes(2, 3)
        o, lse = _banded_window_attention(to_sub(q), to_sub(k), to_sub(v), window // dil)
        outs.append(o.swapaxes(2, 3).reshape(B, H, S, v.shape[-1]))
        lses.append(lse.swapaxes(2, 3).reshape(B, H, S))
    w = jax.nn.softmax(jnp.stack(lses), axis=0)
    return jnp.sum(w[..., None] * jnp.stack(outs), axis=0)


def _moba_attention(q, k, v):
    B, H, S, hd = q.shape
    nb = -(-S // MOBA_BLOCK)
    sp = nb * MOBA_BLOCK
    kp = jnp.pad(k, [(0, 0), (0, 0), (0, sp - S), (0, 0)])
    vp = jnp.pad(v, [(0, 0), (0, 0), (0, sp - S), (0, 0)])
    kblk = kp.reshape(B, H, nb, MOBA_BLOCK, hd)
    vblk = vp.reshape(B, H, nb, MOBA_BLOCK, hd)
    kmean = jnp.mean(kblk.astype(jnp.float32), axis=3)
    gate = jnp.einsum('bhsd,bhnd->bhsn', q.astype(jnp.float32), kmean)
    own = jnp.arange(S) // MOBA_BLOCK
    past = jnp.arange(nb)[None, :] < own[:, None]
    gate = jnp.where(past, gate, -jnp.inf)
    topk = min(MOBA_TOPK, nb)
    _, idx = lax.top_k(gate, topk)
    valid = jnp.arange(topk)[None, :] < own[:, None]
    nc = S // QBLOCK
    q_c = q.reshape(B, H, nc, QBLOCK, hd).transpose(2, 0, 1, 3, 4)
    idx_c = idx.reshape(B, H, nc, QBLOCK, topk).transpose(2, 0, 1, 3, 4)
    valid_c = valid.reshape(nc, QBLOCK, topk)
    starts = jnp.arange(nc) * QBLOCK
    bi = jnp.arange(B)[:, None, None, None]
    hi = jnp.arange(H)[None, :, None, None]
    scale = hd ** -0.5
    nsel = topk * MOBA_BLOCK

    def one_block(args):
        qc, ic, vc, st = args
        kg = kblk[bi, hi, ic]
        vg = vblk[bi, hi, ic]
        s_r = jnp.einsum('bhqd,bhqnmd->bhqnm', qc, kg).astype(jnp.float32) * scale
        s_r = jnp.where(vc[None, None, :, :, None], s_r, -jnp.inf).reshape(B, H, QBLOCK, nsel)
        own_start = (st // MOBA_BLOCK) * MOBA_BLOCK
        ko = lax.dynamic_slice_in_dim(kp, own_start, MOBA_BLOCK, axis=2)
        vo = lax.dynamic_slice_in_dim(vp, own_start, MOBA_BLOCK, axis=2)
        s_o = jnp.einsum('bhqd,bhmd->bhqm', qc, ko).astype(jnp.float32) * scale
        causal = (own_start + jnp.arange(MOBA_BLOCK))[None, :] <= (st + jnp.arange(QBLOCK))[:, None]
        s_o = jnp.where(causal, s_o, -jnp.inf)
        p = jax.nn.softmax(jnp.concatenate([s_r, s_o], axis=-1), axis=-1)
        p_r = p[..., :nsel].reshape(B, H, QBLOCK, topk, MOBA_BLOCK).astype(v.dtype)
        p_o = p[..., nsel:].astype(v.dtype)
        return (jnp.einsum('bhqnm,bhqnmd->bhqd', p_r, vg).astype(jnp.float32)
                + jnp.einsum('bhqm,bhmd->bhqd', p_o, vo).astype(jnp.float32))

    out = lax.map(one_block, (q_c, idx_c, valid_c, starts))
    return out.transpose(1, 2, 0, 3, 4).reshape(B, H, S, hd)


def _diff_attention(q1, q2, k1, k2, v, lam):
    B, H, S, hd = q1.shape
    nc = S // QBLOCK
    scale = hd ** -0.5
    kpos = jnp.arange(S)

    def chunks(t):
        return t.reshape(B, H, nc, QBLOCK, hd).transpose(2, 0, 1, 3, 4)

    def one_block(args):
        q1c, q2c, st = args
        causal = kpos[None, :] <= (st + jnp.arange(QBLOCK))[:, None]

        def attn_map(qc, kk):
            s = jnp.einsum('bhqd,bhkd->bhqk', qc, kk).astype(jnp.float32) * scale
            return jax.nn.softmax(jnp.where(causal, s, -jnp.inf), axis=-1)

        a = attn_map(q1c, k1) - lam * attn_map(q2c, k2)
        return jnp.einsum('bhqk,bhkd->bhqd', a.astype(v.dtype), v).astype(jnp.float32)

    out = lax.map(one_block, (chunks(q1), chunks(q2), jnp.arange(nc) * QBLOCK))
    return out.transpose(1, 2, 0, 3, 4).reshape(B, H, S, v.shape[-1])


def _token_mixer(h, cos, sin, w_in, w_out, lq1, lk1, lq2, lk2, g_sub, lam_init):
    B, S, _ = h.shape
    sizes = [A_WIDTH] * 3 + [B_WIDTH] * 3 + [C_WIDTH] * 3
    splits = [int(v) for v in np.cumsum(sizes)[:-1]]
    proj = jnp.dot(h, w_in)
    qa, ka, va, qb, kb, vb, qc, kc, vc = jnp.split(proj, splits, axis=-1)

    def heads(t, n):
        return t.reshape(B, S, n, -1)

    def bhsd(t):
        return t.transpose(0, 2, 1, 3)

    def flat(o):
        return o.transpose(0, 2, 1, 3).reshape(B, S, -1).astype(h.dtype)

    o_a = _dilated_attention(bhsd(_rope(heads(qa, A_HEADS), cos, sin)),
                             bhsd(_rope(heads(ka, A_HEADS), cos, sin)),
                             bhsd(heads(va, A_HEADS)))
    o_b = _moba_attention(bhsd(_rope(heads(qb, B_HEADS), cos, sin)),
                          bhsd(_rope(heads(kb, B_HEADS), cos, sin)),
                          bhsd(heads(vb, B_HEADS)))
    qc, kc = heads(qc, C_HEADS), heads(kc, C_HEADS)
    q1 = _rope(qc[..., :HEAD_DIM], cos, sin)
    q2 = _rope(qc[..., HEAD_DIM:], cos, sin)
    k1 = _rope(kc[..., :HEAD_DIM], cos, sin)
    k2 = _rope(kc[..., HEAD_DIM:], cos, sin)
    f32 = jnp.float32
    lam = (jnp.exp(jnp.sum(lq1.astype(f32) * lk1.astype(f32)))
           - jnp.exp(jnp.sum(lq2.astype(f32) * lk2.astype(f32))) + lam_init)
    o_c = _diff_attention(bhsd(q1), bhsd(q2), bhsd(k1), bhsd(k2), bhsd(heads(vc, C_HEADS)), lam)
    o_c = _rms_norm(o_c, g_sub) * (1.0 - lam_init)

    merged = jnp.concatenate([flat(o_a), flat(o_b), flat(o_c)], axis=-1)
    return jnp.dot(merged, w_out)


def _moe(h, w_router, b_router, w_gate, w_up, w_down):
    B, S, D = h.shape
    T = B * S
    tok = h.reshape(T, D)
    s = jax.nn.sigmoid(jnp.dot(tok, w_router).astype(jnp.float32))
    sg = (s + b_router.astype(jnp.float32)).reshape(T, N_GROUPS, EXPERTS_PER_GROUP)
    group_score = jnp.sum(lax.top_k(sg, TOP_K)[0], axis=-1)
    g = jnp.argmax(group_score, axis=-1)
    cand = jnp.take_along_axis(sg, g[:, None, None], axis=1)[:, 0]
    _, loc = lax.top_k(cand, TOP_K)
    eid = g[:, None] * EXPERTS_PER_GROUP + loc
    wts = jnp.take_along_axis(s, eid, axis=1)
    wts = wts / jnp.sum(wts, axis=-1, keepdims=True)
    flat_e = eid.reshape(-1)
    order = jnp.argsort(flat_e)
    tok_idx = order // TOP_K
    xs = tok[tok_idx]
    gs = jnp.bincount(flat_e, length=N_EXPERTS).astype(jnp.int32)
    a = lax.ragged_dot(xs, w_gate, gs)
    b = lax.ragged_dot(xs, w_up, gs)
    y = lax.ragged_dot(jax.nn.silu(a) * b, w_down, gs)
    y = y * wts.reshape(-1)[order][:, None].astype(y.dtype)
    out = jnp.zeros_like(tok).at[tok_idx].add(y)
    return out.reshape(B, S, D)


def setup_inputs(seed: int = 0) -> dict:
    key = jax.random.key(seed)
    ks = jax.random.split(key, 20)
    D = D_MODEL

    def nrm(k, shape, std):
        return jax.random.normal(k, shape, jnp.float32) * std

    return {
        "x": nrm(ks[0], (BATCH, SEQ, D), 1.0),
        "c": nrm(ks[1], (BATCH, D), 1.0),
        "positions": jnp.broadcast_to(jnp.arange(SEQ, dtype=jnp.int32)[None, :], (BATCH, SEQ)),
        "w_ada": nrm(ks[2], (DEPTH, D, 6 * D), 0.5 * D ** -0.5),
        "b_ada": nrm(ks[3], (DEPTH, 6 * D), 0.01),
        "g_attn": 1.0 + nrm(ks[4], (DEPTH, D), 0.05),
        "g_mlp": 1.0 + nrm(ks[5], (DEPTH, D), 0.05),
        "w_in": nrm(ks[6], (DEPTH, D, IN_WIDTH), D ** -0.5),
        "w_out": nrm(ks[7], (DEPTH, MIX_WIDTH, D), MIX_WIDTH ** -0.5),
        "lam_q1": nrm(ks[8], (DEPTH, HEAD_DIM), 0.1),
        "lam_k1": nrm(ks[9], (DEPTH, HEAD_DIM), 0.1),
        "lam_q2": nrm(ks[10], (DEPTH, HEAD_DIM), 0.1),
        "lam_k2": nrm(ks[11], (DEPTH, HEAD_DIM), 0.1),
        "g_subln": 1.0 + nrm(ks[12], (DEPTH, C_VDIM), 0.05),
        "w_gate": nrm(ks[13], (DEPTH, N_EXPERTS, D, D_EXPERT), D ** -0.5),
        "w_up": nrm(ks[14], (DEPTH, N_EXPERTS, D, D_EXPERT), D ** -0.5),
        "w_down": nrm(ks[15], (DEPTH, N_EXPERTS, D_EXPERT, D), D_EXPERT ** -0.5),
        "w_router": nrm(ks[16], (D, N_EXPERTS), D ** -0.5),
        "b_router": nrm(ks[17], (N_EXPERTS,), 0.01),
        "g_final": 1.0 + nrm(ks[18], (D,), 0.05),
    }


def reference(x, c, positions, w_ada, b_ada, g_attn, g_mlp, w_in, w_out,
              lam_q1, lam_k1, lam_q2, lam_k2, g_subln, w_gate, w_up, w_down,
              w_router, b_router, g_final):
    cos, sin = _rope_tables(positions)
    for l in range(DEPTH):
        lam_init = 0.8 - 0.6 * math.exp(-0.3 * l)
        mod = jnp.dot(jax.nn.silu(c), w_ada[l]) + b_ada[l]
        sh_a, sc_a, gt_a, sh_m, sc_m, gt_m = jnp.split(mod, 6, axis=-1)
        h = _rms_norm(x, g_attn[l]) * (1.0 + sc_a[:, None, :]) + sh_a[:, None, :]
        x = x + gt_a[:, None, :] * _token_mixer(h, cos, sin, w_in[l], w_out[l], lam_q1[l], lam_k1[l],
                                                lam_q2[l], lam_k2[l], g_subln[l], lam_init)
        h = _rms_norm(x, g_mlp[l]) * (1.0 + sc_m[:, None, :]) + sh_m[:, None, :]
        x = x + gt_m[:, None, :] * _moe(h, w_router, b_router, w_gate[l], w_up[l], w_down[l])
    return _rms_norm(x, g_final)
```

```python
import functools
import math

import numpy as np
import jax
import jax.numpy as jnp
from jax import lax
from jax.experimental import pallas as pl
from jax.experimental.pallas import tpu as pltpu

F32 = jnp.float32
BF16 = jnp.bfloat16
I32 = jnp.int32

HEAD_DIM = 64
LANES = 128
A_PATTERNS = ((128, 1), (512, 4), (2048, 16))
A_SPAN = 128
MOBA_BLOCK = 256
MOBA_TOPK = 3
N_EXPERTS = 16
N_GROUPS = 4
EXPERTS_PER_GROUP = N_EXPERTS // N_GROUPS
ROPE_THETA = 10000.0
RMS_EPS = 1e-6
NEG = -1e30
VMEM_LIMIT = 56 * 1024 * 1024

TQ = 256
TK = 256
NT_DIMS = (((1,), (1,)), ((), ()))


def _cparams(n_axes):
    return pltpu.CompilerParams(dimension_semantics=("arbitrary",) * n_axes,
                                vmem_limit_bytes=VMEM_LIMIT)


def _split_bf16(x):
    hi = x.astype(BF16)
    lo = (x - hi.astype(F32)).astype(BF16)
    return hi, lo


def _sigmoid(x):
    return 1.0 / (1.0 + jnp.exp(-x))


def _mod_kernel(c_ref, w_ref, b_ref, o_ref):
    c = c_ref[...]
    sc = c * _sigmoid(c)
    d = c.shape[0]
    acc = b_ref[0]
    for r in range(0, d, 256):
        acc = acc + jnp.sum(w_ref[0, r:r + 256, :] * sc[r:r + 256, :], axis=0, keepdims=True)
    o_ref[0] = acc


def _modulation(c, w_ada, b_ada):
    depth, d, n = w_ada.shape
    tn = 512
    return pl.pallas_call(
        _mod_kernel,
        grid=(depth, n // tn),
        in_specs=[pl.BlockSpec((d, 1), lambda l, j: (0, 0)),
                  pl.BlockSpec((1, d, tn), lambda l, j: (l, 0, j)),
                  pl.BlockSpec((1, 1, tn), lambda l, j: (l, 0, j))],
        out_specs=pl.BlockSpec((1, 1, tn), lambda l, j: (l, 0, j)),
        out_shape=jax.ShapeDtypeStruct((depth, 1, n), F32),
        compiler_params=_cparams(2),
    )(c.reshape(d, 1), w_ada, b_ada.reshape(depth, 1, n))


def _modulated_norm(x, g, sc, sh):
    var = jnp.mean(x * x, axis=-1, keepdims=True)
    return (x * lax.rsqrt(var + RMS_EPS) * g) * (1.0 + sc) + sh


def _rope_tile(acc, cos, sin):
    lane = lax.broadcasted_iota(I32, (1, LANES), 1)
    first = (lane & (HEAD_DIM - 1)) < (HEAD_DIM // 2)
    outs = []
    for c in range(acc.shape[1] // LANES):
        xc = acc[:, c * LANES:(c + 1) * LANES]
        rot = jnp.where(first, pltpu.roll(xc, LANES - HEAD_DIM // 2, 1), pltpu.roll(xc, HEAD_DIM // 2, 1))
        outs.append(xc * cos + rot * sin)
    return jnp.concatenate(outs, axis=1)


def _inproj_kernel(x_ref, g_ref, sc_ref, sh_ref, w_ref, cos_ref, sin_ref,
                   proj_ref, qbf_ref, ksum_ref, h_scr):
    j = pl.program_id(1)

    @pl.when(j == 0)
    def _():
        h_scr[...] = _modulated_norm(x_ref[...], g_ref[...], sc_ref[...], sh_ref[...]).astype(BF16)

    acc = jnp.dot(h_scr[...], w_ref[...], preferred_element_type=F32)
    is_q = (j == 0) | (j == 3) | (j == 6) | (j == 7)
    is_k = (j == 1) | (j == 4) | (j == 8) | (j == 9)

    @pl.when(is_q | is_k)
    def _():
        r = _rope_tile(acc, cos_ref[...], sin_ref[...])
        scale = jnp.where(is_q, HEAD_DIM ** -0.5, 1.0).astype(F32)
        proj_ref[...] = (r * scale).astype(BF16)

        @pl.when(j == 3)
        def _():
            qbf_ref[...] = r

        @pl.when(j == 4)
        def _():
            tm, tn = r.shape
            ksum_ref[0] = jnp.sum(r.reshape(tm // MOBA_BLOCK, MOBA_BLOCK, tn), axis=1)

    @pl.when(jnp.logical_not(is_q | is_k))
    def _():
        proj_ref[...] = acc.astype(BF16)


def _in_projection(x, g, sc, sh, w_bf16, cos_t, sin_t):
    s, d = x.shape
    n = w_bf16.shape[1]
    tm, tn = 512, 512
    nb = tm // MOBA_BLOCK
    row = lambda i, j: (0, 0)
    return pl.pallas_call(
        _inproj_kernel,
        grid=(s // tm, n // tn),
        in_specs=[pl.BlockSpec((tm, d), lambda i, j: (i, 0)),
                  pl.BlockSpec((1, d), row), pl.BlockSpec((1, d), row), pl.BlockSpec((1, d), row),
                  pl.BlockSpec((d, tn), lambda i, j: (0, j)),
                  pl.BlockSpec((tm, LANES), lambda i, j: (i, 0)),
                  pl.BlockSpec((tm, LANES), lambda i, j: (i, 0))],
        out_specs=[pl.BlockSpec((tm, tn), lambda i, j: (i, j)),
                   pl.BlockSpec((tm, tn), lambda i, j: (i, 0)),
                   pl.BlockSpec((1, nb, tn), lambda i, j: (i, 0, 0))],
        out_shape=[jax.ShapeDtypeStruct((s, n), BF16),
                   jax.ShapeDtypeStruct((s, tn), F32),
                   jax.ShapeDtypeStruct((s // tm, nb, tn), F32)],
        scratch_shapes=[pltpu.VMEM((tm, d), BF16)],
        compiler_params=_cparams(2),
    )(x, g, sc, sh, w_bf16, cos_t, sin_t)


def _flash_first(s, v, m_ref, l_ref, acc_ref):
    m = jnp.max(s, axis=1, keepdims=True)
    p = jnp.exp(s - m)
    m_ref[...] = m
    l_ref[...] = jnp.sum(p, axis=1, keepdims=True)
    acc_ref[...] = jnp.dot(p.astype(BF16), v, preferred_element_type=F32)


def _flash_step(s, v, m_ref, l_ref, acc_ref):
    m_prev = m_ref[...]
    m_new = jnp.maximum(m_prev, jnp.max(s, axis=1, keepdims=True))
    alpha = jnp.exp(m_prev - m_new)
    p = jnp.exp(s - m_new)
    l_ref[...] = alpha * l_ref[...] + jnp.sum(p, axis=1, keepdims=True)
    acc_ref[...] = alpha * acc_ref[...] + jnp.dot(p.astype(BF16), v, preferred_element_type=F32)
    m_ref[...] = m_new


def _qk(q, k):
    return lax.dot_general(q, k, NT_DIMS, preferred_element_type=F32)


def _kv_block(k_ref, v_ref, n):
    start = pl.multiple_of(n * TK, TK)
    return k_ref[pl.ds(start, TK), :], v_ref[pl.ds(start, TK), :]


def _flash_scratch(n_streams):
    return [pltpu.VMEM((n_streams, TQ, 1), F32), pltpu.VMEM((n_streams, TQ, 1), F32),
            pltpu.VMEM((n_streams, TQ, LANES), F32)]


def _dilated_bias_table():
    n_off = A_PATTERNS[-1][0] // TK + 1
    r = np.arange(TQ)[:, None]
    c = np.arange(TK)[None, :]
    tabs = []
    for off in range(n_off):
        delta = off * TK + r - c
        mult = np.zeros_like(delta)
        for window, dil in A_PATTERNS:
            mult += ((delta >= 0) & (delta % dil == 0) & (delta <= window)).astype(delta.dtype)
        tabs.append(np.where(mult > 0, np.log(np.maximum(mult, 1).astype(np.float64)), NEG))
    return jnp.asarray(np.stack(tabs), F32)


def _dilated_kernel(q_ref, k_ref, v_ref, tab_ref, o_ref, m_scr, l_scr, acc_scr):
    i = pl.program_id(1)
    lane = lax.broadcasted_iota(I32, (1, LANES), 1)
    lo = lane < HEAD_DIM
    q = q_ref[...]
    zero = jnp.zeros_like(q)
    qs = (jnp.where(lo, q, zero), jnp.where(lo, zero, q))
    n_off = tab_ref.shape[0]

    kd, vd = _kv_block(k_ref, v_ref, i)
    for h in range(2):
        _flash_first(_qk(qs[h], kd) + tab_ref[0], vd, m_scr.at[h], l_scr.at[h], acc_scr.at[h])

    def body(n, carry):
        kb, vb = _kv_block(k_ref, v_ref, n)
        bias = tab_ref[i - n]
        for h in range(2):
            _flash_step(_qk(qs[h], kb) + bias, vb, m_scr.at[h], l_scr.at[h], acc_scr.at[h])
        return carry

    lax.fori_loop(jnp.maximum(i - (n_off - 1), 0), i, body, 0)
    o = jnp.where(lo, acc_scr[0] / l_scr[0], acc_scr[1] / l_scr[1])
    o_ref[...] = o.astype(o_ref.dtype)


def _dilated_attention(proj, tab, qcol, kcol, vcol):
    s = proj.shape[0]
    n_pairs = 4
    kv = lambda col: pl.BlockSpec((s, LANES), lambda hp, i: (0, col + hp))
    return pl.pallas_call(
        _dilated_kernel,
        grid=(n_pairs, s // TQ),
        in_specs=[pl.BlockSpec((TQ, LANES), lambda hp, i: (i, qcol + hp)), kv(kcol), kv(vcol),
                  pl.BlockSpec(tab.shape, lambda hp, i: (0, 0, 0))],
        out_specs=pl.BlockSpec((TQ, LANES), lambda hp, i: (i, hp)),
        out_shape=jax.ShapeDtypeStruct((s, n_pairs * LANES), BF16),
        scratch_shapes=_flash_scratch(2),
        compiler_params=_cparams(2),
    )(proj, proj, proj, tab)


def _moba_gate_kernel(q_ref, w_ref, bias_ref):
    i = pl.program_id(1)
    tg = q_ref.shape[0]
    q_hi, q_lo = _split_bf16(q_ref[...])
    w_hi, w_lo = _split_bf16(w_ref[0])
    gate = (jnp.dot(q_hi, w_hi, preferred_element_type=F32)
            + jnp.dot(q_hi, w_lo, preferred_element_type=F32)
            + jnp.dot(q_lo, w_hi, preferred_element_type=F32))
    lane = lax.broadcasted_iota(I32, (tg, LANES), 1)
    row = lax.broadcasted_iota(I32, (tg, LANES), 0)
    own = jnp.right_shift(i * tg + row, int(math.log2(MOBA_BLOCK)))
    past = (lane & (HEAD_DIM - 1)) < own
    lo = lane < HEAD_DIM
    lane_f = lane.astype(F32)
    sel = jnp.zeros((tg, LANES), jnp.bool_)
    for _ in range(MOBA_TOPK):
        for half in (lo, jnp.logical_not(lo)):
            cand = past & half & jnp.logical_not(sel)
            g = jnp.where(cand, gate, -jnp.inf)
            mx = jnp.max(g, axis=1, keepdims=True)
            first = jnp.min(jnp.where(cand & (g == mx), lane_f, 2.0 * LANES), axis=1, keepdims=True)
            sel = sel | (lane_f == first)
    bias_ref[...] = jnp.where(sel, 0.0, NEG).astype(BF16)


def _moba_gate(qbf, wgate):
    s, width = qbf.shape
    n_pairs = width // LANES
    tg = 512
    return pl.pallas_call(
        _moba_gate_kernel,
        grid=(n_pairs, s // tg),
        in_specs=[pl.BlockSpec((tg, LANES), lambda hp, i: (i, hp)),
                  pl.BlockSpec((1, LANES, LANES), lambda hp, i: (hp, 0, 0))],
        out_specs=pl.BlockSpec((tg, LANES), lambda hp, i: (i, hp)),
        out_shape=jax.ShapeDtypeStruct((s, width), BF16),
        compiler_params=_cparams(2),
    )(qbf, wgate)


def _moba_kernel(q_ref, bias_ref, k_ref, v_ref, o_ref, m_scr, l_scr, acc_scr):
    i = pl.program_id(1)
    lane = lax.broadcasted_iota(I32, (1, LANES), 1)
    lo = lane < HEAD_DIM
    mine = (lo, jnp.logical_not(lo))
    blk = lane & (HEAD_DIM - 1)
    q = q_ref[...]
    bias = bias_ref[...]
    qs = (jnp.where(lo, q, bias), jnp.where(lo, bias, q))
    causal = (lax.broadcasted_iota(I32, (TQ, TK), 1) <= lax.broadcasted_iota(I32, (TQ, TK), 0))

    kd, vd = _kv_block(k_ref, v_ref, i)
    zero = jnp.zeros_like(kd)
    for h in range(2):
        kz = jnp.where(mine[h], kd, zero)
        s = jnp.where(causal, _qk(qs[h], kz), NEG)
        _flash_first(s, vd, m_scr.at[h], l_scr.at[h], acc_scr.at[h])

    def body(n, carry):
        kb, vb = _kv_block(k_ref, v_ref, n)
        onehot = jnp.where(blk == n, 1.0, 0.0).astype(BF16)
        for h in range(2):
            ka = jnp.where(mine[h], kb, jnp.broadcast_to(onehot, kb.shape))
            _flash_step(_qk(qs[h], ka), vb, m_scr.at[h], l_scr.at[h], acc_scr.at[h])
        return carry

    lax.fori_loop(0, i, body, 0)
    o = jnp.where(lo, acc_scr[0] / l_scr[0], acc_scr[1] / l_scr[1])
    o_ref[...] = o.astype(o_ref.dtype)


def _moba_attention(proj, bias, qcol, kcol, vcol):
    s = proj.shape[0]
    n_pairs = 4
    kv = lambda col: pl.BlockSpec((s, LANES), lambda hp, i: (0, col + hp))
    return pl.pallas_call(
        _moba_kernel,
        grid=(n_pairs, s // TQ),
        in_specs=[pl.BlockSpec((TQ, LANES), lambda hp, i: (i, qcol + hp)),
                  pl.BlockSpec((TQ, LANES), lambda hp, i: (i, hp)), kv(kcol), kv(vcol)],
        out_specs=pl.BlockSpec((TQ, LANES), lambda hp, i: (i, hp)),
        out_shape=jax.ShapeDtypeStruct((s, n_pairs * LANES), BF16),
        scratch_shapes=_flash_scratch(2),
        compiler_params=_cparams(2),
    )(proj, bias, proj, proj)


def _diff_kernel(lam_init, q_ref, k_ref, v_ref, lq1_ref, lk1_ref, lq2_ref, lk2_ref, g_ref, o_ref,
                 m_scr, l_scr, acc_scr):
    i = pl.program_id(1)
    lane = lax.broadcasted_iota(I32, (1, LANES), 1)
    lo = lane < HEAD_DIM
    q = q_ref[...]
    zero = jnp.zeros_like(q)
    qs = (jnp.where(lo, q, zero), jnp.where(lo, zero, q))
    causal = (lax.broadcasted_iota(I32, (TQ, TK), 1) <= lax.broadcasted_iota(I32, (TQ, TK), 0))

    kd, vd = _kv_block(k_ref, v_ref, i)
    for h in range(2):
        s = jnp.where(causal, _qk(qs[h], kd), NEG)
        _flash_first(s, vd, m_scr.at[h], l_scr.at[h], acc_scr.at[h])

    def body(n, carry):
        kb, vb = _kv_block(k_ref, v_ref, n)
        for h in range(2):
            _flash_step(_qk(qs[h], kb), vb, m_scr.at[h], l_scr.at[h], acc_scr.at[h])
        return carry

    lax.fori_loop(0, i, body, 0)
    lam = (jnp.exp(jnp.sum(lq1_ref[...] * lk1_ref[...], axis=1, keepdims=True))
           - jnp.exp(jnp.sum(lq2_ref[...] * lk2_ref[...], axis=1, keepdims=True)) + lam_init)
    o = acc_scr[0] / l_scr[0] - lam * (acc_scr[1] / l_scr[1])
    var = jnp.mean(o * o, axis=-1, keepdims=True)
    o = (o * lax.rsqrt(var + RMS_EPS) * g_ref[...]) * (1.0 - lam_init)
    o_ref[...] = o.astype(o_ref.dtype)


def _diff_attention(proj, lq1, lk1, lq2, lk2, g_sub, lam_init, qcol, kcol, vcol):
    s = proj.shape[0]
    n_heads = 8
    kv = lambda col: pl.BlockSpec((s, LANES), lambda h, i: (0, col + h))
    vec = lambda n: pl.BlockSpec((1, n), lambda h, i: (0, 0))
    return pl.pallas_call(
        functools.partial(_diff_kernel, lam_init),
        grid=(n_heads, s // TQ),
        in_specs=[pl.BlockSpec((TQ, LANES), lambda h, i: (i, qcol + h)), kv(kcol), kv(vcol),
                  vec(HEAD_DIM), vec(HEAD_DIM), vec(HEAD_DIM), vec(HEAD_DIM), vec(LANES)],
        out_specs=pl.BlockSpec((TQ, LANES), lambda h, i: (i, h)),
        out_shape=jax.ShapeDtypeStruct((s, n_heads * LANES), BF16),
        scratch_shapes=_flash_scratch(2),
        compiler_params=_cparams(2),
    )(proj, proj, proj, lq1, lk1, lq2, lk2, g_sub)


def _outproj_kernel(oa_ref, ob_ref, oc_ref, w_ref, x_ref, gt_ref, o_ref):
    wa = oa_ref.shape[1]
    wb = ob_ref.shape[1]
    y = (jnp.dot(oa_ref[...], w_ref[0:wa, :], preferred_element_type=F32)
         + jnp.dot(ob_ref[...], w_ref[wa:wa + wb, :], preferred_element_type=F32)
         + jnp.dot(oc_ref[...], w_ref[wa + wb:, :], preferred_element_type=F32))
    o_ref[...] = x_ref[...] + gt_ref[...] * y


def _out_projection(oa, ob, oc, w_bf16, x, gt):
    s, d = x.shape
    tm = 256
    return pl.pallas_call(
        _outproj_kernel,
        grid=(s // tm,),
        in_specs=[pl.BlockSpec((tm, oa.shape[1]), lambda i: (i, 0)),
                  pl.BlockSpec((tm, ob.shape[1]), lambda i: (i, 0)),
                  pl.BlockSpec((tm, oc.shape[1]), lambda i: (i, 0)),
                  pl.BlockSpec(w_bf16.shape, lambda i: (0, 0)),
                  pl.BlockSpec((tm, d), lambda i: (i, 0)),
                  pl.BlockSpec((1, d), lambda i: (0, 0))],
        out_specs=pl.BlockSpec((tm, d), lambda i: (i, 0)),
        out_shape=jax.ShapeDtypeStruct((s, d), F32),
        compiler_params=_cparams(1),
    )(oa, ob, oc, w_bf16, x, gt)


def _route_kernel(x_ref, g_ref, sc_ref, sh_ref, wr_ref, br_ref, tri_ref,
                  h_ref, route_ref, count_ref, base_scr):
    i = pl.program_id(0)
    tm = x_ref.shape[0]

    @pl.when(i == 0)
    def _():
        base_scr[...] = jnp.zeros_like(base_scr)

    h = _modulated_norm(x_ref[...], g_ref[...], sc_ref[...], sh_ref[...])
    h_ref[...] = h.astype(BF16)
    h_hi, h_lo = _split_bf16(h)
    w_hi, w_lo = _split_bf16(wr_ref[...])
    logits = _qk(w_hi, h_hi) + _qk(w_hi, h_lo) + _qk(w_lo, h_hi)
    s = _sigmoid(logits)
    sg = s + br_ref[...]
    rows = [sg[r:r + 1, :] for r in range(N_EXPERTS)]

    def top2_sum(vals):
        best = None
        for a in range(len(vals)):
            for b in range(a + 1, len(vals)):
                pair = vals[a] + vals[b]
                best = pair if best is None else jnp.maximum(best, pair)
        return best

    score = [top2_sum(rows[g * EXPERTS_PER_GROUP:(g + 1) * EXPERTS_PER_GROUP]) for g in range(N_GROUPS)]
    best, grp = score[0], jnp.zeros((1, tm), I32)
    for g in range(1, N_GROUPS):
        better = score[g] > best
        grp = jnp.where(better, g, grp)
        best = jnp.where(better, score[g], best)
    cand = []
    for j in range(EXPERTS_PER_GROUP):
        c = rows[j]
        for g in range(1, N_GROUPS):
            c = jnp.where(grp == g, rows[g * EXPERTS_PER_GROUP + j], c)
        cand.append(c)
    b0, l0 = cand[0], jnp.zeros((1, tm), I32)
    for j in range(1, EXPERTS_PER_GROUP):
        better = cand[j] > b0
        l0 = jnp.where(better, j, l0)
        b0 = jnp.where(better, cand[j], b0)
    b1, l1 = jnp.full((1, tm), -jnp.inf, F32), jnp.zeros((1, tm), I32)
    for j in range(EXPERTS_PER_GROUP):
        better = (l0 != j) & (cand[j] > b1)
        l1 = jnp.where(better, j, l1)
        b1 = jnp.where(better, cand[j], b1)
    e0 = grp * EXPERTS_PER_GROUP + l0
    e1 = grp * EXPERTS_PER_GROUP + l1
    erow = lax.broadcasted_iota(I32, (N_EXPERTS, tm), 0)
    sel0 = erow == e0
    sel1 = erow == e1
    s0 = jnp.sum(jnp.where(sel0, s, 0.0), axis=0, keepdims=True)
    s1 = jnp.sum(jnp.where(sel1, s, 0.0), axis=0, keepdims=True)
    tot = s0 + s1
    chosen = jnp.where(sel0 | sel1, 1.0, 0.0).astype(BF16)
    csum = jnp.dot(chosen, tri_ref[...], preferred_element_type=F32)
    pos = base_scr[...] + csum - 1.0
    r0 = jnp.sum(jnp.where(sel0, pos, 0.0), axis=0, keepdims=True)
    r1 = jnp.sum(jnp.where(sel1, pos, 0.0), axis=0, keepdims=True)
    base = base_scr[...] + csum[:, tm - 1:tm]
    base_scr[...] = base
    count_ref[...] = jnp.broadcast_to(base, count_ref.shape)
    zero = jnp.zeros((1, tm), F32)
    for r, val in enumerate((e0.astype(F32), e1.astype(F32), r0, r1, s0 / tot, s1 / tot, zero, zero)):
        route_ref[r:r + 1, :] = val


def _norm_and_route(x, g, sc, sh, w_router, b_router):
    s, d = x.shape
    tm = 512
    tri = jnp.asarray(np.triu(np.ones((tm, tm), np.float32)), BF16)
    row = lambda i: (0, 0)
    return pl.pallas_call(
        _route_kernel,
        grid=(s // tm,),
        in_specs=[pl.BlockSpec((tm, d), lambda i: (i, 0)),
                  pl.BlockSpec((1, d), row), pl.BlockSpec((1, d), row), pl.BlockSpec((1, d), row),
                  pl.BlockSpec((N_EXPERTS, d), row), pl.BlockSpec((N_EXPERTS, 1), row),
                  pl.BlockSpec((tm, tm), row)],
        out_specs=[pl.BlockSpec((tm, d), lambda i: (i, 0)),
                   pl.BlockSpec((8, tm), lambda i: (0, i)),
                   pl.BlockSpec((N_EXPERTS, LANES), row)],
        out_shape=[jax.ShapeDtypeStruct((s, d), BF16),
                   jax.ShapeDtypeStruct((8, s), F32),
                   jax.ShapeDtypeStruct((N_EXPERTS, LANES), F32)],
        scratch_shapes=[pltpu.VMEM((N_EXPERTS, 1), F32)],
        compiler_params=_cparams(1),
    )(x, g, sc, sh, w_router.T, b_router.reshape(N_EXPERTS, 1), tri)


FFN_ROWS = 512


def _ffn_kernel(te_ref, nu_ref, xs_ref, wg_ref, wu_ref, wd_ref, y_ref):
    j = pl.program_id(0)

    @pl.when(j < nu_ref[0])
    def _():
        x = xs_ref[...]
        a = jnp.dot(x, wg_ref[0], preferred_element_type=F32)
        b = jnp.dot(x, wu_ref[0], preferred_element_type=F32)
        act = (a * _sigmoid(a) * b).astype(BF16)
        y_ref[...] = jnp.dot(act, wd_ref[0], preferred_element_type=F32)

    @pl.when(j >= nu_ref[0])
    def _():
        y_ref[...] = jnp.zeros_like(y_ref)


def _expert_ffn(tile_expert, n_used, xs, wg, wu, wd):
    r, d = xs.shape
    de = wg.shape[2]
    grid_spec = pltpu.PrefetchScalarGridSpec(
        num_scalar_prefetch=2,
        grid=(r // FFN_ROWS,),
        in_specs=[pl.BlockSpec((FFN_ROWS, d), lambda j, te, nu: (j, 0)),
                  pl.BlockSpec((1, d, de), lambda j, te, nu: (te[j], 0, 0)),
                  pl.BlockSpec((1, d, de), lambda j, te, nu: (te[j], 0, 0)),
                  pl.BlockSpec((1, de, d), lambda j, te, nu: (te[j], 0, 0))],
        out_specs=pl.BlockSpec((FFN_ROWS, d), lambda j, te, nu: (j, 0)),
    )
    return pl.pallas_call(
        _ffn_kernel,
        grid_spec=grid_spec,
        out_shape=jax.ShapeDtypeStruct((r, d), F32),
        compiler_params=_cparams(1),
    )(tile_expert, n_used, xs, wg, wu, wd)


def _combine_kernel(final, x_ref, y0_ref, y1_ref, w0_ref, w1_ref, gt_ref, gf_ref, o_ref):
    x = x_ref[...] + gt_ref[...] * (w0_ref[...] * y0_ref[...] + w1_ref[...] * y1_ref[...])
    if final:
        var = jnp.mean(x * x, axis=-1, keepdims=True)
        x = x * lax.rsqrt(var + RMS_EPS) * gf_ref[...]
    o_ref[...] = x


def _combine(x, y0, y1, w0, w1, gt, g_final, final):
    s, d = x.shape
    tm = 256
    big = pl.BlockSpec((tm, d), lambda i: (i, 0))
    col = pl.BlockSpec((tm, 1), lambda i: (i, 0))
    vec = pl.BlockSpec((1, d), lambda i: (0, 0))
    return pl.pallas_call(
        functools.partial(_combine_kernel, final),
        grid=(s // tm,),
        in_specs=[big, big, big, col, col, vec, vec],
        out_specs=big,
        out_shape=jax.ShapeDtypeStruct((s, d), F32),
        compiler_params=_cparams(1),
    )(x, y0, y1, w0, w1, gt, g_final)


def _rope_tables(positions):
    half = HEAD_DIM // 2
    inv = ROPE_THETA ** (-jnp.arange(0, HEAD_DIM, 2, dtype=F32) / HEAD_DIM)
    ang = positions.astype(F32)[:, None] * inv
    cos, sin = jnp.cos(ang), jnp.sin(ang)
    cos_t = jnp.tile(cos, (1, LANES // half))
    sin_t = jnp.tile(jnp.concatenate([-sin, sin], axis=1), (1, LANES // HEAD_DIM))
    return cos_t, sin_t


def _gate_weights(ksum):
    nblk, width = ksum.shape
    kmean = ksum / MOBA_BLOCK
    km = kmean.reshape(nblk, width // LANES, 2, HEAD_DIM)
    km = jnp.pad(km, ((0, HEAD_DIM - nblk), (0, 0), (0, 0), (0, 0)))
    first = jnp.transpose(km[:, :, 0, :], (1, 2, 0))
    second = jnp.transpose(km[:, :, 1, :], (1, 2, 0))
    z = jnp.zeros_like(first)
    top = jnp.concatenate([z, first], axis=2)
    bot = jnp.concatenate([second, z], axis=2)
    return jnp.concatenate([top, bot], axis=1)


def _moe(x1, h2, route, counts, wg, wu, wd, gt, g_final, final):
    s, d = x1.shape
    e0, e1 = route[0].astype(I32), route[1].astype(I32)
    r0, r1 = route[2].astype(I32), route[3].astype(I32)
    cnt = counts[:, 0].astype(I32)
    cnt_pad = ((cnt + FFN_ROWS - 1) // FFN_ROWS) * FFN_ROWS
    off = jnp.concatenate([jnp.zeros((1,), I32), jnp.cumsum(cnt_pad).astype(I32)])
    d0 = off[e0] + r0
    d1 = off[e1] + r1
    n_rows = 2 * s + N_EXPERTS * FFN_ROWS
    tok = jnp.arange(s, dtype=I32)
    src = jnp.zeros((n_rows,), I32).at[d0].set(tok).at[d1].set(tok)
    n_tiles = n_rows // FFN_ROWS
    tile_expert = jnp.searchsorted(off[1:], jnp.arange(n_tiles, dtype=I32) * FFN_ROWS, side="right")
    tile_expert = jnp.minimum(tile_expert, N_EXPERTS - 1).astype(I32)
    n_used = (off[N_EXPERTS] // FFN_ROWS).reshape(1).astype(I32)
    y = _expert_ffn(tile_expert, n_used, h2[src], wg, wu, wd)
    return _combine(x1, y[d0], y[d1], route[4].reshape(s, 1), route[5].reshape(s, 1), gt, g_final, final)


def kernel(x, c, positions, w_ada, b_ada, g_attn, g_mlp, w_in, w_out, lam_q1, lam_k1, lam_q2, lam_k2,
           g_subln, w_gate, w_up, w_down, w_router, b_router, g_final):
    batch, s, d = x.shape
    assert batch == 1 and s % 512 == 0 and s // MOBA_BLOCK <= HEAD_DIM
    depth = w_ada.shape[0]
    xs = x.reshape(s, d)
    cos_t, sin_t = _rope_tables(positions[0])
    mod = _modulation(c, w_ada, b_ada)
    tab = _dilated_bias_table()
    gf = g_final.reshape(1, d)
    for l in range(depth):
        lam_init = 0.8 - 0.6 * math.exp(-0.3 * l)
        sh_a, sc_a, gt_a, sh_m, sc_m, gt_m = [mod[l, :, k * d:(k + 1) * d] for k in range(6)]
        proj, qbf, ksum = _in_projection(xs, g_attn[l].reshape(1, d), sc_a, sh_a,
                                         w_in[l].astype(BF16), cos_t, sin_t)
        o_a = _dilated_attention(proj, tab, 0, 4, 8)
        bias = _moba_gate(qbf, _gate_weights(ksum.reshape(s // MOBA_BLOCK, -1)))
        o_b = _moba_attention(proj, bias, 12, 16, 20)
        o_c = _diff_attention(proj, lam_q1[l].reshape(1, -1), lam_k1[l].reshape(1, -1),
                              lam_q2[l].reshape(1, -1), lam_k2[l].reshape(1, -1),
                              g_subln[l].reshape(1, -1), lam_init, 24, 32, 40)
        x1 = _out_projection(o_a, o_b, o_c, w_out[l].astype(BF16), xs, gt_a)
        h2, route, counts = _norm_and_route(x1, g_mlp[l].reshape(1, d), sc_m, sh_m, w_router, b_router)
        xs = _moe(x1, h2, route, counts, w_gate[l].astype(BF16), w_up[l].astype(BF16),
                  w_down[l].astype(BF16), gt_m, gf, l == depth - 1)
    return xs.reshape(batch, s, d)
```

```python
import functools
import math

import numpy as np
import jax
import jax.numpy as jnp
from jax import lax
from jax.experimental import pallas as pl
from jax.experimental.pallas import tpu as pltpu

F32 = jnp.float32
BF16 = jnp.bfloat16
I32 = jnp.int32

HEAD_DIM = 64
LANES = 128
A_PATTERNS = ((128, 1), (512, 4), (2048, 16))
MOBA_BLOCK = 256
MOBA_TOPK = 3
N_EXPERTS = 16
N_GROUPS = 4
EXPERTS_PER_GROUP = N_EXPERTS // N_GROUPS
ROPE_THETA = 10000.0
RMS_EPS = 1e-6
NEG = -1e30
VMEM_LIMIT = 56 * 1024 * 1024

TQ = 512
TK = 256
KPQ = TQ // TK
PROJ_TM = 512
PROJ_TN = 512
NT_DIMS = (((1,), (1,)), ((), ()))


def _cparams(n_axes):
    return pltpu.CompilerParams(dimension_semantics=("arbitrary",) * n_axes,
                                vmem_limit_bytes=VMEM_LIMIT)


def _split_bf16(x):
    hi = x.astype(BF16)
    lo = (x - hi.astype(F32)).astype(BF16)
    return hi, lo


def _dot3(a, b, dims=(((1,), (0,)), ((), ()))):
    a_hi, a_lo = _split_bf16(a)
    b_hi, b_lo = _split_bf16(b)
    dg = functools.partial(lax.dot_general, dimension_numbers=dims, preferred_element_type=F32)
    return dg(a_hi, b_hi) + dg(a_hi, b_lo) + dg(a_lo, b_hi)


def _sigmoid(x):
    return 1.0 / (1.0 + jnp.exp(-x))


def _mod_kernel(c_ref, w_ref, b_ref, o_ref):
    c = c_ref[...]
    sc = c * _sigmoid(c)
    d = c.shape[0]
    acc = b_ref[0]
    for r in range(0, d, 256):
        acc = acc + jnp.sum(w_ref[0, r:r + 256, :] * sc[r:r + 256, :], axis=0, keepdims=True)
    o_ref[0] = acc


def _modulation(c, w_ada, b_ada):
    depth, d, n = w_ada.shape
    tn = 512
    return pl.pallas_call(
        _mod_kernel,
        grid=(depth, n // tn),
        in_specs=[pl.BlockSpec((d, 1), lambda l, j: (0, 0)),
                  pl.BlockSpec((1, d, tn), lambda l, j: (l, 0, j)),
                  pl.BlockSpec((1, 1, tn), lambda l, j: (l, 0, j))],
        out_specs=pl.BlockSpec((1, 1, tn), lambda l, j: (l, 0, j)),
        out_shape=jax.ShapeDtypeStruct((depth, 1, n), F32),
        compiler_params=_cparams(2),
    )(c.reshape(d, 1), w_ada, b_ada.reshape(depth, 1, n))


def _modulated_norm(x, g, sc, sh):
    var = jnp.mean(x * x, axis=-1, keepdims=True)
    return (x * lax.rsqrt(var + RMS_EPS) * g) * (1.0 + sc) + sh


def _rope_tile(acc, cos, sin):
    lane = lax.broadcasted_iota(I32, (1, LANES), 1)
    first = (lane & (HEAD_DIM - 1)) < (HEAD_DIM // 2)
    outs = []
    for c in range(acc.shape[1] // LANES):
        xc = acc[:, c * LANES:(c + 1) * LANES]
        rot = jnp.where(first, pltpu.roll(xc, LANES - HEAD_DIM // 2, 1), pltpu.roll(xc, HEAD_DIM // 2, 1))
        outs.append(xc * cos + rot * sin)
    return jnp.concatenate(outs, axis=1)


N_KIND_TILES = 4
KB_TILE = 1
QB_TILE = N_KIND_TILES + 1


def _inproj_kernel(x_ref, g_ref, sc_ref, sh_ref, w_ref, cos_ref, sin_ref,
                   k_ref, qt_ref, qbt_ref, vt_ref, ksum_ref, h_scr):
    j = pl.program_id(1)

    @pl.when(j == 0)
    def _():
        h_scr[...] = _modulated_norm(x_ref[...], g_ref[...], sc_ref[...], sh_ref[...]).astype(BF16)

    acc = jnp.dot(h_scr[...], w_ref[...], preferred_element_type=F32)

    @pl.when(j < N_KIND_TILES)
    def _():
        r = _rope_tile(acc, cos_ref[...], sin_ref[...])
        k_ref[...] = r.astype(BF16)

        @pl.when(j == KB_TILE)
        def _():
            tm, tn = r.shape
            ksum_ref[0] = jnp.sum(r.reshape(tm // MOBA_BLOCK, MOBA_BLOCK, tn), axis=1)

    @pl.when((j >= N_KIND_TILES) & (j < 2 * N_KIND_TILES))
    def _():
        rt = _rope_tile(acc, cos_ref[...], sin_ref[...]).T
        qt_ref[...] = (rt * HEAD_DIM ** -0.5).astype(BF16)

        @pl.when(j == QB_TILE)
        def _():
            qbt_ref[...] = rt

    @pl.when(j >= 2 * N_KIND_TILES)
    def _():
        vt = acc.T.astype(BF16)
        for b in range(vt_ref.shape[0]):
            vt_ref[b] = vt[:, b * TK:(b + 1) * TK]


def _in_projection(x, g, sc, sh, w_bf16, cos_t, sin_t):
    s, d = x.shape
    n = w_bf16.shape[1]
    tm, tn = PROJ_TM, PROJ_TN
    nb = tm // MOBA_BLOCK
    width = N_KIND_TILES * tn
    row = lambda i, j: (0, 0)
    kind = lambda j, first: jnp.clip(j - first, 0, N_KIND_TILES - 1)
    return pl.pallas_call(
        _inproj_kernel,
        grid=(s // tm, n // tn),
        in_specs=[pl.BlockSpec((tm, d), lambda i, j: (i, 0)),
                  pl.BlockSpec((1, d), row), pl.BlockSpec((1, d), row), pl.BlockSpec((1, d), row),
                  pl.BlockSpec((d, tn), lambda i, j: (0, j)),
                  pl.BlockSpec((tm, LANES), lambda i, j: (i, 0)),
                  pl.BlockSpec((tm, LANES), lambda i, j: (i, 0))],
        out_specs=[pl.BlockSpec((tm, tn), lambda i, j: (i, kind(j, 0))),
                   pl.BlockSpec((tn, tm), lambda i, j: (kind(j, N_KIND_TILES), i)),
                   pl.BlockSpec((tn, tm), lambda i, j: (0, i)),
                   pl.BlockSpec((tm // TK, tn, TK), lambda i, j: (i, kind(j, 2 * N_KIND_TILES), 0)),
                   pl.BlockSpec((1, nb, tn), lambda i, j: (i, 0, 0))],
        out_shape=[jax.ShapeDtypeStruct((s, width), BF16),
                   jax.ShapeDtypeStruct((width, s), BF16),
                   jax.ShapeDtypeStruct((tn, s), F32),
                   jax.ShapeDtypeStruct((s // TK, width, TK), BF16),
                   jax.ShapeDtypeStruct((s // tm, nb, tn), F32)],
        scratch_shapes=[pltpu.VMEM((tm, d), BF16)],
        compiler_params=_cparams(2),
    )(x, g, sc, sh, w_bf16, cos_t, sin_t)


def _flash_step(s_t, v_t, state):
    m_prev, l_prev, acc_prev = state
    m_new = jnp.maximum(m_prev, jnp.max(s_t, axis=0, keepdims=True))
    alpha = jnp.exp(m_prev - m_new)
    p = jnp.exp(s_t - m_new)
    l_new = alpha * l_prev + jnp.sum(p, axis=0, keepdims=True)
    acc_new = alpha * acc_prev + jnp.dot(v_t, p.astype(BF16), preferred_element_type=F32)
    return m_new, l_new, acc_new


def _qk_t(k, q_t):
    return jnp.dot(k, q_t, preferred_element_type=F32)


def _k_block(k_ref, n):
    return k_ref[pl.ds(pl.multiple_of(n * TK, TK), TK), :]


def _init_state(refs):
    m_ref, l_ref, acc_ref = refs
    m_ref[...] = jnp.full(m_ref.shape, NEG, F32)
    l_ref[...] = jnp.zeros(l_ref.shape, F32)
    acc_ref[...] = jnp.zeros(acc_ref.shape, F32)


def _load_state(refs):
    return tuple(r[...] for r in refs)


def _store_state(refs, state):
    for r, val in zip(refs, state):
        r[...] = val


def _flash_scratch(n_streams, dv):
    out = []
    for _ in range(n_streams):
        out += [pltpu.VMEM((1, TQ), F32), pltpu.VMEM((1, TQ), F32), pltpu.VMEM((dv, TQ), F32)]
    return out


def _causal_t(d):
    r = lax.broadcasted_iota(I32, (TK, TQ), 0) + d * TK
    return r <= lax.broadcasted_iota(I32, (TK, TQ), 1)


def _stack_rows(top, bottom):
    return jnp.concatenate([top, bottom], axis=0)


def _dilated_bias_table():
    n_off = A_PATTERNS[-1][0] // TK + KPQ
    r = np.arange(TK)[:, None]
    c = np.arange(TQ)[None, :]
    tabs = []
    for t in range(n_off):
        delta = (t - (KPQ - 1)) * TK + c - r
        mult = np.zeros_like(delta)
        for window, dil in A_PATTERNS:
            mult += ((delta >= 0) & (delta % dil == 0) & (delta <= window)).astype(delta.dtype)
        tabs.append(np.where(mult > 0, np.log(np.maximum(mult, 1).astype(np.float64)), NEG))
    return jnp.asarray(np.stack(tabs), F32)


def _dilated_kernel(qt_ref, k_ref, vt_ref, tab_ref, o_ref, *scr):
    i = pl.program_id(1)
    refs = (scr[0:3], scr[3:6])
    q_t = qt_ref[...]
    zero = jnp.zeros((HEAD_DIM, TQ), BF16)
    qs = (_stack_rows(q_t[:HEAD_DIM], zero), _stack_rows(zero, q_t[HEAD_DIM:]))
    n_off = tab_ref.shape[0]
    for h in range(2):
        _init_state(refs[h])

    def body(n, carry):
        state = [_load_state(refs[h]) for h in range(2)]
        kb = _k_block(k_ref, n)
        vb = vt_ref[n]
        bias = tab_ref[KPQ * i + (KPQ - 1) - n]
        for h in range(2):
            v_h = vb[h * HEAD_DIM:(h + 1) * HEAD_DIM, :]
            state[h] = _flash_step(_qk_t(kb, qs[h]) + bias, v_h, state[h])
        for h in range(2):
            _store_state(refs[h], state[h])
        return carry

    last = KPQ * i + (KPQ - 1)
    lax.fori_loop(jnp.maximum(last - (n_off - 1), 0), last + 1, body, 0)
    for h in range(2):
        _, l, acc = _load_state(refs[h])
        o_ref[h * HEAD_DIM:(h + 1) * HEAD_DIM, :] = (acc / l).astype(o_ref.dtype)


def _dilated_attention(qt, k, vt, tab, qrow, kcol, vrow):
    s = k.shape[0]
    n_pairs = 4
    return pl.pallas_call(
        _dilated_kernel,
        grid=(n_pairs, s // TQ),
        in_specs=[pl.BlockSpec((LANES, TQ), lambda hp, i: (qrow + hp, i)),
                  pl.BlockSpec((s, LANES), lambda hp, i: (0, kcol + hp)),
                  pl.BlockSpec((s // TK, LANES, TK), lambda hp, i: (0, vrow + hp, 0)),
                  pl.BlockSpec(tab.shape, lambda hp, i: (0, 0, 0))],
        out_specs=pl.BlockSpec((LANES, TQ), lambda hp, i: (hp, i)),
        out_shape=jax.ShapeDtypeStruct((n_pairs * LANES, s), BF16),
        scratch_shapes=_flash_scratch(2, HEAD_DIM),
        compiler_params=_cparams(2),
    )(qt, k, vt, tab)


def _moba_gate_kernel(qt_ref, km_ref, bias_ref):
    i = pl.program_id(1)
    tg = qt_ref.shape[1]
    gate = _dot3(km_ref[0], qt_ref[...])
    nblk = gate.shape[0]
    blk = lax.broadcasted_iota(I32, (nblk, tg), 0)
    blk_f = blk.astype(F32)
    tok = i * tg + lax.broadcasted_iota(I32, (nblk, tg), 1)
    own = jnp.right_shift(tok, int(math.log2(MOBA_BLOCK)))
    past = blk < own
    sel = blk == own
    for _ in range(MOBA_TOPK):
        cand = past & jnp.logical_not(sel)
        g = jnp.where(cand, gate, -jnp.inf)
        mx = jnp.max(g, axis=0, keepdims=True)
        first = jnp.min(jnp.where(cand & (g == mx), blk_f, 2.0 * nblk), axis=0, keepdims=True)
        sel = sel | (blk_f == first)
    bias_ref[...] = jnp.where(sel, 0.0, NEG).astype(BF16)


def _moba_gate(qbt, kmean):
    width, s = qbt.shape
    n_heads = width // HEAD_DIM
    tg = 512
    return pl.pallas_call(
        _moba_gate_kernel,
        grid=(n_heads, s // tg),
        in_specs=[pl.BlockSpec((HEAD_DIM, tg), lambda h, i: (h, i)),
                  pl.BlockSpec((1, HEAD_DIM, HEAD_DIM), lambda h, i: (h, 0, 0))],
        out_specs=pl.BlockSpec((HEAD_DIM, tg), lambda h, i: (h, i)),
        out_shape=jax.ShapeDtypeStruct((width, s), BF16),
        compiler_params=_cparams(2),
    )(qbt, kmean)


def _moba_kernel(qt_ref, bias_ref, k_ref, vt_ref, o_ref, *scr):
    i = pl.program_id(1)
    refs = (scr[0:3], scr[3:6])
    q_t = qt_ref[...]
    b_t = bias_ref[...]
    qs = (_stack_rows(q_t[:HEAD_DIM], b_t[:HEAD_DIM]), _stack_rows(b_t[HEAD_DIM:], q_t[HEAD_DIM:]))
    lane = lax.broadcasted_iota(I32, (1, LANES), 1)
    lo = lane < HEAD_DIM
    mine = (lo, jnp.logical_not(lo))
    blk = lane & (HEAD_DIM - 1)
    for h in range(2):
        _init_state(refs[h])

    def step(n, mask):
        state = [_load_state(refs[h]) for h in range(2)]
        kb = _k_block(k_ref, n)
        vb = vt_ref[n]
        onehot = jnp.broadcast_to(jnp.where(blk == n, 1.0, 0.0).astype(BF16), kb.shape)
        for h in range(2):
            s_t = _qk_t(jnp.where(mine[h], kb, onehot), qs[h])
            if mask is not None:
                s_t = jnp.where(mask, s_t, NEG)
            state[h] = _flash_step(s_t, vb[h * HEAD_DIM:(h + 1) * HEAD_DIM, :], state[h])
        for h in range(2):
            _store_state(refs[h], state[h])

    def body(n, carry):
        step(n, None)
        return carry

    lax.fori_loop(0, KPQ * i, body, 0)
    for d in range(KPQ):
        step(KPQ * i + d, _causal_t(d))
    for h in range(2):
        _, l, acc = _load_state(refs[h])
        o_ref[h * HEAD_DIM:(h + 1) * HEAD_DIM, :] = (acc / l).astype(o_ref.dtype)


def _moba_attention(qt, bias, k, vt, qrow, kcol, vrow):
    s = k.shape[0]
    n_pairs = 4
    return pl.pallas_call(
        _moba_kernel,
        grid=(n_pairs, s // TQ),
        in_specs=[pl.BlockSpec((LANES, TQ), lambda hp, i: (qrow + hp, i)),
                  pl.BlockSpec((LANES, TQ), lambda hp, i: (hp, i)),
                  pl.BlockSpec((s, LANES), lambda hp, i: (0, kcol + hp)),
                  pl.BlockSpec((s // TK, LANES, TK), lambda hp, i: (0, vrow + hp, 0))],
        out_specs=pl.BlockSpec((LANES, TQ), lambda hp, i: (hp, i)),
        out_shape=jax.ShapeDtypeStruct((n_pairs * LANES, s), BF16),
        scratch_shapes=_flash_scratch(2, HEAD_DIM),
        compiler_params=_cparams(2),
    )(qt, bias, k, vt)


def _diff_kernel(lam_init, qt_ref, k_ref, vt_ref, lq1_ref, lk1_ref, lq2_ref, lk2_ref, g_ref, o_ref,
                 *scr):
    i = pl.program_id(1)
    refs = (scr[0:3], scr[3:6])
    q_t = qt_ref[...]
    zero = jnp.zeros((HEAD_DIM, TQ), BF16)
    qs = (_stack_rows(q_t[:HEAD_DIM], zero), _stack_rows(zero, q_t[HEAD_DIM:]))
    for h in range(2):
        _init_state(refs[h])

    def step(n, mask):
        state = [_load_state(refs[h]) for h in range(2)]
        kb = _k_block(k_ref, n)
        vb = vt_ref[n]
        for h in range(2):
            s_t = _qk_t(kb, qs[h])
            if mask is not None:
                s_t = jnp.where(mask, s_t, NEG)
            state[h] = _flash_step(s_t, vb, state[h])
        for h in range(2):
            _store_state(refs[h], state[h])

    def body(n, carry):
        step(n, None)
        return carry

    lax.fori_loop(0, KPQ * i, body, 0)
    for d in range(KPQ):
        step(KPQ * i + d, _causal_t(d))
    lam = (jnp.exp(jnp.sum(lq1_ref[...] * lk1_ref[...], axis=1, keepdims=True))
           - jnp.exp(jnp.sum(lq2_ref[...] * lk2_ref[...], axis=1, keepdims=True)) + lam_init)
    _, l1, acc1 = _load_state(refs[0])
    _, l2, acc2 = _load_state(refs[1])
    o = acc1 / l1 - lam * (acc2 / l2)
    var = jnp.mean(o * o, axis=0, keepdims=True)
    o = (o * lax.rsqrt(var + RMS_EPS) * g_ref[...]) * (1.0 - lam_init)
    o_ref[...] = o.astype(o_ref.dtype)


def _diff_attention(qt, k, vt, lq1, lk1, lq2, lk2, g_sub, lam_init, qrow, kcol, vrow):
    s = k.shape[0]
    n_heads = 8
    vec = lambda n: pl.BlockSpec((1, n), lambda h, i: (0, 0))
    return pl.pallas_call(
        functools.partial(_diff_kernel, lam_init),
        grid=(n_heads, s // TQ),
        in_specs=[pl.BlockSpec((LANES, TQ), lambda h, i: (qrow + h, i)),
                  pl.BlockSpec((s, LANES), lambda h, i: (0, kcol + h)),
                  pl.BlockSpec((s // TK, LANES, TK), lambda h, i: (0, vrow + h, 0)),
                  vec(HEAD_DIM), vec(HEAD_DIM), vec(HEAD_DIM), vec(HEAD_DIM),
                  pl.BlockSpec((LANES, 1), lambda h, i: (0, 0))],
        out_specs=pl.BlockSpec((LANES, TQ), lambda h, i: (h, i)),
        out_shape=jax.ShapeDtypeStruct((n_heads * LANES, s), BF16),
        scratch_shapes=_flash_scratch(2, LANES),
        compiler_params=_cparams(2),
    )(qt, k, vt, lq1, lk1, lq2, lk2, g_sub)


def _outproj_kernel(oa_ref, ob_ref, oc_ref, wt_ref, x_ref, gt_ref, o_ref):
    wa = oa_ref.shape[0]
    wb = ob_ref.shape[0]
    y_t = (jnp.dot(wt_ref[:, 0:wa], oa_ref[...], preferred_element_type=F32)
           + jnp.dot(wt_ref[:, wa:wa + wb], ob_ref[...], preferred_element_type=F32)
           + jnp.dot(wt_ref[:, wa + wb:], oc_ref[...], preferred_element_type=F32))
    o_ref[...] = x_ref[...] + gt_ref[...] * y_t.T


def _out_projection(oa_t, ob_t, oc_t, wt_bf16, x, gt):
    s, d = x.shape
    tm = 256
    return pl.pallas_call(
        _outproj_kernel,
        grid=(s // tm,),
        in_specs=[pl.BlockSpec((oa_t.shape[0], tm), lambda i: (0, i)),
                  pl.BlockSpec((ob_t.shape[0], tm), lambda i: (0, i)),
                  pl.BlockSpec((oc_t.shape[0], tm), lambda i: (0, i)),
                  pl.BlockSpec(wt_bf16.shape, lambda i: (0, 0)),
                  pl.BlockSpec((tm, d), lambda i: (i, 0)),
                  pl.BlockSpec((1, d), lambda i: (0, 0))],
        out_specs=pl.BlockSpec((tm, d), lambda i: (i, 0)),
        out_shape=jax.ShapeDtypeStruct((s, d), F32),
        compiler_params=_cparams(1),
    )(oa_t, ob_t, oc_t, wt_bf16, x, gt)


def _route_kernel(x_ref, g_ref, sc_ref, sh_ref, wr_ref, br_ref, tri_ref,
                  h_ref, route_ref, count_ref, base_scr):
    i = pl.program_id(0)
    tm = x_ref.shape[0]

    @pl.when(i == 0)
    def _():
        base_scr[...] = jnp.zeros_like(base_scr)

    h = _modulated_norm(x_ref[...], g_ref[...], sc_ref[...], sh_ref[...])
    h_ref[...] = h.astype(BF16)
    logits = _dot3(wr_ref[...], h, NT_DIMS)
    s = _sigmoid(logits)
    sg = s + br_ref[...]
    rows = [sg[r:r + 1, :] for r in range(N_EXPERTS)]

    def top2_sum(vals):
        best = None
        for a in range(len(vals)):
            for b in range(a + 1, len(vals)):
                pair = vals[a] + vals[b]
                best = pair if best is None else jnp.maximum(best, pair)
        return best

    score = [top2_sum(rows[g * EXPERTS_PER_GROUP:(g + 1) * EXPERTS_PER_GROUP]) for g in range(N_GROUPS)]
    best, grp = score[0], jnp.zeros((1, tm), I32)
    for g in range(1, N_GROUPS):
        better = score[g] > best
        grp = jnp.where(better, g, grp)
        best = jnp.where(better, score[g], best)
    cand = []
    for j in range(EXPERTS_PER_GROUP):
        c = rows[j]
        for g in range(1, N_GROUPS):
            c = jnp.where(grp == g, rows[g * EXPERTS_PER_GROUP + j], c)
        cand.append(c)
    b0, l0 = cand[0], jnp.zeros((1, tm), I32)
    for j in range(1, EXPERTS_PER_GROUP):
        better = cand[j] > b0
        l0 = jnp.where(better, j, l0)
        b0 = jnp.where(better, cand[j], b0)
    b1, l1 = jnp.full((1, tm), -jnp.inf, F32), jnp.zeros((1, tm), I32)
    for j in range(EXPERTS_PER_GROUP):
        better = (l0 != j) & (cand[j] > b1)
        l1 = jnp.where(better, j, l1)
        b1 = jnp.where(better, cand[j], b1)
    e0 = grp * EXPERTS_PER_GROUP + l0
    e1 = grp * EXPERTS_PER_GROUP + l1
    erow = lax.broadcasted_iota(I32, (N_EXPERTS, tm), 0)
    sel0 = erow == e0
    sel1 = erow == e1
    s0 = jnp.sum(jnp.where(sel0, s, 0.0), axis=0, keepdims=True)
    s1 = jnp.sum(jnp.where(sel1, s, 0.0), axis=0, keepdims=True)
    tot = s0 + s1
    chosen = jnp.where(sel0 | sel1, 1.0, 0.0).astype(BF16)
    csum = jnp.dot(chosen, tri_ref[...], preferred_element_type=F32)
    pos = base_scr[...] + csum - 1.0
    r0 = jnp.sum(jnp.where(sel0, pos, 0.0), axis=0, keepdims=True)
    r1 = jnp.sum(jnp.where(sel1, pos, 0.0), axis=0, keepdims=True)
    base = base_scr[...] + csum[:, tm - 1:tm]
    base_scr[...] = base
    count_ref[...] = jnp.broadcast_to(base, count_ref.shape)
    zero = jnp.zeros((1, tm), F32)
    for r, val in enumerate((e0.astype(F32), e1.astype(F32), r0, r1, s0 / tot, s1 / tot, zero, zero)):
        route_ref[r:r + 1, :] = val


def _norm_and_route(x, g, sc, sh, w_router, b_router):
    s, d = x.shape
    tm = 512
    tri = jnp.asarray(np.triu(np.ones((tm, tm), np.float32)), BF16)
    row = lambda i: (0, 0)
    return pl.pallas_call(
        _route_kernel,
        grid=(s // tm,),
        in_specs=[pl.BlockSpec((tm, d), lambda i: (i, 0)),
                  pl.BlockSpec((1, d), row), pl.BlockSpec((1, d), row), pl.BlockSpec((1, d), row),
                  pl.BlockSpec((N_EXPERTS, d), row), pl.BlockSpec((N_EXPERTS, 1), row),
                  pl.BlockSpec((tm, tm), row)],
        out_specs=[pl.BlockSpec((tm, d), lambda i: (i, 0)),
                   pl.BlockSpec((8, tm), lambda i: (0, i)),
                   pl.BlockSpec((N_EXPERTS, LANES), row)],
        out_shape=[jax.ShapeDtypeStruct((s, d), BF16),
                   jax.ShapeDtypeStruct((8, s), F32),
                   jax.ShapeDtypeStruct((N_EXPERTS, LANES), F32)],
        scratch_shapes=[pltpu.VMEM((N_EXPERTS, 1), F32)],
        compiler_params=_cparams(1),
    )(x, g, sc, sh, w_router.T, b_router.reshape(N_EXPERTS, 1), tri)


FFN_ROWS = 512


def _ffn_kernel(te_ref, nu_ref, xs_ref, wg_ref, wu_ref, wd_ref, y_ref):
    j = pl.program_id(0)

    @pl.when(j < nu_ref[0])
    def _():
        x = xs_ref[...]
        a = jnp.dot(x, wg_ref[0], preferred_element_type=F32)
        b = jnp.dot(x, wu_ref[0], preferred_element_type=F32)
        act = (a * _sigmoid(a) * b).astype(BF16)
        y_ref[...] = jnp.dot(act, wd_ref[0], preferred_element_type=F32)

    @pl.when(j >= nu_ref[0])
    def _():
        y_ref[...] = jnp.zeros_like(y_ref)


def _expert_ffn(tile_expert, n_used, xs, wg, wu, wd):
    r, d = xs.shape
    de = wg.shape[2]
    grid_spec = pltpu.PrefetchScalarGridSpec(
        num_scalar_prefetch=2,
        grid=(r // FFN_ROWS,),
        in_specs=[pl.BlockSpec((FFN_ROWS, d), lambda j, te, nu: (j, 0)),
                  pl.BlockSpec((1, d, de), lambda j, te, nu: (te[j], 0, 0)),
                  pl.BlockSpec((1, d, de), lambda j, te, nu: (te[j], 0, 0)),
                  pl.BlockSpec((1, de, d), lambda j, te, nu: (te[j], 0, 0))],
        out_specs=pl.BlockSpec((FFN_ROWS, d), lambda j, te, nu: (j, 0)),
    )
    return pl.pallas_call(
        _ffn_kernel,
        grid_spec=grid_spec,
        out_shape=jax.ShapeDtypeStruct((r, d), F32),
        compiler_params=_cparams(1),
    )(tile_expert, n_used, xs, wg, wu, wd)


def _combine_kernel(final, x_ref, y0_ref, y1_ref, w0_ref, w1_ref, gt_ref, gf_ref, o_ref):
    x = x_ref[...] + gt_ref[...] * (w0_ref[...] * y0_ref[...] + w1_ref[...] * y1_ref[...])
    if final:
        var = jnp.mean(x * x, axis=-1, keepdims=True)
        x = x * lax.rsqrt(var + RMS_EPS) * gf_ref[...]
    o_ref[...] = x


def _combine(x, y0, y1, w0, w1, gt, g_final, final):
    s, d = x.shape
    tm = 256
    big = pl.BlockSpec((tm, d), lambda i: (i, 0))
    col = pl.BlockSpec((tm, 1), lambda i: (i, 0))
    vec = pl.BlockSpec((1, d), lambda i: (0, 0))
    return pl.pallas_call(
        functools.partial(_combine_kernel, final),
        grid=(s // tm,),
        in_specs=[big, big, big, col, col, vec, vec],
        out_specs=big,
        out_shape=jax.ShapeDtypeStruct((s, d), F32),
        compiler_params=_cparams(1),
    )(x, y0, y1, w0, w1, gt, g_final)


def _rope_tables(positions):
    half = HEAD_DIM // 2
    inv = ROPE_THETA ** (-jnp.arange(0, HEAD_DIM, 2, dtype=F32) / HEAD_DIM)
    ang = positions.astype(F32)[:, None] * inv
    cos, sin = jnp.cos(ang), jnp.sin(ang)
    cos_t = jnp.tile(cos, (1, LANES // half))
    sin_t = jnp.tile(jnp.concatenate([-sin, sin], axis=1), (1, LANES // HEAD_DIM))
    return cos_t, sin_t


def _regroup_in_weights(w_in):
    a = b = w_in.shape[1] // 12
    c = 2 * a
    sizes = [a, a, a, b, b, b, c, c, c]
    starts = np.concatenate([[0], np.cumsum(sizes)[:-1]])
    cols = lambda idx: [w_in[:, starts[k]:starts[k] + sizes[k]] for k in idx]
    return jnp.concatenate(cols((1, 4, 7)) + cols((0, 3, 6)) + cols((2, 5, 8)), axis=1)


def _block_means(ksum):
    nblk, width = ksum.shape
    km = (ksum / MOBA_BLOCK).reshape(nblk, width // HEAD_DIM, HEAD_DIM)
    km = jnp.pad(km, ((0, HEAD_DIM - nblk), (0, 0), (0, 0)))
    return jnp.transpose(km, (1, 0, 2))


def _moe(x1, h2, route, counts, wg, wu, wd, gt, g_final, final):
    s, d = x1.shape
    e0, e1 = route[0].astype(I32), route[1].astype(I32)
    r0, r1 = route[2].astype(I32), route[3].astype(I32)
    cnt = counts[:, 0].astype(I32)
    cnt_pad = ((cnt + FFN_ROWS - 1) // FFN_ROWS) * FFN_ROWS
    off = jnp.concatenate([jnp.zeros((1,), I32), jnp.cumsum(cnt_pad).astype(I32)])
    d0 = off[e0] + r0
    d1 = off[e1] + r1
    n_rows = 2 * s + N_EXPERTS * FFN_ROWS
    tok = jnp.arange(s, dtype=I32)
    src = jnp.zeros((n_rows,), I32).at[d0].set(tok).at[d1].set(tok)
    n_tiles = n_rows // FFN_ROWS
    tile_expert = jnp.searchsorted(off[1:], jnp.arange(n_tiles, dtype=I32) * FFN_ROWS, side="right")
    tile_expert = jnp.minimum(tile_expert, N_EXPERTS - 1).astype(I32)
    n_used = (off[N_EXPERTS] // FFN_ROWS).reshape(1).astype(I32)
    y = _expert_ffn(tile_expert, n_used, h2[src], wg, wu, wd)
    return _combine(x1, y[d0], y[d1], route[4].reshape(s, 1), route[5].reshape(s, 1), gt, g_final, final)


def kernel(x, c, positions, w_ada, b_ada, g_attn, g_mlp, w_in, w_out, lam_q1, lam_k1, lam_q2, lam_k2,
           g_subln, w_gate, w_up, w_down, w_router, b_router, g_final):
    batch, s, d = x.shape
    assert batch == 1 and s % 512 == 0 and s // MOBA_BLOCK <= HEAD_DIM
    assert w_in.shape[2] == 3 * N_KIND_TILES * PROJ_TN
    depth = w_ada.shape[0]
    xs = x.reshape(s, d)
    cos_t, sin_t = _rope_tables(positions[0])
    mod = _modulation(c, w_ada, b_ada)
    tab = _dilated_bias_table()
    gf = g_final.reshape(1, d)
    row = lambda v: v.reshape(1, -1)
    for l in range(depth):
        lam_init = 0.8 - 0.6 * math.exp(-0.3 * l)
        sh_a, sc_a, gt_a, sh_m, sc_m, gt_m = [mod[l, :, k * d:(k + 1) * d] for k in range(6)]
        k, qt, qbt, vt, ksum = _in_projection(xs, row(g_attn[l]), sc_a, sh_a,
                                              _regroup_in_weights(w_in[l]).astype(BF16), cos_t, sin_t)
        o_a = _dilated_attention(qt, k, vt, tab, 0, 0, 0)
        bias = _moba_gate(qbt, _block_means(ksum.reshape(s // MOBA_BLOCK, -1)))
        o_b = _moba_attention(qt, bias, k, vt, 4, 4, 4)
        o_c = _diff_attention(qt, k, vt, row(lam_q1[l]), row(lam_k1[l]), row(lam_q2[l]), row(lam_k2[l]),
                              g_subln[l].reshape(-1, 1), lam_init, 8, 8, 8)
        x1 = _out_projection(o_a, o_b, o_c, w_out[l].T.astype(BF16), xs, gt_a)
        h2, route, counts = _norm_and_route(x1, row(g_mlp[l]), sc_m, sh_m, w_router, b_router)
        xs = _moe(x1, h2, route, counts, w_gate[l].astype(BF16), w_up[l].astype(BF16),
                  w_down[l].astype(BF16), gt_m, gf, l == depth - 1)
    return xs.reshape(batch, s, d)
```

```python
import functools
import math

import numpy as np
import jax
import jax.numpy as jnp
from jax import lax
from jax.experimental import pallas as pl
from jax.experimental.pallas import tpu as pltpu

F32 = jnp.float32
BF16 = jnp.bfloat16
I32 = jnp.int32

HEAD_DIM = 64
LANES = 128
A_PATTERNS = ((128, 1), (512, 4), (2048, 16))
MOBA_BLOCK = 256
MOBA_TOPK = 3
N_EXPERTS = 16
N_GROUPS = 4
EXPERTS_PER_GROUP = N_EXPERTS // N_GROUPS
ROPE_THETA = 10000.0
RMS_EPS = 1e-6
NEG = -1e30
VMEM_LIMIT = 56 * 1024 * 1024

TQ = 512
TK = 256
KPQ = TQ // TK
assert KPQ == 2
PROJ_TM = 512
PROJ_TN = 512
NT_DIMS = (((1,), (1,)), ((), ()))
Q_SCALE = HEAD_DIM ** -0.5 * math.log2(math.e)


def _cparams(n_axes):
    return pltpu.CompilerParams(dimension_semantics=("arbitrary",) * n_axes,
                                vmem_limit_bytes=VMEM_LIMIT)


def _split_bf16(x):
    hi = x.astype(BF16)
    lo = (x - hi.astype(F32)).astype(BF16)
    return hi, lo


def _dot3(a, b, dims=(((1,), (0,)), ((), ()))):
    a_hi, a_lo = _split_bf16(a)
    b_hi, b_lo = _split_bf16(b)
    dg = functools.partial(lax.dot_general, dimension_numbers=dims, preferred_element_type=F32)
    return dg(a_hi, b_hi) + dg(a_hi, b_lo) + dg(a_lo, b_hi)


def _sigmoid(x):
    return 1.0 / (1.0 + jnp.exp(-x))


def _mod_kernel(c_ref, w_ref, b_ref, o_ref):
    c = c_ref[...]
    sc = c * _sigmoid(c)
    d = c.shape[0]
    acc = b_ref[0]
    for r in range(0, d, 256):
        acc = acc + jnp.sum(w_ref[0, r:r + 256, :] * sc[r:r + 256, :], axis=0, keepdims=True)
    o_ref[0] = acc


def _modulation(c, w_ada, b_ada):
    depth, d, n = w_ada.shape
    tn = 512
    return pl.pallas_call(
        _mod_kernel,
        grid=(depth, n // tn),
        in_specs=[pl.BlockSpec((d, 1), lambda l, j: (0, 0)),
                  pl.BlockSpec((1, d, tn), lambda l, j: (l, 0, j)),
                  pl.BlockSpec((1, 1, tn), lambda l, j: (l, 0, j))],
        out_specs=pl.BlockSpec((1, 1, tn), lambda l, j: (l, 0, j)),
        out_shape=jax.ShapeDtypeStruct((depth, 1, n), F32),
        compiler_params=_cparams(2),
    )(c.reshape(d, 1), w_ada, b_ada.reshape(depth, 1, n))


def _modulated_norm(x, g, sc, sh):
    var = jnp.mean(x * x, axis=-1, keepdims=True)
    return (x * lax.rsqrt(var + RMS_EPS) * g) * (1.0 + sc) + sh


def _rope_tile(acc, cos, sin):
    lane = lax.broadcasted_iota(I32, (1, LANES), 1)
    first = (lane & (HEAD_DIM - 1)) < (HEAD_DIM // 2)
    outs = []
    for c in range(acc.shape[1] // LANES):
        xc = acc[:, c * LANES:(c + 1) * LANES]
        rot = jnp.where(first, pltpu.roll(xc, LANES - HEAD_DIM // 2, 1), pltpu.roll(xc, HEAD_DIM // 2, 1))
        outs.append(xc * cos + rot * sin)
    return jnp.concatenate(outs, axis=1)


N_KIND_TILES = 4
KB_TILE = 1
QB_TILE = N_KIND_TILES + 1


def _inproj_kernel(x_ref, g_ref, sc_ref, sh_ref, w_ref, cos_ref, sin_ref,
                   k_ref, qt_ref, qbt_ref, vt_ref, ksum_ref, h_scr):
    j = pl.program_id(1)

    @pl.when(j == 0)
    def _():
        h_scr[...] = _modulated_norm(x_ref[...], g_ref[...], sc_ref[...], sh_ref[...]).astype(BF16)

    acc = jnp.dot(h_scr[...], w_ref[...], preferred_element_type=F32)

    @pl.when(j < N_KIND_TILES)
    def _():
        r = _rope_tile(acc, cos_ref[...], sin_ref[...])
        k_ref[...] = r.astype(BF16)

        @pl.when(j == KB_TILE)
        def _():
            tm, tn = r.shape
            ksum_ref[0] = jnp.sum(r.reshape(tm // MOBA_BLOCK, MOBA_BLOCK, tn), axis=1)

    @pl.when((j >= N_KIND_TILES) & (j < 2 * N_KIND_TILES))
    def _():
        rt = _rope_tile(acc, cos_ref[...], sin_ref[...]).T
        qt_ref[...] = (rt * Q_SCALE).astype(BF16)

        @pl.when(j == QB_TILE)
        def _():
            qbt_ref[...] = rt

    @pl.when(j >= 2 * N_KIND_TILES)
    def _():
        vt = acc.T.astype(BF16)
        for b in range(vt_ref.shape[0]):
            vt_ref[b] = vt[:, b * TK:(b + 1) * TK]


def _in_projection(x, g, sc, sh, w_bf16, cos_t, sin_t):
    s, d = x.shape
    n = w_bf16.shape[1]
    tm, tn = PROJ_TM, PROJ_TN
    nb = tm // MOBA_BLOCK
    width = N_KIND_TILES * tn
    row = lambda i, j: (0, 0)
    kind = lambda j, first: jnp.clip(j - first, 0, N_KIND_TILES - 1)
    return pl.pallas_call(
        _inproj_kernel,
        grid=(s // tm, n // tn),
        in_specs=[pl.BlockSpec((tm, d), lambda i, j: (i, 0)),
                  pl.BlockSpec((1, d), row), pl.BlockSpec((1, d), row), pl.BlockSpec((1, d), row),
                  pl.BlockSpec((d, tn), lambda i, j: (0, j)),
                  pl.BlockSpec((tm, LANES), lambda i, j: (i, 0)),
                  pl.BlockSpec((tm, LANES), lambda i, j: (i, 0))],
        out_specs=[pl.BlockSpec((tm, tn), lambda i, j: (i, kind(j, 0))),
                   pl.BlockSpec((tn, tm), lambda i, j: (kind(j, N_KIND_TILES), i)),
                   pl.BlockSpec((tn, tm), lambda i, j: (0, i)),
                   pl.BlockSpec((tm // TK, tn, TK), lambda i, j: (i, kind(j, 2 * N_KIND_TILES), 0)),
                   pl.BlockSpec((1, nb, tn), lambda i, j: (i, 0, 0))],
        out_shape=[jax.ShapeDtypeStruct((s, width), BF16),
                   jax.ShapeDtypeStruct((width, s), BF16),
                   jax.ShapeDtypeStruct((tn, s), F32),
                   jax.ShapeDtypeStruct((s // TK, width, TK), BF16),
                   jax.ShapeDtypeStruct((s // tm, nb, tn), F32)],
        scratch_shapes=[pltpu.VMEM((tm, d), BF16)],
        compiler_params=_cparams(2),
    )(x, g, sc, sh, w_bf16, cos_t, sin_t)


def _flash_step(s_t, v_t, state):
    m_prev, l_prev, acc_prev = state
    m_new = jnp.maximum(m_prev, jnp.max(s_t, axis=0, keepdims=True))
    alpha = jnp.exp2(m_prev - m_new)
    p = jnp.exp2(s_t - m_new)
    l_new = alpha * l_prev + jnp.sum(p, axis=0, keepdims=True)
    acc_new = alpha * acc_prev + jnp.dot(v_t, p.astype(BF16), preferred_element_type=F32)
    return m_new, l_new, acc_new


def _sweep(first, n_loop, scores, consume, s_a, s_b, tail_masks):
    def park(buf, vals):
        for h, val in enumerate(vals):
            buf[h] = val

    def fetch(buf):
        return [buf[h] for h in range(buf.shape[0])]

    park(s_a, scores(first))

    def pair(j, carry):
        a = first + 2 * j
        park(s_b, scores(a + 1))
        consume(a, fetch(s_a), None)
        park(s_a, scores(a + 2))
        consume(a + 1, fetch(s_b), None)
        return carry

    lax.fori_loop(0, n_loop // 2, pair, 0)
    t0 = first + n_loop
    park(s_b, scores(t0 + 1))
    consume(t0, fetch(s_a), tail_masks[0])
    consume(t0 + 1, fetch(s_b), tail_masks[1])


def _score_scratch(n_streams):
    return [pltpu.VMEM((n_streams, TK, TQ), F32), pltpu.VMEM((n_streams, TK, TQ), F32)]


def _qk_t(k, q_t):
    return jnp.dot(k, q_t, preferred_element_type=F32)


def _k_block(k_ref, n):
    return k_ref[pl.ds(pl.multiple_of(n * TK, TK), TK), :]


def _init_state(refs):
    m_ref, l_ref, acc_ref = refs
    m_ref[...] = jnp.full(m_ref.shape, NEG, F32)
    l_ref[...] = jnp.zeros(l_ref.shape, F32)
    acc_ref[...] = jnp.zeros(acc_ref.shape, F32)


def _load_state(refs):
    return tuple(r[...] for r in refs)


def _store_state(refs, state):
    for r, val in zip(refs, state):
        r[...] = val


def _flash_scratch(n_streams, dv):
    out = []
    for _ in range(n_streams):
        out += [pltpu.VMEM((1, TQ), F32), pltpu.VMEM((1, TQ), F32), pltpu.VMEM((dv, TQ), F32)]
    return out


def _causal_t(d):
    r = lax.broadcasted_iota(I32, (TK, TQ), 0) + d * TK
    return r <= lax.broadcasted_iota(I32, (TK, TQ), 1)


def _stack_rows(top, bottom):
    return jnp.concatenate([top, bottom], axis=0)


def _dilated_bias_table():
    n_off = A_PATTERNS[-1][0] // TK + KPQ
    r = np.arange(TK)[:, None]
    c = np.arange(TQ)[None, :]
    tabs = []
    for t in range(n_off):
        delta = (t - (KPQ - 1)) * TK + c - r
        mult = np.zeros_like(delta)
        for window, dil in A_PATTERNS:
            mult += ((delta >= 0) & (delta % dil == 0) & (delta <= window)).astype(delta.dtype)
        tabs.append(np.where(mult > 0, np.log2(np.maximum(mult, 1).astype(np.float64)), NEG))
    return jnp.asarray(np.stack(tabs), F32)


def _dilated_kernel(qt_ref, k_ref, vt_ref, tab_ref, o_ref, s_a, s_b, *scr):
    i = pl.program_id(1)
    refs = (scr[0:3], scr[3:6])
    q_t = qt_ref[...]
    zero = jnp.zeros((HEAD_DIM, TQ), BF16)
    qs = (_stack_rows(q_t[:HEAD_DIM], zero), _stack_rows(zero, q_t[HEAD_DIM:]))
    n_off = tab_ref.shape[0]
    last = KPQ * i + (KPQ - 1)
    for h in range(2):
        _init_state(refs[h])

    def scores(n):
        kb = _k_block(k_ref, n)
        bias = tab_ref[last - n]
        return [_qk_t(kb, qs[h]) + bias for h in range(2)]

    def consume(n, s_list, mask):
        state = [_load_state(refs[h]) for h in range(2)]
        vb = vt_ref[n]
        for h in range(2):
            state[h] = _flash_step(s_list[h], vb[h * HEAD_DIM:(h + 1) * HEAD_DIM, :], state[h])
        for h in range(2):
            _store_state(refs[h], state[h])

    first = jnp.maximum(last - (n_off - 1), 0)
    _sweep(first, last - 1 - first, scores, consume, s_a, s_b, (None, None))
    for h in range(2):
        _, l, acc = _load_state(refs[h])
        o_ref[h * HEAD_DIM:(h + 1) * HEAD_DIM, :] = (acc / l).astype(o_ref.dtype)


def _dilated_attention(qt, k, vt, tab, qrow, kcol, vrow):
    s = k.shape[0]
    n_pairs = 4
    return pl.pallas_call(
        _dilated_kernel,
        grid=(n_pairs, s // TQ),
        in_specs=[pl.BlockSpec((LANES, TQ), lambda hp, i: (qrow + hp, i)),
                  pl.BlockSpec((s, LANES), lambda hp, i: (0, kcol + hp)),
                  pl.BlockSpec((s // TK, LANES, TK), lambda hp, i: (0, vrow + hp, 0)),
                  pl.BlockSpec(tab.shape, lambda hp, i: (0, 0, 0))],
        out_specs=pl.BlockSpec((LANES, TQ), lambda hp, i: (hp, i)),
        out_shape=jax.ShapeDtypeStruct((n_pairs * LANES, s), BF16),
        scratch_shapes=_score_scratch(2) + _flash_scratch(2, HEAD_DIM),
        compiler_params=_cparams(2),
    )(qt, k, vt, tab)


def _moba_gate_kernel(qt_ref, km_ref, bias_ref):
    i = pl.program_id(1)
    tg = qt_ref.shape[1]
    gate = _dot3(km_ref[0], qt_ref[...])
    nblk = gate.shape[0]
    blk = lax.broadcasted_iota(I32, (nblk, tg), 0)
    blk_f = blk.astype(F32)
    tok = i * tg + lax.broadcasted_iota(I32, (nblk, tg), 1)
    own = jnp.right_shift(tok, int(math.log2(MOBA_BLOCK)))
    past = blk < own
    sel = blk == own
    for _ in range(MOBA_TOPK):
        cand = past & jnp.logical_not(sel)
        g = jnp.where(cand, gate, -jnp.inf)
        mx = jnp.max(g, axis=0, keepdims=True)
        first = jnp.min(jnp.where(cand & (g == mx), blk_f, 2.0 * nblk), axis=0, keepdims=True)
        sel = sel | (blk_f == first)
    bias_ref[...] = jnp.where(sel, 0.0, NEG).astype(BF16)


def _moba_gate(qbt, kmean):
    width, s = qbt.shape
    n_heads = width // HEAD_DIM
    tg = 512
    return pl.pallas_call(
        _moba_gate_kernel,
        grid=(n_heads, s // tg),
        in_specs=[pl.BlockSpec((HEAD_DIM, tg), lambda h, i: (h, i)),
                  pl.BlockSpec((1, HEAD_DIM, HEAD_DIM), lambda h, i: (h, 0, 0))],
        out_specs=pl.BlockSpec((HEAD_DIM, tg), lambda h, i: (h, i)),
        out_shape=jax.ShapeDtypeStruct((width, s), BF16),
        compiler_params=_cparams(2),
    )(qbt, kmean)


def _moba_kernel(qt_ref, bias_ref, k_ref, vt_ref, o_ref, s_a, s_b, *scr):
    i = pl.program_id(1)
    refs = (scr[0:3], scr[3:6])
    q_t = qt_ref[...]
    b_t = bias_ref[...]
    qs = (_stack_rows(q_t[:HEAD_DIM], b_t[:HEAD_DIM]), _stack_rows(b_t[HEAD_DIM:], q_t[HEAD_DIM:]))
    lane = lax.broadcasted_iota(I32, (1, LANES), 1)
    lo = lane < HEAD_DIM
    mine = (lo, jnp.logical_not(lo))
    blk = lane & (HEAD_DIM - 1)
    for h in range(2):
        _init_state(refs[h])

    def scores(n):
        kb = _k_block(k_ref, n)
        onehot = jnp.broadcast_to(jnp.where(blk == n, 1.0, 0.0).astype(BF16), kb.shape)
        return [_qk_t(jnp.where(mine[h], kb, onehot), qs[h]) for h in range(2)]

    def consume(n, s_list, mask):
        state = [_load_state(refs[h]) for h in range(2)]
        vb = vt_ref[n]
        for h in range(2):
            s_t = s_list[h] if mask is None else jnp.where(mask, s_list[h], NEG)
            state[h] = _flash_step(s_t, vb[h * HEAD_DIM:(h + 1) * HEAD_DIM, :], state[h])
        for h in range(2):
            _store_state(refs[h], state[h])

    _sweep(0, KPQ * i, scores, consume, s_a, s_b, (_causal_t(0), _causal_t(1)))
    for h in range(2):
        _, l, acc = _load_state(refs[h])
        o_ref[h * HEAD_DIM:(h + 1) * HEAD_DIM, :] = (acc / l).astype(o_ref.dtype)


def _moba_attention(qt, bias, k, vt, qrow, kcol, vrow):
    s = k.shape[0]
    n_pairs = 4
    return pl.pallas_call(
        _moba_kernel,
        grid=(n_pairs, s // TQ),
        in_specs=[pl.BlockSpec((LANES, TQ), lambda hp, i: (qrow + hp, i)),
                  pl.BlockSpec((LANES, TQ), lambda hp, i: (hp, i)),
                  pl.BlockSpec((s, LANES), lambda hp, i: (0, kcol + hp)),
                  pl.BlockSpec((s // TK, LANES, TK), lambda hp, i: (0, vrow + hp, 0))],
        out_specs=pl.BlockSpec((LANES, TQ), lambda hp, i: (hp, i)),
        out_shape=jax.ShapeDtypeStruct((n_pairs * LANES, s), BF16),
        scratch_shapes=_score_scratch(2) + _flash_scratch(2, HEAD_DIM),
        compiler_params=_cparams(2),
    )(qt, bias, k, vt)


def _diff_kernel(lam_init, qt_ref, k_ref, vt_ref, lq1_ref, lk1_ref, lq2_ref, lk2_ref, g_ref, o_ref,
                 s_a, s_b, *scr):
    i = pl.program_id(1)
    refs = (scr[0:3], scr[3:6])
    q_t = qt_ref[...]
    zero = jnp.zeros((HEAD_DIM, TQ), BF16)
    qs = (_stack_rows(q_t[:HEAD_DIM], zero), _stack_rows(zero, q_t[HEAD_DIM:]))
    for h in range(2):
        _init_state(refs[h])

    def scores(n):
        kb = _k_block(k_ref, n)
        return [_qk_t(kb, qs[h]) for h in range(2)]

    def consume(n, s_list, mask):
        state = [_load_state(refs[h]) for h in range(2)]
        vb = vt_ref[n]
        for h in range(2):
            s_t = s_list[h] if mask is None else jnp.where(mask, s_list[h], NEG)
            state[h] = _flash_step(s_t, vb, state[h])
        for h in range(2):
            _store_state(refs[h], state[h])

    _sweep(0, KPQ * i, scores, consume, s_a, s_b, (_causal_t(0), _causal_t(1)))
    lam = (jnp.exp(jnp.sum(lq1_ref[...] * lk1_ref[...], axis=1, keepdims=True))
           - jnp.exp(jnp.sum(lq2_ref[...] * lk2_ref[...], axis=1, keepdims=True)) + lam_init)
    _, l1, acc1 = _load_state(refs[0])
    _, l2, acc2 = _load_state(refs[1])
    o = acc1 / l1 - lam * (acc2 / l2)
    var = jnp.mean(o * o, axis=0, keepdims=True)
    o = (o * lax.rsqrt(var + RMS_EPS) * g_ref[...]) * (1.0 - lam_init)
    o_ref[...] = o.astype(o_ref.dtype)


def _diff_attention(qt, k, vt, lq1, lk1, lq2, lk2, g_sub, lam_init, qrow, kcol, vrow):
    s = k.shape[0]
    n_heads = 8
    vec = lambda n: pl.BlockSpec((1, n), lambda h, i: (0, 0))
    return pl.pallas_call(
        functools.partial(_diff_kernel, lam_init),
        grid=(n_heads, s // TQ),
        in_specs=[pl.BlockSpec((LANES, TQ), lambda h, i: (qrow + h, i)),
                  pl.BlockSpec((s, LANES), lambda h, i: (0, kcol + h)),
                  pl.BlockSpec((s // TK, LANES, TK), lambda h, i: (0, vrow + h, 0)),
                  vec(HEAD_DIM), vec(HEAD_DIM), vec(HEAD_DIM), vec(HEAD_DIM),
                  pl.BlockSpec((LANES, 1), lambda h, i: (0, 0))],
        out_specs=pl.BlockSpec((LANES, TQ), lambda h, i: (h, i)),
        out_shape=jax.ShapeDtypeStruct((n_heads * LANES, s), BF16),
        scratch_shapes=_score_scratch(2) + _flash_scratch(2, LANES),
        compiler_params=_cparams(2),
    )(qt, k, vt, lq1, lk1, lq2, lk2, g_sub)


def _outproj_kernel(oa_ref, ob_ref, oc_ref, wt_ref, x_ref, gt_ref, o_ref):
    wa = oa_ref.shape[0]
    wb = ob_ref.shape[0]
    y_t = (jnp.dot(wt_ref[:, 0:wa], oa_ref[...], preferred_element_type=F32)
           + jnp.dot(wt_ref[:, wa:wa + wb], ob_ref[...], preferred_element_type=F32)
           + jnp.dot(wt_ref[:, wa + wb:], oc_ref[...], preferred_element_type=F32))
    o_ref[...] = x_ref[...] + gt_ref[...] * y_t.T


def _out_projection(oa_t, ob_t, oc_t, wt_bf16, x, gt):
    s, d = x.shape
    tm = 256
    return pl.pallas_call(
        _outproj_kernel,
        grid=(s // tm,),
        in_specs=[pl.BlockSpec((oa_t.shape[0], tm), lambda i: (0, i)),
                  pl.BlockSpec((ob_t.shape[0], tm), lambda i: (0, i)),
                  pl.BlockSpec((oc_t.shape[0], tm), lambda i: (0, i)),
                  pl.BlockSpec(wt_bf16.shape, lambda i: (0, 0)),
                  pl.BlockSpec((tm, d), lambda i: (i, 0)),
                  pl.BlockSpec((1, d), lambda i: (0, 0))],
        out_specs=pl.BlockSpec((tm, d), lambda i: (i, 0)),
        out_shape=jax.ShapeDtypeStruct((s, d), F32),
        compiler_params=_cparams(1),
    )(oa_t, ob_t, oc_t, wt_bf16, x, gt)


def _route_kernel(x_ref, g_ref, sc_ref, sh_ref, wr_ref, br_ref, tri_ref,
                  h_ref, route_ref, count_ref, base_scr):
    i = pl.program_id(0)
    tm = x_ref.shape[0]

    @pl.when(i == 0)
    def _():
        base_scr[...] = jnp.zeros_like(base_scr)

    h = _modulated_norm(x_ref[...], g_ref[...], sc_ref[...], sh_ref[...])
    h_ref[...] = h.astype(BF16)
    logits = _dot3(wr_ref[...], h, NT_DIMS)
    s = _sigmoid(logits)
    sg = s + br_ref[...]
    rows = [sg[r:r + 1, :] for r in range(N_EXPERTS)]

    def top2_sum(vals):
        best = None
        for a in range(len(vals)):
            for b in range(a + 1, len(vals)):
                pair = vals[a] + vals[b]
                best = pair if best is None else jnp.maximum(best, pair)
        return best

    score = [top2_sum(rows[g * EXPERTS_PER_GROUP:(g + 1) * EXPERTS_PER_GROUP]) for g in range(N_GROUPS)]
    best, grp = score[0], jnp.zeros((1, tm), I32)
    for g in range(1, N_GROUPS):
        better = score[g] > best
        grp = jnp.where(better, g, grp)
        best = jnp.where(better, score[g], best)
    cand = []
    for j in range(EXPERTS_PER_GROUP):
        c = rows[j]
        for g in range(1, N_GROUPS):
            c = jnp.where(grp == g, rows[g * EXPERTS_PER_GROUP + j], c)
        cand.append(c)
    b0, l0 = cand[0], jnp.zeros((1, tm), I32)
    for j in range(1, EXPERTS_PER_GROUP):
        better = cand[j] > b0
        l0 = jnp.where(better, j, l0)
        b0 = jnp.where(better, cand[j], b0)
    b1, l1 = jnp.full((1, tm), -jnp.inf, F32), jnp.zeros((1, tm), I32)
    for j in range(EXPERTS_PER_GROUP):
        better = (l0 != j) & (cand[j] > b1)
        l1 = jnp.where(better, j, l1)
        b1 = jnp.where(better, cand[j], b1)
    e0 = grp * EXPERTS_PER_GROUP + l0
    e1 = grp * EXPERTS_PER_GROUP + l1
    erow = lax.broadcasted_iota(I32, (N_EXPERTS, tm), 0)
    sel0 = erow == e0
    sel1 = erow == e1
    s0 = jnp.sum(jnp.where(sel0, s, 0.0), axis=0, keepdims=True)
    s1 = jnp.sum(jnp.where(sel1, s, 0.0), axis=0, keepdims=True)
    tot = s0 + s1
    chosen = jnp.where(sel0 | sel1, 1.0, 0.0).astype(BF16)
    csum = jnp.dot(chosen, tri_ref[...], preferred_element_type=F32)
    pos = base_scr[...] + csum - 1.0
    r0 = jnp.sum(jnp.where(sel0, pos, 0.0), axis=0, keepdims=True)
    r1 = jnp.sum(jnp.where(sel1, pos, 0.0), axis=0, keepdims=True)
    base = base_scr[...] + csum[:, tm - 1:tm]
    base_scr[...] = base
    count_ref[...] = jnp.broadcast_to(base, count_ref.shape)
    zero = jnp.zeros((1, tm), F32)
    for r, val in enumerate((e0.astype(F32), e1.astype(F32), r0, r1, s0 / tot, s1 / tot, zero, zero)):
        route_ref[r:r + 1, :] = val


def _norm_and_route(x, g, sc, sh, w_router, b_router):
    s, d = x.shape
    tm = 512
    tri = jnp.asarray(np.triu(np.ones((tm, tm), np.float32)), BF16)
    row = lambda i: (0, 0)
    return pl.pallas_call(
        _route_kernel,
        grid=(s // tm,),
        in_specs=[pl.BlockSpec((tm, d), lambda i: (i, 0)),
                  pl.BlockSpec((1, d), row), pl.BlockSpec((1, d), row), pl.BlockSpec((1, d), row),
                  pl.BlockSpec((N_EXPERTS, d), row), pl.BlockSpec((N_EXPERTS, 1), row),
                  pl.BlockSpec((tm, tm), row)],
        out_specs=[pl.BlockSpec((tm, d), lambda i: (i, 0)),
                   pl.BlockSpec((8, tm), lambda i: (0, i)),
                   pl.BlockSpec((N_EXPERTS, LANES), row)],
        out_shape=[jax.ShapeDtypeStruct((s, d), BF16),
                   jax.ShapeDtypeStruct((8, s), F32),
                   jax.ShapeDtypeStruct((N_EXPERTS, LANES), F32)],
        scratch_shapes=[pltpu.VMEM((N_EXPERTS, 1), F32)],
        compiler_params=_cparams(1),
    )(x, g, sc, sh, w_router.T, b_router.reshape(N_EXPERTS, 1), tri)


FFN_ROWS = 512


def _ffn_kernel(te_ref, nu_ref, xs_ref, wg_ref, wu_ref, wd_ref, y_ref):
    j = pl.program_id(0)

    @pl.when(j < nu_ref[0])
    def _():
        x = xs_ref[...]
        a = jnp.dot(x, wg_ref[0], preferred_element_type=F32)
        b = jnp.dot(x, wu_ref[0], preferred_element_type=F32)
        act = (a * _sigmoid(a) * b).astype(BF16)
        y_ref[...] = jnp.dot(act, wd_ref[0], preferred_element_type=F32)

    @pl.when(j >= nu_ref[0])
    def _():
        y_ref[...] = jnp.zeros_like(y_ref)


def _expert_ffn(tile_expert, n_used, xs, wg, wu, wd):
    r, d = xs.shape
    de = wg.shape[2]
    grid_spec = pltpu.PrefetchScalarGridSpec(
        num_scalar_prefetch=2,
        grid=(r // FFN_ROWS,),
        in_specs=[pl.BlockSpec((FFN_ROWS, d), lambda j, te, nu: (j, 0)),
                  pl.BlockSpec((1, d, de), lambda j, te, nu: (te[j], 0, 0)),
                  pl.BlockSpec((1, d, de), lambda j, te, nu: (te[j], 0, 0)),
                  pl.BlockSpec((1, de, d), lambda j, te, nu: (te[j], 0, 0))],
        out_specs=pl.BlockSpec((FFN_ROWS, d), lambda j, te, nu: (j, 0)),
    )
    return pl.pallas_call(
        _ffn_kernel,
        grid_spec=grid_spec,
        out_shape=jax.ShapeDtypeStruct((r, d), F32),
        compiler_params=_cparams(1),
    )(tile_expert, n_used, xs, wg, wu, wd)


def _combine_kernel(final, x_ref, y0_ref, y1_ref, w0_ref, w1_ref, gt_ref, gf_ref, o_ref):
    x = x_ref[...] + gt_ref[...] * (w0_ref[...] * y0_ref[...] + w1_ref[...] * y1_ref[...])
    if final:
        var = jnp.mean(x * x, axis=-1, keepdims=True)
        x = x * lax.rsqrt(var + RMS_EPS) * gf_ref[...]
    o_ref[...] = x


def _combine(x, y0, y1, w0, w1, gt, g_final, final):
    s, d = x.shape
    tm = 256
    big = pl.BlockSpec((tm, d), lambda i: (i, 0))
    col = pl.BlockSpec((tm, 1), lambda i: (i, 0))
    vec = pl.BlockSpec((1, d), lambda i: (0, 0))
    return pl.pallas_call(
        functools.partial(_combine_kernel, final),
        grid=(s // tm,),
        in_specs=[big, big, big, col, col, vec, vec],
        out_specs=big,
        out_shape=jax.ShapeDtypeStruct((s, d), F32),
        compiler_params=_cparams(1),
    )(x, y0, y1, w0, w1, gt, g_final)


def _rope_tables(positions):
    half = HEAD_DIM // 2
    inv = ROPE_THETA ** (-jnp.arange(0, HEAD_DIM, 2, dtype=F32) / HEAD_DIM)
    ang = positions.astype(F32)[:, None] * inv
    cos, sin = jnp.cos(ang), jnp.sin(ang)
    cos_t = jnp.tile(cos, (1, LANES // half))
    sin_t = jnp.tile(jnp.concatenate([-sin, sin], axis=1), (1, LANES // HEAD_DIM))
    return cos_t, sin_t


def _regroup_in_weights(w_in):
    a = b = w_in.shape[1] // 12
    c = 2 * a
    sizes = [a, a, a, b, b, b, c, c, c]
    starts = np.concatenate([[0], np.cumsum(sizes)[:-1]])
    cols = lambda idx: [w_in[:, starts[k]:starts[k] + sizes[k]] for k in idx]
    return jnp.concatenate(cols((1, 4, 7)) + cols((0, 3, 6)) + cols((2, 5, 8)), axis=1)


def _block_means(ksum):
    nblk, width = ksum.shape
    km = (ksum / MOBA_BLOCK).reshape(nblk, width // HEAD_DIM, HEAD_DIM)
    km = jnp.pad(km, ((0, HEAD_DIM - nblk), (0, 0), (0, 0)))
    return jnp.transpose(km, (1, 0, 2))


def _moe(x1, h2, route, counts, wg, wu, wd, gt, g_final, final):
    s, d = x1.shape
    e0, e1 = route[0].astype(I32), route[1].astype(I32)
    r0, r1 = route[2].astype(I32), route[3].astype(I32)
    cnt = counts[:, 0].astype(I32)
    cnt_pad = ((cnt + FFN_ROWS - 1) // FFN_ROWS) * FFN_ROWS
    off = jnp.concatenate([jnp.zeros((1,), I32), jnp.cumsum(cnt_pad).astype(I32)])
    d0 = off[e0] + r0
    d1 = off[e1] + r1
    n_rows = 2 * s + N_EXPERTS * FFN_ROWS
    tok = jnp.arange(s, dtype=I32)
    src = jnp.zeros((n_rows,), I32).at[d0].set(tok).at[d1].set(tok)
    n_tiles = n_rows // FFN_ROWS
    tile_expert = jnp.searchsorted(off[1:], jnp.arange(n_tiles, dtype=I32) * FFN_ROWS, side="right")
    tile_expert = jnp.minimum(tile_expert, N_EXPERTS - 1).astype(I32)
    n_used = (off[N_EXPERTS] // FFN_ROWS).reshape(1).astype(I32)
    y = _expert_ffn(tile_expert, n_used, h2[src], wg, wu, wd)
    return _combine(x1, y[d0], y[d1], route[4].reshape(s, 1), route[5].reshape(s, 1), gt, g_final, final)


def kernel(x, c, positions, w_ada, b_ada, g_attn, g_mlp, w_in, w_out, lam_q1, lam_k1, lam_q2, lam_k2,
           g_subln, w_gate, w_up, w_down, w_router, b_router, g_final):
    batch, s, d = x.shape
    assert batch == 1 and s % 512 == 0 and s // MOBA_BLOCK <= HEAD_DIM
    assert w_in.shape[2] == 3 * N_KIND_TILES * PROJ_TN
    depth = w_ada.shape[0]
    xs = x.reshape(s, d)
    cos_t, sin_t = _rope_tables(positions[0])
    mod = _modulation(c, w_ada, b_ada)
    tab = _dilated_bias_table()
    gf = g_final.reshape(1, d)
    row = lambda v: v.reshape(1, -1)
    for l in range(depth):
        lam_init = 0.8 - 0.6 * math.exp(-0.3 * l)
        sh_a, sc_a, gt_a, sh_m, sc_m, gt_m = [mod[l, :, k * d:(k + 1) * d] for k in range(6)]
        k, qt, qbt, vt, ksum = _in_projection(xs, row(g_attn[l]), sc_a, sh_a,
                                              _regroup_in_weights(w_in[l]).astype(BF16), cos_t, sin_t)
        o_a = _dilated_attention(qt, k, vt, tab, 0, 0, 0)
        bias = _moba_gate(qbt, _block_means(ksum.reshape(s // MOBA_BLOCK, -1)))
        o_b = _moba_attention(qt, bias, k, vt, 4, 4, 4)
        o_c = _diff_attention(qt, k, vt, row(lam_q1[l]), row(lam_k1[l]), row(lam_q2[l]), row(lam_k2[l]),
                              g_subln[l].reshape(-1, 1), lam_init, 8, 8, 8)
        x1 = _out_projection(o_a, o_b, o_c, w_out[l].T.astype(BF16), xs, gt_a)
        h2, route, counts = _norm_and_route(x1, row(g_mlp[l]), sc_m, sh_m, w_router, b_router)
        xs = _moe(x1, h2, route, counts, w_gate[l].astype(BF16), w_up[l].astype(BF16),
                  w_down[l].astype(BF16), gt_m, gf, l == depth - 1)
    return xs.reshape(batch, s, d)
```

```python
import functools
import math

import numpy as np
import jax
import jax.numpy as jnp
from jax import lax
from jax.experimental import pallas as pl
from jax.experimental.pallas import tpu as pltpu

F32 = jnp.float32
BF16 = jnp.bfloat16
I32 = jnp.int32

HEAD_DIM = 64
LANES = 128
A_PATTERNS = ((128, 1), (512, 4), (2048, 16))
MOBA_BLOCK = 256
MOBA_TOPK = 3
N_EXPERTS = 16
N_GROUPS = 4
EXPERTS_PER_GROUP = N_EXPERTS // N_GROUPS
ROPE_THETA = 10000.0
RMS_EPS = 1e-6
NEG = -1e30
VMEM_LIMIT = 56 * 1024 * 1024

TQ = 512
TK = 256
KPQ = TQ // TK
assert KPQ == 2
PROJ_TM = 512
PROJ_TN = 512
NT_DIMS = (((1,), (1,)), ((), ()))
Q_SCALE = HEAD_DIM ** -0.5 * math.log2(math.e)


def _cparams(n_axes):
    return pltpu.CompilerParams(dimension_semantics=("arbitrary",) * n_axes,
                                vmem_limit_bytes=VMEM_LIMIT)


def _split_bf16(x):
    hi = x.astype(BF16)
    lo = (x - hi.astype(F32)).astype(BF16)
    return hi, lo


def _dot3(a, b, dims=(((1,), (0,)), ((), ()))):
    a_hi, a_lo = _split_bf16(a)
    b_hi, b_lo = _split_bf16(b)
    dg = functools.partial(lax.dot_general, dimension_numbers=dims, preferred_element_type=F32)
    return dg(a_hi, b_hi) + dg(a_hi, b_lo) + dg(a_lo, b_hi)


def _sigmoid(x):
    return 1.0 / (1.0 + jnp.exp(-x))


def _cast_kernel(x_ref, o_ref):
    o_ref[...] = x_ref[0].astype(o_ref.dtype)


def _layer_to_bf16(w, l):
    cols = w.shape[-1]
    w3 = w.reshape(w.shape[0], -1, cols)
    rows = w3.shape[1]
    tr, tc = min(rows, 1024), min(cols, 1024)
    return pl.pallas_call(
        _cast_kernel,
        grid=(rows // tr, cols // tc),
        in_specs=[pl.BlockSpec((1, tr, tc), lambda i, j: (l, i, j))],
        out_specs=pl.BlockSpec((tr, tc), lambda i, j: (i, j)),
        out_shape=jax.ShapeDtypeStruct((rows, cols), BF16),
        compiler_params=_cparams(2),
    )(w3)


def _mod_kernel(c_ref, w_ref, b_ref, o_ref):
    c = c_ref[...]
    sc = c * _sigmoid(c)
    d = c.shape[0]
    acc = b_ref[0]
    for r in range(0, d, 256):
        acc = acc + jnp.sum(w_ref[0, r:r + 256, :] * sc[r:r + 256, :], axis=0, keepdims=True)
    o_ref[0] = acc


def _modulation(c, w_ada, b_ada):
    depth, d, n = w_ada.shape
    tn = 512
    return pl.pallas_call(
        _mod_kernel,
        grid=(depth, n // tn),
        in_specs=[pl.BlockSpec((d, 1), lambda l, j: (0, 0)),
                  pl.BlockSpec((1, d, tn), lambda l, j: (l, 0, j)),
                  pl.BlockSpec((1, 1, tn), lambda l, j: (l, 0, j))],
        out_specs=pl.BlockSpec((1, 1, tn), lambda l, j: (l, 0, j)),
        out_shape=jax.ShapeDtypeStruct((depth, 1, n), F32),
        compiler_params=_cparams(2),
    )(c.reshape(d, 1), w_ada, b_ada.reshape(depth, 1, n))


def _modulated_norm(x, g, sc, sh):
    var = jnp.mean(x * x, axis=-1, keepdims=True)
    return (x * lax.rsqrt(var + RMS_EPS) * g) * (1.0 + sc) + sh


def _rope_tile(acc, cos, sin):
    lane = lax.broadcasted_iota(I32, (1, LANES), 1)
    first = (lane & (HEAD_DIM - 1)) < (HEAD_DIM // 2)
    outs = []
    for c in range(acc.shape[1] // LANES):
        xc = acc[:, c * LANES:(c + 1) * LANES]
        rot = jnp.where(first, pltpu.roll(xc, LANES - HEAD_DIM // 2, 1), pltpu.roll(xc, HEAD_DIM // 2, 1))
        outs.append(xc * cos + rot * sin)
    return jnp.concatenate(outs, axis=1)


N_KIND_TILES = 4
COLUMN_TILE_ORDER = (1, 4, 8, 9, 0, 3, 6, 7, 2, 5, 10, 11)
KB_TILE = 1
QB_TILE = N_KIND_TILES + 1


def _inproj_kernel(order_ref, x_ref, g_ref, sc_ref, sh_ref, w_ref, cos_ref, sin_ref,
                   k_ref, qt_ref, qbt_ref, vt_ref, ksum_ref, h_scr):
    j = pl.program_id(1)

    @pl.when(j == 0)
    def _():
        h_scr[...] = _modulated_norm(x_ref[...], g_ref[...], sc_ref[...], sh_ref[...]).astype(BF16)

    acc = jnp.dot(h_scr[...], w_ref[...], preferred_element_type=F32)

    @pl.when(j < N_KIND_TILES)
    def _():
        r = _rope_tile(acc, cos_ref[...], sin_ref[...])
        k_ref[...] = r.astype(BF16)

        @pl.when(j == KB_TILE)
        def _():
            tm, tn = r.shape
            ksum_ref[0] = jnp.sum(r.reshape(tm // MOBA_BLOCK, MOBA_BLOCK, tn), axis=1)

    @pl.when((j >= N_KIND_TILES) & (j < 2 * N_KIND_TILES))
    def _():
        rt = _rope_tile(acc, cos_ref[...], sin_ref[...]).T
        qt_ref[...] = (rt * Q_SCALE).astype(BF16)

        @pl.when(j == QB_TILE)
        def _():
            qbt_ref[...] = rt

    @pl.when(j >= 2 * N_KIND_TILES)
    def _():
        vt = acc.T.astype(BF16)
        for b in range(vt_ref.shape[0]):
            vt_ref[b] = vt[:, b * TK:(b + 1) * TK]


def _in_projection(x, g, sc, sh, w_bf16, cos_t, sin_t):
    s, d = x.shape
    n = w_bf16.shape[1]
    tm, tn = PROJ_TM, PROJ_TN
    nb = tm // MOBA_BLOCK
    width = N_KIND_TILES * tn
    row = lambda i, j, order: (0, 0)
    kind = lambda j, first: jnp.clip(j - first, 0, N_KIND_TILES - 1)
    grid_spec = pltpu.PrefetchScalarGridSpec(
        num_scalar_prefetch=1,
        grid=(s // tm, n // tn),
        in_specs=[pl.BlockSpec((tm, d), lambda i, j, order: (i, 0)),
                  pl.BlockSpec((1, d), row), pl.BlockSpec((1, d), row), pl.BlockSpec((1, d), row),
                  pl.BlockSpec((d, tn), lambda i, j, order: (0, order[j])),
                  pl.BlockSpec((tm, LANES), lambda i, j, order: (i, 0)),
                  pl.BlockSpec((tm, LANES), lambda i, j, order: (i, 0))],
        out_specs=[pl.BlockSpec((tm, tn), lambda i, j, order: (i, kind(j, 0))),
                   pl.BlockSpec((tn, tm), lambda i, j, order: (kind(j, N_KIND_TILES), i)),
                   pl.BlockSpec((tn, tm), lambda i, j, order: (0, i)),
                   pl.BlockSpec((tm // TK, tn, TK), lambda i, j, order: (i, kind(j, 2 * N_KIND_TILES), 0)),
                   pl.BlockSpec((1, nb, tn), lambda i, j, order: (i, 0, 0))],
        scratch_shapes=[pltpu.VMEM((tm, d), BF16)],
    )
    return pl.pallas_call(
        _inproj_kernel,
        grid_spec=grid_spec,
        out_shape=[jax.ShapeDtypeStruct((s, width), BF16),
                   jax.ShapeDtypeStruct((width, s), BF16),
                   jax.ShapeDtypeStruct((tn, s), F32),
                   jax.ShapeDtypeStruct((s // TK, width, TK), BF16),
                   jax.ShapeDtypeStruct((s // tm, nb, tn), F32)],
        compiler_params=_cparams(2),
    )(jnp.asarray(COLUMN_TILE_ORDER, I32), x, g, sc, sh, w_bf16, cos_t, sin_t)


def _flash_step(s_t, v_t, state):
    m_prev, l_prev, acc_prev = state
    m_new = jnp.maximum(m_prev, jnp.max(s_t, axis=0, keepdims=True))
    alpha = jnp.exp2(m_prev - m_new)
    p = jnp.exp2(s_t - m_new)
    l_new = alpha * l_prev + jnp.sum(p, axis=0, keepdims=True)
    acc_new = alpha * acc_prev + jnp.dot(v_t, p.astype(BF16), preferred_element_type=F32)
    return m_new, l_new, acc_new


def _sweep(first, n_loop, scores, consume, s_a, s_b, tail_masks):
    def park(buf, vals):
        for h, val in enumerate(vals):
            buf[h] = val

    def fetch(buf):
        return [buf[h] for h in range(buf.shape[0])]

    park(s_a, scores(first))

    def pair(j, carry):
        a = first + 2 * j
        park(s_b, scores(a + 1))
        consume(a, fetch(s_a), None)
        park(s_a, scores(a + 2))
        consume(a + 1, fetch(s_b), None)
        return carry

    lax.fori_loop(0, n_loop // 2, pair, 0)
    t0 = first + n_loop
    park(s_b, scores(t0 + 1))
    consume(t0, fetch(s_a), tail_masks[0])
    consume(t0 + 1, fetch(s_b), tail_masks[1])


def _score_scratch(n_streams):
    return [pltpu.VMEM((n_streams, TK, TQ), F32), pltpu.VMEM((n_streams, TK, TQ), F32)]


def _qk_t(k, q_t):
    return jnp.dot(k, q_t, preferred_element_type=F32)


def _k_block(k_ref, n):
    return k_ref[pl.ds(pl.multiple_of(n * TK, TK), TK), :]


def _init_state(refs):
    m_ref, l_ref, acc_ref = refs
    m_ref[...] = jnp.full(m_ref.shape, NEG, F32)
    l_ref[...] = jnp.zeros(l_ref.shape, F32)
    acc_ref[...] = jnp.zeros(acc_ref.shape, F32)


def _load_state(refs):
    return tuple(r[...] for r in refs)


def _store_state(refs, state):
    for r, val in zip(refs, state):
        r[...] = val


def _flash_scratch(n_streams, dv):
    out = []
    for _ in range(n_streams):
        out += [pltpu.VMEM((1, TQ), F32), pltpu.VMEM((1, TQ), F32), pltpu.VMEM((dv, TQ), F32)]
    return out


def _causal_t(d):
    r = lax.broadcasted_iota(I32, (TK, TQ), 0) + d * TK
    return r <= lax.broadcasted_iota(I32, (TK, TQ), 1)


def _stack_rows(top, bottom):
    return jnp.concatenate([top, bottom], axis=0)


def _dilated_bias_table():
    n_off = A_PATTERNS[-1][0] // TK + KPQ
    r = np.arange(TK)[:, None]
    c = np.arange(TQ)[None, :]
    tabs = []
    for t in range(n_off):
        delta = (t - (KPQ - 1)) * TK + c - r
        mult = np.zeros_like(delta)
        for window, dil in A_PATTERNS:
            mult += ((delta >= 0) & (delta % dil == 0) & (delta <= window)).astype(delta.dtype)
        tabs.append(np.where(mult > 0, np.log2(np.maximum(mult, 1).astype(np.float64)), NEG))
    return jnp.asarray(np.stack(tabs), F32)


def _dilated_kernel(qt_ref, k_ref, vt_ref, tab_ref, o_ref, s_a, s_b, *scr):
    i = pl.program_id(1)
    refs = (scr[0:3], scr[3:6])
    q_t = qt_ref[...]
    zero = jnp.zeros((HEAD_DIM, TQ), BF16)
    qs = (_stack_rows(q_t[:HEAD_DIM], zero), _stack_rows(zero, q_t[HEAD_DIM:]))
    n_off = tab_ref.shape[0]
    last = KPQ * i + (KPQ - 1)
    for h in range(2):
        _init_state(refs[h])

    def scores(n):
        kb = _k_block(k_ref, n)
        bias = tab_ref[last - n]
        return [_qk_t(kb, qs[h]) + bias for h in range(2)]

    def consume(n, s_list, mask):
        state = [_load_state(refs[h]) for h in range(2)]
        vb = vt_ref[n]
        for h in range(2):
            state[h] = _flash_step(s_list[h], vb[h * HEAD_DIM:(h + 1) * HEAD_DIM, :], state[h])
        for h in range(2):
            _store_state(refs[h], state[h])

    first = jnp.maximum(last - (n_off - 1), 0)
    _sweep(first, last - 1 - first, scores, consume, s_a, s_b, (None, None))
    for h in range(2):
        _, l, acc = _load_state(refs[h])
        o_ref[h * HEAD_DIM:(h + 1) * HEAD_DIM, :] = (acc / l).astype(o_ref.dtype)


def _dilated_attention(qt, k, vt, tab, qrow, kcol, vrow):
    s = k.shape[0]
    n_pairs = 4
    return pl.pallas_call(
        _dilated_kernel,
        grid=(n_pairs, s // TQ),
        in_specs=[pl.BlockSpec((LANES, TQ), lambda hp, i: (qrow + hp, i)),
                  pl.BlockSpec((s, LANES), lambda hp, i: (0, kcol + hp)),
                  pl.BlockSpec((s // TK, LANES, TK), lambda hp, i: (0, vrow + hp, 0)),
                  pl.BlockSpec(tab.shape, lambda hp, i: (0, 0, 0))],
        out_specs=pl.BlockSpec((LANES, TQ), lambda hp, i: (hp, i)),
        out_shape=jax.ShapeDtypeStruct((n_pairs * LANES, s), BF16),
        scratch_shapes=_score_scratch(2) + _flash_scratch(2, HEAD_DIM),
        compiler_params=_cparams(2),
    )(qt, k, vt, tab)


def _moba_gate_kernel(qt_ref, km_ref, bias_ref):
    i = pl.program_id(1)
    tg = qt_ref.shape[1]
    gate = _dot3(km_ref[0], qt_ref[...])
    nblk = gate.shape[0]
    blk = lax.broadcasted_iota(I32, (nblk, tg), 0)
    blk_f = blk.astype(F32)
    tok = i * tg + lax.broadcasted_iota(I32, (nblk, tg), 1)
    own = jnp.right_shift(tok, int(math.log2(MOBA_BLOCK)))
    past = blk < own
    sel = blk == own
    for _ in range(MOBA_TOPK):
        cand = past & jnp.logical_not(sel)
        g = jnp.where(cand, gate, -jnp.inf)
        mx = jnp.max(g, axis=0, keepdims=True)
        first = jnp.min(jnp.where(cand & (g == mx), blk_f, 2.0 * nblk), axis=0, keepdims=True)
        sel = sel | (blk_f == first)
    bias_ref[...] = jnp.where(sel, 0.0, NEG).astype(BF16)


def _moba_gate(qbt, kmean):
    width, s = qbt.shape
    n_heads = width // HEAD_DIM
    tg = 512
    return pl.pallas_call(
        _moba_gate_kernel,
        grid=(n_heads, s // tg),
        in_specs=[pl.BlockSpec((HEAD_DIM, tg), lambda h, i: (h, i)),
                  pl.BlockSpec((1, HEAD_DIM, HEAD_DIM), lambda h, i: (h, 0, 0))],
        out_specs=pl.BlockSpec((HEAD_DIM, tg), lambda h, i: (h, i)),
        out_shape=jax.ShapeDtypeStruct((width, s), BF16),
        compiler_params=_cparams(2),
    )(qbt, kmean)


def _moba_kernel(qt_ref, bias_ref, k_ref, vt_ref, o_ref, s_a, s_b, *scr):
    i = pl.program_id(1)
    refs = (scr[0:3], scr[3:6])
    q_t = qt_ref[...]
    b_t = bias_ref[...]
    qs = (_stack_rows(q_t[:HEAD_DIM], b_t[:HEAD_DIM]), _stack_rows(b_t[HEAD_DIM:], q_t[HEAD_DIM:]))
    lane = lax.broadcasted_iota(I32, (1, LANES), 1)
    lo = lane < HEAD_DIM
    mine = (lo, jnp.logical_not(lo))
    blk = lane & (HEAD_DIM - 1)
    for h in range(2):
        _init_state(refs[h])

    def scores(n):
        kb = _k_block(k_ref, n)
        onehot = jnp.broadcast_to(jnp.where(blk == n, 1.0, 0.0).astype(BF16), kb.shape)
        return [_qk_t(jnp.where(mine[h], kb, onehot), qs[h]) for h in range(2)]

    def consume(n, s_list, mask):
        state = [_load_state(refs[h]) for h in range(2)]
        vb = vt_ref[n]
        for h in range(2):
            s_t = s_list[h] if mask is None else jnp.where(mask, s_list[h], NEG)
            state[h] = _flash_step(s_t, vb[h * HEAD_DIM:(h + 1) * HEAD_DIM, :], state[h])
        for h in range(2):
            _store_state(refs[h], state[h])

    _sweep(0, KPQ * i, scores, consume, s_a, s_b, (_causal_t(0), _causal_t(1)))
    for h in range(2):
        _, l, acc = _load_state(refs[h])
        o_ref[h * HEAD_DIM:(h + 1) * HEAD_DIM, :] = (acc / l).astype(o_ref.dtype)


def _moba_attention(qt, bias, k, vt, qrow, kcol, vrow):
    s = k.shape[0]
    n_pairs = 4
    return pl.pallas_call(
        _moba_kernel,
        grid=(n_pairs, s // TQ),
        in_specs=[pl.BlockSpec((LANES, TQ), lambda hp, i: (qrow + hp, i)),
                  pl.BlockSpec((LANES, TQ), lambda hp, i: (hp, i)),
                  pl.BlockSpec((s, LANES), lambda hp, i: (0, kcol + hp)),
                  pl.BlockSpec((s // TK, LANES, TK), lambda hp, i: (0, vrow + hp, 0))],
        out_specs=pl.BlockSpec((LANES, TQ), lambda hp, i: (hp, i)),
        out_shape=jax.ShapeDtypeStruct((n_pairs * LANES, s), BF16),
        scratch_shapes=_score_scratch(2) + _flash_scratch(2, HEAD_DIM),
        compiler_params=_cparams(2),
    )(qt, bias, k, vt)


def _diff_kernel(lam_init, qt_ref, k_ref, vt_ref, lq1_ref, lk1_ref, lq2_ref, lk2_ref, g_ref, o_ref,
                 s_a, s_b, *scr):
    i = pl.program_id(1)
    refs = (scr[0:3], scr[3:6])
    q_t = qt_ref[...]
    zero = jnp.zeros((HEAD_DIM, TQ), BF16)
    qs = (_stack_rows(q_t[:HEAD_DIM], zero), _stack_rows(zero, q_t[HEAD_DIM:]))
    for h in range(2):
        _init_state(refs[h])

    def scores(n):
        kb = _k_block(k_ref, n)
        return [_qk_t(kb, qs[h]) for h in range(2)]

    def consume(n, s_list, mask):
        state = [_load_state(refs[h]) for h in range(2)]
        vb = vt_ref[n]
        for h in range(2):
            s_t = s_list[h] if mask is None else jnp.where(mask, s_list[h], NEG)
            state[h] = _flash_step(s_t, vb, state[h])
        for h in range(2):
            _store_state(refs[h], state[h])

    _sweep(0, KPQ * i, scores, consume, s_a, s_b, (_causal_t(0), _causal_t(1)))
    lam = (jnp.exp(jnp.sum(lq1_ref[...] * lk1_ref[...], axis=1, keepdims=True))
           - jnp.exp(jnp.sum(lq2_ref[...] * lk2_ref[...], axis=1, keepdims=True)) + lam_init)
    _, l1, acc1 = _load_state(refs[0])
    _, l2, acc2 = _load_state(refs[1])
    o = acc1 / l1 - lam * (acc2 / l2)
    var = jnp.mean(o * o, axis=0, keepdims=True)
    o = (o * lax.rsqrt(var + RMS_EPS) * g_ref[...]) * (1.0 - lam_init)
    o_ref[...] = o.astype(o_ref.dtype)


def _diff_attention(qt, k, vt, lq1, lk1, lq2, lk2, g_sub, lam_init, qrow, kcol, vrow):
    s = k.shape[0]
    n_heads = 8
    vec = lambda n: pl.BlockSpec((1, n), lambda h, i: (0, 0))
    return pl.pallas_call(
        functools.partial(_diff_kernel, lam_init),
        grid=(n_heads, s // TQ),
        in_specs=[pl.BlockSpec((LANES, TQ), lambda h, i: (qrow + h, i)),
                  pl.BlockSpec((s, LANES), lambda h, i: (0, kcol + h)),
                  pl.BlockSpec((s // TK, LANES, TK), lambda h, i: (0, vrow + h, 0)),
                  vec(HEAD_DIM), vec(HEAD_DIM), vec(HEAD_DIM), vec(HEAD_DIM),
                  pl.BlockSpec((LANES, 1), lambda h, i: (0, 0))],
        out_specs=pl.BlockSpec((LANES, TQ), lambda h, i: (h, i)),
        out_shape=jax.ShapeDtypeStruct((n_heads * LANES, s), BF16),
        scratch_shapes=_score_scratch(2) + _flash_scratch(2, LANES),
        compiler_params=_cparams(2),
    )(qt, k, vt, lq1, lk1, lq2, lk2, g_sub)


def _outproj_kernel(oa_ref, ob_ref, oc_ref, wt_ref, x_ref, gt_ref, o_ref):
    wa = oa_ref.shape[0]
    wb = ob_ref.shape[0]
    y_t = (jnp.dot(wt_ref[:, 0:wa], oa_ref[...], preferred_element_type=F32)
           + jnp.dot(wt_ref[:, wa:wa + wb], ob_ref[...], preferred_element_type=F32)
           + jnp.dot(wt_ref[:, wa + wb:], oc_ref[...], preferred_element_type=F32))
    o_ref[...] = x_ref[...] + gt_ref[...] * y_t.T


def _out_projection(oa_t, ob_t, oc_t, wt_bf16, x, gt):
    s, d = x.shape
    tm = 256
    return pl.pallas_call(
        _outproj_kernel,
        grid=(s // tm,),
        in_specs=[pl.BlockSpec((oa_t.shape[0], tm), lambda i: (0, i)),
                  pl.BlockSpec((ob_t.shape[0], tm), lambda i: (0, i)),
                  pl.BlockSpec((oc_t.shape[0], tm), lambda i: (0, i)),
                  pl.BlockSpec(wt_bf16.shape, lambda i: (0, 0)),
                  pl.BlockSpec((tm, d), lambda i: (i, 0)),
                  pl.BlockSpec((1, d), lambda i: (0, 0))],
        out_specs=pl.BlockSpec((tm, d), lambda i: (i, 0)),
        out_shape=jax.ShapeDtypeStruct((s, d), F32),
        compiler_params=_cparams(1),
    )(oa_t, ob_t, oc_t, wt_bf16, x, gt)


def _route_kernel(x_ref, g_ref, sc_ref, sh_ref, wr_ref, br_ref, tri_ref,
                  h_ref, route_ref, count_ref, base_scr):
    i = pl.program_id(0)
    tm = x_ref.shape[0]

    @pl.when(i == 0)
    def _():
        base_scr[...] = jnp.zeros_like(base_scr)

    h = _modulated_norm(x_ref[...], g_ref[...], sc_ref[...], sh_ref[...])
    h_ref[...] = h.astype(BF16)
    logits = _dot3(wr_ref[...], h, NT_DIMS)
    s = _sigmoid(logits)
    sg = s + br_ref[...]
    rows = [sg[r:r + 1, :] for r in range(N_EXPERTS)]

    def top2_sum(vals):
        best = None
        for a in range(len(vals)):
            for b in range(a + 1, len(vals)):
                pair = vals[a] + vals[b]
                best = pair if best is None else jnp.maximum(best, pair)
        return best

    score = [top2_sum(rows[g * EXPERTS_PER_GROUP:(g + 1) * EXPERTS_PER_GROUP]) for g in range(N_GROUPS)]
    best, grp = score[0], jnp.zeros((1, tm), I32)
    for g in range(1, N_GROUPS):
        better = score[g] > best
        grp = jnp.where(better, g, grp)
        best = jnp.where(better, score[g], best)
    cand = []
    for j in range(EXPERTS_PER_GROUP):
        c = rows[j]
        for g in range(1, N_GROUPS):
            c = jnp.where(grp == g, rows[g * EXPERTS_PER_GROUP + j], c)
        cand.append(c)
    b0, l0 = cand[0], jnp.zeros((1, tm), I32)
    for j in range(1, EXPERTS_PER_GROUP):
        better = cand[j] > b0
        l0 = jnp.where(better, j, l0)
        b0 = jnp.where(better, cand[j], b0)
    b1, l1 = jnp.full((1, tm), -jnp.inf, F32), jnp.zeros((1, tm), I32)
    for j in range(EXPERTS_PER_GROUP):
        better = (l0 != j) & (cand[j] > b1)
        l1 = jnp.where(better, j, l1)
        b1 = jnp.where(better, cand[j], b1)
    e0 = grp * EXPERTS_PER_GROUP + l0
    e1 = grp * EXPERTS_PER_GROUP + l1
    erow = lax.broadcasted_iota(I32, (N_EXPERTS, tm), 0)
    sel0 = erow == e0
    sel1 = erow == e1
    s0 = jnp.sum(jnp.where(sel0, s, 0.0), axis=0, keepdims=True)
    s1 = jnp.sum(jnp.where(sel1, s, 0.0), axis=0, keepdims=True)
    tot = s0 + s1
    chosen = jnp.where(sel0 | sel1, 1.0, 0.0).astype(BF16)
    csum = jnp.dot(chosen, tri_ref[...], preferred_element_type=F32)
    pos = base_scr[...] + csum - 1.0
    r0 = jnp.sum(jnp.where(sel0, pos, 0.0), axis=0, keepdims=True)
    r1 = jnp.sum(jnp.where(sel1, pos, 0.0), axis=0, keepdims=True)
    base = base_scr[...] + csum[:, tm - 1:tm]
    base_scr[...] = base
    count_ref[...] = jnp.broadcast_to(base, count_ref.shape)
    zero = jnp.zeros((1, tm), F32)
    for r, val in enumerate((e0.astype(F32), e1.astype(F32), r0, r1, s0 / tot, s1 / tot, zero, zero)):
        route_ref[r:r + 1, :] = val


def _norm_and_route(x, g, sc, sh, w_router, b_router):
    s, d = x.shape
    tm = 512
    tri = jnp.asarray(np.triu(np.ones((tm, tm), np.float32)), BF16)
    row = lambda i: (0, 0)
    return pl.pallas_call(
        _route_kernel,
        grid=(s // tm,),
        in_specs=[pl.BlockSpec((tm, d), lambda i: (i, 0)),
                  pl.BlockSpec((1, d), row), pl.BlockSpec((1, d), row), pl.BlockSpec((1, d), row),
                  pl.BlockSpec((N_EXPERTS, d), row), pl.BlockSpec((N_EXPERTS, 1), row),
                  pl.BlockSpec((tm, tm), row)],
        out_specs=[pl.BlockSpec((tm, d), lambda i: (i, 0)),
                   pl.BlockSpec((8, tm), lambda i: (0, i)),
                   pl.BlockSpec((N_EXPERTS, LANES), row)],
        out_shape=[jax.ShapeDtypeStruct((s, d), BF16),
                   jax.ShapeDtypeStruct((8, s), F32),
                   jax.ShapeDtypeStruct((N_EXPERTS, LANES), F32)],
        scratch_shapes=[pltpu.VMEM((N_EXPERTS, 1), F32)],
        compiler_params=_cparams(1),
    )(x, g, sc, sh, w_router.T, b_router.reshape(N_EXPERTS, 1), tri)


FFN_ROWS = 512


def _ffn_kernel(te_ref, nu_ref, xs_ref, wg_ref, wu_ref, wd_ref, y_ref):
    j = pl.program_id(0)

    @pl.when(j < nu_ref[0])
    def _():
        x = xs_ref[...]
        a = jnp.dot(x, wg_ref[0], preferred_element_type=F32)
        b = jnp.dot(x, wu_ref[0], preferred_element_type=F32)
        act = (a * _sigmoid(a) * b).astype(BF16)
        y_ref[...] = jnp.dot(act, wd_ref[0], preferred_element_type=F32).astype(y_ref.dtype)

    @pl.when(j >= nu_ref[0])
    def _():
        y_ref[...] = jnp.zeros_like(y_ref)


def _expert_ffn(tile_expert, n_used, xs, wg, wu, wd):
    r, d = xs.shape
    de = wg.shape[2]
    grid_spec = pltpu.PrefetchScalarGridSpec(
        num_scalar_prefetch=2,
        grid=(r // FFN_ROWS,),
        in_specs=[pl.BlockSpec((FFN_ROWS, d), lambda j, te, nu: (j, 0)),
                  pl.BlockSpec((1, d, de), lambda j, te, nu: (te[j], 0, 0)),
                  pl.BlockSpec((1, d, de), lambda j, te, nu: (te[j], 0, 0)),
                  pl.BlockSpec((1, de, d), lambda j, te, nu: (te[j], 0, 0))],
        out_specs=pl.BlockSpec((FFN_ROWS, d), lambda j, te, nu: (j, 0)),
    )
    return pl.pallas_call(
        _ffn_kernel,
        grid_spec=grid_spec,
        out_shape=jax.ShapeDtypeStruct((r, d), BF16),
        compiler_params=_cparams(1),
    )(tile_expert, n_used, xs, wg, wu, wd)


def _combine_kernel(final, x_ref, y0_ref, y1_ref, w0_ref, w1_ref, gt_ref, gf_ref, o_ref):
    x = x_ref[...] + gt_ref[...] * (w0_ref[...] * y0_ref[...].astype(F32)
                                    + w1_ref[...] * y1_ref[...].astype(F32))
    if final:
        var = jnp.mean(x * x, axis=-1, keepdims=True)
        x = x * lax.rsqrt(var + RMS_EPS) * gf_ref[...]
    o_ref[...] = x


def _combine(x, y0, y1, w0, w1, gt, g_final, final):
    s, d = x.shape
    tm = 256
    big = pl.BlockSpec((tm, d), lambda i: (i, 0))
    col = pl.BlockSpec((tm, 1), lambda i: (i, 0))
    vec = pl.BlockSpec((1, d), lambda i: (0, 0))
    return pl.pallas_call(
        functools.partial(_combine_kernel, final),
        grid=(s // tm,),
        in_specs=[big, big, big, col, col, vec, vec],
        out_specs=big,
        out_shape=jax.ShapeDtypeStruct((s, d), F32),
        compiler_params=_cparams(1),
    )(x, y0, y1, w0, w1, gt, g_final)


def _rope_tables(positions):
    half = HEAD_DIM // 2
    inv = ROPE_THETA ** (-jnp.arange(0, HEAD_DIM, 2, dtype=F32) / HEAD_DIM)
    ang = positions.astype(F32)[:, None] * inv
    cos, sin = jnp.cos(ang), jnp.sin(ang)
    cos_t = jnp.tile(cos, (1, LANES // half))
    sin_t = jnp.tile(jnp.concatenate([-sin, sin], axis=1), (1, LANES // HEAD_DIM))
    return cos_t, sin_t


def _block_means(ksum):
    nblk, width = ksum.shape
    km = (ksum / MOBA_BLOCK).reshape(nblk, width // HEAD_DIM, HEAD_DIM)
    km = jnp.pad(km, ((0, HEAD_DIM - nblk), (0, 0), (0, 0)))
    return jnp.transpose(km, (1, 0, 2))


def _moe(x1, h2, route, counts, wg, wu, wd, gt, g_final, final):
    s, d = x1.shape
    e0, e1 = route[0].astype(I32), route[1].astype(I32)
    r0, r1 = route[2].astype(I32), route[3].astype(I32)
    cnt = counts[:, 0].astype(I32)
    cnt_pad = ((cnt + FFN_ROWS - 1) // FFN_ROWS) * FFN_ROWS
    off = jnp.concatenate([jnp.zeros((1,), I32), jnp.cumsum(cnt_pad).astype(I32)])
    d0 = off[e0] + r0
    d1 = off[e1] + r1
    n_rows = 2 * s + N_EXPERTS * FFN_ROWS
    tok = jnp.arange(s, dtype=I32)
    src = jnp.zeros((n_rows,), I32).at[d0].set(tok).at[d1].set(tok)
    n_tiles = n_rows // FFN_ROWS
    tile_expert = jnp.searchsorted(off[1:], jnp.arange(n_tiles, dtype=I32) * FFN_ROWS, side="right")
    tile_expert = jnp.minimum(tile_expert, N_EXPERTS - 1).astype(I32)
    n_used = (off[N_EXPERTS] // FFN_ROWS).reshape(1).astype(I32)
    y = _expert_ffn(tile_expert, n_used, h2[src], wg, wu, wd)
    return _combine(x1, y[d0], y[d1], route[4].reshape(s, 1), route[5].reshape(s, 1), gt, g_final, final)


def kernel(x, c, positions, w_ada, b_ada, g_attn, g_mlp, w_in, w_out, lam_q1, lam_k1, lam_q2, lam_k2,
           g_subln, w_gate, w_up, w_down, w_router, b_router, g_final):
    batch, s, d = x.shape
    assert batch == 1 and s % 512 == 0 and s // MOBA_BLOCK <= HEAD_DIM
    assert w_in.shape[2] == 3 * N_KIND_TILES * PROJ_TN
    depth = w_ada.shape[0]
    xs = x.reshape(s, d)
    cos_t, sin_t = _rope_tables(positions[0])
    mod = _modulation(c, w_ada, b_ada)
    tab = _dilated_bias_table()
    gf = g_final.reshape(1, d)
    row = lambda v: v.reshape(1, -1)
    for l in range(depth):
        lam_init = 0.8 - 0.6 * math.exp(-0.3 * l)
        sh_a, sc_a, gt_a, sh_m, sc_m, gt_m = [mod[l, :, k * d:(k + 1) * d] for k in range(6)]
        k, qt, qbt, vt, ksum = _in_projection(xs, row(g_attn[l]), sc_a, sh_a,
                                              _layer_to_bf16(w_in, l), cos_t, sin_t)
        o_a = _dilated_attention(qt, k, vt, tab, 0, 0, 0)
        bias = _moba_gate(qbt, _block_means(ksum.reshape(s // MOBA_BLOCK, -1)))
        o_b = _moba_attention(qt, bias, k, vt, 4, 4, 4)
        o_c = _diff_attention(qt, k, vt, row(lam_q1[l]), row(lam_k1[l]), row(lam_q2[l]), row(lam_k2[l]),
                              g_subln[l].reshape(-1, 1), lam_init, 8, 8, 8)
        x1 = _out_projection(o_a, o_b, o_c, w_out[l].T.astype(BF16), xs, gt_a)
        h2, route, counts = _norm_and_route(x1, row(g_mlp[l]), sc_m, sh_m, w_router, b_router)
        experts = lambda w: _layer_to_bf16(w, l).reshape(w.shape[1:])
        xs = _moe(x1, h2, route, counts, experts(w_gate), experts(w_up), experts(w_down),
                  gt_m, gf, l == depth - 1)
    return xs.reshape(batch, s, d)
```

```python
import functools
import math

import numpy as np
import jax
import jax.numpy as jnp
from jax import lax
from jax.experimental import pallas as pl
from jax.experimental.pallas import tpu as pltpu

F32 = jnp.float32
BF16 = jnp.bfloat16
I32 = jnp.int32

HEAD_DIM = 64
LANES = 128
A_PATTERNS = ((128, 1), (512, 4), (2048, 16))
MOBA_BLOCK = 256
MOBA_TOPK = 3
N_EXPERTS = 16
N_GROUPS = 4
EXPERTS_PER_GROUP = N_EXPERTS // N_GROUPS
ROPE_THETA = 10000.0
RMS_EPS = 1e-6
NEG = -1e30
VMEM_LIMIT = 56 * 1024 * 1024

TQ = 512
TK = 256
KPQ = TQ // TK
assert KPQ == 2
PROJ_TM = 512
PROJ_TN = 512
NT_DIMS = (((1,), (1,)), ((), ()))
Q_SCALE = HEAD_DIM ** -0.5 * math.log2(math.e)


def _cparams(n_axes):
    return pltpu.CompilerParams(dimension_semantics=("arbitrary",) * n_axes,
                                vmem_limit_bytes=VMEM_LIMIT)


def _split_bf16(x):
    hi = x.astype(BF16)
    lo = (x - hi.astype(F32)).astype(BF16)
    return hi, lo


def _dot3(a, b, dims=(((1,), (0,)), ((), ()))):
    a_hi, a_lo = _split_bf16(a)
    b_hi, b_lo = _split_bf16(b)
    dg = functools.partial(lax.dot_general, dimension_numbers=dims, preferred_element_type=F32)
    return dg(a_hi, b_hi) + dg(a_hi, b_lo) + dg(a_lo, b_hi)


def _sigmoid(x):
    return 1.0 / (1.0 + jnp.exp(-x))


CAST_STREAMS = 4
CAST_BLOCK_ELEMS = 512 * 1024
CAST_COLS = 1024


def _cast_kernel(*refs):
    o_ref = refs[-1]
    col_tile = o_ref.shape[2]
    for k, x_ref in enumerate(refs[:-1]):
        rows = x_ref.shape[1]
        x = x_ref[0].astype(o_ref.dtype)
        for t in range(o_ref.shape[0]):
            o_ref[t, k * rows:(k + 1) * rows, :] = x[:, t * col_tile:(t + 1) * col_tile]


def _layer_to_bf16(w, l, col_tile=None):
    cols = w.shape[-1]
    col_tile = col_tile or cols
    w3 = w.reshape(w.shape[0], -1, cols)
    rows = w3.shape[1]
    tc = max(min(cols, CAST_COLS), col_tile)
    sr = CAST_BLOCK_ELEMS // tc
    tr = CAST_STREAMS * sr
    stream = lambda k: pl.BlockSpec((1, sr, tc), lambda i, j: (l, CAST_STREAMS * i + k, j))
    return pl.pallas_call(
        _cast_kernel,
        grid=(rows // tr, cols // tc),
        in_specs=[stream(k) for k in range(CAST_STREAMS)],
        out_specs=pl.BlockSpec((tc // col_tile, tr, col_tile), lambda i, j: (j, i, 0)),
        out_shape=jax.ShapeDtypeStruct((cols // col_tile, rows, col_tile), BF16),
        compiler_params=_cparams(2),
    )(*([w3] * CAST_STREAMS))


def _mod_kernel(c_ref, w_ref, b_ref, o_ref):
    c = c_ref[...]
    sc = c * _sigmoid(c)
    d = c.shape[0]
    acc = b_ref[0]
    for r in range(0, d, 256):
        acc = acc + jnp.sum(w_ref[0, r:r + 256, :] * sc[r:r + 256, :], axis=0, keepdims=True)
    o_ref[0] = acc


def _modulation(c, w_ada, b_ada):
    depth, d, n = w_ada.shape
    tn = 512
    return pl.pallas_call(
        _mod_kernel,
        grid=(depth, n // tn),
        in_specs=[pl.BlockSpec((d, 1), lambda l, j: (0, 0)),
                  pl.BlockSpec((1, d, tn), lambda l, j: (l, 0, j)),
                  pl.BlockSpec((1, 1, tn), lambda l, j: (l, 0, j))],
        out_specs=pl.BlockSpec((1, 1, tn), lambda l, j: (l, 0, j)),
        out_shape=jax.ShapeDtypeStruct((depth, 1, n), F32),
        compiler_params=_cparams(2),
    )(c.reshape(d, 1), w_ada, b_ada.reshape(depth, 1, n))


def _modulated_norm(x, g, sc, sh):
    var = jnp.mean(x * x, axis=-1, keepdims=True)
    return (x * lax.rsqrt(var + RMS_EPS) * g) * (1.0 + sc) + sh


def _rope_tile(acc, cos, sin):
    lane = lax.broadcasted_iota(I32, (1, LANES), 1)
    first = (lane & (HEAD_DIM - 1)) < (HEAD_DIM // 2)
    outs = []
    for c in range(acc.shape[1] // LANES):
        xc = acc[:, c * LANES:(c + 1) * LANES]
        rot = jnp.where(first, pltpu.roll(xc, LANES - HEAD_DIM // 2, 1), pltpu.roll(xc, HEAD_DIM // 2, 1))
        outs.append(xc * cos + rot * sin)
    return jnp.concatenate(outs, axis=1)


N_KIND_TILES = 4
COLUMN_TILE_ORDER = (1, 4, 8, 9, 0, 3, 6, 7, 2, 5, 10, 11)
KB_TILE = 1
QB_TILE = N_KIND_TILES + 1


def _inproj_kernel(order_ref, x_ref, g_ref, sc_ref, sh_ref, w_ref, cos_ref, sin_ref,
                   k_ref, qt_ref, qbt_ref, vt_ref, ksum_ref, h_scr):
    j = pl.program_id(1)

    @pl.when(j == 0)
    def _():
        h_scr[...] = _modulated_norm(x_ref[...], g_ref[...], sc_ref[...], sh_ref[...]).astype(BF16)

    acc = jnp.dot(h_scr[...], w_ref[order_ref[j]], preferred_element_type=F32)

    @pl.when(j < N_KIND_TILES)
    def _():
        r = _rope_tile(acc, cos_ref[...], sin_ref[...])
        k_ref[...] = r.astype(BF16)

        @pl.when(j == KB_TILE)
        def _():
            tm, tn = r.shape
            ksum_ref[0] = jnp.sum(r.reshape(tm // MOBA_BLOCK, MOBA_BLOCK, tn), axis=1)

    @pl.when((j >= N_KIND_TILES) & (j < 2 * N_KIND_TILES))
    def _():
        rt = _rope_tile(acc, cos_ref[...], sin_ref[...]).T
        qt_ref[...] = (rt * Q_SCALE).astype(BF16)

        @pl.when(j == QB_TILE)
        def _():
            qbt_ref[...] = rt

    @pl.when(j >= 2 * N_KIND_TILES)
    def _():
        vt = acc.T.astype(BF16)
        for b in range(vt_ref.shape[0]):
            vt_ref[b] = vt[:, b * TK:(b + 1) * TK]


def _in_projection(x, g, sc, sh, w_bf16, cos_t, sin_t):
    s, d = x.shape
    n = w_bf16.shape[0] * w_bf16.shape[2]
    tm, tn = PROJ_TM, PROJ_TN
    nb = tm // MOBA_BLOCK
    width = N_KIND_TILES * tn
    row = lambda i, j, order: (0, 0)
    kind = lambda j, first: jnp.clip(j - first, 0, N_KIND_TILES - 1)
    grid_spec = pltpu.PrefetchScalarGridSpec(
        num_scalar_prefetch=1,
        grid=(s // tm, n // tn),
        in_specs=[pl.BlockSpec((tm, d), lambda i, j, order: (i, 0)),
                  pl.BlockSpec((1, d), row), pl.BlockSpec((1, d), row), pl.BlockSpec((1, d), row),
                  pl.BlockSpec(w_bf16.shape, lambda i, j, order: (0, 0, 0),
                               pipeline_mode=pl.Buffered(1)),
                  pl.BlockSpec((tm, LANES), lambda i, j, order: (i, 0)),
                  pl.BlockSpec((tm, LANES), lambda i, j, order: (i, 0))],
        out_specs=[pl.BlockSpec((tm, tn), lambda i, j, order: (i, kind(j, 0))),
                   pl.BlockSpec((tn, tm), lambda i, j, order: (kind(j, N_KIND_TILES), i)),
                   pl.BlockSpec((tn, tm), lambda i, j, order: (0, i)),
                   pl.BlockSpec((tm // TK, tn, TK), lambda i, j, order: (i, kind(j, 2 * N_KIND_TILES), 0)),
                   pl.BlockSpec((1, nb, tn), lambda i, j, order: (i, 0, 0))],
        scratch_shapes=[pltpu.VMEM((tm, d), BF16)],
    )
    return pl.pallas_call(
        _inproj_kernel,
        grid_spec=grid_spec,
        out_shape=[jax.ShapeDtypeStruct((s, width), BF16),
                   jax.ShapeDtypeStruct((width, s), BF16),
                   jax.ShapeDtypeStruct((tn, s), F32),
                   jax.ShapeDtypeStruct((s // TK, width, TK), BF16),
                   jax.ShapeDtypeStruct((s // tm, nb, tn), F32)],
        compiler_params=_cparams(2),
    )(jnp.asarray(COLUMN_TILE_ORDER, I32), x, g, sc, sh, w_bf16, cos_t, sin_t)


def _flash_step(s_t, v_t, state):
    m_prev, l_prev, acc_prev = state
    m_new = jnp.maximum(m_prev, jnp.max(s_t, axis=0, keepdims=True))
    alpha = jnp.exp2(m_prev - m_new)
    p = jnp.exp2(s_t - m_new)
    l_new = alpha * l_prev + jnp.sum(p, axis=0, keepdims=True)
    acc_new = alpha * acc_prev + jnp.dot(v_t, p.astype(BF16), preferred_element_type=F32)
    return m_new, l_new, acc_new


def _sweep(first, n_loop, scores, consume, s_a, s_b, tail_masks):
    def park(buf, vals):
        for h, val in enumerate(vals):
            buf[h] = val

    def fetch(buf):
        return [buf[h] for h in range(buf.shape[0])]

    park(s_a, scores(first))

    def pair(j, carry):
        a = first + 2 * j
        park(s_b, scores(a + 1))
        consume(a, fetch(s_a), None)
        park(s_a, scores(a + 2))
        consume(a + 1, fetch(s_b), None)
        return carry

    lax.fori_loop(0, n_loop // 2, pair, 0)
    t0 = first + n_loop
    park(s_b, scores(t0 + 1))
    consume(t0, fetch(s_a), tail_masks[0])
    consume(t0 + 1, fetch(s_b), tail_masks[1])


def _score_scratch(n_streams):
    return [pltpu.VMEM((n_streams, TK, TQ), F32), pltpu.VMEM((n_streams, TK, TQ), F32)]


def _qk_t(k, q_t):
    return jnp.dot(k, q_t, preferred_element_type=F32)


def _k_block(k_ref, n):
    return k_ref[pl.ds(pl.multiple_of(n * TK, TK), TK), :]


def _init_state(refs):
    m_ref, l_ref, acc_ref = refs
    m_ref[...] = jnp.full(m_ref.shape, NEG, F32)
    l_ref[...] = jnp.zeros(l_ref.shape, F32)
    acc_ref[...] = jnp.zeros(acc_ref.shape, F32)


def _load_state(refs):
    return tuple(r[...] for r in refs)


def _store_state(refs, state):
    for r, val in zip(refs, state):
        r[...] = val


def _flash_scratch(n_streams, dv):
    out = []
    for _ in range(n_streams):
        out += [pltpu.VMEM((1, TQ), F32), pltpu.VMEM((1, TQ), F32), pltpu.VMEM((dv, TQ), F32)]
    return out


def _causal_t(d):
    r = lax.broadcasted_iota(I32, (TK, TQ), 0) + d * TK
    return r <= lax.broadcasted_iota(I32, (TK, TQ), 1)


def _stack_rows(top, bottom):
    return jnp.concatenate([top, bottom], axis=0)


def _dilated_bias_table():
    n_off = A_PATTERNS[-1][0] // TK + KPQ
    r = np.arange(TK)[:, None]
    c = np.arange(TQ)[None, :]
    tabs = []
    for t in range(n_off):
        delta = (t - (KPQ - 1)) * TK + c - r
        mult = np.zeros_like(delta)
        for window, dil in A_PATTERNS:
            mult += ((delta >= 0) & (delta % dil == 0) & (delta <= window)).astype(delta.dtype)
        tabs.append(np.where(mult > 0, np.log2(np.maximum(mult, 1).astype(np.float64)), NEG))
    return jnp.asarray(np.stack(tabs), F32)


def _dilated_kernel(qt_ref, k_ref, vt_ref, tab_ref, o_ref, s_a, s_b, *scr):
    i = pl.program_id(1)
    refs = (scr[0:3], scr[3:6])
    q_t = qt_ref[...]
    zero = jnp.zeros((HEAD_DIM, TQ), BF16)
    qs = (_stack_rows(q_t[:HEAD_DIM], zero), _stack_rows(zero, q_t[HEAD_DIM:]))
    n_off = tab_ref.shape[0]
    last = KPQ * i + (KPQ - 1)
    for h in range(2):
        _init_state(refs[h])

    def scores(n):
        kb = _k_block(k_ref, n)
        bias = tab_ref[last - n]
        return [_qk_t(kb, qs[h]) + bias for h in range(2)]

    def consume(n, s_list, mask):
        state = [_load_state(refs[h]) for h in range(2)]
        vb = vt_ref[n]
        for h in range(2):
            state[h] = _flash_step(s_list[h], vb[h * HEAD_DIM:(h + 1) * HEAD_DIM, :], state[h])
        for h in range(2):
            _store_state(refs[h], state[h])

    first = jnp.maximum(last - (n_off - 1), 0)
    _sweep(first, last - 1 - first, scores, consume, s_a, s_b, (None, None))
    for h in range(2):
        _, l, acc = _load_state(refs[h])
        o_ref[h * HEAD_DIM:(h + 1) * HEAD_DIM, :] = (acc / l).astype(o_ref.dtype)


def _dilated_attention(qt, k, vt, tab, qrow, kcol, vrow):
    s = k.shape[0]
    n_pairs = 4
    return pl.pallas_call(
        _dilated_kernel,
        grid=(n_pairs, s // TQ),
        in_specs=[pl.BlockSpec((LANES, TQ), lambda hp, i: (qrow + hp, i)),
                  pl.BlockSpec((s, LANES), lambda hp, i: (0, kcol + hp)),
                  pl.BlockSpec((s // TK, LANES, TK), lambda hp, i: (0, vrow + hp, 0)),
                  pl.BlockSpec(tab.shape, lambda hp, i: (0, 0, 0))],
        out_specs=pl.BlockSpec((LANES, TQ), lambda hp, i: (hp, i)),
        out_shape=jax.ShapeDtypeStruct((n_pairs * LANES, s), BF16),
        scratch_shapes=_score_scratch(2) + _flash_scratch(2, HEAD_DIM),
        compiler_params=_cparams(2),
    )(qt, k, vt, tab)


def _moba_gate_kernel(qt_ref, km_ref, bias_ref):
    i = pl.program_id(1)
    tg = qt_ref.shape[1]
    gate = _dot3(km_ref[0], qt_ref[...])
    nblk = gate.shape[0]
    blk = lax.broadcasted_iota(I32, (nblk, tg), 0)
    blk_f = blk.astype(F32)
    tok = i * tg + lax.broadcasted_iota(I32, (nblk, tg), 1)
    own = jnp.right_shift(tok, int(math.log2(MOBA_BLOCK)))
    past = blk < own
    sel = blk == own
    for _ in range(MOBA_TOPK):
        cand = past & jnp.logical_not(sel)
        g = jnp.where(cand, gate, -jnp.inf)
        mx = jnp.max(g, axis=0, keepdims=True)
        first = jnp.min(jnp.where(cand & (g == mx), blk_f, 2.0 * nblk), axis=0, keepdims=True)
        sel = sel | (blk_f == first)
    bias_ref[...] = jnp.where(sel, 0.0, NEG).astype(BF16)


def _moba_gate(qbt, kmean):
    width, s = qbt.shape
    n_heads = width // HEAD_DIM
    tg = 512
    return pl.pallas_call(
        _moba_gate_kernel,
        grid=(n_heads, s // tg),
        in_specs=[pl.BlockSpec((HEAD_DIM, tg), lambda h, i: (h, i)),
                  pl.BlockSpec((1, HEAD_DIM, HEAD_DIM), lambda h, i: (h, 0, 0))],
        out_specs=pl.BlockSpec((HEAD_DIM, tg), lambda h, i: (h, i)),
        out_shape=jax.ShapeDtypeStruct((width, s), BF16),
        compiler_params=_cparams(2),
    )(qbt, kmean)


def _moba_kernel(qt_ref, bias_ref, k_ref, vt_ref, o_ref, s_a, s_b, *scr):
    i = pl.program_id(1)
    refs = (scr[0:3], scr[3:6])
    q_t = qt_ref[...]
    b_t = bias_ref[...]
    qs = (_stack_rows(q_t[:HEAD_DIM], b_t[:HEAD_DIM]), _stack_rows(b_t[HEAD_DIM:], q_t[HEAD_DIM:]))
    lane = lax.broadcasted_iota(I32, (1, LANES), 1)
    lo = lane < HEAD_DIM
    mine = (lo, jnp.logical_not(lo))
    blk = lane & (HEAD_DIM - 1)
    for h in range(2):
        _init_state(refs[h])

    def scores(n):
        kb = _k_block(k_ref, n)
        onehot = jnp.broadcast_to(jnp.where(blk == n, 1.0, 0.0).astype(BF16), kb.shape)
        return [_qk_t(jnp.where(mine[h], kb, onehot), qs[h]) for h in range(2)]

    def consume(n, s_list, mask):
        state = [_load_state(refs[h]) for h in range(2)]
        vb = vt_ref[n]
        for h in range(2):
            s_t = s_list[h] if mask is None else jnp.where(mask, s_list[h], NEG)
            state[h] = _flash_step(s_t, vb[h * HEAD_DIM:(h + 1) * HEAD_DIM, :], state[h])
        for h in range(2):
            _store_state(refs[h], state[h])

    _sweep(0, KPQ * i, scores, consume, s_a, s_b, (_causal_t(0), _causal_t(1)))
    for h in range(2):
        _, l, acc = _load_state(refs[h])
        o_ref[h * HEAD_DIM:(h + 1) * HEAD_DIM, :] = (acc / l).astype(o_ref.dtype)


def _moba_attention(qt, bias, k, vt, qrow, kcol, vrow):
    s = k.shape[0]
    n_pairs = 4
    return pl.pallas_call(
        _moba_kernel,
        grid=(n_pairs, s // TQ),
        in_specs=[pl.BlockSpec((LANES, TQ), lambda hp, i: (qrow + hp, i)),
                  pl.BlockSpec((LANES, TQ), lambda hp, i: (hp, i)),
                  pl.BlockSpec((s, LANES), lambda hp, i: (0, kcol + hp)),
                  pl.BlockSpec((s // TK, LANES, TK), lambda hp, i: (0, vrow + hp, 0))],
        out_specs=pl.BlockSpec((LANES, TQ), lambda hp, i: (hp, i)),
        out_shape=jax.ShapeDtypeStruct((n_pairs * LANES, s), BF16),
        scratch_shapes=_score_scratch(2) + _flash_scratch(2, HEAD_DIM),
        compiler_params=_cparams(2),
    )(qt, bias, k, vt)


def _diff_kernel(lam_init, qt_ref, k_ref, vt_ref, lq1_ref, lk1_ref, lq2_ref, lk2_ref, g_ref, o_ref,
                 s_a, s_b, *scr):
    i = pl.program_id(1)
    refs = (scr[0:3], scr[3:6])
    q_t = qt_ref[...]
    zero = jnp.zeros((HEAD_DIM, TQ), BF16)
    qs = (_stack_rows(q_t[:HEAD_DIM], zero), _stack_rows(zero, q_t[HEAD_DIM:]))
    for h in range(2):
        _init_state(refs[h])

    def scores(n):
        kb = _k_block(k_ref, n)
        return [_qk_t(kb, qs[h]) for h in range(2)]

    def consume(n, s_list, mask):
        state = [_load_state(refs[h]) for h in range(2)]
        vb = vt_ref[n]
        for h in range(2):
            s_t = s_list[h] if mask is None else jnp.where(mask, s_list[h], NEG)
            state[h] = _flash_step(s_t, vb, state[h])
        for h in range(2):
            _store_state(refs[h], state[h])

    _sweep(0, KPQ * i, scores, consume, s_a, s_b, (_causal_t(0), _causal_t(1)))
    lam = (jnp.exp(jnp.sum(lq1_ref[...] * lk1_ref[...], axis=1, keepdims=True))
           - jnp.exp(jnp.sum(lq2_ref[...] * lk2_ref[...], axis=1, keepdims=True)) + lam_init)
    _, l1, acc1 = _load_state(refs[0])
    _, l2, acc2 = _load_state(refs[1])
    o = acc1 / l1 - lam * (acc2 / l2)
    var = jnp.mean(o * o, axis=0, keepdims=True)
    o = (o * lax.rsqrt(var + RMS_EPS) * g_ref[...]) * (1.0 - lam_init)
    o_ref[...] = o.astype(o_ref.dtype)


def _diff_attention(qt, k, vt, lq1, lk1, lq2, lk2, g_sub, lam_init, qrow, kcol, vrow):
    s = k.shape[0]
    n_heads = 8
    vec = lambda n: pl.BlockSpec((1, n), lambda h, i: (0, 0))
    return pl.pallas_call(
        functools.partial(_diff_kernel, lam_init),
        grid=(n_heads, s // TQ),
        in_specs=[pl.BlockSpec((LANES, TQ), lambda h, i: (qrow + h, i)),
                  pl.BlockSpec((s, LANES), lambda h, i: (0, kcol + h)),
                  pl.BlockSpec((s // TK, LANES, TK), lambda h, i: (0, vrow + h, 0)),
                  vec(HEAD_DIM), vec(HEAD_DIM), vec(HEAD_DIM), vec(HEAD_DIM),
                  pl.BlockSpec((LANES, 1), lambda h, i: (0, 0))],
        out_specs=pl.BlockSpec((LANES, TQ), lambda h, i: (h, i)),
        out_shape=jax.ShapeDtypeStruct((n_heads * LANES, s), BF16),
        scratch_shapes=_score_scratch(2) + _flash_scratch(2, LANES),
        compiler_params=_cparams(2),
    )(qt, k, vt, lq1, lk1, lq2, lk2, g_sub)


def _outproj_kernel(oa_ref, ob_ref, oc_ref, wt_ref, x_ref, gt_ref, o_ref):
    wa = oa_ref.shape[0]
    wb = ob_ref.shape[0]
    y_t = (jnp.dot(wt_ref[:, 0:wa], oa_ref[...], preferred_element_type=F32)
           + jnp.dot(wt_ref[:, wa:wa + wb], ob_ref[...], preferred_element_type=F32)
           + jnp.dot(wt_ref[:, wa + wb:], oc_ref[...], preferred_element_type=F32))
    o_ref[...] = x_ref[...] + gt_ref[...] * y_t.T


def _out_projection(oa_t, ob_t, oc_t, wt_bf16, x, gt):
    s, d = x.shape
    tm = 256
    return pl.pallas_call(
        _outproj_kernel,
        grid=(s // tm,),
        in_specs=[pl.BlockSpec((oa_t.shape[0], tm), lambda i: (0, i)),
                  pl.BlockSpec((ob_t.shape[0], tm), lambda i: (0, i)),
                  pl.BlockSpec((oc_t.shape[0], tm), lambda i: (0, i)),
                  pl.BlockSpec(wt_bf16.shape, lambda i: (0, 0)),
                  pl.BlockSpec((tm, d), lambda i: (i, 0)),
                  pl.BlockSpec((1, d), lambda i: (0, 0))],
        out_specs=pl.BlockSpec((tm, d), lambda i: (i, 0)),
        out_shape=jax.ShapeDtypeStruct((s, d), F32),
        compiler_params=_cparams(1),
    )(oa_t, ob_t, oc_t, wt_bf16, x, gt)


def _route_kernel(x_ref, g_ref, sc_ref, sh_ref, wr_ref, br_ref, tri_ref,
                  h_ref, route_ref, count_ref, base_scr):
    i = pl.program_id(0)
    tm = x_ref.shape[0]

    @pl.when(i == 0)
    def _():
        base_scr[...] = jnp.zeros_like(base_scr)

    h = _modulated_norm(x_ref[...], g_ref[...], sc_ref[...], sh_ref[...])
    h_ref[...] = h.astype(BF16)
    logits = _dot3(wr_ref[...], h, NT_DIMS)
    s = _sigmoid(logits)
    sg = s + br_ref[...]
    rows = [sg[r:r + 1, :] for r in range(N_EXPERTS)]

    def top2_sum(vals):
        best = None
        for a in range(len(vals)):
            for b in range(a + 1, len(vals)):
                pair = vals[a] + vals[b]
                best = pair if best is None else jnp.maximum(best, pair)
        return best

    score = [top2_sum(rows[g * EXPERTS_PER_GROUP:(g + 1) * EXPERTS_PER_GROUP]) for g in range(N_GROUPS)]
    best, grp = score[0], jnp.zeros((1, tm), I32)
    for g in range(1, N_GROUPS):
        better = score[g] > best
        grp = jnp.where(better, g, grp)
        best = jnp.where(better, score[g], best)
    cand = []
    for j in range(EXPERTS_PER_GROUP):
        c = rows[j]
        for g in range(1, N_GROUPS):
            c = jnp.where(grp == g, rows[g * EXPERTS_PER_GROUP + j], c)
        cand.append(c)
    b0, l0 = cand[0], jnp.zeros((1, tm), I32)
    for j in range(1, EXPERTS_PER_GROUP):
        better = cand[j] > b0
        l0 = jnp.where(better, j, l0)
        b0 = jnp.where(better, cand[j], b0)
    b1, l1 = jnp.full((1, tm), -jnp.inf, F32), jnp.zeros((1, tm), I32)
    for j in range(EXPERTS_PER_GROUP):
        better = (l0 != j) & (cand[j] > b1)
        l1 = jnp.where(better, j, l1)
        b1 = jnp.where(better, cand[j], b1)
    e0 = grp * EXPERTS_PER_GROUP + l0
    e1 = grp * EXPERTS_PER_GROUP + l1
    erow = lax.broadcasted_iota(I32, (N_EXPERTS, tm), 0)
    sel0 = erow == e0
    sel1 = erow == e1
    s0 = jnp.sum(jnp.where(sel0, s, 0.0), axis=0, keepdims=True)
    s1 = jnp.sum(jnp.where(sel1, s, 0.0), axis=0, keepdims=True)
    tot = s0 + s1
    chosen = jnp.where(sel0 | sel1, 1.0, 0.0).astype(BF16)
    csum = jnp.dot(chosen, tri_ref[...], preferred_element_type=F32)
    pos = base_scr[...] + csum - 1.0
    r0 = jnp.sum(jnp.where(sel0, pos, 0.0), axis=0, keepdims=True)
    r1 = jnp.sum(jnp.where(sel1, pos, 0.0), axis=0, keepdims=True)
    base = base_scr[...] + csum[:, tm - 1:tm]
    base_scr[...] = base
    count_ref[...] = jnp.broadcast_to(base, count_ref.shape)
    zero = jnp.zeros((1, tm), F32)
    for r, val in enumerate((e0.astype(F32), e1.astype(F32), r0, r1, s0 / tot, s1 / tot, zero, zero)):
        route_ref[r:r + 1, :] = val


def _norm_and_route(x, g, sc, sh, w_router, b_router):
    s, d = x.shape
    tm = 512
    tri = jnp.asarray(np.triu(np.ones((tm, tm), np.float32)), BF16)
    row = lambda i: (0, 0)
    return pl.pallas_call(
        _route_kernel,
        grid=(s // tm,),
        in_specs=[pl.BlockSpec((tm, d), lambda i: (i, 0)),
                  pl.BlockSpec((1, d), row), pl.BlockSpec((1, d), row), pl.BlockSpec((1, d), row),
                  pl.BlockSpec((N_EXPERTS, d), row), pl.BlockSpec((N_EXPERTS, 1), row),
                  pl.BlockSpec((tm, tm), row)],
        out_specs=[pl.BlockSpec((tm, d), lambda i: (i, 0)),
                   pl.BlockSpec((8, tm), lambda i: (0, i)),
                   pl.BlockSpec((N_EXPERTS, LANES), row)],
        out_shape=[jax.ShapeDtypeStruct((s, d), BF16),
                   jax.ShapeDtypeStruct((8, s), F32),
                   jax.ShapeDtypeStruct((N_EXPERTS, LANES), F32)],
        scratch_shapes=[pltpu.VMEM((N_EXPERTS, 1), F32)],
        compiler_params=_cparams(1),
    )(x, g, sc, sh, w_router.T, b_router.reshape(N_EXPERTS, 1), tri)


FFN_ROWS = 512


def _ffn_kernel(te_ref, nu_ref, xs_ref, wg_ref, wu_ref, wd_ref, y_ref):
    j = pl.program_id(0)

    @pl.when(j < nu_ref[0])
    def _():
        x = xs_ref[...]
        a = jnp.dot(x, wg_ref[0], preferred_element_type=F32)
        b = jnp.dot(x, wu_ref[0], preferred_element_type=F32)
        act = (a * _sigmoid(a) * b).astype(BF16)
        y_ref[...] = jnp.dot(act, wd_ref[0], preferred_element_type=F32).astype(y_ref.dtype)

    @pl.when(j >= nu_ref[0])
    def _():
        y_ref[...] = jnp.zeros_like(y_ref)


def _expert_ffn(tile_expert, n_used, xs, wg, wu, wd):
    r, d = xs.shape
    de = wg.shape[2]
    grid_spec = pltpu.PrefetchScalarGridSpec(
        num_scalar_prefetch=2,
        grid=(r // FFN_ROWS,),
        in_specs=[pl.BlockSpec((FFN_ROWS, d), lambda j, te, nu: (j, 0)),
                  pl.BlockSpec((1, d, de), lambda j, te, nu: (te[j], 0, 0)),
                  pl.BlockSpec((1, d, de), lambda j, te, nu: (te[j], 0, 0)),
                  pl.BlockSpec((1, de, d), lambda j, te, nu: (te[j], 0, 0))],
        out_specs=pl.BlockSpec((FFN_ROWS, d), lambda j, te, nu: (j, 0)),
    )
    return pl.pallas_call(
        _ffn_kernel,
        grid_spec=grid_spec,
        out_shape=jax.ShapeDtypeStruct((r, d), BF16),
        compiler_params=_cparams(1),
    )(tile_expert, n_used, xs, wg, wu, wd)


def _combine_kernel(final, x_ref, y0_ref, y1_ref, w0_ref, w1_ref, gt_ref, gf_ref, o_ref):
    x = x_ref[...] + gt_ref[...] * (w0_ref[...] * y0_ref[...].astype(F32)
                                    + w1_ref[...] * y1_ref[...].astype(F32))
    if final:
        var = jnp.mean(x * x, axis=-1, keepdims=True)
        x = x * lax.rsqrt(var + RMS_EPS) * gf_ref[...]
    o_ref[...] = x


def _combine(x, y0, y1, w0, w1, gt, g_final, final):
    s, d = x.shape
    tm = 256
    big = pl.BlockSpec((tm, d), lambda i: (i, 0))
    col = pl.BlockSpec((tm, 1), lambda i: (i, 0))
    vec = pl.BlockSpec((1, d), lambda i: (0, 0))
    return pl.pallas_call(
        functools.partial(_combine_kernel, final),
        grid=(s // tm,),
        in_specs=[big, big, big, col, col, vec, vec],
        out_specs=big,
        out_shape=jax.ShapeDtypeStruct((s, d), F32),
        compiler_params=_cparams(1),
    )(x, y0, y1, w0, w1, gt, g_final)


def _rope_tables(positions):
    half = HEAD_DIM // 2
    inv = ROPE_THETA ** (-jnp.arange(0, HEAD_DIM, 2, dtype=F32) / HEAD_DIM)
    ang = positions.astype(F32)[:, None] * inv
    cos, sin = jnp.cos(ang), jnp.sin(ang)
    cos_t = jnp.tile(cos, (1, LANES // half))
    sin_t = jnp.tile(jnp.concatenate([-sin, sin], axis=1), (1, LANES // HEAD_DIM))
    return cos_t, sin_t


def _block_means(ksum):
    nblk, width = ksum.shape
    km = (ksum / MOBA_BLOCK).reshape(nblk, width // HEAD_DIM, HEAD_DIM)
    km = jnp.pad(km, ((0, HEAD_DIM - nblk), (0, 0), (0, 0)))
    return jnp.transpose(km, (1, 0, 2))


def _moe(x1, h2, route, counts, wg, wu, wd, gt, g_final, final):
    s, d = x1.shape
    e0, e1 = route[0].astype(I32), route[1].astype(I32)
    r0, r1 = route[2].astype(I32), route[3].astype(I32)
    cnt = counts[:, 0].astype(I32)
    cnt_pad = ((cnt + FFN_ROWS - 1) // FFN_ROWS) * FFN_ROWS
    off = jnp.concatenate([jnp.zeros((1,), I32), jnp.cumsum(cnt_pad).astype(I32)])
    d0 = off[e0] + r0
    d1 = off[e1] + r1
    n_rows = 2 * s + N_EXPERTS * FFN_ROWS
    tok = jnp.arange(s, dtype=I32)
    src = jnp.zeros((n_rows,), I32).at[d0].set(tok).at[d1].set(tok)
    n_tiles = n_rows // FFN_ROWS
    tile_expert = jnp.searchsorted(off[1:], jnp.arange(n_tiles, dtype=I32) * FFN_ROWS, side="right")
    tile_expert = jnp.minimum(tile_expert, N_EXPERTS - 1).astype(I32)
    n_used = (off[N_EXPERTS] // FFN_ROWS).reshape(1).astype(I32)
    y = _expert_ffn(tile_expert, n_used, h2[src], wg, wu, wd)
    return _combine(x1, y[d0], y[d1], route[4].reshape(s, 1), route[5].reshape(s, 1), gt, g_final, final)


def kernel(x, c, positions, w_ada, b_ada, g_attn, g_mlp, w_in, w_out, lam_q1, lam_k1, lam_q2, lam_k2,
           g_subln, w_gate, w_up, w_down, w_router, b_router, g_final):
    batch, s, d = x.shape
    assert batch == 1 and s % 512 == 0 and s // MOBA_BLOCK <= HEAD_DIM
    assert w_in.shape[2] == 3 * N_KIND_TILES * PROJ_TN
    depth = w_ada.shape[0]
    xs = x.reshape(s, d)
    cos_t, sin_t = _rope_tables(positions[0])
    mod = _modulation(c, w_ada, b_ada)
    tab = _dilated_bias_table()
    gf = g_final.reshape(1, d)
    row = lambda v: v.reshape(1, -1)
    for l in range(depth):
        lam_init = 0.8 - 0.6 * math.exp(-0.3 * l)
        sh_a, sc_a, gt_a, sh_m, sc_m, gt_m = [mod[l, :, k * d:(k + 1) * d] for k in range(6)]
        k, qt, qbt, vt, ksum = _in_projection(xs, row(g_attn[l]), sc_a, sh_a,
                                              _layer_to_bf16(w_in, l, PROJ_TN), cos_t, sin_t)
        o_a = _dilated_attention(qt, k, vt, tab, 0, 0, 0)
        bias = _moba_gate(qbt, _block_means(ksum.reshape(s // MOBA_BLOCK, -1)))
        o_b = _moba_attention(qt, bias, k, vt, 4, 4, 4)
        o_c = _diff_attention(qt, k, vt, row(lam_q1[l]), row(lam_k1[l]), row(lam_q2[l]), row(lam_k2[l]),
                              g_subln[l].reshape(-1, 1), lam_init, 8, 8, 8)
        x1 = _out_projection(o_a, o_b, o_c, w_out[l].T.astype(BF16), xs, gt_a)
        h2, route, counts = _norm_and_route(x1, row(g_mlp[l]), sc_m, sh_m, w_router, b_router)
        experts = lambda w: _layer_to_bf16(w, l).reshape(w.shape[1:])
        xs = _moe(x1, h2, route, counts, experts(w_gate), experts(w_up), experts(w_down),
                  gt_m, gf, l == depth - 1)
    return xs.reshape(batch, s, d)
```

```python
import functools
import math

import numpy as np
import jax
import jax.numpy as jnp
from jax import lax
from jax.experimental import pallas as pl
from jax.experimental.pallas import tpu as pltpu

F32 = jnp.float32
BF16 = jnp.bfloat16
I32 = jnp.int32

HEAD_DIM = 64
LANES = 128
A_PATTERNS = ((128, 1), (512, 4), (2048, 16))
MOBA_BLOCK = 256
MOBA_TOPK = 3
N_EXPERTS = 16
N_GROUPS = 4
EXPERTS_PER_GROUP = N_EXPERTS // N_GROUPS
ROPE_THETA = 10000.0
RMS_EPS = 1e-6
NEG = -1e30
VMEM_LIMIT = 56 * 1024 * 1024

TQ = 512
TK = 256
KPQ = TQ // TK
assert KPQ == 2
PROJ_TM = 512
PROJ_TN = 512
NT_DIMS = (((1,), (1,)), ((), ()))
Q_SCALE = HEAD_DIM ** -0.5 * math.log2(math.e)


def _cparams(n_axes):
    return pltpu.CompilerParams(dimension_semantics=("arbitrary",) * n_axes,
                                vmem_limit_bytes=VMEM_LIMIT)


def _split_bf16(x):
    hi = x.astype(BF16)
    lo = (x - hi.astype(F32)).astype(BF16)
    return hi, lo


def _dot3(a, b, dims=(((1,), (0,)), ((), ()))):
    a_hi, a_lo = _split_bf16(a)
    b_hi, b_lo = _split_bf16(b)
    dg = functools.partial(lax.dot_general, dimension_numbers=dims, preferred_element_type=F32)
    return dg(a_hi, b_hi) + dg(a_hi, b_lo) + dg(a_lo, b_hi)


def _sigmoid(x):
    return 1.0 / (1.0 + jnp.exp(-x))


CAST_STREAMS = 4
CAST_BLOCK_ELEMS = 512 * 1024
CAST_COLS = 1024


def _cast_kernel(*refs):
    o_ref = refs[-1]
    col_tile = o_ref.shape[2]
    for k, x_ref in enumerate(refs[:-1]):
        rows = x_ref.shape[1]
        x = x_ref[0].astype(o_ref.dtype)
        for t in range(o_ref.shape[0]):
            o_ref[t, k * rows:(k + 1) * rows, :] = x[:, t * col_tile:(t + 1) * col_tile]


def _layer_to_bf16(w, l, col_tile=None):
    cols = w.shape[-1]
    col_tile = col_tile or cols
    w3 = w.reshape(w.shape[0], -1, cols)
    rows = w3.shape[1]
    tc = max(min(cols, CAST_COLS), col_tile)
    sr = CAST_BLOCK_ELEMS // tc
    tr = CAST_STREAMS * sr
    stream = lambda k: pl.BlockSpec((1, sr, tc), lambda i, j: (l, CAST_STREAMS * i + k, j))
    return pl.pallas_call(
        _cast_kernel,
        grid=(rows // tr, cols // tc),
        in_specs=[stream(k) for k in range(CAST_STREAMS)],
        out_specs=pl.BlockSpec((tc // col_tile, tr, col_tile), lambda i, j: (j, i, 0)),
        out_shape=jax.ShapeDtypeStruct((cols // col_tile, rows, col_tile), BF16),
        compiler_params=_cparams(2),
    )(*([w3] * CAST_STREAMS))


def _mod_kernel(c_ref, w_ref, b_ref, o_ref):
    c = c_ref[...]
    sc = c * _sigmoid(c)
    d = c.shape[0]
    acc = b_ref[0]
    for r in range(0, d, 256):
        acc = acc + jnp.sum(w_ref[0, r:r + 256, :] * sc[r:r + 256, :], axis=0, keepdims=True)
    o_ref[0] = acc


def _modulation(c, w_ada, b_ada):
    depth, d, n = w_ada.shape
    tn = 512
    return pl.pallas_call(
        _mod_kernel,
        grid=(depth, n // tn),
        in_specs=[pl.BlockSpec((d, 1), lambda l, j: (0, 0)),
                  pl.BlockSpec((1, d, tn), lambda l, j: (l, 0, j)),
                  pl.BlockSpec((1, 1, tn), lambda l, j: (l, 0, j))],
        out_specs=pl.BlockSpec((1, 1, tn), lambda l, j: (l, 0, j)),
        out_shape=jax.ShapeDtypeStruct((depth, 1, n), F32),
        compiler_params=_cparams(2),
    )(c.reshape(d, 1), w_ada, b_ada.reshape(depth, 1, n))


def _modulated_norm(x, g, sc, sh):
    var = jnp.mean(x * x, axis=-1, keepdims=True)
    return (x * lax.rsqrt(var + RMS_EPS) * g) * (1.0 + sc) + sh


def _rope_tile(acc, cos, sin):
    lane = lax.broadcasted_iota(I32, (1, LANES), 1)
    first = (lane & (HEAD_DIM - 1)) < (HEAD_DIM // 2)
    outs = []
    for c in range(acc.shape[1] // LANES):
        xc = acc[:, c * LANES:(c + 1) * LANES]
        rot = jnp.where(first, pltpu.roll(xc, LANES - HEAD_DIM // 2, 1), pltpu.roll(xc, HEAD_DIM // 2, 1))
        outs.append(xc * cos + rot * sin)
    return jnp.concatenate(outs, axis=1)


N_KIND_TILES = 4
COLUMN_TILE_ORDER = (1, 4, 8, 9, 0, 3, 6, 7, 2, 5, 10, 11)
KB_TILE = 1
QB_TILE = N_KIND_TILES + 1


def _inproj_kernel(x_ref, g_ref, sc_ref, sh_ref, w_ref, cos_ref, sin_ref,
                   k_ref, qt_ref, qbt_ref, vt_ref, ksum_ref, h_scr):
    tm = x_ref.shape[0]
    tn = w_ref.shape[2]
    h_scr[...] = _modulated_norm(x_ref[...], g_ref[...], sc_ref[...], sh_ref[...]).astype(BF16)
    for pos, tile in enumerate(COLUMN_TILE_ORDER):
        acc = jnp.dot(h_scr[...], w_ref[tile], preferred_element_type=F32)
        kind, t = divmod(pos, N_KIND_TILES)
        cols = slice(t * tn, (t + 1) * tn)
        if kind == 0:
            r = _rope_tile(acc, cos_ref[...], sin_ref[...])
            k_ref[:, cols] = r.astype(BF16)
            if pos == KB_TILE:
                ksum_ref[0] = jnp.sum(r.reshape(tm // MOBA_BLOCK, MOBA_BLOCK, tn), axis=1)
        elif kind == 1:
            rt = _rope_tile(acc, cos_ref[...], sin_ref[...]).T
            qt_ref[cols, :] = (rt * Q_SCALE).astype(BF16)
            if pos == QB_TILE:
                qbt_ref[...] = rt
        else:
            vt = acc.T.astype(BF16)
            for b in range(vt_ref.shape[0]):
                vt_ref[b, cols, :] = vt[:, b * TK:(b + 1) * TK]


def _in_projection(x, g, sc, sh, w_bf16, cos_t, sin_t):
    s, d = x.shape
    tm, tn = PROJ_TM, w_bf16.shape[2]
    nb = tm // MOBA_BLOCK
    width = N_KIND_TILES * tn
    row = lambda i: (0, 0)
    return pl.pallas_call(
        _inproj_kernel,
        grid=(s // tm,),
        in_specs=[pl.BlockSpec((tm, d), lambda i: (i, 0)),
                  pl.BlockSpec((1, d), row), pl.BlockSpec((1, d), row), pl.BlockSpec((1, d), row),
                  pl.BlockSpec(w_bf16.shape, lambda i: (0, 0, 0),
                               pipeline_mode=pl.Buffered(1)),
                  pl.BlockSpec((tm, LANES), lambda i: (i, 0)),
                  pl.BlockSpec((tm, LANES), lambda i: (i, 0))],
        out_specs=[pl.BlockSpec((tm, width), lambda i: (i, 0)),
                   pl.BlockSpec((width, tm), lambda i: (0, i)),
                   pl.BlockSpec((tn, tm), lambda i: (0, i)),
                   pl.BlockSpec((tm // TK, width, TK), lambda i: (i, 0, 0)),
                   pl.BlockSpec((1, nb, tn), lambda i: (i, 0, 0))],
        out_shape=[jax.ShapeDtypeStruct((s, width), BF16),
                   jax.ShapeDtypeStruct((width, s), BF16),
                   jax.ShapeDtypeStruct((tn, s), F32),
                   jax.ShapeDtypeStruct((s // TK, width, TK), BF16),
                   jax.ShapeDtypeStruct((s // tm, nb, tn), F32)],
        scratch_shapes=[pltpu.VMEM((tm, d), BF16)],
        compiler_params=_cparams(1),
    )(x, g, sc, sh, w_bf16, cos_t, sin_t)


def _flash_step(s_t, v_t, state):
    m_prev, l_prev, acc_prev = state
    m_new = jnp.maximum(m_prev, jnp.max(s_t, axis=0, keepdims=True))
    alpha = jnp.exp2(m_prev - m_new)
    p = jnp.exp2(s_t - m_new)
    l_new = alpha * l_prev + jnp.sum(p, axis=0, keepdims=True)
    acc_new = alpha * acc_prev + jnp.dot(v_t, p.astype(BF16), preferred_element_type=F32)
    return m_new, l_new, acc_new


def _sweep(first, n_loop, scores, consume, s_a, s_b, tail_masks):
    def park(buf, vals):
        for h, val in enumerate(vals):
            buf[h] = val

    def fetch(buf):
        return [buf[h] for h in range(buf.shape[0])]

    park(s_a, scores(first))

    def pair(j, carry):
        a = first + 2 * j
        park(s_b, scores(a + 1))
        consume(a, fetch(s_a), None)
        park(s_a, scores(a + 2))
        consume(a + 1, fetch(s_b), None)
        return carry

    lax.fori_loop(0, n_loop // 2, pair, 0)
    t0 = first + n_loop
    park(s_b, scores(t0 + 1))
    consume(t0, fetch(s_a), tail_masks[0])
    consume(t0 + 1, fetch(s_b), tail_masks[1])


def _score_scratch(n_streams):
    return [pltpu.VMEM((n_streams, TK, TQ), F32), pltpu.VMEM((n_streams, TK, TQ), F32)]


def _qk_t(k, q_t):
    return jnp.dot(k, q_t, preferred_element_type=F32)


def _k_block(k_ref, n):
    return k_ref[pl.ds(pl.multiple_of(n * TK, TK), TK), :]


def _init_state(refs):
    m_ref, l_ref, acc_ref = refs
    m_ref[...] = jnp.full(m_ref.shape, NEG, F32)
    l_ref[...] = jnp.zeros(l_ref.shape, F32)
    acc_ref[...] = jnp.zeros(acc_ref.shape, F32)


def _load_state(refs):
    return tuple(r[...] for r in refs)


def _store_state(refs, state):
    for r, val in zip(refs, state):
        r[...] = val


def _flash_scratch(n_streams, dv):
    out = []
    for _ in range(n_streams):
        out += [pltpu.VMEM((1, TQ), F32), pltpu.VMEM((1, TQ), F32), pltpu.VMEM((dv, TQ), F32)]
    return out


def _causal_t(d):
    r = lax.broadcasted_iota(I32, (TK, TQ), 0) + d * TK
    return r <= lax.broadcasted_iota(I32, (TK, TQ), 1)


def _stack_rows(top, bottom):
    return jnp.concatenate([top, bottom], axis=0)


def _dilated_bias_table():
    n_off = A_PATTERNS[-1][0] // TK + KPQ
    r = np.arange(TK)[:, None]
    c = np.arange(TQ)[None, :]
    tabs = []
    for t in range(n_off):
        delta = (t - (KPQ - 1)) * TK + c - r
        mult = np.zeros_like(delta)
        for window, dil in A_PATTERNS:
            mult += ((delta >= 0) & (delta % dil == 0) & (delta <= window)).astype(delta.dtype)
        tabs.append(np.where(mult > 0, np.log2(np.maximum(mult, 1).astype(np.float64)), NEG))
    return jnp.asarray(np.stack(tabs), F32)


def _dilated_kernel(qt_ref, k_ref, vt_ref, tab_ref, o_ref, s_a, s_b, *scr):
    i = pl.program_id(1)
    refs = (scr[0:3], scr[3:6])
    q_t = qt_ref[...]
    zero = jnp.zeros((HEAD_DIM, TQ), BF16)
    qs = (_stack_rows(q_t[:HEAD_DIM], zero), _stack_rows(zero, q_t[HEAD_DIM:]))
    n_off = tab_ref.shape[0]
    last = KPQ * i + (KPQ - 1)
    for h in range(2):
        _init_state(refs[h])

    def scores(n):
        kb = _k_block(k_ref, n)
        bias = tab_ref[last - n]
        return [_qk_t(kb, qs[h]) + bias for h in range(2)]

    def consume(n, s_list, mask):
        state = [_load_state(refs[h]) for h in range(2)]
        vb = vt_ref[n]
        for h in range(2):
            state[h] = _flash_step(s_list[h], vb[h * HEAD_DIM:(h + 1) * HEAD_DIM, :], state[h])
        for h in range(2):
            _store_state(refs[h], state[h])

    first = jnp.maximum(last - (n_off - 1), 0)
    _sweep(first, last - 1 - first, scores, consume, s_a, s_b, (None, None))
    for h in range(2):
        _, l, acc = _load_state(refs[h])
        o_ref[h * HEAD_DIM:(h + 1) * HEAD_DIM, :] = (acc / l).astype(o_ref.dtype)


def _dilated_attention(qt, k, vt, tab, qrow, kcol, vrow):
    s = k.shape[0]
    n_pairs = 4
    return pl.pallas_call(
        _dilated_kernel,
        grid=(n_pairs, s // TQ),
        in_specs=[pl.BlockSpec((LANES, TQ), lambda hp, i: (qrow + hp, i)),
                  pl.BlockSpec((s, LANES), lambda hp, i: (0, kcol + hp)),
                  pl.BlockSpec((s // TK, LANES, TK), lambda hp, i: (0, vrow + hp, 0)),
                  pl.BlockSpec(tab.shape, lambda hp, i: (0, 0, 0))],
        out_specs=pl.BlockSpec((LANES, TQ), lambda hp, i: (hp, i)),
        out_shape=jax.ShapeDtypeStruct((n_pairs * LANES, s), BF16),
        scratch_shapes=_score_scratch(2) + _flash_scratch(2, HEAD_DIM),
        compiler_params=_cparams(2),
    )(qt, k, vt, tab)


def _moba_gate_kernel(qt_ref, km_ref, bias_ref):
    i = pl.program_id(1)
    tg = qt_ref.shape[1]
    gate = _dot3(km_ref[0], qt_ref[...])
    nblk = gate.shape[0]
    blk = lax.broadcasted_iota(I32, (nblk, tg), 0)
    blk_f = blk.astype(F32)
    tok = i * tg + lax.broadcasted_iota(I32, (nblk, tg), 1)
    own = jnp.right_shift(tok, int(math.log2(MOBA_BLOCK)))
    past = blk < own
    sel = blk == own
    for _ in range(MOBA_TOPK):
        cand = past & jnp.logical_not(sel)
        g = jnp.where(cand, gate, -jnp.inf)
        mx = jnp.max(g, axis=0, keepdims=True)
        first = jnp.min(jnp.where(cand & (g == mx), blk_f, 2.0 * nblk), axis=0, keepdims=True)
        sel = sel | (blk_f == first)
    bias_ref[...] = jnp.where(sel, 0.0, NEG).astype(BF16)


def _moba_gate(qbt, kmean):
    width, s = qbt.shape
    n_heads = width // HEAD_DIM
    tg = 512
    return pl.pallas_call(
        _moba_gate_kernel,
        grid=(n_heads, s // tg),
        in_specs=[pl.BlockSpec((HEAD_DIM, tg), lambda h, i: (h, i)),
                  pl.BlockSpec((1, HEAD_DIM, HEAD_DIM), lambda h, i: (h, 0, 0))],
        out_specs=pl.BlockSpec((HEAD_DIM, tg), lambda h, i: (h, i)),
        out_shape=jax.ShapeDtypeStruct((width, s), BF16),
        compiler_params=_cparams(2),
    )(qbt, kmean)


def _moba_kernel(qt_ref, bias_ref, k_ref, vt_ref, o_ref, s_a, s_b, *scr):
    i = pl.program_id(1)
    refs = (scr[0:3], scr[3:6])
    q_t = qt_ref[...]
    b_t = bias_ref[...]
    qs = (_stack_rows(q_t[:HEAD_DIM], b_t[:HEAD_DIM]), _stack_rows(b_t[HEAD_DIM:], q_t[HEAD_DIM:]))
    lane = lax.broadcasted_iota(I32, (1, LANES), 1)
    lo = lane < HEAD_DIM
    mine = (lo, jnp.logical_not(lo))
    blk = lane & (HEAD_DIM - 1)
    for h in range(2):
        _init_state(refs[h])

    def scores(n):
        kb = _k_block(k_ref, n)
        onehot = jnp.broadcast_to(jnp.where(blk == n, 1.0, 0.0).astype(BF16), kb.shape)
        return [_qk_t(jnp.where(mine[h], kb, onehot), qs[h]) for h in range(2)]

    def consume(n, s_list, mask):
        state = [_load_state(refs[h]) for h in range(2)]
        vb = vt_ref[n]
        for h in range(2):
            s_t = s_list[h] if mask is None else jnp.where(mask, s_list[h], NEG)
            state[h] = _flash_step(s_t, vb[h * HEAD_DIM:(h + 1) * HEAD_DIM, :], state[h])
        for h in range(2):
            _store_state(refs[h], state[h])

    _sweep(0, KPQ * i, scores, consume, s_a, s_b, (_causal_t(0), _causal_t(1)))
    for h in range(2):
        _, l, acc = _load_state(refs[h])
        o_ref[h * HEAD_DIM:(h + 1) * HEAD_DIM, :] = (acc / l).astype(o_ref.dtype)


def _moba_attention(qt, bias, k, vt, qrow, kcol, vrow):
    s = k.shape[0]
    n_pairs = 4
    return pl.pallas_call(
        _moba_kernel,
        grid=(n_pairs, s // TQ),
        in_specs=[pl.BlockSpec((LANES, TQ), lambda hp, i: (qrow + hp, i)),
                  pl.BlockSpec((LANES, TQ), lambda hp, i: (hp, i)),
                  pl.BlockSpec((s, LANES), lambda hp, i: (0, kcol + hp)),
                  pl.BlockSpec((s // TK, LANES, TK), lambda hp, i: (0, vrow + hp, 0))],
        out_specs=pl.BlockSpec((LANES, TQ), lambda hp, i: (hp, i)),
        out_shape=jax.ShapeDtypeStruct((n_pairs * LANES, s), BF16),
        scratch_shapes=_score_scratch(2) + _flash_scratch(2, HEAD_DIM),
        compiler_params=_cparams(2),
    )(qt, bias, k, vt)


def _diff_kernel(lam_init, qt_ref, k_ref, vt_ref, lq1_ref, lk1_ref, lq2_ref, lk2_ref, g_ref, o_ref,
                 s_a, s_b, *scr):
    i = pl.program_id(1)
    refs = (scr[0:3], scr[3:6])
    q_t = qt_ref[...]
    zero = jnp.zeros((HEAD_DIM, TQ), BF16)
    qs = (_stack_rows(q_t[:HEAD_DIM], zero), _stack_rows(zero, q_t[HEAD_DIM:]))
    for h in range(2):
        _init_state(refs[h])

    def scores(n):
        kb = _k_block(k_ref, n)
        return [_qk_t(kb, qs[h]) for h in range(2)]

    def consume(n, s_list, mask):
        state = [_load_state(refs[h]) for h in range(2)]
        vb = vt_ref[n]
        for h in range(2):
            s_t = s_list[h] if mask is None else jnp.where(mask, s_list[h], NEG)
            state[h] = _flash_step(s_t, vb, state[h])
        for h in range(2):
            _store_state(refs[h], state[h])

    _sweep(0, KPQ * i, scores, consume, s_a, s_b, (_causal_t(0), _causal_t(1)))
    lam = (jnp.exp(jnp.sum(lq1_ref[...] * lk1_ref[...], axis=1, keepdims=True))
           - jnp.exp(jnp.sum(lq2_ref[...] * lk2_ref[...], axis=1, keepdims=True)) + lam_init)
    _, l1, acc1 = _load_state(refs[0])
    _, l2, acc2 = _load_state(refs[1])
    o = acc1 / l1 - lam * (acc2 / l2)
    var = jnp.mean(o * o, axis=0, keepdims=True)
    o = (o * lax.rsqrt(var + RMS_EPS) * g_ref[...]) * (1.0 - lam_init)
    o_ref[...] = o.astype(o_ref.dtype)


def _diff_attention(qt, k, vt, lq1, lk1, lq2, lk2, g_sub, lam_init, qrow, kcol, vrow):
    s = k.shape[0]
    n_heads = 8
    vec = lambda n: pl.BlockSpec((1, n), lambda h, i: (0, 0))
    return pl.pallas_call(
        functools.partial(_diff_kernel, lam_init),
        grid=(n_heads, s // TQ),
        in_specs=[pl.BlockSpec((LANES, TQ), lambda h, i: (qrow + h, i)),
                  pl.BlockSpec((s, LANES), lambda h, i: (0, kcol + h)),
                  pl.BlockSpec((s // TK, LANES, TK), lambda h, i: (0, vrow + h, 0)),
                  vec(HEAD_DIM), vec(HEAD_DIM), vec(HEAD_DIM), vec(HEAD_DIM),
                  pl.BlockSpec((LANES, 1), lambda h, i: (0, 0))],
        out_specs=pl.BlockSpec((LANES, TQ), lambda h, i: (h, i)),
        out_shape=jax.ShapeDtypeStruct((n_heads * LANES, s), BF16),
        scratch_shapes=_score_scratch(2) + _flash_scratch(2, LANES),
        compiler_params=_cparams(2),
    )(qt, k, vt, lq1, lk1, lq2, lk2, g_sub)


def _outproj_kernel(oa_ref, ob_ref, oc_ref, wt_ref, x_ref, gt_ref, o_ref):
    wa = oa_ref.shape[0]
    wb = ob_ref.shape[0]
    y_t = (jnp.dot(wt_ref[:, 0:wa], oa_ref[...], preferred_element_type=F32)
           + jnp.dot(wt_ref[:, wa:wa + wb], ob_ref[...], preferred_element_type=F32)
           + jnp.dot(wt_ref[:, wa + wb:], oc_ref[...], preferred_element_type=F32))
    o_ref[...] = x_ref[...] + gt_ref[...] * y_t.T


def _out_projection(oa_t, ob_t, oc_t, wt_bf16, x, gt):
    s, d = x.shape
    tm = 256
    return pl.pallas_call(
        _outproj_kernel,
        grid=(s // tm,),
        in_specs=[pl.BlockSpec((oa_t.shape[0], tm), lambda i: (0, i)),
                  pl.BlockSpec((ob_t.shape[0], tm), lambda i: (0, i)),
                  pl.BlockSpec((oc_t.shape[0], tm), lambda i: (0, i)),
                  pl.BlockSpec(wt_bf16.shape, lambda i: (0, 0)),
                  pl.BlockSpec((tm, d), lambda i: (i, 0)),
                  pl.BlockSpec((1, d), lambda i: (0, 0))],
        out_specs=pl.BlockSpec((tm, d), lambda i: (i, 0)),
        out_shape=jax.ShapeDtypeStruct((s, d), F32),
        compiler_params=_cparams(1),
    )(oa_t, ob_t, oc_t, wt_bf16, x, gt)


def _route_kernel(x_ref, g_ref, sc_ref, sh_ref, wr_ref, br_ref, tri_ref,
                  h_ref, route_ref, count_ref, base_scr):
    i = pl.program_id(0)
    tm = x_ref.shape[0]

    @pl.when(i == 0)
    def _():
        base_scr[...] = jnp.zeros_like(base_scr)

    h = _modulated_norm(x_ref[...], g_ref[...], sc_ref[...], sh_ref[...])
    h_ref[...] = h.astype(BF16)
    logits = _dot3(wr_ref[...], h, NT_DIMS)
    s = _sigmoid(logits)
    sg = s + br_ref[...]
    rows = [sg[r:r + 1, :] for r in range(N_EXPERTS)]

    def top2_sum(vals):
        best = None
        for a in range(len(vals)):
            for b in range(a + 1, len(vals)):
                pair = vals[a] + vals[b]
                best = pair if best is None else jnp.maximum(best, pair)
        return best

    score = [top2_sum(rows[g * EXPERTS_PER_GROUP:(g + 1) * EXPERTS_PER_GROUP]) for g in range(N_GROUPS)]
    best, grp = score[0], jnp.zeros((1, tm), I32)
    for g in range(1, N_GROUPS):
        better = score[g] > best
        grp = jnp.where(better, g, grp)
        best = jnp.where(better, score[g], best)
    cand = []
    for j in range(EXPERTS_PER_GROUP):
        c = rows[j]
        for g in range(1, N_GROUPS):
            c = jnp.where(grp == g, rows[g * EXPERTS_PER_GROUP + j], c)
        cand.append(c)
    b0, l0 = cand[0], jnp.zeros((1, tm), I32)
    for j in range(1, EXPERTS_PER_GROUP):
        better = cand[j] > b0
        l0 = jnp.where(better, j, l0)
        b0 = jnp.where(better, cand[j], b0)
    b1, l1 = jnp.full((1, tm), -jnp.inf, F32), jnp.zeros((1, tm), I32)
    for j in range(EXPERTS_PER_GROUP):
        better = (l0 != j) & (cand[j] > b1)
        l1 = jnp.where(better, j, l1)
        b1 = jnp.where(better, cand[j], b1)
    e0 = grp * EXPERTS_PER_GROUP + l0
    e1 = grp * EXPERTS_PER_GROUP + l1
    erow = lax.broadcasted_iota(I32, (N_EXPERTS, tm), 0)
    sel0 = erow == e0
    sel1 = erow == e1
    s0 = jnp.sum(jnp.where(sel0, s, 0.0), axis=0, keepdims=True)
    s1 = jnp.sum(jnp.where(sel1, s, 0.0), axis=0, keepdims=True)
    tot = s0 + s1
    chosen = jnp.where(sel0 | sel1, 1.0, 0.0).astype(BF16)
    csum = jnp.dot(chosen, tri_ref[...], preferred_element_type=F32)
    pos = base_scr[...] + csum - 1.0
    r0 = jnp.sum(jnp.where(sel0, pos, 0.0), axis=0, keepdims=True)
    r1 = jnp.sum(jnp.where(sel1, pos, 0.0), axis=0, keepdims=True)
    base = base_scr[...] + csum[:, tm - 1:tm]
    base_scr[...] = base
    count_ref[...] = jnp.broadcast_to(base, count_ref.shape)
    zero = jnp.zeros((1, tm), F32)
    for r, val in enumerate((e0.astype(F32), e1.astype(F32), r0, r1, s0 / tot, s1 / tot, zero, zero)):
        route_ref[r:r + 1, :] = val


def _norm_and_route(x, g, sc, sh, w_router, b_router):
    s, d = x.shape
    tm = 512
    tri = jnp.asarray(np.triu(np.ones((tm, tm), np.float32)), BF16)
    row = lambda i: (0, 0)
    return pl.pallas_call(
        _route_kernel,
        grid=(s // tm,),
        in_specs=[pl.BlockSpec((tm, d), lambda i: (i, 0)),
                  pl.BlockSpec((1, d), row), pl.BlockSpec((1, d), row), pl.BlockSpec((1, d), row),
                  pl.BlockSpec((N_EXPERTS, d), row), pl.BlockSpec((N_EXPERTS, 1), row),
                  pl.BlockSpec((tm, tm), row)],
        out_specs=[pl.BlockSpec((tm, d), lambda i: (i, 0)),
                   pl.BlockSpec((8, tm), lambda i: (0, i)),
                   pl.BlockSpec((N_EXPERTS, LANES), row)],
        out_shape=[jax.ShapeDtypeStruct((s, d), BF16),
                   jax.ShapeDtypeStruct((8, s), F32),
                   jax.ShapeDtypeStruct((N_EXPERTS, LANES), F32)],
        scratch_shapes=[pltpu.VMEM((N_EXPERTS, 1), F32)],
        compiler_params=_cparams(1),
    )(x, g, sc, sh, w_router.T, b_router.reshape(N_EXPERTS, 1), tri)


FFN_ROWS = 512


def _ffn_kernel(te_ref, nu_ref, xs_ref, wg_ref, wu_ref, wd_ref, y_ref):
    j = pl.program_id(0)

    @pl.when(j < nu_ref[0])
    def _():
        x = xs_ref[...]
        a = jnp.dot(x, wg_ref[0], preferred_element_type=F32)
        b = jnp.dot(x, wu_ref[0], preferred_element_type=F32)
        act = (a * _sigmoid(a) * b).astype(BF16)
        y_ref[...] = jnp.dot(act, wd_ref[0], preferred_element_type=F32).astype(y_ref.dtype)

    @pl.when(j >= nu_ref[0])
    def _():
        y_ref[...] = jnp.zeros_like(y_ref)


def _expert_ffn(tile_expert, n_used, xs, wg, wu, wd):
    r, d = xs.shape
    de = wg.shape[2]
    grid_spec = pltpu.PrefetchScalarGridSpec(
        num_scalar_prefetch=2,
        grid=(r // FFN_ROWS,),
        in_specs=[pl.BlockSpec((FFN_ROWS, d), lambda j, te, nu: (j, 0)),
                  pl.BlockSpec((1, d, de), lambda j, te, nu: (te[j], 0, 0)),
                  pl.BlockSpec((1, d, de), lambda j, te, nu: (te[j], 0, 0)),
                  pl.BlockSpec((1, de, d), lambda j, te, nu: (te[j], 0, 0))],
        out_specs=pl.BlockSpec((FFN_ROWS, d), lambda j, te, nu: (j, 0)),
    )
    return pl.pallas_call(
        _ffn_kernel,
        grid_spec=grid_spec,
        out_shape=jax.ShapeDtypeStruct((r, d), BF16),
        compiler_params=_cparams(1),
    )(tile_expert, n_used, xs, wg, wu, wd)


def _combine_kernel(final, x_ref, y0_ref, y1_ref, w0_ref, w1_ref, gt_ref, gf_ref, o_ref):
    x = x_ref[...] + gt_ref[...] * (w0_ref[...] * y0_ref[...].astype(F32)
                                    + w1_ref[...] * y1_ref[...].astype(F32))
    if final:
        var = jnp.mean(x * x, axis=-1, keepdims=True)
        x = x * lax.rsqrt(var + RMS_EPS) * gf_ref[...]
    o_ref[...] = x


def _combine(x, y0, y1, w0, w1, gt, g_final, final):
    s, d = x.shape
    tm = 256
    big = pl.BlockSpec((tm, d), lambda i: (i, 0))
    col = pl.BlockSpec((tm, 1), lambda i: (i, 0))
    vec = pl.BlockSpec((1, d), lambda i: (0, 0))
    return pl.pallas_call(
        functools.partial(_combine_kernel, final),
        grid=(s // tm,),
        in_specs=[big, big, big, col, col, vec, vec],
        out_specs=big,
        out_shape=jax.ShapeDtypeStruct((s, d), F32),
        compiler_params=_cparams(1),
    )(x, y0, y1, w0, w1, gt, g_final)


def _rope_tables(positions):
    half = HEAD_DIM // 2
    inv = ROPE_THETA ** (-jnp.arange(0, HEAD_DIM, 2, dtype=F32) / HEAD_DIM)
    ang = positions.astype(F32)[:, None] * inv
    cos, sin = jnp.cos(ang), jnp.sin(ang)
    cos_t = jnp.tile(cos, (1, LANES // half))
    sin_t = jnp.tile(jnp.concatenate([-sin, sin], axis=1), (1, LANES // HEAD_DIM))
    return cos_t, sin_t


def _block_means(ksum):
    nblk, width = ksum.shape
    km = (ksum / MOBA_BLOCK).reshape(nblk, width // HEAD_DIM, HEAD_DIM)
    km = jnp.pad(km, ((0, HEAD_DIM - nblk), (0, 0), (0, 0)))
    return jnp.transpose(km, (1, 0, 2))


def _moe(x1, h2, route, counts, wg, wu, wd, gt, g_final, final):
    s, d = x1.shape
    e0, e1 = route[0].astype(I32), route[1].astype(I32)
    r0, r1 = route[2].astype(I32), route[3].astype(I32)
    cnt = counts[:, 0].astype(I32)
    cnt_pad = ((cnt + FFN_ROWS - 1) // FFN_ROWS) * FFN_ROWS
    off = jnp.concatenate([jnp.zeros((1,), I32), jnp.cumsum(cnt_pad).astype(I32)])
    d0 = off[e0] + r0
    d1 = off[e1] + r1
    n_rows = 2 * s + N_EXPERTS * FFN_ROWS
    tok = jnp.arange(s, dtype=I32)
    src = (jnp.arange(n_rows, dtype=I32) % s).at[d0].set(tok).at[d1].set(tok)
    n_tiles = n_rows // FFN_ROWS
    tile_expert = jnp.searchsorted(off[1:], jnp.arange(n_tiles, dtype=I32) * FFN_ROWS, side="right")
    tile_expert = jnp.minimum(tile_expert, N_EXPERTS - 1).astype(I32)
    n_used = (off[N_EXPERTS] // FFN_ROWS).reshape(1).astype(I32)
    y = _expert_ffn(tile_expert, n_used, h2[src], wg, wu, wd)
    return _combine(x1, y[d0], y[d1], route[4].reshape(s, 1), route[5].reshape(s, 1), gt, g_final, final)


def kernel(x, c, positions, w_ada, b_ada, g_attn, g_mlp, w_in, w_out, lam_q1, lam_k1, lam_q2, lam_k2,
           g_subln, w_gate, w_up, w_down, w_router, b_router, g_final):
    batch, s, d = x.shape
    assert batch == 1 and s % 512 == 0 and s // MOBA_BLOCK <= HEAD_DIM
    assert w_in.shape[2] == 3 * N_KIND_TILES * PROJ_TN
    depth = w_ada.shape[0]
    xs = x.reshape(s, d)
    cos_t, sin_t = _rope_tables(positions[0])
    mod = _modulation(c, w_ada, b_ada)
    tab = _dilated_bias_table()
    gf = g_final.reshape(1, d)
    row = lambda v: v.reshape(1, -1)
    for l in range(depth):
        lam_init = 0.8 - 0.6 * math.exp(-0.3 * l)
        sh_a, sc_a, gt_a, sh_m, sc_m, gt_m = [mod[l, :, k * d:(k + 1) * d] for k in range(6)]
        k, qt, qbt, vt, ksum = _in_projection(xs, row(g_attn[l]), sc_a, sh_a,
                                              _layer_to_bf16(w_in, l, PROJ_TN), cos_t, sin_t)
        o_a = _dilated_attention(qt, k, vt, tab, 0, 0, 0)
        bias = _moba_gate(qbt, _block_means(ksum.reshape(s // MOBA_BLOCK, -1)))
        o_b = _moba_attention(qt, bias, k, vt, 4, 4, 4)
        o_c = _diff_attention(qt, k, vt, row(lam_q1[l]), row(lam_k1[l]), row(lam_q2[l]), row(lam_k2[l]),
                              g_subln[l].reshape(-1, 1), lam_init, 8, 8, 8)
        x1 = _out_projection(o_a, o_b, o_c, w_out[l].T.astype(BF16), xs, gt_a)
        h2, route, counts = _norm_and_route(x1, row(g_mlp[l]), sc_m, sh_m, w_router, b_router)
        experts = lambda w: _layer_to_bf16(w, l).reshape(w.shape[1:])
        xs = _moe(x1, h2, route, counts, experts(w_gate), experts(w_up), experts(w_down),
                  gt_m, gf, l == depth - 1)
    return xs.reshape(batch, s, d)
```

```python
import functools
import math

import numpy as np
import jax
import jax.numpy as jnp
from jax import lax
from jax.experimental import pallas as pl
from jax.experimental.pallas import tpu as pltpu

F32 = jnp.float32
BF16 = jnp.bfloat16
I32 = jnp.int32

HEAD_DIM = 64
LANES = 128
A_PATTERNS = ((128, 1), (512, 4), (2048, 16))
MOBA_BLOCK = 256
MOBA_TOPK = 3
N_EXPERTS = 16
N_GROUPS = 4
EXPERTS_PER_GROUP = N_EXPERTS // N_GROUPS
ROPE_THETA = 10000.0
RMS_EPS = 1e-6
NEG = -1e30
VMEM_LIMIT = 56 * 1024 * 1024

TQ = 1024
TK = 512
KPQ = TQ // TK
assert KPQ == 2
PROJ_TM = 512
PROJ_TN = 512
NT_DIMS = (((1,), (1,)), ((), ()))
Q_SCALE = HEAD_DIM ** -0.5 * math.log2(math.e)


def _cparams(n_axes):
    return pltpu.CompilerParams(dimension_semantics=("arbitrary",) * n_axes,
                                vmem_limit_bytes=VMEM_LIMIT)


def _split_bf16(x):
    hi = x.astype(BF16)
    lo = (x - hi.astype(F32)).astype(BF16)
    return hi, lo


def _dot3(a, b, dims=(((1,), (0,)), ((), ()))):
    a_hi, a_lo = _split_bf16(a)
    b_hi, b_lo = _split_bf16(b)
    dg = functools.partial(lax.dot_general, dimension_numbers=dims, preferred_element_type=F32)
    return dg(a_hi, b_hi) + dg(a_hi, b_lo) + dg(a_lo, b_hi)


def _sigmoid(x):
    return 1.0 / (1.0 + jnp.exp(-x))


CAST_STREAMS = 4
CAST_BLOCK_ELEMS = 512 * 1024
CAST_COLS = 1024


def _cast_kernel(*refs):
    o_ref = refs[-1]
    col_tile = o_ref.shape[2]
    for k, x_ref in enumerate(refs[:-1]):
        rows = x_ref.shape[1]
        x = x_ref[0].astype(o_ref.dtype)
        for t in range(o_ref.shape[0]):
            o_ref[t, k * rows:(k + 1) * rows, :] = x[:, t * col_tile:(t + 1) * col_tile]


def _layer_to_bf16(w, l, col_tile=None):
    cols = w.shape[-1]
    col_tile = col_tile or cols
    w3 = w.reshape(w.shape[0], -1, cols)
    rows = w3.shape[1]
    tc = max(min(cols, CAST_COLS), col_tile)
    sr = CAST_BLOCK_ELEMS // tc
    tr = CAST_STREAMS * sr
    stream = lambda k: pl.BlockSpec((1, sr, tc), lambda i, j: (l, CAST_STREAMS * i + k, j))
    return pl.pallas_call(
        _cast_kernel,
        grid=(rows // tr, cols // tc),
        in_specs=[stream(k) for k in range(CAST_STREAMS)],
        out_specs=pl.BlockSpec((tc // col_tile, tr, col_tile), lambda i, j: (j, i, 0)),
        out_shape=jax.ShapeDtypeStruct((cols // col_tile, rows, col_tile), BF16),
        compiler_params=_cparams(2),
    )(*([w3] * CAST_STREAMS))


def _mod_kernel(c_ref, w_ref, b_ref, o_ref):
    c = c_ref[...]
    sc = c * _sigmoid(c)
    d = c.shape[0]
    acc = b_ref[0]
    for r in range(0, d, 256):
        acc = acc + jnp.sum(w_ref[0, r:r + 256, :] * sc[r:r + 256, :], axis=0, keepdims=True)
    o_ref[0] = acc


def _modulation(c, w_ada, b_ada):
    depth, d, n = w_ada.shape
    tn = 512
    return pl.pallas_call(
        _mod_kernel,
        grid=(depth, n // tn),
        in_specs=[pl.BlockSpec((d, 1), lambda l, j: (0, 0)),
                  pl.BlockSpec((1, d, tn), lambda l, j: (l, 0, j)),
                  pl.BlockSpec((1, 1, tn), lambda l, j: (l, 0, j))],
        out_specs=pl.BlockSpec((1, 1, tn), lambda l, j: (l, 0, j)),
        out_shape=jax.ShapeDtypeStruct((depth, 1, n), F32),
        compiler_params=_cparams(2),
    )(c.reshape(d, 1), w_ada, b_ada.reshape(depth, 1, n))


def _modulated_norm(x, g, sc, sh):
    var = jnp.mean(x * x, axis=-1, keepdims=True)
    return (x * lax.rsqrt(var + RMS_EPS) * g) * (1.0 + sc) + sh


def _rope_tile(acc, cos, sin):
    lane = lax.broadcasted_iota(I32, (1, LANES), 1)
    first = (lane & (HEAD_DIM - 1)) < (HEAD_DIM // 2)
    outs = []
    for c in range(acc.shape[1] // LANES):
        xc = acc[:, c * LANES:(c + 1) * LANES]
        rot = jnp.where(first, pltpu.roll(xc, LANES - HEAD_DIM // 2, 1), pltpu.roll(xc, HEAD_DIM // 2, 1))
        outs.append(xc * cos + rot * sin)
    return jnp.concatenate(outs, axis=1)


N_KIND_TILES = 4
COLUMN_TILE_ORDER = (1, 4, 8, 9, 0, 3, 6, 7, 2, 5, 10, 11)
KB_TILE = 1
QB_TILE = N_KIND_TILES + 1


def _inproj_kernel(x_ref, g_ref, sc_ref, sh_ref, w_ref, cos_ref, sin_ref,
                   k_ref, qt_ref, qbt_ref, vt_ref, ksum_ref, h_scr):
    tm = x_ref.shape[0]
    tn = w_ref.shape[2]
    h_scr[...] = _modulated_norm(x_ref[...], g_ref[...], sc_ref[...], sh_ref[...]).astype(BF16)
    for pos, tile in enumerate(COLUMN_TILE_ORDER):
        acc = jnp.dot(h_scr[...], w_ref[tile], preferred_element_type=F32)
        kind, t = divmod(pos, N_KIND_TILES)
        cols = slice(t * tn, (t + 1) * tn)
        if kind == 0:
            r = _rope_tile(acc, cos_ref[...], sin_ref[...])
            k_ref[:, cols] = r.astype(BF16)
            if pos == KB_TILE:
                ksum_ref[0] = jnp.sum(r.reshape(tm // MOBA_BLOCK, MOBA_BLOCK, tn), axis=1)
        elif kind == 1:
            rt = _rope_tile(acc, cos_ref[...], sin_ref[...]).T
            qt_ref[cols, :] = (rt * Q_SCALE).astype(BF16)
            if pos == QB_TILE:
                qbt_ref[...] = rt
        else:
            vt = acc.T.astype(BF16)
            for b in range(vt_ref.shape[0]):
                vt_ref[b, cols, :] = vt[:, b * TK:(b + 1) * TK]


def _in_projection(x, g, sc, sh, w_bf16, cos_t, sin_t):
    s, d = x.shape
    tm, tn = PROJ_TM, w_bf16.shape[2]
    nb = tm // MOBA_BLOCK
    width = N_KIND_TILES * tn
    row = lambda i: (0, 0)
    return pl.pallas_call(
        _inproj_kernel,
        grid=(s // tm,),
        in_specs=[pl.BlockSpec((tm, d), lambda i: (i, 0)),
                  pl.BlockSpec((1, d), row), pl.BlockSpec((1, d), row), pl.BlockSpec((1, d), row),
                  pl.BlockSpec(w_bf16.shape, lambda i: (0, 0, 0),
                               pipeline_mode=pl.Buffered(1)),
                  pl.BlockSpec((tm, LANES), lambda i: (i, 0)),
                  pl.BlockSpec((tm, LANES), lambda i: (i, 0))],
        out_specs=[pl.BlockSpec((tm, width), lambda i: (i, 0)),
                   pl.BlockSpec((width, tm), lambda i: (0, i)),
                   pl.BlockSpec((tn, tm), lambda i: (0, i)),
                   pl.BlockSpec((tm // TK, width, TK), lambda i: (i, 0, 0)),
                   pl.BlockSpec((1, nb, tn), lambda i: (i, 0, 0))],
        out_shape=[jax.ShapeDtypeStruct((s, width), BF16),
                   jax.ShapeDtypeStruct((width, s), BF16),
                   jax.ShapeDtypeStruct((tn, s), F32),
                   jax.ShapeDtypeStruct((s // TK, width, TK), BF16),
                   jax.ShapeDtypeStruct((s // tm, nb, tn), F32)],
        scratch_shapes=[pltpu.VMEM((tm, d), BF16)],
        compiler_params=_cparams(1),
    )(x, g, sc, sh, w_bf16, cos_t, sin_t)


def _flash_step(s_t, v_t, state):
    m_prev, l_prev, acc_prev = state
    m_new = jnp.maximum(m_prev, jnp.max(s_t, axis=0, keepdims=True))
    alpha = jnp.exp2(m_prev - m_new)
    p = jnp.exp2(s_t - m_new)
    l_new = alpha * l_prev + jnp.sum(p, axis=0, keepdims=True)
    acc_new = alpha * acc_prev + jnp.dot(v_t, p.astype(BF16), preferred_element_type=F32)
    return m_new, l_new, acc_new


def _sweep(first, n_loop, scores, consume, s_a, s_b, tail_masks):
    def park(buf, vals):
        for h, val in enumerate(vals):
            buf[h] = val

    def fetch(buf):
        return [buf[h] for h in range(buf.shape[0])]

    park(s_a, scores(first))

    def pair(j, carry):
        a = first + 2 * j
        park(s_b, scores(a + 1))
        consume(a, fetch(s_a), None)
        park(s_a, scores(a + 2))
        consume(a + 1, fetch(s_b), None)
        return carry

    lax.fori_loop(0, n_loop // 2, pair, 0)
    t0 = first + n_loop
    park(s_b, scores(t0 + 1))
    consume(t0, fetch(s_a), tail_masks[0])
    consume(t0 + 1, fetch(s_b), tail_masks[1])


def _score_scratch(n_streams):
    return [pltpu.VMEM((n_streams, TK, TQ), F32), pltpu.VMEM((n_streams, TK, TQ), F32)]


def _qk_t(k, q_t):
    return jnp.dot(k, q_t, preferred_element_type=F32)


def _k_block(k_ref, n):
    return k_ref[pl.ds(pl.multiple_of(n * TK, TK), TK), :]


def _init_state(refs):
    m_ref, l_ref, acc_ref = refs
    m_ref[...] = jnp.full(m_ref.shape, NEG, F32)
    l_ref[...] = jnp.zeros(l_ref.shape, F32)
    acc_ref[...] = jnp.zeros(acc_ref.shape, F32)


def _load_state(refs):
    return tuple(r[...] for r in refs)


def _store_state(refs, state):
    for r, val in zip(refs, state):
        r[...] = val


def _flash_scratch(n_streams, dv):
    out = []
    for _ in range(n_streams):
        out += [pltpu.VMEM((1, TQ), F32), pltpu.VMEM((1, TQ), F32), pltpu.VMEM((dv, TQ), F32)]
    return out


def _causal_t(d):
    r = lax.broadcasted_iota(I32, (TK, TQ), 0) + d * TK
    return r <= lax.broadcasted_iota(I32, (TK, TQ), 1)


def _stack_rows(top, bottom):
    return jnp.concatenate([top, bottom], axis=0)


def _dilated_bias_table():
    n_off = A_PATTERNS[-1][0] // TK + KPQ
    r = np.arange(TK)[:, None]
    c = np.arange(TQ)[None, :]
    tabs = []
    for t in range(n_off):
        delta = (t - (KPQ - 1)) * TK + c - r
        mult = np.zeros_like(delta)
        for window, dil in A_PATTERNS:
            mult += ((delta >= 0) & (delta % dil == 0) & (delta <= window)).astype(delta.dtype)
        tabs.append(np.where(mult > 0, np.log2(np.maximum(mult, 1).astype(np.float64)), NEG))
    return jnp.asarray(np.stack(tabs), F32)


def _dilated_kernel(qt_ref, k_ref, vt_ref, tab_ref, o_ref, s_a, s_b, *scr):
    i = pl.program_id(1)
    refs = (scr[0:3], scr[3:6])
    q_t = qt_ref[...]
    zero = jnp.zeros((HEAD_DIM, TQ), BF16)
    qs = (_stack_rows(q_t[:HEAD_DIM], zero), _stack_rows(zero, q_t[HEAD_DIM:]))
    n_off = tab_ref.shape[0]
    last = KPQ * i + (KPQ - 1)
    for h in range(2):
        _init_state(refs[h])

    def scores(n):
        kb = _k_block(k_ref, n)
        bias = tab_ref[last - n]
        return [_qk_t(kb, qs[h]) + bias for h in range(2)]

    def consume(n, s_list, mask):
        state = [_load_state(refs[h]) for h in range(2)]
        vb = vt_ref[n]
        for h in range(2):
            state[h] = _flash_step(s_list[h], vb[h * HEAD_DIM:(h + 1) * HEAD_DIM, :], state[h])
        for h in range(2):
            _store_state(refs[h], state[h])

    first = jnp.maximum(last - (n_off - 1), 0)
    _sweep(first, last - 1 - first, scores, consume, s_a, s_b, (None, None))
    for h in range(2):
        _, l, acc = _load_state(refs[h])
        o_ref[h * HEAD_DIM:(h + 1) * HEAD_DIM, :] = (acc / l).astype(o_ref.dtype)


def _dilated_attention(qt, k, vt, tab, qrow, kcol, vrow):
    s = k.shape[0]
    n_pairs = 4
    return pl.pallas_call(
        _dilated_kernel,
        grid=(n_pairs, s // TQ),
        in_specs=[pl.BlockSpec((LANES, TQ), lambda hp, i: (qrow + hp, i)),
                  pl.BlockSpec((s, LANES), lambda hp, i: (0, kcol + hp)),
                  pl.BlockSpec((s // TK, LANES, TK), lambda hp, i: (0, vrow + hp, 0)),
                  pl.BlockSpec(tab.shape, lambda hp, i: (0, 0, 0), pipeline_mode=pl.Buffered(1))],
        out_specs=pl.BlockSpec((LANES, TQ), lambda hp, i: (hp, i)),
        out_shape=jax.ShapeDtypeStruct((n_pairs * LANES, s), BF16),
        scratch_shapes=_score_scratch(2) + _flash_scratch(2, HEAD_DIM),
        compiler_params=_cparams(2),
    )(qt, k, vt, tab)


def _moba_gate_kernel(qt_ref, km_ref, bias_ref):
    i = pl.program_id(1)
    tg = qt_ref.shape[1]
    gate = _dot3(km_ref[0], qt_ref[...])
    nblk = gate.shape[0]
    blk = lax.broadcasted_iota(I32, (nblk, tg), 0)
    blk_f = blk.astype(F32)
    tok = i * tg + lax.broadcasted_iota(I32, (nblk, tg), 1)
    own = jnp.right_shift(tok, int(math.log2(MOBA_BLOCK)))
    past = blk < own
    sel = blk == own
    for _ in range(MOBA_TOPK):
        cand = past & jnp.logical_not(sel)
        g = jnp.where(cand, gate, -jnp.inf)
        mx = jnp.max(g, axis=0, keepdims=True)
        first = jnp.min(jnp.where(cand & (g == mx), blk_f, 2.0 * nblk), axis=0, keepdims=True)
        sel = sel | (blk_f == first)
    bias_ref[...] = jnp.where(sel, 0.0, NEG).astype(BF16)


def _moba_gate(qbt, kmean):
    width, s = qbt.shape
    n_heads = width // HEAD_DIM
    tg = 512
    return pl.pallas_call(
        _moba_gate_kernel,
        grid=(n_heads, s // tg),
        in_specs=[pl.BlockSpec((HEAD_DIM, tg), lambda h, i: (h, i)),
                  pl.BlockSpec((1, HEAD_DIM, HEAD_DIM), lambda h, i: (h, 0, 0))],
        out_specs=pl.BlockSpec((HEAD_DIM, tg), lambda h, i: (h, i)),
        out_shape=jax.ShapeDtypeStruct((width, s), BF16),
        compiler_params=_cparams(2),
    )(qbt, kmean)


def _moba_kernel(qt_ref, bias_ref, k_ref, vt_ref, o_ref, s_a, s_b, *scr):
    i = pl.program_id(1)
    refs = (scr[0:3], scr[3:6])
    q_t = qt_ref[...]
    b_t = bias_ref[...]
    qs = (_stack_rows(q_t[:HEAD_DIM], b_t[:HEAD_DIM]), _stack_rows(b_t[HEAD_DIM:], q_t[HEAD_DIM:]))
    lane = lax.broadcasted_iota(I32, (1, LANES), 1)
    lo = lane < HEAD_DIM
    mine = (lo, jnp.logical_not(lo))
    blk = lane & (HEAD_DIM - 1)
    sub_block = jnp.right_shift(lax.broadcasted_iota(I32, (TK, LANES), 0), int(math.log2(MOBA_BLOCK)))
    for h in range(2):
        _init_state(refs[h])

    def scores(n):
        kb = _k_block(k_ref, n)
        onehot = jnp.where(blk == n * (TK // MOBA_BLOCK) + sub_block, 1.0, 0.0).astype(BF16)
        return [_qk_t(jnp.where(mine[h], kb, onehot), qs[h]) for h in range(2)]

    def consume(n, s_list, mask):
        state = [_load_state(refs[h]) for h in range(2)]
        vb = vt_ref[n]
        for h in range(2):
            s_t = s_list[h] if mask is None else jnp.where(mask, s_list[h], NEG)
            state[h] = _flash_step(s_t, vb[h * HEAD_DIM:(h + 1) * HEAD_DIM, :], state[h])
        for h in range(2):
            _store_state(refs[h], state[h])

    _sweep(0, KPQ * i, scores, consume, s_a, s_b, (_causal_t(0), _causal_t(1)))
    for h in range(2):
        _, l, acc = _load_state(refs[h])
        o_ref[h * HEAD_DIM:(h + 1) * HEAD_DIM, :] = (acc / l).astype(o_ref.dtype)


def _moba_attention(qt, bias, k, vt, qrow, kcol, vrow):
    s = k.shape[0]
    n_pairs = 4
    return pl.pallas_call(
        _moba_kernel,
        grid=(n_pairs, s // TQ),
        in_specs=[pl.BlockSpec((LANES, TQ), lambda hp, i: (qrow + hp, i)),
                  pl.BlockSpec((LANES, TQ), lambda hp, i: (hp, i)),
                  pl.BlockSpec((s, LANES), lambda hp, i: (0, kcol + hp)),
                  pl.BlockSpec((s // TK, LANES, TK), lambda hp, i: (0, vrow + hp, 0))],
        out_specs=pl.BlockSpec((LANES, TQ), lambda hp, i: (hp, i)),
        out_shape=jax.ShapeDtypeStruct((n_pairs * LANES, s), BF16),
        scratch_shapes=_score_scratch(2) + _flash_scratch(2, HEAD_DIM),
        compiler_params=_cparams(2),
    )(qt, bias, k, vt)


def _diff_kernel(lam_init, qt_ref, k_ref, vt_ref, lq1_ref, lk1_ref, lq2_ref, lk2_ref, g_ref, o_ref,
                 s_a, s_b, *scr):
    i = pl.program_id(1)
    refs = (scr[0:3], scr[3:6])
    q_t = qt_ref[...]
    zero = jnp.zeros((HEAD_DIM, TQ), BF16)
    qs = (_stack_rows(q_t[:HEAD_DIM], zero), _stack_rows(zero, q_t[HEAD_DIM:]))
    for h in range(2):
        _init_state(refs[h])

    def scores(n):
        kb = _k_block(k_ref, n)
        return [_qk_t(kb, qs[h]) for h in range(2)]

    def consume(n, s_list, mask):
        state = [_load_state(refs[h]) for h in range(2)]
        vb = vt_ref[n]
        for h in range(2):
            s_t = s_list[h] if mask is None else jnp.where(mask, s_list[h], NEG)
            state[h] = _flash_step(s_t, vb, state[h])
        for h in range(2):
            _store_state(refs[h], state[h])

    _sweep(0, KPQ * i, scores, consume, s_a, s_b, (_causal_t(0), _causal_t(1)))
    lam = (jnp.exp(jnp.sum(lq1_ref[...] * lk1_ref[...], axis=1, keepdims=True))
           - jnp.exp(jnp.sum(lq2_ref[...] * lk2_ref[...], axis=1, keepdims=True)) + lam_init)
    _, l1, acc1 = _load_state(refs[0])
    _, l2, acc2 = _load_state(refs[1])
    o = acc1 / l1 - lam * (acc2 / l2)
    var = jnp.mean(o * o, axis=0, keepdims=True)
    o = (o * lax.rsqrt(var + RMS_EPS) * g_ref[...]) * (1.0 - lam_init)
    o_ref[...] = o.astype(o_ref.dtype)


def _diff_attention(qt, k, vt, lq1, lk1, lq2, lk2, g_sub, lam_init, qrow, kcol, vrow):
    s = k.shape[0]
    n_heads = 8
    vec = lambda n: pl.BlockSpec((1, n), lambda h, i: (0, 0))
    return pl.pallas_call(
        functools.partial(_diff_kernel, lam_init),
        grid=(n_heads, s // TQ),
        in_specs=[pl.BlockSpec((LANES, TQ), lambda h, i: (qrow + h, i)),
                  pl.BlockSpec((s, LANES), lambda h, i: (0, kcol + h)),
                  pl.BlockSpec((s // TK, LANES, TK), lambda h, i: (0, vrow + h, 0)),
                  vec(HEAD_DIM), vec(HEAD_DIM), vec(HEAD_DIM), vec(HEAD_DIM),
                  pl.BlockSpec((LANES, 1), lambda h, i: (0, 0))],
        out_specs=pl.BlockSpec((LANES, TQ), lambda h, i: (h, i)),
        out_shape=jax.ShapeDtypeStruct((n_heads * LANES, s), BF16),
        scratch_shapes=_score_scratch(2) + _flash_scratch(2, LANES),
        compiler_params=_cparams(2),
    )(qt, k, vt, lq1, lk1, lq2, lk2, g_sub)


def _outproj_kernel(oa_ref, ob_ref, oc_ref, wt_ref, x_ref, gt_ref, o_ref):
    wa = oa_ref.shape[0]
    wb = ob_ref.shape[0]
    y_t = (jnp.dot(wt_ref[:, 0:wa], oa_ref[...], preferred_element_type=F32)
           + jnp.dot(wt_ref[:, wa:wa + wb], ob_ref[...], preferred_element_type=F32)
           + jnp.dot(wt_ref[:, wa + wb:], oc_ref[...], preferred_element_type=F32))
    o_ref[...] = x_ref[...] + gt_ref[...] * y_t.T


def _out_projection(oa_t, ob_t, oc_t, wt_bf16, x, gt):
    s, d = x.shape
    tm = 256
    return pl.pallas_call(
        _outproj_kernel,
        grid=(s // tm,),
        in_specs=[pl.BlockSpec((oa_t.shape[0], tm), lambda i: (0, i)),
                  pl.BlockSpec((ob_t.shape[0], tm), lambda i: (0, i)),
                  pl.BlockSpec((oc_t.shape[0], tm), lambda i: (0, i)),
                  pl.BlockSpec(wt_bf16.shape, lambda i: (0, 0)),
                  pl.BlockSpec((tm, d), lambda i: (i, 0)),
                  pl.BlockSpec((1, d), lambda i: (0, 0))],
        out_specs=pl.BlockSpec((tm, d), lambda i: (i, 0)),
        out_shape=jax.ShapeDtypeStruct((s, d), F32),
        compiler_params=_cparams(1),
    )(oa_t, ob_t, oc_t, wt_bf16, x, gt)


def _route_kernel(x_ref, g_ref, sc_ref, sh_ref, wr_ref, br_ref, tri_ref,
                  h_ref, route_ref, count_ref, base_scr):
    i = pl.program_id(0)
    tm = x_ref.shape[0]

    @pl.when(i == 0)
    def _():
        base_scr[...] = jnp.zeros_like(base_scr)

    h = _modulated_norm(x_ref[...], g_ref[...], sc_ref[...], sh_ref[...])
    h_ref[...] = h.astype(BF16)
    logits = _dot3(wr_ref[...], h, NT_DIMS)
    s = _sigmoid(logits)
    sg = s + br_ref[...]
    rows = [sg[r:r + 1, :] for r in range(N_EXPERTS)]

    def top2_sum(vals):
        best = None
        for a in range(len(vals)):
            for b in range(a + 1, len(vals)):
                pair = vals[a] + vals[b]
                best = pair if best is None else jnp.maximum(best, pair)
        return best

    score = [top2_sum(rows[g * EXPERTS_PER_GROUP:(g + 1) * EXPERTS_PER_GROUP]) for g in range(N_GROUPS)]
    best, grp = score[0], jnp.zeros((1, tm), I32)
    for g in range(1, N_GROUPS):
        better = score[g] > best
        grp = jnp.where(better, g, grp)
        best = jnp.where(better, score[g], best)
    cand = []
    for j in range(EXPERTS_PER_GROUP):
        c = rows[j]
        for g in range(1, N_GROUPS):
            c = jnp.where(grp == g, rows[g * EXPERTS_PER_GROUP + j], c)
        cand.append(c)
    b0, l0 = cand[0], jnp.zeros((1, tm), I32)
    for j in range(1, EXPERTS_PER_GROUP):
        better = cand[j] > b0
        l0 = jnp.where(better, j, l0)
        b0 = jnp.where(better, cand[j], b0)
    b1, l1 = jnp.full((1, tm), -jnp.inf, F32), jnp.zeros((1, tm), I32)
    for j in range(EXPERTS_PER_GROUP):
        better = (l0 != j) & (cand[j] > b1)
        l1 = jnp.where(better, j, l1)
        b1 = jnp.where(better, cand[j], b1)
    e0 = grp * EXPERTS_PER_GROUP + l0
    e1 = grp * EXPERTS_PER_GROUP + l1
    erow = lax.broadcasted_iota(I32, (N_EXPERTS, tm), 0)
    sel0 = erow == e0
    sel1 = erow == e1
    s0 = jnp.sum(jnp.where(sel0, s, 0.0), axis=0, keepdims=True)
    s1 = jnp.sum(jnp.where(sel1, s, 0.0), axis=0, keepdims=True)
    tot = s0 + s1
    chosen = jnp.where(sel0 | sel1, 1.0, 0.0).astype(BF16)
    csum = jnp.dot(chosen, tri_ref[...], preferred_element_type=F32)
    pos = base_scr[...] + csum - 1.0
    r0 = jnp.sum(jnp.where(sel0, pos, 0.0), axis=0, keepdims=True)
    r1 = jnp.sum(jnp.where(sel1, pos, 0.0), axis=0, keepdims=True)
    base = base_scr[...] + csum[:, tm - 1:tm]
    base_scr[...] = base
    count_ref[...] = jnp.broadcast_to(base, count_ref.shape)
    zero = jnp.zeros((1, tm), F32)
    for r, val in enumerate((e0.astype(F32), e1.astype(F32), r0, r1, s0 / tot, s1 / tot, zero, zero)):
        route_ref[r:r + 1, :] = val


def _norm_and_route(x, g, sc, sh, w_router, b_router):
    s, d = x.shape
    tm = 512
    tri = jnp.asarray(np.triu(np.ones((tm, tm), np.float32)), BF16)
    row = lambda i: (0, 0)
    return pl.pallas_call(
        _route_kernel,
        grid=(s // tm,),
        in_specs=[pl.BlockSpec((tm, d), lambda i: (i, 0)),
                  pl.BlockSpec((1, d), row), pl.BlockSpec((1, d), row), pl.BlockSpec((1, d), row),
                  pl.BlockSpec((N_EXPERTS, d), row), pl.BlockSpec((N_EXPERTS, 1), row),
                  pl.BlockSpec((tm, tm), row)],
        out_specs=[pl.BlockSpec((tm, d), lambda i: (i, 0)),
                   pl.BlockSpec((8, tm), lambda i: (0, i)),
                   pl.BlockSpec((N_EXPERTS, LANES), row)],
        out_shape=[jax.ShapeDtypeStruct((s, d), BF16),
                   jax.ShapeDtypeStruct((8, s), F32),
                   jax.ShapeDtypeStruct((N_EXPERTS, LANES), F32)],
        scratch_shapes=[pltpu.VMEM((N_EXPERTS, 1), F32)],
        compiler_params=_cparams(1),
    )(x, g, sc, sh, w_router.T, b_router.reshape(N_EXPERTS, 1), tri)


FFN_ROWS = 512


def _ffn_kernel(te_ref, nu_ref, xs_ref, wg_ref, wu_ref, wd_ref, y_ref):
    j = pl.program_id(0)

    @pl.when(j < nu_ref[0])
    def _():
        x = xs_ref[...]
        a = jnp.dot(x, wg_ref[0], preferred_element_type=F32)
        b = jnp.dot(x, wu_ref[0], preferred_element_type=F32)
        act = (a * _sigmoid(a) * b).astype(BF16)
        y_ref[...] = jnp.dot(act, wd_ref[0], preferred_element_type=F32).astype(y_ref.dtype)

    @pl.when(j >= nu_ref[0])
    def _():
        y_ref[...] = jnp.zeros_like(y_ref)


def _expert_ffn(tile_expert, n_used, xs, wg, wu, wd):
    r, d = xs.shape
    de = wg.shape[2]
    grid_spec = pltpu.PrefetchScalarGridSpec(
        num_scalar_prefetch=2,
        grid=(r // FFN_ROWS,),
        in_specs=[pl.BlockSpec((FFN_ROWS, d), lambda j, te, nu: (j, 0)),
                  pl.BlockSpec((1, d, de), lambda j, te, nu: (te[j], 0, 0)),
                  pl.BlockSpec((1, d, de), lambda j, te, nu: (te[j], 0, 0)),
                  pl.BlockSpec((1, de, d), lambda j, te, nu: (te[j], 0, 0))],
        out_specs=pl.BlockSpec((FFN_ROWS, d), lambda j, te, nu: (j, 0)),
    )
    return pl.pallas_call(
        _ffn_kernel,
        grid_spec=grid_spec,
        out_shape=jax.ShapeDtypeStruct((r, d), BF16),
        compiler_params=_cparams(1),
    )(tile_expert, n_used, xs, wg, wu, wd)


def _combine_kernel(final, x_ref, y0_ref, y1_ref, w0_ref, w1_ref, gt_ref, gf_ref, o_ref):
    x = x_ref[...] + gt_ref[...] * (w0_ref[...] * y0_ref[...].astype(F32)
                                    + w1_ref[...] * y1_ref[...].astype(F32))
    if final:
        var = jnp.mean(x * x, axis=-1, keepdims=True)
        x = x * lax.rsqrt(var + RMS_EPS) * gf_ref[...]
    o_ref[...] = x


def _combine(x, y0, y1, w0, w1, gt, g_final, final):
    s, d = x.shape
    tm = 256
    big = pl.BlockSpec((tm, d), lambda i: (i, 0))
    col = pl.BlockSpec((tm, 1), lambda i: (i, 0))
    vec = pl.BlockSpec((1, d), lambda i: (0, 0))
    return pl.pallas_call(
        functools.partial(_combine_kernel, final),
        grid=(s // tm,),
        in_specs=[big, big, big, col, col, vec, vec],
        out_specs=big,
        out_shape=jax.ShapeDtypeStruct((s, d), F32),
        compiler_params=_cparams(1),
    )(x, y0, y1, w0, w1, gt, g_final)


def _rope_tables(positions):
    half = HEAD_DIM // 2
    inv = ROPE_THETA ** (-jnp.arange(0, HEAD_DIM, 2, dtype=F32) / HEAD_DIM)
    ang = positions.astype(F32)[:, None] * inv
    cos, sin = jnp.cos(ang), jnp.sin(ang)
    cos_t = jnp.tile(cos, (1, LANES // half))
    sin_t = jnp.tile(jnp.concatenate([-sin, sin], axis=1), (1, LANES // HEAD_DIM))
    return cos_t, sin_t


def _block_means(ksum):
    nblk, width = ksum.shape
    km = (ksum / MOBA_BLOCK).reshape(nblk, width // HEAD_DIM, HEAD_DIM)
    km = jnp.pad(km, ((0, HEAD_DIM - nblk), (0, 0), (0, 0)))
    return jnp.transpose(km, (1, 0, 2))


def _moe(x1, h2, route, counts, wg, wu, wd, gt, g_final, final):
    s, d = x1.shape
    e0, e1 = route[0].astype(I32), route[1].astype(I32)
    r0, r1 = route[2].astype(I32), route[3].astype(I32)
    cnt = counts[:, 0].astype(I32)
    cnt_pad = ((cnt + FFN_ROWS - 1) // FFN_ROWS) * FFN_ROWS
    off = jnp.concatenate([jnp.zeros((1,), I32), jnp.cumsum(cnt_pad).astype(I32)])
    d0 = off[e0] + r0
    d1 = off[e1] + r1
    n_rows = 2 * s + N_EXPERTS * FFN_ROWS
    tok = jnp.arange(s, dtype=I32)
    src = (jnp.arange(n_rows, dtype=I32) % s).at[d0].set(tok).at[d1].set(tok)
    n_tiles = n_rows // FFN_ROWS
    tile_expert = jnp.searchsorted(off[1:], jnp.arange(n_tiles, dtype=I32) * FFN_ROWS, side="right")
    tile_expert = jnp.minimum(tile_expert, N_EXPERTS - 1).astype(I32)
    n_used = (off[N_EXPERTS] // FFN_ROWS).reshape(1).astype(I32)
    y = _expert_ffn(tile_expert, n_used, h2[src], wg, wu, wd)
    return _combine(x1, y[d0], y[d1], route[4].reshape(s, 1), route[5].reshape(s, 1), gt, g_final, final)


def kernel(x, c, positions, w_ada, b_ada, g_attn, g_mlp, w_in, w_out, lam_q1, lam_k1, lam_q2, lam_k2,
           g_subln, w_gate, w_up, w_down, w_router, b_router, g_final):
    batch, s, d = x.shape
    assert batch == 1 and s % TQ == 0 and s // MOBA_BLOCK <= HEAD_DIM
    assert w_in.shape[2] == 3 * N_KIND_TILES * PROJ_TN
    depth = w_ada.shape[0]
    xs = x.reshape(s, d)
    cos_t, sin_t = _rope_tables(positions[0])
    mod = _modulation(c, w_ada, b_ada)
    tab = _dilated_bias_table()
    gf = g_final.reshape(1, d)
    row = lambda v: v.reshape(1, -1)
    for l in range(depth):
        lam_init = 0.8 - 0.6 * math.exp(-0.3 * l)
        sh_a, sc_a, gt_a, sh_m, sc_m, gt_m = [mod[l, :, k * d:(k + 1) * d] for k in range(6)]
        k, qt, qbt, vt, ksum = _in_projection(xs, row(g_attn[l]), sc_a, sh_a,
                                              _layer_to_bf16(w_in, l, PROJ_TN), cos_t, sin_t)
        o_a = _dilated_attention(qt, k, vt, tab, 0, 0, 0)
        bias = _moba_gate(qbt, _block_means(ksum.reshape(s // MOBA_BLOCK, -1)))
        o_b = _moba_attention(qt, bias, k, vt, 4, 4, 4)
        o_c = _diff_attention(qt, k, vt, row(lam_q1[l]), row(lam_k1[l]), row(lam_q2[l]), row(lam_k2[l]),
                              g_subln[l].reshape(-1, 1), lam_init, 8, 8, 8)
        x1 = _out_projection(o_a, o_b, o_c, w_out[l].T.astype(BF16), xs, gt_a)
        h2, route, counts = _norm_and_route(x1, row(g_mlp[l]), sc_m, sh_m, w_router, b_router)
        experts = lambda w: _layer_to_bf16(w, l).reshape(w.shape[1:])
        xs = _moe(x1, h2, route, counts, experts(w_gate), experts(w_up), experts(w_down),
                  gt_m, gf, l == depth - 1)
    return xs.reshape(batch, s, d)
```

```python
import functools
import math

import numpy as np
import jax
import jax.numpy as jnp
from jax import lax
from jax.experimental import pallas as pl
from jax.experimental.pallas import tpu as pltpu

F32 = jnp.float32
BF16 = jnp.bfloat16
I32 = jnp.int32

HEAD_DIM = 64
LANES = 128
A_PATTERNS = ((128, 1), (512, 4), (2048, 16))
MOBA_BLOCK = 256
MOBA_TOPK = 3
N_EXPERTS = 16
N_GROUPS = 4
EXPERTS_PER_GROUP = N_EXPERTS // N_GROUPS
ROPE_THETA = 10000.0
RMS_EPS = 1e-6
NEG = -1e30
VMEM_LIMIT = 56 * 1024 * 1024

TQ = 1024
TK = 512
KPQ = TQ // TK
assert KPQ == 2
PROJ_TM = 512
PROJ_TN = 512
NT_DIMS = (((1,), (1,)), ((), ()))
Q_SCALE = HEAD_DIM ** -0.5 * math.log2(math.e)


def _cparams(n_axes):
    return pltpu.CompilerParams(dimension_semantics=("arbitrary",) * n_axes,
                                vmem_limit_bytes=VMEM_LIMIT)


def _split_bf16(x):
    hi = x.astype(BF16)
    lo = (x - hi.astype(F32)).astype(BF16)
    return hi, lo


def _dot3(a, b, dims=(((1,), (0,)), ((), ()))):
    a_hi, a_lo = _split_bf16(a)
    b_hi, b_lo = _split_bf16(b)
    dg = functools.partial(lax.dot_general, dimension_numbers=dims, preferred_element_type=F32)
    return dg(a_hi, b_hi) + dg(a_hi, b_lo) + dg(a_lo, b_hi)


def _sigmoid(x):
    return 1.0 / (1.0 + jnp.exp(-x))


CAST_STREAMS = 4
CAST_BLOCK_ELEMS = 512 * 1024
CAST_COLS = 1024


def _cast_kernel(*refs):
    o_ref = refs[-1]
    col_tile = o_ref.shape[2]
    for k, x_ref in enumerate(refs[:-1]):
        rows = x_ref.shape[1]
        x = x_ref[0].astype(o_ref.dtype)
        for t in range(o_ref.shape[0]):
            o_ref[t, k * rows:(k + 1) * rows, :] = x[:, t * col_tile:(t + 1) * col_tile]


def _layer_to_bf16(w, l, col_tile=None):
    cols = w.shape[-1]
    col_tile = col_tile or cols
    w3 = w.reshape(w.shape[0], -1, cols)
    rows = w3.shape[1]
    tc = max(min(cols, CAST_COLS), col_tile)
    sr = CAST_BLOCK_ELEMS // tc
    tr = CAST_STREAMS * sr
    stream = lambda k: pl.BlockSpec((1, sr, tc), lambda i, j: (l, CAST_STREAMS * i + k, j))
    return pl.pallas_call(
        _cast_kernel,
        grid=(rows // tr, cols // tc),
        in_specs=[stream(k) for k in range(CAST_STREAMS)],
        out_specs=pl.BlockSpec((tc // col_tile, tr, col_tile), lambda i, j: (j, i, 0)),
        out_shape=jax.ShapeDtypeStruct((cols // col_tile, rows, col_tile), BF16),
        compiler_params=_cparams(2),
    )(*([w3] * CAST_STREAMS))


def _mod_kernel(c_ref, w_ref, b_ref, o_ref):
    c = c_ref[...]
    sc = c * _sigmoid(c)
    d = c.shape[0]
    acc = b_ref[0]
    for r in range(0, d, 256):
        acc = acc + jnp.sum(w_ref[0, r:r + 256, :] * sc[r:r + 256, :], axis=0, keepdims=True)
    o_ref[0] = acc


def _modulation(c, w_ada, b_ada):
    depth, d, n = w_ada.shape
    tn = 512
    return pl.pallas_call(
        _mod_kernel,
        grid=(depth, n // tn),
        in_specs=[pl.BlockSpec((d, 1), lambda l, j: (0, 0)),
                  pl.BlockSpec((1, d, tn), lambda l, j: (l, 0, j)),
                  pl.BlockSpec((1, 1, tn), lambda l, j: (l, 0, j))],
        out_specs=pl.BlockSpec((1, 1, tn), lambda l, j: (l, 0, j)),
        out_shape=jax.ShapeDtypeStruct((depth, 1, n), F32),
        compiler_params=_cparams(2),
    )(c.reshape(d, 1), w_ada, b_ada.reshape(depth, 1, n))


def _modulated_norm(x, g, sc, sh):
    var = jnp.mean(x * x, axis=-1, keepdims=True)
    return (x * lax.rsqrt(var + RMS_EPS) * g) * (1.0 + sc) + sh


def _rope_tile(acc, cos, sin):
    lane = lax.broadcasted_iota(I32, (1, LANES), 1)
    first = (lane & (HEAD_DIM - 1)) < (HEAD_DIM // 2)
    outs = []
    for c in range(acc.shape[1] // LANES):
        xc = acc[:, c * LANES:(c + 1) * LANES]
        rot = jnp.where(first, pltpu.roll(xc, LANES - HEAD_DIM // 2, 1), pltpu.roll(xc, HEAD_DIM // 2, 1))
        outs.append(xc * cos + rot * sin)
    return jnp.concatenate(outs, axis=1)


N_KIND_TILES = 4
COLUMN_TILE_ORDER = (1, 4, 8, 9, 0, 3, 6, 7, 2, 5, 10, 11)
KB_TILE = 1
QB_TILE = N_KIND_TILES + 1


def _inproj_kernel(x_ref, g_ref, sc_ref, sh_ref, w_ref, cos_ref, sin_ref,
                   k_ref, qt_ref, qbt_ref, vt_ref, ksum_ref, h_scr):
    tm = x_ref.shape[0]
    tn = w_ref.shape[2]
    h_scr[...] = _modulated_norm(x_ref[...], g_ref[...], sc_ref[...], sh_ref[...]).astype(BF16)
    for pos, tile in enumerate(COLUMN_TILE_ORDER):
        acc = jnp.dot(h_scr[...], w_ref[tile], preferred_element_type=F32)
        kind, t = divmod(pos, N_KIND_TILES)
        cols = slice(t * tn, (t + 1) * tn)
        if kind == 0:
            r = _rope_tile(acc, cos_ref[...], sin_ref[...])
            k_ref[:, cols] = r.astype(BF16)
            if pos == KB_TILE:
                ksum_ref[0] = jnp.sum(r.reshape(tm // MOBA_BLOCK, MOBA_BLOCK, tn), axis=1)
        elif kind == 1:
            rt = _rope_tile(acc, cos_ref[...], sin_ref[...]).T
            qt_ref[cols, :] = (rt * Q_SCALE).astype(BF16)
            if pos == QB_TILE:
                qbt_ref[...] = rt
        else:
            vt = acc.T.astype(BF16)
            for b in range(vt_ref.shape[0]):
                vt_ref[b, cols, :] = vt[:, b * TK:(b + 1) * TK]


def _in_projection(x, g, sc, sh, w_bf16, cos_t, sin_t):
    s, d = x.shape
    tm, tn = PROJ_TM, w_bf16.shape[2]
    nb = tm // MOBA_BLOCK
    width = N_KIND_TILES * tn
    row = lambda i: (0, 0)
    return pl.pallas_call(
        _inproj_kernel,
        grid=(s // tm,),
        in_specs=[pl.BlockSpec((tm, d), lambda i: (i, 0)),
                  pl.BlockSpec((1, d), row), pl.BlockSpec((1, d), row), pl.BlockSpec((1, d), row),
                  pl.BlockSpec(w_bf16.shape, lambda i: (0, 0, 0),
                               pipeline_mode=pl.Buffered(1)),
                  pl.BlockSpec((tm, LANES), lambda i: (i, 0)),
                  pl.BlockSpec((tm, LANES), lambda i: (i, 0))],
        out_specs=[pl.BlockSpec((tm, width), lambda i: (i, 0)),
                   pl.BlockSpec((width, tm), lambda i: (0, i)),
                   pl.BlockSpec((tn, tm), lambda i: (0, i)),
                   pl.BlockSpec((tm // TK, width, TK), lambda i: (i, 0, 0)),
                   pl.BlockSpec((1, nb, tn), lambda i: (i, 0, 0))],
        out_shape=[jax.ShapeDtypeStruct((s, width), BF16),
                   jax.ShapeDtypeStruct((width, s), BF16),
                   jax.ShapeDtypeStruct((tn, s), F32),
                   jax.ShapeDtypeStruct((s // TK, width, TK), BF16),
                   jax.ShapeDtypeStruct((s // tm, nb, tn), F32)],
        scratch_shapes=[pltpu.VMEM((tm, d), BF16)],
        compiler_params=_cparams(1),
    )(x, g, sc, sh, w_bf16, cos_t, sin_t)


def _flash_step(s_t, v_t, state):
    m_prev, l_prev, acc_prev = state
    m_new = jnp.maximum(m_prev, jnp.max(s_t, axis=0, keepdims=True))
    alpha = jnp.exp2(m_prev - m_new)
    p = jnp.exp2(s_t - m_new)
    l_new = alpha * l_prev + jnp.sum(p, axis=0, keepdims=True)
    acc_new = alpha * acc_prev + jnp.dot(v_t, p.astype(BF16), preferred_element_type=F32)
    return m_new, l_new, acc_new


def _sweep(first, n_loop, scores, consume, s_a, s_b, tail_masks):
    def park(buf, vals):
        for h, val in enumerate(vals):
            buf[h] = val

    def fetch(buf):
        return [buf[h] for h in range(buf.shape[0])]

    park(s_a, scores(first))

    def pair(j, carry):
        a = first + 2 * j
        park(s_b, scores(a + 1))
        consume(a, fetch(s_a), None)
        park(s_a, scores(a + 2))
        consume(a + 1, fetch(s_b), None)
        return carry

    lax.fori_loop(0, n_loop // 2, pair, 0)
    t0 = first + n_loop
    park(s_b, scores(t0 + 1))
    consume(t0, fetch(s_a), tail_masks[0])
    consume(t0 + 1, fetch(s_b), tail_masks[1])


def _score_scratch(n_streams):
    return [pltpu.VMEM((n_streams, TK, TQ), F32), pltpu.VMEM((n_streams, TK, TQ), F32)]


def _qk_t(k, q_t):
    return jnp.dot(k, q_t, preferred_element_type=F32)


def _k_block(k_ref, n):
    return k_ref[pl.ds(pl.multiple_of(n * TK, TK), TK), :]


def _init_state(refs):
    m_ref, l_ref, acc_ref = refs
    m_ref[...] = jnp.full(m_ref.shape, NEG, F32)
    l_ref[...] = jnp.zeros(l_ref.shape, F32)
    acc_ref[...] = jnp.zeros(acc_ref.shape, F32)


def _load_state(refs):
    return tuple(r[...] for r in refs)


def _store_state(refs, state):
    for r, val in zip(refs, state):
        r[...] = val


def _flash_scratch(n_streams, dv):
    out = []
    for _ in range(n_streams):
        out += [pltpu.VMEM((1, TQ), F32), pltpu.VMEM((1, TQ), F32), pltpu.VMEM((dv, TQ), F32)]
    return out


def _causal_t(d):
    r = lax.broadcasted_iota(I32, (TK, TQ), 0) + d * TK
    return r <= lax.broadcasted_iota(I32, (TK, TQ), 1)


def _stack_rows(top, bottom):
    return jnp.concatenate([top, bottom], axis=0)


def _dilated_bias_table():
    n_off = A_PATTERNS[-1][0] // TK + KPQ
    r = np.arange(TK)[:, None]
    c = np.arange(TQ)[None, :]
    tabs = []
    for t in range(n_off):
        delta = (t - (KPQ - 1)) * TK + c - r
        mult = np.zeros_like(delta)
        for window, dil in A_PATTERNS:
            mult += ((delta >= 0) & (delta % dil == 0) & (delta <= window)).astype(delta.dtype)
        tabs.append(np.where(mult > 0, np.log2(np.maximum(mult, 1).astype(np.float64)), NEG))
    return jnp.asarray(np.stack(tabs), F32)


def _dilated_kernel(qt_ref, k_ref, vt_ref, tab_ref, o_ref, s_a, s_b, *scr):
    i = pl.program_id(1)
    refs = (scr[0:3], scr[3:6])
    q_t = qt_ref[...]
    zero = jnp.zeros((HEAD_DIM, TQ), BF16)
    qs = (_stack_rows(q_t[:HEAD_DIM], zero), _stack_rows(zero, q_t[HEAD_DIM:]))
    n_off = tab_ref.shape[0]
    last = KPQ * i + (KPQ - 1)
    for h in range(2):
        _init_state(refs[h])

    def scores(n):
        kb = _k_block(k_ref, n)
        bias = tab_ref[last - n]
        return [_qk_t(kb, qs[h]) + bias for h in range(2)]

    def consume(n, s_list, mask):
        state = [_load_state(refs[h]) for h in range(2)]
        vb = vt_ref[n]
        for h in range(2):
            state[h] = _flash_step(s_list[h], vb[h * HEAD_DIM:(h + 1) * HEAD_DIM, :], state[h])
        for h in range(2):
            _store_state(refs[h], state[h])

    first = jnp.maximum(last - (n_off - 1), 0)
    _sweep(first, last - 1 - first, scores, consume, s_a, s_b, (None, None))
    for h in range(2):
        _, l, acc = _load_state(refs[h])
        o_ref[h * HEAD_DIM:(h + 1) * HEAD_DIM, :] = (acc / l).astype(o_ref.dtype)


def _dilated_attention(qt, k, vt, tab, qrow, kcol, vrow):
    s = k.shape[0]
    n_pairs = 4
    return pl.pallas_call(
        _dilated_kernel,
        grid=(n_pairs, s // TQ),
        in_specs=[pl.BlockSpec((LANES, TQ), lambda hp, i: (qrow + hp, i)),
                  pl.BlockSpec((s, LANES), lambda hp, i: (0, kcol + hp)),
                  pl.BlockSpec((s // TK, LANES, TK), lambda hp, i: (0, vrow + hp, 0)),
                  pl.BlockSpec(tab.shape, lambda hp, i: (0, 0, 0), pipeline_mode=pl.Buffered(1))],
        out_specs=pl.BlockSpec((LANES, TQ), lambda hp, i: (hp, i)),
        out_shape=jax.ShapeDtypeStruct((n_pairs * LANES, s), BF16),
        scratch_shapes=_score_scratch(2) + _flash_scratch(2, HEAD_DIM),
        compiler_params=_cparams(2),
    )(qt, k, vt, tab)


def _moba_gate_kernel(qt_ref, km_ref, bias_ref):
    i = pl.program_id(0)
    tg = qt_ref.shape[1]
    n_heads, nblk, _ = km_ref.shape
    blk = lax.broadcasted_iota(I32, (nblk, tg), 0)
    blk_f = blk.astype(F32)
    tok = i * tg + lax.broadcasted_iota(I32, (nblk, tg), 1)
    own = jnp.right_shift(tok, int(math.log2(MOBA_BLOCK)))
    past = blk < own
    for h in range(n_heads):
        rows = slice(h * HEAD_DIM, (h + 1) * HEAD_DIM)
        gate = _dot3(km_ref[h], qt_ref[rows, :])
        sel = blk == own
        for _ in range(MOBA_TOPK):
            cand = past & jnp.logical_not(sel)
            g = jnp.where(cand, gate, -jnp.inf)
            mx = jnp.max(g, axis=0, keepdims=True)
            first = jnp.min(jnp.where(cand & (g == mx), blk_f, 2.0 * nblk), axis=0, keepdims=True)
            sel = sel | (blk_f == first)
        bias_ref[rows, :] = jnp.where(sel, 0.0, NEG).astype(BF16)


def _moba_gate(qbt, kmean):
    width, s = qbt.shape
    tg = 512
    return pl.pallas_call(
        _moba_gate_kernel,
        grid=(s // tg,),
        in_specs=[pl.BlockSpec((width, tg), lambda i: (0, i)),
                  pl.BlockSpec(kmean.shape, lambda i: (0, 0, 0))],
        out_specs=pl.BlockSpec((width, tg), lambda i: (0, i)),
        out_shape=jax.ShapeDtypeStruct((width, s), BF16),
        compiler_params=_cparams(1),
    )(qbt, kmean)


def _moba_kernel(qt_ref, bias_ref, k_ref, vt_ref, o_ref, s_a, s_b, *scr):
    i = pl.program_id(1)
    refs = (scr[0:3], scr[3:6])
    q_t = qt_ref[...]
    b_t = bias_ref[...]
    qs = (_stack_rows(q_t[:HEAD_DIM], b_t[:HEAD_DIM]), _stack_rows(b_t[HEAD_DIM:], q_t[HEAD_DIM:]))
    lane = lax.broadcasted_iota(I32, (1, LANES), 1)
    lo = lane < HEAD_DIM
    mine = (lo, jnp.logical_not(lo))
    blk = lane & (HEAD_DIM - 1)
    sub_block = jnp.right_shift(lax.broadcasted_iota(I32, (TK, LANES), 0), int(math.log2(MOBA_BLOCK)))
    for h in range(2):
        _init_state(refs[h])

    def scores(n):
        kb = _k_block(k_ref, n)
        onehot = jnp.where(blk == n * (TK // MOBA_BLOCK) + sub_block, 1.0, 0.0).astype(BF16)
        return [_qk_t(jnp.where(mine[h], kb, onehot), qs[h]) for h in range(2)]

    def consume(n, s_list, mask):
        state = [_load_state(refs[h]) for h in range(2)]
        vb = vt_ref[n]
        for h in range(2):
            s_t = s_list[h] if mask is None else jnp.where(mask, s_list[h], NEG)
            state[h] = _flash_step(s_t, vb[h * HEAD_DIM:(h + 1) * HEAD_DIM, :], state[h])
        for h in range(2):
            _store_state(refs[h], state[h])

    _sweep(0, KPQ * i, scores, consume, s_a, s_b, (_causal_t(0), _causal_t(1)))
    for h in range(2):
        _, l, acc = _load_state(refs[h])
        o_ref[h * HEAD_DIM:(h + 1) * HEAD_DIM, :] = (acc / l).astype(o_ref.dtype)


def _moba_attention(qt, bias, k, vt, qrow, kcol, vrow):
    s = k.shape[0]
    n_pairs = 4
    return pl.pallas_call(
        _moba_kernel,
        grid=(n_pairs, s // TQ),
        in_specs=[pl.BlockSpec((LANES, TQ), lambda hp, i: (qrow + hp, i)),
                  pl.BlockSpec((LANES, TQ), lambda hp, i: (hp, i)),
                  pl.BlockSpec((s, LANES), lambda hp, i: (0, kcol + hp)),
                  pl.BlockSpec((s // TK, LANES, TK), lambda hp, i: (0, vrow + hp, 0))],
        out_specs=pl.BlockSpec((LANES, TQ), lambda hp, i: (hp, i)),
        out_shape=jax.ShapeDtypeStruct((n_pairs * LANES, s), BF16),
        scratch_shapes=_score_scratch(2) + _flash_scratch(2, HEAD_DIM),
        compiler_params=_cparams(2),
    )(qt, bias, k, vt)


def _diff_kernel(lam_init, qt_ref, k_ref, vt_ref, lq1_ref, lk1_ref, lq2_ref, lk2_ref, g_ref, o_ref,
                 s_a, s_b, *scr):
    i = pl.program_id(1)
    refs = (scr[0:3], scr[3:6])
    q_t = qt_ref[...]
    zero = jnp.zeros((HEAD_DIM, TQ), BF16)
    qs = (_stack_rows(q_t[:HEAD_DIM], zero), _stack_rows(zero, q_t[HEAD_DIM:]))
    for h in range(2):
        _init_state(refs[h])

    def scores(n):
        kb = _k_block(k_ref, n)
        return [_qk_t(kb, qs[h]) for h in range(2)]

    def consume(n, s_list, mask):
        state = [_load_state(refs[h]) for h in range(2)]
        vb = vt_ref[n]
        for h in range(2):
            s_t = s_list[h] if mask is None else jnp.where(mask, s_list[h], NEG)
            state[h] = _flash_step(s_t, vb, state[h])
        for h in range(2):
            _store_state(refs[h], state[h])

    _sweep(0, KPQ * i, scores, consume, s_a, s_b, (_causal_t(0), _causal_t(1)))
    lam = (jnp.exp(jnp.sum(lq1_ref[...] * lk1_ref[...], axis=1, keepdims=True))
           - jnp.exp(jnp.sum(lq2_ref[...] * lk2_ref[...], axis=1, keepdims=True)) + lam_init)
    _, l1, acc1 = _load_state(refs[0])
    _, l2, acc2 = _load_state(refs[1])
    o = acc1 / l1 - lam * (acc2 / l2)
    var = jnp.mean(o * o, axis=0, keepdims=True)
    o = (o * lax.rsqrt(var + RMS_EPS) * g_ref[...]) * (1.0 - lam_init)
    o_ref[...] = o.astype(o_ref.dtype)


def _diff_attention(qt, k, vt, lq1, lk1, lq2, lk2, g_sub, lam_init, qrow, kcol, vrow):
    s = k.shape[0]
    n_heads = 8
    vec = lambda n: pl.BlockSpec((1, n), lambda h, i: (0, 0))
    return pl.pallas_call(
        functools.partial(_diff_kernel, lam_init),
        grid=(n_heads, s // TQ),
        in_specs=[pl.BlockSpec((LANES, TQ), lambda h, i: (qrow + h, i)),
                  pl.BlockSpec((s, LANES), lambda h, i: (0, kcol + h)),
                  pl.BlockSpec((s // TK, LANES, TK), lambda h, i: (0, vrow + h, 0)),
                  vec(HEAD_DIM), vec(HEAD_DIM), vec(HEAD_DIM), vec(HEAD_DIM),
                  pl.BlockSpec((LANES, 1), lambda h, i: (0, 0))],
        out_specs=pl.BlockSpec((LANES, TQ), lambda h, i: (h, i)),
        out_shape=jax.ShapeDtypeStruct((n_heads * LANES, s), BF16),
        scratch_shapes=_score_scratch(2) + _flash_scratch(2, LANES),
        compiler_params=_cparams(2),
    )(qt, k, vt, lq1, lk1, lq2, lk2, g_sub)


def _outproj_kernel(oa_ref, ob_ref, oc_ref, wt_ref, x_ref, gt_ref, o_ref):
    wa = oa_ref.shape[0]
    wb = ob_ref.shape[0]
    y_t = (jnp.dot(wt_ref[:, 0:wa], oa_ref[...], preferred_element_type=F32)
           + jnp.dot(wt_ref[:, wa:wa + wb], ob_ref[...], preferred_element_type=F32)
           + jnp.dot(wt_ref[:, wa + wb:], oc_ref[...], preferred_element_type=F32))
    o_ref[...] = x_ref[...] + gt_ref[...] * y_t.T


def _out_projection(oa_t, ob_t, oc_t, wt_bf16, x, gt):
    s, d = x.shape
    tm = 256
    return pl.pallas_call(
        _outproj_kernel,
        grid=(s // tm,),
        in_specs=[pl.BlockSpec((oa_t.shape[0], tm), lambda i: (0, i)),
                  pl.BlockSpec((ob_t.shape[0], tm), lambda i: (0, i)),
                  pl.BlockSpec((oc_t.shape[0], tm), lambda i: (0, i)),
                  pl.BlockSpec(wt_bf16.shape, lambda i: (0, 0)),
                  pl.BlockSpec((tm, d), lambda i: (i, 0)),
                  pl.BlockSpec((1, d), lambda i: (0, 0))],
        out_specs=pl.BlockSpec((tm, d), lambda i: (i, 0)),
        out_shape=jax.ShapeDtypeStruct((s, d), F32),
        compiler_params=_cparams(1),
    )(oa_t, ob_t, oc_t, wt_bf16, x, gt)


def _route_kernel(x_ref, g_ref, sc_ref, sh_ref, wr_ref, br_ref, tri_ref,
                  h_ref, route_ref, count_ref, base_scr):
    i = pl.program_id(0)
    tm = x_ref.shape[0]

    @pl.when(i == 0)
    def _():
        base_scr[...] = jnp.zeros_like(base_scr)

    h = _modulated_norm(x_ref[...], g_ref[...], sc_ref[...], sh_ref[...])
    h_ref[...] = h.astype(BF16)
    logits = _dot3(wr_ref[...], h, NT_DIMS)
    s = _sigmoid(logits)
    sg = s + br_ref[...]
    rows = [sg[r:r + 1, :] for r in range(N_EXPERTS)]

    def top2_sum(vals):
        best = None
        for a in range(len(vals)):
            for b in range(a + 1, len(vals)):
                pair = vals[a] + vals[b]
                best = pair if best is None else jnp.maximum(best, pair)
        return best

    score = [top2_sum(rows[g * EXPERTS_PER_GROUP:(g + 1) * EXPERTS_PER_GROUP]) for g in range(N_GROUPS)]
    best, grp = score[0], jnp.zeros((1, tm), I32)
    for g in range(1, N_GROUPS):
        better = score[g] > best
        grp = jnp.where(better, g, grp)
        best = jnp.where(better, score[g], best)
    cand = []
    for j in range(EXPERTS_PER_GROUP):
        c = rows[j]
        for g in range(1, N_GROUPS):
            c = jnp.where(grp == g, rows[g * EXPERTS_PER_GROUP + j], c)
        cand.append(c)
    b0, l0 = cand[0], jnp.zeros((1, tm), I32)
    for j in range(1, EXPERTS_PER_GROUP):
        better = cand[j] > b0
        l0 = jnp.where(better, j, l0)
        b0 = jnp.where(better, cand[j], b0)
    b1, l1 = jnp.full((1, tm), -jnp.inf, F32), jnp.zeros((1, tm), I32)
    for j in range(EXPERTS_PER_GROUP):
        better = (l0 != j) & (cand[j] > b1)
        l1 = jnp.where(better, j, l1)
        b1 = jnp.where(better, cand[j], b1)
    e0 = grp * EXPERTS_PER_GROUP + l0
    e1 = grp * EXPERTS_PER_GROUP + l1
    erow = lax.broadcasted_iota(I32, (N_EXPERTS, tm), 0)
    sel0 = erow == e0
    sel1 = erow == e1
    s0 = jnp.sum(jnp.where(sel0, s, 0.0), axis=0, keepdims=True)
    s1 = jnp.sum(jnp.where(sel1, s, 0.0), axis=0, keepdims=True)
    tot = s0 + s1
    chosen = jnp.where(sel0 | sel1, 1.0, 0.0).astype(BF16)
    csum = jnp.dot(chosen, tri_ref[...], preferred_element_type=F32)
    pos = base_scr[...] + csum - 1.0
    r0 = jnp.sum(jnp.where(sel0, pos, 0.0), axis=0, keepdims=True)
    r1 = jnp.sum(jnp.where(sel1, pos, 0.0), axis=0, keepdims=True)
    base = base_scr[...] + csum[:, tm - 1:tm]
    base_scr[...] = base
    count_ref[...] = jnp.broadcast_to(base, count_ref.shape)
    zero = jnp.zeros((1, tm), F32)
    for r, val in enumerate((e0.astype(F32), e1.astype(F32), r0, r1, s0 / tot, s1 / tot, zero, zero)):
        route_ref[r:r + 1, :] = val


def _norm_and_route(x, g, sc, sh, w_router, b_router):
    s, d = x.shape
    tm = 512
    tri = jnp.asarray(np.triu(np.ones((tm, tm), np.float32)), BF16)
    row = lambda i: (0, 0)
    return pl.pallas_call(
        _route_kernel,
        grid=(s // tm,),
        in_specs=[pl.BlockSpec((tm, d), lambda i: (i, 0)),
                  pl.BlockSpec((1, d), row), pl.BlockSpec((1, d), row), pl.BlockSpec((1, d), row),
                  pl.BlockSpec((N_EXPERTS, d), row), pl.BlockSpec((N_EXPERTS, 1), row),
                  pl.BlockSpec((tm, tm), row)],
        out_specs=[pl.BlockSpec((tm, d), lambda i: (i, 0)),
                   pl.BlockSpec((8, tm), lambda i: (0, i)),
                   pl.BlockSpec((N_EXPERTS, LANES), row)],
        out_shape=[jax.ShapeDtypeStruct((s, d), BF16),
                   jax.ShapeDtypeStruct((8, s), F32),
                   jax.ShapeDtypeStruct((N_EXPERTS, LANES), F32)],
        scratch_shapes=[pltpu.VMEM((N_EXPERTS, 1), F32)],
        compiler_params=_cparams(1),
    )(x, g, sc, sh, w_router.T, b_router.reshape(N_EXPERTS, 1), tri)


FFN_ROWS = 512


def _ffn_kernel(te_ref, nu_ref, xs_ref, wg_ref, wu_ref, wd_ref, y_ref):
    j = pl.program_id(0)

    @pl.when(j < nu_ref[0])
    def _():
        x = xs_ref[...]
        de = wg_ref.shape[3]
        y = None
        for c in range(de // FFN_CHUNK):
            cs = slice(c * FFN_CHUNK, (c + 1) * FFN_CHUNK)
            a = jnp.dot(x, wg_ref[0, 0, :, cs].astype(BF16), preferred_element_type=F32)
            b = jnp.dot(x, wu_ref[0, 0, :, cs].astype(BF16), preferred_element_type=F32)
            act = (a * _sigmoid(a) * b).astype(BF16)
            part = jnp.dot(act, wd_ref[0, 0, cs, :].astype(BF16), preferred_element_type=F32)
            y = part if y is None else y + part
        y_ref[...] = y.astype(y_ref.dtype)

    @pl.when(j >= nu_ref[0])
    def _():
        y_ref[...] = jnp.zeros_like(y_ref)


FFN_CHUNK = 512


def _expert_ffn(tile_expert, n_used, xs, w_gate, w_up, w_down, layer):
    r, d = xs.shape
    de = w_gate.shape[3]
    expert = lambda j, te, nu: (layer, te[j], 0, 0)
    grid_spec = pltpu.PrefetchScalarGridSpec(
        num_scalar_prefetch=2,
        grid=(r // FFN_ROWS,),
        in_specs=[pl.BlockSpec((FFN_ROWS, d), lambda j, te, nu: (j, 0)),
                  pl.BlockSpec((1, 1, d, de), expert, pipeline_mode=pl.Buffered(1)),
                  pl.BlockSpec((1, 1, d, de), expert, pipeline_mode=pl.Buffered(1)),
                  pl.BlockSpec((1, 1, de, d), expert, pipeline_mode=pl.Buffered(1))],
        out_specs=pl.BlockSpec((FFN_ROWS, d), lambda j, te, nu: (j, 0)),
    )
    return pl.pallas_call(
        _ffn_kernel,
        grid_spec=grid_spec,
        out_shape=jax.ShapeDtypeStruct((r, d), BF16),
        compiler_params=_cparams(1),
    )(tile_expert, n_used, xs, w_gate, w_up, w_down)


def _combine_kernel(final, x_ref, y0_ref, y1_ref, w0_ref, w1_ref, gt_ref, gf_ref, o_ref):
    x = x_ref[...] + gt_ref[...] * (w0_ref[...] * y0_ref[...].astype(F32)
                                    + w1_ref[...] * y1_ref[...].astype(F32))
    if final:
        var = jnp.mean(x * x, axis=-1, keepdims=True)
        x = x * lax.rsqrt(var + RMS_EPS) * gf_ref[...]
    o_ref[...] = x


def _combine(x, y0, y1, w0, w1, gt, g_final, final):
    s, d = x.shape
    tm = 256
    big = pl.BlockSpec((tm, d), lambda i: (i, 0))
    col = pl.BlockSpec((tm, 1), lambda i: (i, 0))
    vec = pl.BlockSpec((1, d), lambda i: (0, 0))
    return pl.pallas_call(
        functools.partial(_combine_kernel, final),
        grid=(s // tm,),
        in_specs=[big, big, big, col, col, vec, vec],
        out_specs=big,
        out_shape=jax.ShapeDtypeStruct((s, d), F32),
        compiler_params=_cparams(1),
    )(x, y0, y1, w0, w1, gt, g_final)


def _rope_tables(positions):
    half = HEAD_DIM // 2
    inv = ROPE_THETA ** (-jnp.arange(0, HEAD_DIM, 2, dtype=F32) / HEAD_DIM)
    ang = positions.astype(F32)[:, None] * inv
    cos, sin = jnp.cos(ang), jnp.sin(ang)
    cos_t = jnp.tile(cos, (1, LANES // half))
    sin_t = jnp.tile(jnp.concatenate([-sin, sin], axis=1), (1, LANES // HEAD_DIM))
    return cos_t, sin_t


def _block_means(ksum):
    nblk, width = ksum.shape
    km = (ksum / MOBA_BLOCK).reshape(nblk, width // HEAD_DIM, HEAD_DIM)
    km = jnp.pad(km, ((0, HEAD_DIM - nblk), (0, 0), (0, 0)))
    return jnp.transpose(km, (1, 0, 2))


def _moe(x1, h2, route, counts, w_gate, w_up, w_down, layer, gt, g_final, final):
    s, d = x1.shape
    e0, e1 = route[0].astype(I32), route[1].astype(I32)
    r0, r1 = route[2].astype(I32), route[3].astype(I32)
    cnt = counts[:, 0].astype(I32)
    cnt_pad = ((cnt + FFN_ROWS - 1) // FFN_ROWS) * FFN_ROWS
    off = jnp.concatenate([jnp.zeros((1,), I32), jnp.cumsum(cnt_pad).astype(I32)])
    d0 = off[e0] + r0
    d1 = off[e1] + r1
    n_rows = 2 * s + N_EXPERTS * FFN_ROWS
    tok = jnp.arange(s, dtype=I32)
    src = (jnp.arange(n_rows, dtype=I32) % s).at[jnp.concatenate([d0, d1])].set(jnp.concatenate([tok, tok]))
    n_tiles = n_rows // FFN_ROWS
    tile_start = jnp.arange(n_tiles, dtype=I32) * FFN_ROWS
    tile_expert = jnp.sum((off[None, 1:] <= tile_start[:, None]).astype(I32), axis=1)
    tile_expert = jnp.minimum(tile_expert, N_EXPERTS - 1).astype(I32)
    n_used = (off[N_EXPERTS] // FFN_ROWS).reshape(1).astype(I32)
    y = _expert_ffn(tile_expert, n_used, h2[src], w_gate, w_up, w_down, layer)
    return _combine(x1, y[d0], y[d1], route[4].reshape(s, 1), route[5].reshape(s, 1), gt, g_final, final)


def kernel(x, c, positions, w_ada, b_ada, g_attn, g_mlp, w_in, w_out, lam_q1, lam_k1, lam_q2, lam_k2,
           g_subln, w_gate, w_up, w_down, w_router, b_router, g_final):
    batch, s, d = x.shape
    assert batch == 1 and s % TQ == 0 and s // MOBA_BLOCK <= HEAD_DIM
    assert w_in.shape[2] == 3 * N_KIND_TILES * PROJ_TN
    depth = w_ada.shape[0]
    xs = x.reshape(s, d)
    cos_t, sin_t = _rope_tables(positions[0])
    mod = _modulation(c, w_ada, b_ada)
    tab = _dilated_bias_table()
    gf = g_final.reshape(1, d)
    row = lambda v: v.reshape(1, -1)
    for l in range(depth):
        lam_init = 0.8 - 0.6 * math.exp(-0.3 * l)
        sh_a, sc_a, gt_a, sh_m, sc_m, gt_m = [mod[l, :, k * d:(k + 1) * d] for k in range(6)]
        k, qt, qbt, vt, ksum = _in_projection(xs, row(g_attn[l]), sc_a, sh_a,
                                              _layer_to_bf16(w_in, l, PROJ_TN), cos_t, sin_t)
        o_a = _dilated_attention(qt, k, vt, tab, 0, 0, 0)
        bias = _moba_gate(qbt, _block_means(ksum.reshape(s // MOBA_BLOCK, -1)))
        o_b = _moba_attention(qt, bias, k, vt, 4, 4, 4)
        o_c = _diff_attention(qt, k, vt, row(lam_q1[l]), row(lam_k1[l]), row(lam_q2[l]), row(lam_k2[l]),
                              g_subln[l].reshape(-1, 1), lam_init, 8, 8, 8)
        x1 = _out_projection(o_a, o_b, o_c, w_out[l].T.astype(BF16), xs, gt_a)
        h2, route, counts = _norm_and_route(x1, row(g_mlp[l]), sc_m, sh_m, w_router, b_router)
        xs = _moe(x1, h2, route, counts, w_gate, w_up, w_down, l, gt_m, gf, l == depth - 1)
    return xs.reshape(batch, s, d)
```

```python
import functools
import math

import numpy as np
import jax
import jax.numpy as jnp
from jax import lax
from jax.experimental import pallas as pl
from jax.experimental.pallas import tpu as pltpu

F32 = jnp.float32
BF16 = jnp.bfloat16
I32 = jnp.int32

HEAD_DIM = 64
LANES = 128
A_PATTERNS = ((128, 1), (512, 4), (2048, 16))
MOBA_BLOCK = 256
MOBA_TOPK = 3
N_EXPERTS = 16
N_GROUPS = 4
EXPERTS_PER_GROUP = N_EXPERTS // N_GROUPS
ROPE_THETA = 10000.0
RMS_EPS = 1e-6
NEG = -1e30
VMEM_LIMIT = 56 * 1024 * 1024

TQ = 1024
TK = 512
KPQ = TQ // TK
assert KPQ == 2
ALL_QUERIES = slice(None)
LATE_QUERIES = slice(TQ // 2, TQ)
PROJ_TM = 512
PROJ_TN = 512
NT_DIMS = (((1,), (1,)), ((), ()))
Q_SCALE = HEAD_DIM ** -0.5 * math.log2(math.e)


def _cparams(n_axes):
    return pltpu.CompilerParams(dimension_semantics=("arbitrary",) * n_axes,
                                vmem_limit_bytes=VMEM_LIMIT)


def _split_bf16(x):
    hi = x.astype(BF16)
    lo = (x - hi.astype(F32)).astype(BF16)
    return hi, lo


def _dot3(a, b, dims=(((1,), (0,)), ((), ()))):
    a_hi, a_lo = _split_bf16(a)
    b_hi, b_lo = _split_bf16(b)
    dg = functools.partial(lax.dot_general, dimension_numbers=dims, preferred_element_type=F32)
    return dg(a_hi, b_hi) + dg(a_hi, b_lo) + dg(a_lo, b_hi)


def _sigmoid(x):
    return 1.0 / (1.0 + jnp.exp(-x))


CAST_STREAMS = 4
CAST_BLOCK_ELEMS = 512 * 1024
CAST_COLS = 1024


def _cast_kernel(*refs):
    o_ref = refs[-1]
    col_tile = o_ref.shape[2]
    for k, x_ref in enumerate(refs[:-1]):
        rows = x_ref.shape[1]
        x = x_ref[0].astype(o_ref.dtype)
        for t in range(o_ref.shape[0]):
            o_ref[t, k * rows:(k + 1) * rows, :] = x[:, t * col_tile:(t + 1) * col_tile]


def _layer_to_bf16(w, l, col_tile=None):
    cols = w.shape[-1]
    col_tile = col_tile or cols
    w3 = w.reshape(w.shape[0], -1, cols)
    rows = w3.shape[1]
    tc = max(min(cols, CAST_COLS), col_tile)
    sr = CAST_BLOCK_ELEMS // tc
    tr = CAST_STREAMS * sr
    stream = lambda k: pl.BlockSpec((1, sr, tc), lambda i, j: (l, CAST_STREAMS * i + k, j))
    return pl.pallas_call(
        _cast_kernel,
        grid=(rows // tr, cols // tc),
        in_specs=[stream(k) for k in range(CAST_STREAMS)],
        out_specs=pl.BlockSpec((tc // col_tile, tr, col_tile), lambda i, j: (j, i, 0)),
        out_shape=jax.ShapeDtypeStruct((cols // col_tile, rows, col_tile), BF16),
        compiler_params=_cparams(2),
    )(*([w3] * CAST_STREAMS))


def _mod_kernel(c_ref, w_ref, b_ref, o_ref):
    c = c_ref[...]
    sc = c * _sigmoid(c)
    d = c.shape[0]
    acc = b_ref[0]
    for r in range(0, d, 256):
        acc = acc + jnp.sum(w_ref[0, r:r + 256, :] * sc[r:r + 256, :], axis=0, keepdims=True)
    o_ref[0] = acc


def _modulation(c, w_ada, b_ada):
    depth, d, n = w_ada.shape
    tn = 512
    return pl.pallas_call(
        _mod_kernel,
        grid=(depth, n // tn),
        in_specs=[pl.BlockSpec((d, 1), lambda l, j: (0, 0)),
                  pl.BlockSpec((1, d, tn), lambda l, j: (l, 0, j)),
                  pl.BlockSpec((1, 1, tn), lambda l, j: (l, 0, j))],
        out_specs=pl.BlockSpec((1, 1, tn), lambda l, j: (l, 0, j)),
        out_shape=jax.ShapeDtypeStruct((depth, 1, n), F32),
        compiler_params=_cparams(2),
    )(c.reshape(d, 1), w_ada, b_ada.reshape(depth, 1, n))


def _modulated_norm(x, g, sc, sh):
    var = jnp.mean(x * x, axis=-1, keepdims=True)
    return (x * lax.rsqrt(var + RMS_EPS) * g) * (1.0 + sc) + sh


def _rope_tile(acc, cos, sin):
    lane = lax.broadcasted_iota(I32, (1, LANES), 1)
    first = (lane & (HEAD_DIM - 1)) < (HEAD_DIM // 2)
    outs = []
    for c in range(acc.shape[1] // LANES):
        xc = acc[:, c * LANES:(c + 1) * LANES]
        rot = jnp.where(first, pltpu.roll(xc, LANES - HEAD_DIM // 2, 1), pltpu.roll(xc, HEAD_DIM // 2, 1))
        outs.append(xc * cos + rot * sin)
    return jnp.concatenate(outs, axis=1)


N_KIND_TILES = 4
COLUMN_TILE_ORDER = (1, 4, 8, 9, 0, 3, 6, 7, 2, 5, 10, 11)
KB_TILE = 1
QB_TILE = N_KIND_TILES + 1


def _inproj_kernel(x_ref, g_ref, sc_ref, sh_ref, w_ref, cos_ref, sin_ref,
                   k_ref, qt_ref, qbt_ref, vt_ref, ksum_ref, h_scr):
    tm = x_ref.shape[0]
    tn = w_ref.shape[2]
    h_scr[...] = _modulated_norm(x_ref[...], g_ref[...], sc_ref[...], sh_ref[...]).astype(BF16)
    for pos, tile in enumerate(COLUMN_TILE_ORDER):
        acc = jnp.dot(h_scr[...], w_ref[tile], preferred_element_type=F32)
        kind, t = divmod(pos, N_KIND_TILES)
        cols = slice(t * tn, (t + 1) * tn)
        if kind == 0:
            r = _rope_tile(acc, cos_ref[...], sin_ref[...])
            k_ref[:, cols] = r.astype(BF16)
            if pos == KB_TILE:
                ksum_ref[0] = jnp.sum(r.reshape(tm // MOBA_BLOCK, MOBA_BLOCK, tn), axis=1)
        elif kind == 1:
            rt = _rope_tile(acc, cos_ref[...], sin_ref[...]).T
            qt_ref[cols, :] = (rt * Q_SCALE).astype(BF16)
            if pos == QB_TILE:
                qbt_ref[...] = rt
        else:
            vt = acc.T.astype(BF16)
            for b in range(vt_ref.shape[0]):
                vt_ref[b, cols, :] = vt[:, b * TK:(b + 1) * TK]


def _in_projection(x, g, sc, sh, w_bf16, cos_t, sin_t):
    s, d = x.shape
    tm, tn = PROJ_TM, w_bf16.shape[2]
    nb = tm // MOBA_BLOCK
    width = N_KIND_TILES * tn
    row = lambda i: (0, 0)
    return pl.pallas_call(
        _inproj_kernel,
        grid=(s // tm,),
        in_specs=[pl.BlockSpec((tm, d), lambda i: (i, 0)),
                  pl.BlockSpec((1, d), row), pl.BlockSpec((1, d), row), pl.BlockSpec((1, d), row),
                  pl.BlockSpec(w_bf16.shape, lambda i: (0, 0, 0),
                               pipeline_mode=pl.Buffered(1)),
                  pl.BlockSpec((tm, LANES), lambda i: (i, 0)),
                  pl.BlockSpec((tm, LANES), lambda i: (i, 0))],
        out_specs=[pl.BlockSpec((tm, width), lambda i: (i, 0)),
                   pl.BlockSpec((width, tm), lambda i: (0, i)),
                   pl.BlockSpec((tn, tm), lambda i: (0, i)),
                   pl.BlockSpec((tm // TK, width, TK), lambda i: (i, 0, 0)),
                   pl.BlockSpec((1, nb, tn), lambda i: (i, 0, 0))],
        out_shape=[jax.ShapeDtypeStruct((s, width), BF16),
                   jax.ShapeDtypeStruct((width, s), BF16),
                   jax.ShapeDtypeStruct((tn, s), F32),
                   jax.ShapeDtypeStruct((s // TK, width, TK), BF16),
                   jax.ShapeDtypeStruct((s // tm, nb, tn), F32)],
        scratch_shapes=[pltpu.VMEM((tm, d), BF16)],
        compiler_params=_cparams(1),
    )(x, g, sc, sh, w_bf16, cos_t, sin_t)


def _flash_step(s_t, v_t, state):
    m_prev, l_prev, acc_prev = state
    m_new = jnp.maximum(m_prev, jnp.max(s_t, axis=0, keepdims=True))
    alpha = jnp.exp2(m_prev - m_new)
    p = jnp.exp2(s_t - m_new)
    l_new = alpha * l_prev + jnp.sum(p, axis=0, keepdims=True)
    acc_new = alpha * acc_prev + jnp.dot(v_t, p.astype(BF16), preferred_element_type=F32)
    return m_new, l_new, acc_new


def _sweep(first, n_loop, scores, consume, s_a, s_b, tail_masks):
    def park(buf, vals, cols=ALL_QUERIES):
        for h, val in enumerate(vals):
            buf[h, :, cols] = val

    def fetch(buf, cols=ALL_QUERIES):
        return [buf[h, :, cols] for h in range(buf.shape[0])]

    park(s_a, scores(first, ALL_QUERIES))

    def pair(j, carry):
        a = first + 2 * j
        park(s_b, scores(a + 1, ALL_QUERIES))
        consume(a, fetch(s_a), None, ALL_QUERIES)
        park(s_a, scores(a + 2, ALL_QUERIES))
        consume(a + 1, fetch(s_b), None, ALL_QUERIES)
        return carry

    lax.fori_loop(0, n_loop // 2, pair, 0)
    t0 = first + n_loop
    park(s_b, scores(t0 + 1, LATE_QUERIES), LATE_QUERIES)
    consume(t0, fetch(s_a), tail_masks[0], ALL_QUERIES)
    last_mask = None if tail_masks[1] is None else tail_masks[1][:, LATE_QUERIES]
    consume(t0 + 1, fetch(s_b, LATE_QUERIES), last_mask, LATE_QUERIES)


def _score_scratch(n_streams):
    return [pltpu.VMEM((n_streams, TK, TQ), F32), pltpu.VMEM((n_streams, TK, TQ), F32)]


def _qk_t(k, q_t):
    return jnp.dot(k, q_t, preferred_element_type=F32)


def _k_block(k_ref, n):
    return k_ref[pl.ds(pl.multiple_of(n * TK, TK), TK), :]


def _init_state(refs):
    m_ref, l_ref, acc_ref = refs
    m_ref[...] = jnp.full(m_ref.shape, NEG, F32)
    l_ref[...] = jnp.zeros(l_ref.shape, F32)
    acc_ref[...] = jnp.zeros(acc_ref.shape, F32)


def _load_state(refs, cols=ALL_QUERIES):
    return tuple(r[:, cols] for r in refs)


def _store_state(refs, state, cols=ALL_QUERIES):
    for r, val in zip(refs, state):
        r[:, cols] = val


def _flash_scratch(n_streams, dv):
    out = []
    for _ in range(n_streams):
        out += [pltpu.VMEM((1, TQ), F32), pltpu.VMEM((1, TQ), F32), pltpu.VMEM((dv, TQ), F32)]
    return out


def _causal_t(d):
    r = lax.broadcasted_iota(I32, (TK, TQ), 0) + d * TK
    return r <= lax.broadcasted_iota(I32, (TK, TQ), 1)


def _stack_rows(top, bottom):
    return jnp.concatenate([top, bottom], axis=0)


def _dilated_bias_table():
    n_off = A_PATTERNS[-1][0] // TK + KPQ
    r = np.arange(TK)[:, None]
    c = np.arange(TQ)[None, :]
    tabs = []
    for t in range(n_off):
        delta = (t - (KPQ - 1)) * TK + c - r
        mult = np.zeros_like(delta)
        for window, dil in A_PATTERNS:
            mult += ((delta >= 0) & (delta % dil == 0) & (delta <= window)).astype(delta.dtype)
        tabs.append(np.where(mult > 0, np.log2(np.maximum(mult, 1).astype(np.float64)), NEG))
    return jnp.asarray(np.stack(tabs), F32)


def _dilated_kernel(qt_ref, k_ref, vt_ref, tab_ref, o_ref, s_a, s_b, *scr):
    i = pl.program_id(1)
    refs = (scr[0:3], scr[3:6])
    q_t = qt_ref[...]
    zero = jnp.zeros((HEAD_DIM, TQ), BF16)
    qs = (_stack_rows(q_t[:HEAD_DIM], zero), _stack_rows(zero, q_t[HEAD_DIM:]))
    n_off = tab_ref.shape[0]
    last = KPQ * i + (KPQ - 1)
    for h in range(2):
        _init_state(refs[h])

    def scores(n, cols):
        kb = _k_block(k_ref, n)
        bias = tab_ref[last - n, :, cols]
        return [_qk_t(kb, qs[h][:, cols]) + bias for h in range(2)]

    def consume(n, s_list, mask, cols):
        state = [_load_state(refs[h], cols) for h in range(2)]
        vb = vt_ref[n]
        for h in range(2):
            state[h] = _flash_step(s_list[h], vb[h * HEAD_DIM:(h + 1) * HEAD_DIM, :], state[h])
        for h in range(2):
            _store_state(refs[h], state[h], cols)

    first = jnp.maximum(last - (n_off - 1), 0)
    _sweep(first, last - 1 - first, scores, consume, s_a, s_b, (None, None))
    for h in range(2):
        _, l, acc = _load_state(refs[h])
        o_ref[h * HEAD_DIM:(h + 1) * HEAD_DIM, :] = (acc / l).astype(o_ref.dtype)


def _dilated_attention(qt, k, vt, tab, qrow, kcol, vrow):
    s = k.shape[0]
    n_pairs = 4
    return pl.pallas_call(
        _dilated_kernel,
        grid=(n_pairs, s // TQ),
        in_specs=[pl.BlockSpec((LANES, TQ), lambda hp, i: (qrow + hp, i)),
                  pl.BlockSpec((s, LANES), lambda hp, i: (0, kcol + hp)),
                  pl.BlockSpec((s // TK, LANES, TK), lambda hp, i: (0, vrow + hp, 0)),
                  pl.BlockSpec(tab.shape, lambda hp, i: (0, 0, 0), pipeline_mode=pl.Buffered(1))],
        out_specs=pl.BlockSpec((LANES, TQ), lambda hp, i: (hp, i)),
        out_shape=jax.ShapeDtypeStruct((n_pairs * LANES, s), BF16),
        scratch_shapes=_score_scratch(2) + _flash_scratch(2, HEAD_DIM),
        compiler_params=_cparams(2),
    )(qt, k, vt, tab)


def _moba_gate_kernel(qt_ref, km_ref, bias_ref):
    i = pl.program_id(0)
    tg = qt_ref.shape[1]
    n_heads, nblk, _ = km_ref.shape
    blk = lax.broadcasted_iota(I32, (nblk, tg), 0)
    blk_f = blk.astype(F32)
    tok = i * tg + lax.broadcasted_iota(I32, (nblk, tg), 1)
    own = jnp.right_shift(tok, int(math.log2(MOBA_BLOCK)))
    past = blk < own
    for h in range(n_heads):
        rows = slice(h * HEAD_DIM, (h + 1) * HEAD_DIM)
        gate = _dot3(km_ref[h], qt_ref[rows, :])
        sel = blk == own
        for _ in range(MOBA_TOPK):
            cand = past & jnp.logical_not(sel)
            g = jnp.where(cand, gate, -jnp.inf)
            mx = jnp.max(g, axis=0, keepdims=True)
            first = jnp.min(jnp.where(cand & (g == mx), blk_f, 2.0 * nblk), axis=0, keepdims=True)
            sel = sel | (blk_f == first)
        bias_ref[rows, :] = jnp.where(sel, 0.0, NEG).astype(BF16)


def _moba_gate(qbt, kmean):
    width, s = qbt.shape
    tg = 512
    return pl.pallas_call(
        _moba_gate_kernel,
        grid=(s // tg,),
        in_specs=[pl.BlockSpec((width, tg), lambda i: (0, i)),
                  pl.BlockSpec(kmean.shape, lambda i: (0, 0, 0))],
        out_specs=pl.BlockSpec((width, tg), lambda i: (0, i)),
        out_shape=jax.ShapeDtypeStruct((width, s), BF16),
        compiler_params=_cparams(1),
    )(qbt, kmean)


def _moba_kernel(qt_ref, bias_ref, k_ref, vt_ref, o_ref, s_a, s_b, *scr):
    i = pl.program_id(1)
    refs = (scr[0:3], scr[3:6])
    q_t = qt_ref[...]
    b_t = bias_ref[...]
    qs = (_stack_rows(q_t[:HEAD_DIM], b_t[:HEAD_DIM]), _stack_rows(b_t[HEAD_DIM:], q_t[HEAD_DIM:]))
    lane = lax.broadcasted_iota(I32, (1, LANES), 1)
    lo = lane < HEAD_DIM
    mine = (lo, jnp.logical_not(lo))
    blk = lane & (HEAD_DIM - 1)
    sub_block = jnp.right_shift(lax.broadcasted_iota(I32, (TK, LANES), 0), int(math.log2(MOBA_BLOCK)))
    for h in range(2):
        _init_state(refs[h])

    def scores(n, cols):
        kb = _k_block(k_ref, n)
        onehot = jnp.where(blk == n * (TK // MOBA_BLOCK) + sub_block, 1.0, 0.0).astype(BF16)
        return [_qk_t(jnp.where(mine[h], kb, onehot), qs[h][:, cols]) for h in range(2)]

    def consume(n, s_list, mask, cols):
        state = [_load_state(refs[h], cols) for h in range(2)]
        vb = vt_ref[n]
        for h in range(2):
            s_t = s_list[h] if mask is None else jnp.where(mask, s_list[h], NEG)
            state[h] = _flash_step(s_t, vb[h * HEAD_DIM:(h + 1) * HEAD_DIM, :], state[h])
        for h in range(2):
            _store_state(refs[h], state[h], cols)

    _sweep(0, KPQ * i, scores, consume, s_a, s_b, (_causal_t(0), _causal_t(1)))
    for h in range(2):
        _, l, acc = _load_state(refs[h])
        o_ref[h * HEAD_DIM:(h + 1) * HEAD_DIM, :] = (acc / l).astype(o_ref.dtype)


def _moba_attention(qt, bias, k, vt, qrow, kcol, vrow):
    s = k.shape[0]
    n_pairs = 4
    return pl.pallas_call(
        _moba_kernel,
        grid=(n_pairs, s // TQ),
        in_specs=[pl.BlockSpec((LANES, TQ), lambda hp, i: (qrow + hp, i)),
                  pl.BlockSpec((LANES, TQ), lambda hp, i: (hp, i)),
                  pl.BlockSpec((s, LANES), lambda hp, i: (0, kcol + hp)),
                  pl.BlockSpec((s // TK, LANES, TK), lambda hp, i: (0, vrow + hp, 0))],
        out_specs=pl.BlockSpec((LANES, TQ), lambda hp, i: (hp, i)),
        out_shape=jax.ShapeDtypeStruct((n_pairs * LANES, s), BF16),
        scratch_shapes=_score_scratch(2) + _flash_scratch(2, HEAD_DIM),
        compiler_params=_cparams(2),
    )(qt, bias, k, vt)


def _diff_kernel(lam_init, qt_ref, k_ref, vt_ref, lq1_ref, lk1_ref, lq2_ref, lk2_ref, g_ref, o_ref,
                 s_a, s_b, *scr):
    i = pl.program_id(1)
    refs = (scr[0:3], scr[3:6])
    q_t = qt_ref[...]
    zero = jnp.zeros((HEAD_DIM, TQ), BF16)
    qs = (_stack_rows(q_t[:HEAD_DIM], zero), _stack_rows(zero, q_t[HEAD_DIM:]))
    for h in range(2):
        _init_state(refs[h])

    def scores(n, cols):
        kb = _k_block(k_ref, n)
        return [_qk_t(kb, qs[h][:, cols]) for h in range(2)]

    def consume(n, s_list, mask, cols):
        state = [_load_state(refs[h], cols) for h in range(2)]
        vb = vt_ref[n]
        for h in range(2):
            s_t = s_list[h] if mask is None else jnp.where(mask, s_list[h], NEG)
            state[h] = _flash_step(s_t, vb, state[h])
        for h in range(2):
            _store_state(refs[h], state[h], cols)

    _sweep(0, KPQ * i, scores, consume, s_a, s_b, (_causal_t(0), _causal_t(1)))
    lam = (jnp.exp(jnp.sum(lq1_ref[...] * lk1_ref[...], axis=1, keepdims=True))
           - jnp.exp(jnp.sum(lq2_ref[...] * lk2_ref[...], axis=1, keepdims=True)) + lam_init)
    _, l1, acc1 = _load_state(refs[0])
    _, l2, acc2 = _load_state(refs[1])
    o = acc1 / l1 - lam * (acc2 / l2)
    var = jnp.mean(o * o, axis=0, keepdims=True)
    o = (o * lax.rsqrt(var + RMS_EPS) * g_ref[...]) * (1.0 - lam_init)
    o_ref[...] = o.astype(o_ref.dtype)


def _diff_attention(qt, k, vt, lq1, lk1, lq2, lk2, g_sub, lam_init, qrow, kcol, vrow):
    s = k.shape[0]
    n_heads = 8
    vec = lambda n: pl.BlockSpec((1, n), lambda h, i: (0, 0))
    return pl.pallas_call(
        functools.partial(_diff_kernel, lam_init),
        grid=(n_heads, s // TQ),
        in_specs=[pl.BlockSpec((LANES, TQ), lambda h, i: (qrow + h, i)),
                  pl.BlockSpec((s, LANES), lambda h, i: (0, kcol + h)),
                  pl.BlockSpec((s // TK, LANES, TK), lambda h, i: (0, vrow + h, 0)),
                  vec(HEAD_DIM), vec(HEAD_DIM), vec(HEAD_DIM), vec(HEAD_DIM),
                  pl.BlockSpec((LANES, 1), lambda h, i: (0, 0))],
        out_specs=pl.BlockSpec((LANES, TQ), lambda h, i: (h, i)),
        out_shape=jax.ShapeDtypeStruct((n_heads * LANES, s), BF16),
        scratch_shapes=_score_scratch(2) + _flash_scratch(2, LANES),
        compiler_params=_cparams(2),
    )(qt, k, vt, lq1, lk1, lq2, lk2, g_sub)


def _outproj_kernel(oa_ref, ob_ref, oc_ref, wt_ref, x_ref, gt_ref, o_ref):
    wa = oa_ref.shape[0]
    wb = ob_ref.shape[0]
    y_t = (jnp.dot(wt_ref[:, 0:wa], oa_ref[...], preferred_element_type=F32)
           + jnp.dot(wt_ref[:, wa:wa + wb], ob_ref[...], preferred_element_type=F32)
           + jnp.dot(wt_ref[:, wa + wb:], oc_ref[...], preferred_element_type=F32))
    o_ref[...] = x_ref[...] + gt_ref[...] * y_t.T


def _out_projection(oa_t, ob_t, oc_t, wt_bf16, x, gt):
    s, d = x.shape
    tm = 256
    return pl.pallas_call(
        _outproj_kernel,
        grid=(s // tm,),
        in_specs=[pl.BlockSpec((oa_t.shape[0], tm), lambda i: (0, i)),
                  pl.BlockSpec((ob_t.shape[0], tm), lambda i: (0, i)),
                  pl.BlockSpec((oc_t.shape[0], tm), lambda i: (0, i)),
                  pl.BlockSpec(wt_bf16.shape, lambda i: (0, 0)),
                  pl.BlockSpec((tm, d), lambda i: (i, 0)),
                  pl.BlockSpec((1, d), lambda i: (0, 0))],
        out_specs=pl.BlockSpec((tm, d), lambda i: (i, 0)),
        out_shape=jax.ShapeDtypeStruct((s, d), F32),
        compiler_params=_cparams(1),
    )(oa_t, ob_t, oc_t, wt_bf16, x, gt)


def _route_kernel(x_ref, g_ref, sc_ref, sh_ref, wr_ref, br_ref, tri_ref,
                  h_ref, route_ref, count_ref, base_scr):
    i = pl.program_id(0)
    tm = x_ref.shape[0]

    @pl.when(i == 0)
    def _():
        base_scr[...] = jnp.zeros_like(base_scr)

    h = _modulated_norm(x_ref[...], g_ref[...], sc_ref[...], sh_ref[...])
    h_ref[...] = h.astype(BF16)
    logits = _dot3(wr_ref[...], h, NT_DIMS)
    s = _sigmoid(logits)
    sg = s + br_ref[...]
    rows = [sg[r:r + 1, :] for r in range(N_EXPERTS)]

    def top2_sum(vals):
        best = None
        for a in range(len(vals)):
            for b in range(a + 1, len(vals)):
                pair = vals[a] + vals[b]
                best = pair if best is None else jnp.maximum(best, pair)
        return best

    score = [top2_sum(rows[g * EXPERTS_PER_GROUP:(g + 1) * EXPERTS_PER_GROUP]) for g in range(N_GROUPS)]
    best, grp = score[0], jnp.zeros((1, tm), I32)
    for g in range(1, N_GROUPS):
        better = score[g] > best
        grp = jnp.where(better, g, grp)
        best = jnp.where(better, score[g], best)
    cand = []
    for j in range(EXPERTS_PER_GROUP):
        c = rows[j]
        for g in range(1, N_GROUPS):
            c = jnp.where(grp == g, rows[g * EXPERTS_PER_GROUP + j], c)
        cand.append(c)
    b0, l0 = cand[0], jnp.zeros((1, tm), I32)
    for j in range(1, EXPERTS_PER_GROUP):
        better = cand[j] > b0
        l0 = jnp.where(better, j, l0)
        b0 = jnp.where(better, cand[j], b0)
    b1, l1 = jnp.full((1, tm), -jnp.inf, F32), jnp.zeros((1, tm), I32)
    for j in range(EXPERTS_PER_GROUP):
        better = (l0 != j) & (cand[j] > b1)
        l1 = jnp.where(better, j, l1)
        b1 = jnp.where(better, cand[j], b1)
    e0 = grp * EXPERTS_PER_GROUP + l0
    e1 = grp * EXPERTS_PER_GROUP + l1
    erow = lax.broadcasted_iota(I32, (N_EXPERTS, tm), 0)
    sel0 = erow == e0
    sel1 = erow == e1
    s0 = jnp.sum(jnp.where(sel0, s, 0.0), axis=0, keepdims=True)
    s1 = jnp.sum(jnp.where(sel1, s, 0.0), axis=0, keepdims=True)
    tot = s0 + s1
    chosen = jnp.where(sel0 | sel1, 1.0, 0.0).astype(BF16)
    csum = jnp.dot(chosen, tri_ref[...], preferred_element_type=F32)
    pos = base_scr[...] + csum - 1.0
    r0 = jnp.sum(jnp.where(sel0, pos, 0.0), axis=0, keepdims=True)
    r1 = jnp.sum(jnp.where(sel1, pos, 0.0), axis=0, keepdims=True)
    base = base_scr[...] + csum[:, tm - 1:tm]
    base_scr[...] = base
    count_ref[...] = jnp.broadcast_to(base, count_ref.shape)
    zero = jnp.zeros((1, tm), F32)
    for r, val in enumerate((e0.astype(F32), e1.astype(F32), r0, r1, s0 / tot, s1 / tot, zero, zero)):
        route_ref[r:r + 1, :] = val


def _norm_and_route(x, g, sc, sh, w_router, b_router):
    s, d = x.shape
    tm = 512
    tri = jnp.asarray(np.triu(np.ones((tm, tm), np.float32)), BF16)
    row = lambda i: (0, 0)
    return pl.pallas_call(
        _route_kernel,
        grid=(s // tm,),
        in_specs=[pl.BlockSpec((tm, d), lambda i: (i, 0)),
                  pl.BlockSpec((1, d), row), pl.BlockSpec((1, d), row), pl.BlockSpec((1, d), row),
                  pl.BlockSpec((N_EXPERTS, d), row), pl.BlockSpec((N_EXPERTS, 1), row),
                  pl.BlockSpec((tm, tm), row)],
        out_specs=[pl.BlockSpec((tm, d), lambda i: (i, 0)),
                   pl.BlockSpec((8, tm), lambda i: (0, i)),
                   pl.BlockSpec((N_EXPERTS, LANES), row)],
        out_shape=[jax.ShapeDtypeStruct((s, d), BF16),
                   jax.ShapeDtypeStruct((8, s), F32),
                   jax.ShapeDtypeStruct((N_EXPERTS, LANES), F32)],
        scratch_shapes=[pltpu.VMEM((N_EXPERTS, 1), F32)],
        compiler_params=_cparams(1),
    )(x, g, sc, sh, w_router.T, b_router.reshape(N_EXPERTS, 1), tri)


FFN_ROWS = 512


def _ffn_kernel(te_ref, nu_ref, xs_ref, wg_ref, wu_ref, wd_ref, y_ref):
    j = pl.program_id(0)

    @pl.when(j < nu_ref[0])
    def _():
        x = xs_ref[...]
        de = wg_ref.shape[3]
        y = None
        for c in range(de // FFN_CHUNK):
            cs = slice(c * FFN_CHUNK, (c + 1) * FFN_CHUNK)
            a = jnp.dot(x, wg_ref[0, 0, :, cs].astype(BF16), preferred_element_type=F32)
            b = jnp.dot(x, wu_ref[0, 0, :, cs].astype(BF16), preferred_element_type=F32)
            act = (a * _sigmoid(a) * b).astype(BF16)
            part = jnp.dot(act, wd_ref[0, 0, cs, :].astype(BF16), preferred_element_type=F32)
            y = part if y is None else y + part
        y_ref[...] = y.astype(y_ref.dtype)

    @pl.when(j >= nu_ref[0])
    def _():
        y_ref[...] = jnp.zeros_like(y_ref)


FFN_CHUNK = 512


def _expert_ffn(tile_expert, n_used, xs, w_gate, w_up, w_down, layer):
    r, d = xs.shape
    de = w_gate.shape[3]
    expert = lambda j, te, nu: (layer, te[j], 0, 0)
    grid_spec = pltpu.PrefetchScalarGridSpec(
        num_scalar_prefetch=2,
        grid=(r // FFN_ROWS,),
        in_specs=[pl.BlockSpec((FFN_ROWS, d), lambda j, te, nu: (j, 0)),
                  pl.BlockSpec((1, 1, d, de), expert, pipeline_mode=pl.Buffered(1)),
                  pl.BlockSpec((1, 1, d, de), expert, pipeline_mode=pl.Buffered(1)),
                  pl.BlockSpec((1, 1, de, d), expert, pipeline_mode=pl.Buffered(1))],
        out_specs=pl.BlockSpec((FFN_ROWS, d), lambda j, te, nu: (j, 0)),
    )
    return pl.pallas_call(
        _ffn_kernel,
        grid_spec=grid_spec,
        out_shape=jax.ShapeDtypeStruct((r, d), BF16),
        compiler_params=_cparams(1),
    )(tile_expert, n_used, xs, w_gate, w_up, w_down)


def _combine_kernel(final, x_ref, y0_ref, y1_ref, w0_ref, w1_ref, gt_ref, gf_ref, o_ref):
    x = x_ref[...] + gt_ref[...] * (w0_ref[...] * y0_ref[...].astype(F32)
                                    + w1_ref[...] * y1_ref[...].astype(F32))
    if final:
        var = jnp.mean(x * x, axis=-1, keepdims=True)
        x = x * lax.rsqrt(var + RMS_EPS) * gf_ref[...]
    o_ref[...] = x


def _combine(x, y01, w0, w1, gt, g_final, final):
    s, d = x.shape
    tm = 256
    big = pl.BlockSpec((tm, d), lambda i: (i, 0))
    second = pl.BlockSpec((tm, d), lambda i: (i + s // tm, 0))
    col = pl.BlockSpec((tm, 1), lambda i: (i, 0))
    vec = pl.BlockSpec((1, d), lambda i: (0, 0))
    return pl.pallas_call(
        functools.partial(_combine_kernel, final),
        grid=(s // tm,),
        in_specs=[big, big, second, col, col, vec, vec],
        out_specs=big,
        out_shape=jax.ShapeDtypeStruct((s, d), F32),
        compiler_params=_cparams(1),
    )(x, y01, y01, w0, w1, gt, g_final)


def _rope_tables(positions):
    half = HEAD_DIM // 2
    inv = ROPE_THETA ** (-jnp.arange(0, HEAD_DIM, 2, dtype=F32) / HEAD_DIM)
    ang = positions.astype(F32)[:, None] * inv
    cos, sin = jnp.cos(ang), jnp.sin(ang)
    cos_t = jnp.tile(cos, (1, LANES // half))
    sin_t = jnp.tile(jnp.concatenate([-sin, sin], axis=1), (1, LANES // HEAD_DIM))
    return cos_t, sin_t


def _block_means(ksum):
    nblk, width = ksum.shape
    km = (ksum / MOBA_BLOCK).reshape(nblk, width // HEAD_DIM, HEAD_DIM)
    km = jnp.pad(km, ((0, HEAD_DIM - nblk), (0, 0), (0, 0)))
    return jnp.transpose(km, (1, 0, 2))


def _moe(x1, h2, route, counts, w_gate, w_up, w_down, layer, gt, g_final, final):
    s, d = x1.shape
    e0, e1 = route[0].astype(I32), route[1].astype(I32)
    r0, r1 = route[2].astype(I32), route[3].astype(I32)
    cnt = counts[:, 0].astype(I32)
    cnt_pad = ((cnt + FFN_ROWS - 1) // FFN_ROWS) * FFN_ROWS
    off = jnp.concatenate([jnp.zeros((1,), I32), jnp.cumsum(cnt_pad).astype(I32)])
    d0 = off[e0] + r0
    d1 = off[e1] + r1
    n_rows = 2 * s + N_EXPERTS * FFN_ROWS
    tok = jnp.arange(s, dtype=I32)
    src = (jnp.arange(n_rows, dtype=I32) % s).at[jnp.concatenate([d0, d1])].set(jnp.concatenate([tok, tok]))
    n_tiles = n_rows // FFN_ROWS
    tile_start = jnp.arange(n_tiles, dtype=I32) * FFN_ROWS
    tile_expert = jnp.sum((off[None, 1:] <= tile_start[:, None]).astype(I32), axis=1)
    tile_expert = jnp.minimum(tile_expert, N_EXPERTS - 1).astype(I32)
    n_used = (off[N_EXPERTS] // FFN_ROWS).reshape(1).astype(I32)
    y = _expert_ffn(tile_expert, n_used, h2[src], w_gate, w_up, w_down, layer)
    y01 = y[jnp.concatenate([d0, d1])]
    return _combine(x1, y01, route[4].reshape(s, 1), route[5].reshape(s, 1), gt, g_final, final)


def kernel(x, c, positions, w_ada, b_ada, g_attn, g_mlp, w_in, w_out, lam_q1, lam_k1, lam_q2, lam_k2,
           g_subln, w_gate, w_up, w_down, w_router, b_router, g_final):
    batch, s, d = x.shape
    assert batch == 1 and s % TQ == 0 and s // MOBA_BLOCK <= HEAD_DIM
    assert w_in.shape[2] == 3 * N_KIND_TILES * PROJ_TN
    depth = w_ada.shape[0]
    xs = x.reshape(s, d)
    cos_t, sin_t = _rope_tables(positions[0])
    mod = _modulation(c, w_ada, b_ada)
    tab = _dilated_bias_table()
    gf = g_final.reshape(1, d)
    row = lambda v: v.reshape(1, -1)
    for l in range(depth):
        lam_init = 0.8 - 0.6 * math.exp(-0.3 * l)
        sh_a, sc_a, gt_a, sh_m, sc_m, gt_m = [mod[l, :, k * d:(k + 1) * d] for k in range(6)]
        k, qt, qbt, vt, ksum = _in_projection(xs, row(g_attn[l]), sc_a, sh_a,
                                              _layer_to_bf16(w_in, l, PROJ_TN), cos_t, sin_t)
        o_a = _dilated_attention(qt, k, vt, tab, 0, 0, 0)
        bias = _moba_gate(qbt, _block_means(ksum.reshape(s // MOBA_BLOCK, -1)))
        o_b = _moba_attention(qt, bias, k, vt, 4, 4, 4)
        o_c = _diff_attention(qt, k, vt, row(lam_q1[l]), row(lam_k1[l]), row(lam_q2[l]), row(lam_k2[l]),
                              g_subln[l].reshape(-1, 1), lam_init, 8, 8, 8)
        x1 = _out_projection(o_a, o_b, o_c, w_out[l].T.astype(BF16), xs, gt_a)
        h2, route, counts = _norm_and_route(x1, row(g_mlp[l]), sc_m, sh_m, w_router, b_router)
        xs = _moe(x1, h2, route, counts, w_gate, w_up, w_down, l, gt_m, gf, l == depth - 1)
    return xs.reshape(batch, s, d)
```

```python
import functools
import math

import numpy as np
import jax
import jax.numpy as jnp
from jax import lax
from jax.experimental import pallas as pl
from jax.experimental.pallas import tpu as pltpu

F32 = jnp.float32
BF16 = jnp.bfloat16
I32 = jnp.int32

HEAD_DIM = 64
LANES = 128
A_PATTERNS = ((128, 1), (512, 4), (2048, 16))
MOBA_BLOCK = 256
MOBA_TOPK = 3
N_EXPERTS = 16
N_GROUPS = 4
EXPERTS_PER_GROUP = N_EXPERTS // N_GROUPS
ROPE_THETA = 10000.0
RMS_EPS = 1e-6
NEG = -1e30
VMEM_LIMIT = 56 * 1024 * 1024

TQ = 1024
TK = 512
KPQ = TQ // TK
assert KPQ == 2
ALL_QUERIES = slice(None)
LATE_QUERIES = slice(TQ // 2, TQ)
PROJ_TM = 512
PROJ_TN = 512
NT_DIMS = (((1,), (1,)), ((), ()))
Q_SCALE = HEAD_DIM ** -0.5 * math.log2(math.e)


def _cparams(n_axes):
    return pltpu.CompilerParams(dimension_semantics=("arbitrary",) * n_axes,
                                vmem_limit_bytes=VMEM_LIMIT)


def _split_bf16(x):
    hi = x.astype(BF16)
    lo = (x - hi.astype(F32)).astype(BF16)
    return hi, lo


def _dot3(a, b, dims=(((1,), (0,)), ((), ()))):
    a_hi, a_lo = _split_bf16(a)
    b_hi, b_lo = _split_bf16(b)
    dg = functools.partial(lax.dot_general, dimension_numbers=dims, preferred_element_type=F32)
    return dg(a_hi, b_hi) + dg(a_hi, b_lo) + dg(a_lo, b_hi)


def _sigmoid(x):
    return 1.0 / (1.0 + jnp.exp(-x))


CAST_STREAMS = 4
CAST_BLOCK_ELEMS = 512 * 1024
CAST_COLS = 1024


def _cast_kernel(*refs):
    o_ref = refs[-1]
    col_tile = o_ref.shape[2]
    for k, x_ref in enumerate(refs[:-1]):
        rows = x_ref.shape[1]
        x = x_ref[0].astype(o_ref.dtype)
        for t in range(o_ref.shape[0]):
            o_ref[t, k * rows:(k + 1) * rows, :] = x[:, t * col_tile:(t + 1) * col_tile]


def _layer_to_bf16(w, l, col_tile=None):
    cols = w.shape[-1]
    col_tile = col_tile or cols
    w3 = w.reshape(w.shape[0], -1, cols)
    rows = w3.shape[1]
    tc = max(min(cols, CAST_COLS), col_tile)
    sr = CAST_BLOCK_ELEMS // tc
    tr = CAST_STREAMS * sr
    stream = lambda k: pl.BlockSpec((1, sr, tc), lambda i, j: (l, CAST_STREAMS * i + k, j))
    return pl.pallas_call(
        _cast_kernel,
        grid=(rows // tr, cols // tc),
        in_specs=[stream(k) for k in range(CAST_STREAMS)],
        out_specs=pl.BlockSpec((tc // col_tile, tr, col_tile), lambda i, j: (j, i, 0)),
        out_shape=jax.ShapeDtypeStruct((cols // col_tile, rows, col_tile), BF16),
        compiler_params=_cparams(2),
    )(*([w3] * CAST_STREAMS))


def _mod_kernel(c_ref, w_ref, b_ref, o_ref):
    c = c_ref[...]
    sc = c * _sigmoid(c)
    d = c.shape[0]
    acc = b_ref[0]
    for r in range(0, d, 256):
        acc = acc + jnp.sum(w_ref[0, r:r + 256, :] * sc[r:r + 256, :], axis=0, keepdims=True)
    o_ref[0] = acc


def _modulation(c, w_ada, b_ada):
    depth, d, n = w_ada.shape
    tn = 512
    return pl.pallas_call(
        _mod_kernel,
        grid=(depth, n // tn),
        in_specs=[pl.BlockSpec((d, 1), lambda l, j: (0, 0)),
                  pl.BlockSpec((1, d, tn), lambda l, j: (l, 0, j)),
                  pl.BlockSpec((1, 1, tn), lambda l, j: (l, 0, j))],
        out_specs=pl.BlockSpec((1, 1, tn), lambda l, j: (l, 0, j)),
        out_shape=jax.ShapeDtypeStruct((depth, 1, n), F32),
        compiler_params=_cparams(2),
    )(c.reshape(d, 1), w_ada, b_ada.reshape(depth, 1, n))


def _modulated_norm(x, g, sc, sh):
    var = jnp.mean(x * x, axis=-1, keepdims=True)
    return (x * lax.rsqrt(var + RMS_EPS) * g) * (1.0 + sc) + sh


def _rope_tile(acc, cos, sin):
    lane = lax.broadcasted_iota(I32, (1, LANES), 1)
    first = (lane & (HEAD_DIM - 1)) < (HEAD_DIM // 2)
    outs = []
    for c in range(acc.shape[1] // LANES):
        xc = acc[:, c * LANES:(c + 1) * LANES]
        rot = jnp.where(first, pltpu.roll(xc, LANES - HEAD_DIM // 2, 1), pltpu.roll(xc, HEAD_DIM // 2, 1))
        outs.append(xc * cos + rot * sin)
    return jnp.concatenate(outs, axis=1)


N_KIND_TILES = 4
COLUMN_TILE_ORDER = (1, 4, 8, 9, 0, 3, 6, 7, 2, 5, 10, 11)
KB_TILE = 1
QB_TILE = N_KIND_TILES + 1


def _inproj_kernel(x_ref, g_ref, sc_ref, sh_ref, w_ref, cos_ref, sin_ref,
                   k_ref, qt_ref, qbt_ref, vt_ref, ksum_ref, h_scr):
    tm = x_ref.shape[0]
    tn = w_ref.shape[2]
    h_scr[...] = _modulated_norm(x_ref[...], g_ref[...], sc_ref[...], sh_ref[...]).astype(BF16)
    for pos, tile in enumerate(COLUMN_TILE_ORDER):
        acc = jnp.dot(h_scr[...], w_ref[tile], preferred_element_type=F32)
        kind, t = divmod(pos, N_KIND_TILES)
        cols = slice(t * tn, (t + 1) * tn)
        if kind == 0:
            r = _rope_tile(acc, cos_ref[...], sin_ref[...])
            k_ref[:, cols] = r.astype(BF16)
            if pos == KB_TILE:
                ksum_ref[0] = jnp.sum(r.reshape(tm // MOBA_BLOCK, MOBA_BLOCK, tn), axis=1)
        elif kind == 1:
            rt = _rope_tile(acc, cos_ref[...], sin_ref[...]).T
            qt_ref[cols, :] = (rt * Q_SCALE).astype(BF16)
            if pos == QB_TILE:
                qbt_ref[...] = rt
        else:
            vt = acc.T.astype(BF16)
            for b in range(vt_ref.shape[0]):
                vt_ref[b, cols, :] = vt[:, b * TK:(b + 1) * TK]


def _in_projection(x, g, sc, sh, w_bf16, cos_t, sin_t):
    s, d = x.shape
    tm, tn = PROJ_TM, w_bf16.shape[2]
    nb = tm // MOBA_BLOCK
    width = N_KIND_TILES * tn
    row = lambda i: (0, 0)
    return pl.pallas_call(
        _inproj_kernel,
        grid=(s // tm,),
        in_specs=[pl.BlockSpec((tm, d), lambda i: (i, 0)),
                  pl.BlockSpec((1, d), row), pl.BlockSpec((1, d), row), pl.BlockSpec((1, d), row),
                  pl.BlockSpec(w_bf16.shape, lambda i: (0, 0, 0),
                               pipeline_mode=pl.Buffered(1)),
                  pl.BlockSpec((tm, LANES), lambda i: (i, 0)),
                  pl.BlockSpec((tm, LANES), lambda i: (i, 0))],
        out_specs=[pl.BlockSpec((tm, width), lambda i: (i, 0)),
                   pl.BlockSpec((width, tm), lambda i: (0, i)),
                   pl.BlockSpec((tn, tm), lambda i: (0, i)),
                   pl.BlockSpec((tm // TK, width, TK), lambda i: (i, 0, 0)),
                   pl.BlockSpec((1, nb, tn), lambda i: (i, 0, 0))],
        out_shape=[jax.ShapeDtypeStruct((s, width), BF16),
                   jax.ShapeDtypeStruct((width, s), BF16),
                   jax.ShapeDtypeStruct((tn, s), F32),
                   jax.ShapeDtypeStruct((s // TK, width, TK), BF16),
                   jax.ShapeDtypeStruct((s // tm, nb, tn), F32)],
        scratch_shapes=[pltpu.VMEM((tm, d), BF16)],
        compiler_params=_cparams(1),
    )(x, g, sc, sh, w_bf16, cos_t, sin_t)


def _flash_step(s_t, v_t, state):
    m_prev, l_prev, acc_prev = state
    m_new = jnp.maximum(m_prev, jnp.max(s_t, axis=0, keepdims=True))
    alpha = jnp.exp2(m_prev - m_new)
    p = jnp.exp2(s_t - m_new)
    l_new = alpha * l_prev + jnp.sum(p, axis=0, keepdims=True)
    acc_new = alpha * acc_prev + jnp.dot(v_t, p.astype(BF16), preferred_element_type=F32)
    return m_new, l_new, acc_new


def _sweep(first, n_loop, scores, consume, s_a, s_b, tail_masks):
    def park(buf, vals, cols=ALL_QUERIES):
        for h, val in enumerate(vals):
            buf[h, :, cols] = val

    def fetch(buf, cols=ALL_QUERIES):
        return [buf[h, :, cols] for h in range(buf.shape[0])]

    park(s_a, scores(first, ALL_QUERIES))

    def pair(j, carry):
        a = first + 2 * j
        park(s_b, scores(a + 1, ALL_QUERIES))
        consume(a, fetch(s_a), None, ALL_QUERIES)
        park(s_a, scores(a + 2, ALL_QUERIES))
        consume(a + 1, fetch(s_b), None, ALL_QUERIES)
        return carry

    lax.fori_loop(0, n_loop // 2, pair, 0)
    t0 = first + n_loop
    park(s_b, scores(t0 + 1, LATE_QUERIES), LATE_QUERIES)
    consume(t0, fetch(s_a), tail_masks[0], ALL_QUERIES)
    last_mask = None if tail_masks[1] is None else tail_masks[1][:, LATE_QUERIES]
    consume(t0 + 1, fetch(s_b, LATE_QUERIES), last_mask, LATE_QUERIES)


def _score_scratch(n_streams):
    return [pltpu.VMEM((n_streams, TK, TQ), F32), pltpu.VMEM((n_streams, TK, TQ), F32)]


def _qk_t(k, q_t):
    return jnp.dot(k, q_t, preferred_element_type=F32)


def _k_block(k_ref, n):
    return k_ref[pl.ds(pl.multiple_of(n * TK, TK), TK), :]


def _init_state(refs):
    m_ref, l_ref, acc_ref = refs
    m_ref[...] = jnp.full(m_ref.shape, NEG, F32)
    l_ref[...] = jnp.zeros(l_ref.shape, F32)
    acc_ref[...] = jnp.zeros(acc_ref.shape, F32)


def _load_state(refs, cols=ALL_QUERIES):
    return tuple(r[:, cols] for r in refs)


def _store_state(refs, state, cols=ALL_QUERIES):
    for r, val in zip(refs, state):
        r[:, cols] = val


def _flash_scratch(n_streams, dv):
    out = []
    for _ in range(n_streams):
        out += [pltpu.VMEM((1, TQ), F32), pltpu.VMEM((1, TQ), F32), pltpu.VMEM((dv, TQ), F32)]
    return out


def _causal_t(d):
    r = lax.broadcasted_iota(I32, (TK, TQ), 0) + d * TK
    return r <= lax.broadcasted_iota(I32, (TK, TQ), 1)


def _stack_rows(top, bottom):
    return jnp.concatenate([top, bottom], axis=0)


def _dilated_bias_table():
    n_off = A_PATTERNS[-1][0] // TK + KPQ
    r = np.arange(TK)[:, None]
    c = np.arange(TQ)[None, :]
    tabs = []
    for t in range(n_off):
        delta = (t - (KPQ - 1)) * TK + c - r
        mult = np.zeros_like(delta)
        for window, dil in A_PATTERNS:
            mult += ((delta >= 0) & (delta % dil == 0) & (delta <= window)).astype(delta.dtype)
        tabs.append(np.where(mult > 0, np.log2(np.maximum(mult, 1).astype(np.float64)), NEG))
    return jnp.asarray(np.stack(tabs), F32)


def _dilated_kernel(qt_ref, k_ref, vt_ref, tab_ref, o_ref, s_a, s_b, *scr):
    i = pl.program_id(1)
    refs = (scr[0:3], scr[3:6])
    q_t = qt_ref[...]
    zero = jnp.zeros((HEAD_DIM, TQ), BF16)
    qs = (_stack_rows(q_t[:HEAD_DIM], zero), _stack_rows(zero, q_t[HEAD_DIM:]))
    n_off = tab_ref.shape[0]
    last = KPQ * i + (KPQ - 1)
    for h in range(2):
        _init_state(refs[h])

    def scores(n, cols):
        kb = _k_block(k_ref, n)
        bias = tab_ref[last - n, :, cols]
        return [_qk_t(kb, qs[h][:, cols]) + bias for h in range(2)]

    def consume(n, s_list, mask, cols):
        state = [_load_state(refs[h], cols) for h in range(2)]
        vb = vt_ref[n]
        for h in range(2):
            state[h] = _flash_step(s_list[h], vb[h * HEAD_DIM:(h + 1) * HEAD_DIM, :], state[h])
        for h in range(2):
            _store_state(refs[h], state[h], cols)

    first = jnp.maximum(last - (n_off - 1), 0)
    _sweep(first, last - 1 - first, scores, consume, s_a, s_b, (None, None))
    for h in range(2):
        _, l, acc = _load_state(refs[h])
        o_ref[h * HEAD_DIM:(h + 1) * HEAD_DIM, :] = (acc / l).astype(o_ref.dtype)


def _dilated_attention(qt, k, vt, tab, qrow, kcol, vrow):
    s = k.shape[0]
    n_pairs = 4
    return pl.pallas_call(
        _dilated_kernel,
        grid=(n_pairs, s // TQ),
        in_specs=[pl.BlockSpec((LANES, TQ), lambda hp, i: (qrow + hp, i)),
                  pl.BlockSpec((s, LANES), lambda hp, i: (0, kcol + hp)),
                  pl.BlockSpec((s // TK, LANES, TK), lambda hp, i: (0, vrow + hp, 0)),
                  pl.BlockSpec(tab.shape, lambda hp, i: (0, 0, 0), pipeline_mode=pl.Buffered(1))],
        out_specs=pl.BlockSpec((LANES, TQ), lambda hp, i: (hp, i)),
        out_shape=jax.ShapeDtypeStruct((n_pairs * LANES, s), BF16),
        scratch_shapes=_score_scratch(2) + _flash_scratch(2, HEAD_DIM),
        compiler_params=_cparams(2),
    )(qt, k, vt, tab)


def _moba_gate_kernel(qt_ref, km_ref, bias_ref):
    i = pl.program_id(0)
    tg = qt_ref.shape[1]
    n_heads, nblk, _ = km_ref.shape
    blk = lax.broadcasted_iota(I32, (nblk, tg), 0)
    blk_f = blk.astype(F32)
    tok = i * tg + lax.broadcasted_iota(I32, (nblk, tg), 1)
    own = jnp.right_shift(tok, int(math.log2(MOBA_BLOCK)))
    past = blk < own
    for h in range(n_heads):
        rows = slice(h * HEAD_DIM, (h + 1) * HEAD_DIM)
        gate = _dot3(km_ref[h], qt_ref[rows, :])
        sel = blk == own
        for _ in range(MOBA_TOPK):
            cand = past & jnp.logical_not(sel)
            g = jnp.where(cand, gate, -jnp.inf)
            mx = jnp.max(g, axis=0, keepdims=True)
            first = jnp.min(jnp.where(cand & (g == mx), blk_f, 2.0 * nblk), axis=0, keepdims=True)
            sel = sel | (blk_f == first)
        bias_ref[rows, :] = jnp.where(sel, 0.0, NEG).astype(BF16)


def _moba_gate(qbt, kmean):
    width, s = qbt.shape
    tg = 512
    return pl.pallas_call(
        _moba_gate_kernel,
        grid=(s // tg,),
        in_specs=[pl.BlockSpec((width, tg), lambda i: (0, i)),
                  pl.BlockSpec(kmean.shape, lambda i: (0, 0, 0))],
        out_specs=pl.BlockSpec((width, tg), lambda i: (0, i)),
        out_shape=jax.ShapeDtypeStruct((width, s), BF16),
        compiler_params=_cparams(1),
    )(qbt, kmean)


def _moba_kernel(qt_ref, bias_ref, k_ref, vt_ref, o_ref, s_a, s_b, *scr):
    i = pl.program_id(1)
    refs = (scr[0:3], scr[3:6])
    q_t = qt_ref[...]
    b_t = bias_ref[...]
    qs = (_stack_rows(q_t[:HEAD_DIM], b_t[:HEAD_DIM]), _stack_rows(b_t[HEAD_DIM:], q_t[HEAD_DIM:]))
    lane = lax.broadcasted_iota(I32, (1, LANES), 1)
    lo = lane < HEAD_DIM
    mine = (lo, jnp.logical_not(lo))
    blk = lane & (HEAD_DIM - 1)
    sub_block = jnp.right_shift(lax.broadcasted_iota(I32, (TK, LANES), 0), int(math.log2(MOBA_BLOCK)))
    for h in range(2):
        _init_state(refs[h])

    def scores(n, cols):
        kb = _k_block(k_ref, n)
        onehot = jnp.where(blk == n * (TK // MOBA_BLOCK) + sub_block, 1.0, 0.0).astype(BF16)
        return [_qk_t(jnp.where(mine[h], kb, onehot), qs[h][:, cols]) for h in range(2)]

    def consume(n, s_list, mask, cols):
        state = [_load_state(refs[h], cols) for h in range(2)]
        vb = vt_ref[n]
        for h in range(2):
            s_t = s_list[h] if mask is None else jnp.where(mask, s_list[h], NEG)
            state[h] = _flash_step(s_t, vb[h * HEAD_DIM:(h + 1) * HEAD_DIM, :], state[h])
        for h in range(2):
            _store_state(refs[h], state[h], cols)

    _sweep(0, KPQ * i, scores, consume, s_a, s_b, (_causal_t(0), _causal_t(1)))
    for h in range(2):
        _, l, acc = _load_state(refs[h])
        o_ref[h * HEAD_DIM:(h + 1) * HEAD_DIM, :] = (acc / l).astype(o_ref.dtype)


def _moba_attention(qt, bias, k, vt, qrow, kcol, vrow):
    s = k.shape[0]
    n_pairs = 4
    return pl.pallas_call(
        _moba_kernel,
        grid=(n_pairs, s // TQ),
        in_specs=[pl.BlockSpec((LANES, TQ), lambda hp, i: (qrow + hp, i)),
                  pl.BlockSpec((LANES, TQ), lambda hp, i: (hp, i)),
                  pl.BlockSpec((s, LANES), lambda hp, i: (0, kcol + hp)),
                  pl.BlockSpec((s // TK, LANES, TK), lambda hp, i: (0, vrow + hp, 0))],
        out_specs=pl.BlockSpec((LANES, TQ), lambda hp, i: (hp, i)),
        out_shape=jax.ShapeDtypeStruct((n_pairs * LANES, s), BF16),
        scratch_shapes=_score_scratch(2) + _flash_scratch(2, HEAD_DIM),
        compiler_params=_cparams(2),
    )(qt, bias, k, vt)


def _diff_kernel(lam_init, qt_ref, k_ref, vt_ref, lq1_ref, lk1_ref, lq2_ref, lk2_ref, g_ref, o_ref,
                 s_a, s_b, *scr):
    i = pl.program_id(1)
    refs = (scr[0:3], scr[3:6])
    q_t = qt_ref[...]
    zero = jnp.zeros((HEAD_DIM, TQ), BF16)
    qs = (_stack_rows(q_t[:HEAD_DIM], zero), _stack_rows(zero, q_t[HEAD_DIM:]))
    for h in range(2):
        _init_state(refs[h])

    def scores(n, cols):
        kb = _k_block(k_ref, n)
        return [_qk_t(kb, qs[h][:, cols]) for h in range(2)]

    def consume(n, s_list, mask, cols):
        state = [_load_state(refs[h], cols) for h in range(2)]
        vb = vt_ref[n]
        for h in range(2):
            s_t = s_list[h] if mask is None else jnp.where(mask, s_list[h], NEG)
            state[h] = _flash_step(s_t, vb, state[h])
        for h in range(2):
            _store_state(refs[h], state[h], cols)

    _sweep(0, KPQ * i, scores, consume, s_a, s_b, (_causal_t(0), _causal_t(1)))
    lam = (jnp.exp(jnp.sum(lq1_ref[...] * lk1_ref[...], axis=1, keepdims=True))
           - jnp.exp(jnp.sum(lq2_ref[...] * lk2_ref[...], axis=1, keepdims=True)) + lam_init)
    _, l1, acc1 = _load_state(refs[0])
    _, l2, acc2 = _load_state(refs[1])
    o = acc1 / l1 - lam * (acc2 / l2)
    var = jnp.mean(o * o, axis=0, keepdims=True)
    o = (o * lax.rsqrt(var + RMS_EPS) * g_ref[...]) * (1.0 - lam_init)
    o_ref[...] = o.astype(o_ref.dtype)


def _diff_attention(qt, k, vt, lq1, lk1, lq2, lk2, g_sub, lam_init, qrow, kcol, vrow):
    s = k.shape[0]
    n_heads = 8
    vec = lambda n: pl.BlockSpec((1, n), lambda h, i: (0, 0))
    return pl.pallas_call(
        functools.partial(_diff_kernel, lam_init),
        grid=(n_heads, s // TQ),
        in_specs=[pl.BlockSpec((LANES, TQ), lambda h, i: (qrow + h, i)),
                  pl.BlockSpec((s, LANES), lambda h, i: (0, kcol + h)),
                  pl.BlockSpec((s // TK, LANES, TK), lambda h, i: (0, vrow + h, 0)),
                  vec(HEAD_DIM), vec(HEAD_DIM), vec(HEAD_DIM), vec(HEAD_DIM),
                  pl.BlockSpec((LANES, 1), lambda h, i: (0, 0))],
        out_specs=pl.BlockSpec((LANES, TQ), lambda h, i: (h, i)),
        out_shape=jax.ShapeDtypeStruct((n_heads * LANES, s), BF16),
        scratch_shapes=_score_scratch(2) + _flash_scratch(2, LANES),
        compiler_params=_cparams(2),
    )(qt, k, vt, lq1, lk1, lq2, lk2, g_sub)


def _outproj_kernel(oa_ref, ob_ref, oc_ref, wt_ref, x_ref, gt_ref, o_ref):
    wa = oa_ref.shape[0]
    wb = ob_ref.shape[0]
    y_t = (jnp.dot(wt_ref[:, 0:wa], oa_ref[...], preferred_element_type=F32)
           + jnp.dot(wt_ref[:, wa:wa + wb], ob_ref[...], preferred_element_type=F32)
           + jnp.dot(wt_ref[:, wa + wb:], oc_ref[...], preferred_element_type=F32))
    o_ref[...] = x_ref[...] + gt_ref[...] * y_t.T


def _out_projection(oa_t, ob_t, oc_t, wt_bf16, x, gt):
    s, d = x.shape
    tm = 256
    return pl.pallas_call(
        _outproj_kernel,
        grid=(s // tm,),
        in_specs=[pl.BlockSpec((oa_t.shape[0], tm), lambda i: (0, i)),
                  pl.BlockSpec((ob_t.shape[0], tm), lambda i: (0, i)),
                  pl.BlockSpec((oc_t.shape[0], tm), lambda i: (0, i)),
                  pl.BlockSpec(wt_bf16.shape, lambda i: (0, 0)),
                  pl.BlockSpec((tm, d), lambda i: (i, 0)),
                  pl.BlockSpec((1, d), lambda i: (0, 0))],
        out_specs=pl.BlockSpec((tm, d), lambda i: (i, 0)),
        out_shape=jax.ShapeDtypeStruct((s, d), F32),
        compiler_params=_cparams(1),
    )(oa_t, ob_t, oc_t, wt_bf16, x, gt)


def _route_kernel(x_ref, g_ref, sc_ref, sh_ref, wr_ref, br_ref, tri_ref,
                  h_ref, route_ref, count_ref, base_scr):
    i = pl.program_id(0)
    tm = x_ref.shape[0]

    @pl.when(i == 0)
    def _():
        base_scr[...] = jnp.zeros_like(base_scr)

    h = _modulated_norm(x_ref[...], g_ref[...], sc_ref[...], sh_ref[...])
    h_ref[...] = h.astype(BF16)
    logits = _dot3(wr_ref[...], h, NT_DIMS)
    s = _sigmoid(logits)
    sg = s + br_ref[...]
    rows = [sg[r:r + 1, :] for r in range(N_EXPERTS)]

    def top2_sum(vals):
        best = None
        for a in range(len(vals)):
            for b in range(a + 1, len(vals)):
                pair = vals[a] + vals[b]
                best = pair if best is None else jnp.maximum(best, pair)
        return best

    score = [top2_sum(rows[g * EXPERTS_PER_GROUP:(g + 1) * EXPERTS_PER_GROUP]) for g in range(N_GROUPS)]
    best, grp = score[0], jnp.zeros((1, tm), I32)
    for g in range(1, N_GROUPS):
        better = score[g] > best
        grp = jnp.where(better, g, grp)
        best = jnp.where(better, score[g], best)
    cand = []
    for j in range(EXPERTS_PER_GROUP):
        c = rows[j]
        for g in range(1, N_GROUPS):
            c = jnp.where(grp == g, rows[g * EXPERTS_PER_GROUP + j], c)
        cand.append(c)
    b0, l0 = cand[0], jnp.zeros((1, tm), I32)
    for j in range(1, EXPERTS_PER_GROUP):
        better = cand[j] > b0
        l0 = jnp.where(better, j, l0)
        b0 = jnp.where(better, cand[j], b0)
    b1, l1 = jnp.full((1, tm), -jnp.inf, F32), jnp.zeros((1, tm), I32)
    for j in range(EXPERTS_PER_GROUP):
        better = (l0 != j) & (cand[j] > b1)
        l1 = jnp.where(better, j, l1)
        b1 = jnp.where(better, cand[j], b1)
    e0 = grp * EXPERTS_PER_GROUP + l0
    e1 = grp * EXPERTS_PER_GROUP + l1
    erow = lax.broadcasted_iota(I32, (N_EXPERTS, tm), 0)
    sel0 = erow == e0
    sel1 = erow == e1
    s0 = jnp.sum(jnp.where(sel0, s, 0.0), axis=0, keepdims=True)
    s1 = jnp.sum(jnp.where(sel1, s, 0.0), axis=0, keepdims=True)
    tot = s0 + s1
    chosen = jnp.where(sel0 | sel1, 1.0, 0.0).astype(BF16)
    csum = jnp.dot(chosen, tri_ref[...], preferred_element_type=F32)
    pos = base_scr[...] + csum - 1.0
    r0 = jnp.sum(jnp.where(sel0, pos, 0.0), axis=0, keepdims=True)
    r1 = jnp.sum(jnp.where(sel1, pos, 0.0), axis=0, keepdims=True)
    base = base_scr[...] + csum[:, tm - 1:tm]
    base_scr[...] = base
    count_ref[...] = jnp.broadcast_to(base, count_ref.shape)
    zero = jnp.zeros((1, tm), F32)
    for r, val in enumerate((e0.astype(F32), e1.astype(F32), r0, r1, s0 / tot, s1 / tot, zero, zero)):
        route_ref[r:r + 1, :] = val


def _norm_and_route(x, g, sc, sh, w_router, b_router):
    s, d = x.shape
    tm = 512
    tri = jnp.asarray(np.triu(np.ones((tm, tm), np.float32)), BF16)
    row = lambda i: (0, 0)
    return pl.pallas_call(
        _route_kernel,
        grid=(s // tm,),
        in_specs=[pl.BlockSpec((tm, d), lambda i: (i, 0)),
                  pl.BlockSpec((1, d), row), pl.BlockSpec((1, d), row), pl.BlockSpec((1, d), row),
                  pl.BlockSpec((N_EXPERTS, d), row), pl.BlockSpec((N_EXPERTS, 1), row),
                  pl.BlockSpec((tm, tm), row)],
        out_specs=[pl.BlockSpec((tm, d), lambda i: (i, 0)),
                   pl.BlockSpec((8, tm), lambda i: (0, i)),
                   pl.BlockSpec((N_EXPERTS, LANES), row)],
        out_shape=[jax.ShapeDtypeStruct((s, d), BF16),
                   jax.ShapeDtypeStruct((8, s), F32),
                   jax.ShapeDtypeStruct((N_EXPERTS, LANES), F32)],
        scratch_shapes=[pltpu.VMEM((N_EXPERTS, 1), F32)],
        compiler_params=_cparams(1),
    )(x, g, sc, sh, w_router.T, b_router.reshape(N_EXPERTS, 1), tri)


FFN_ROWS = 512


def _ffn_kernel(tile_base, te_ref, nu_ref, xs_ref, wg_ref, wu_ref, wd_ref, *rest):
    y_ref = rest[-1]
    j = pl.program_id(0) + tile_base

    @pl.when(j < nu_ref[0])
    def _():
        x = xs_ref[...]
        de = wg_ref.shape[3]
        y = None
        for c in range(de // FFN_CHUNK):
            cs = slice(c * FFN_CHUNK, (c + 1) * FFN_CHUNK)
            a = jnp.dot(x, wg_ref[0, 0, :, cs].astype(BF16), preferred_element_type=F32)
            b = jnp.dot(x, wu_ref[0, 0, :, cs].astype(BF16), preferred_element_type=F32)
            act = (a * _sigmoid(a) * b).astype(BF16)
            part = jnp.dot(act, wd_ref[0, 0, cs, :].astype(BF16), preferred_element_type=F32)
            y = part if y is None else y + part
        y_ref[...] = y.astype(y_ref.dtype)

    @pl.when(j >= nu_ref[0])
    def _():
        y_ref[...] = jnp.zeros_like(y_ref)


FFN_CHUNK = 512
MOE_PARTS = 2


def _expert_ffn(tile_expert, n_used, xs_part, w_gate, w_up, w_down, layer, tile_base, n_tiles, y_prev):
    r, d = xs_part.shape
    de = w_gate.shape[3]
    expert = lambda j, te, nu: (layer, te[j + tile_base], 0, 0)
    in_specs = [pl.BlockSpec((FFN_ROWS, d), lambda j, te, nu: (j, 0)),
                pl.BlockSpec((1, 1, d, de), expert, pipeline_mode=pl.Buffered(1)),
                pl.BlockSpec((1, 1, d, de), expert, pipeline_mode=pl.Buffered(1)),
                pl.BlockSpec((1, 1, de, d), expert, pipeline_mode=pl.Buffered(1))]
    args = [tile_expert, n_used, xs_part, w_gate, w_up, w_down]
    aliases = {}
    if y_prev is not None:
        in_specs.append(pl.BlockSpec(memory_space=pl.ANY))
        aliases = {len(args): 0}
        args.append(y_prev)
    grid_spec = pltpu.PrefetchScalarGridSpec(
        num_scalar_prefetch=2,
        grid=(r // FFN_ROWS,),
        in_specs=in_specs,
        out_specs=pl.BlockSpec((FFN_ROWS, d), lambda j, te, nu: (j + tile_base, 0)),
    )
    return pl.pallas_call(
        functools.partial(_ffn_kernel, tile_base),
        grid_spec=grid_spec,
        out_shape=jax.ShapeDtypeStruct((n_tiles * FFN_ROWS, d), BF16),
        input_output_aliases=aliases,
        compiler_params=_cparams(1),
    )(*args)


def _combine_kernel(final, x_ref, y0_ref, y1_ref, w0_ref, w1_ref, gt_ref, gf_ref, *rest):
    o_ref = rest[-1]
    x = x_ref[...] + gt_ref[...] * (w0_ref[...] * y0_ref[...].astype(F32)
                                    + w1_ref[...] * y1_ref[...].astype(F32))
    if final:
        var = jnp.mean(x * x, axis=-1, keepdims=True)
        x = x * lax.rsqrt(var + RMS_EPS) * gf_ref[...]
    o_ref[...] = x


def _combine(x, y01_part, w0, w1, gt, g_final, final, row_base, out_prev):
    s, d = x.shape
    tm = 256
    part = y01_part.shape[0] // 2
    base = row_base // tm
    big = pl.BlockSpec((tm, d), lambda i: (i + base, 0))
    first = pl.BlockSpec((tm, d), lambda i: (i, 0))
    second = pl.BlockSpec((tm, d), lambda i: (i + part // tm, 0))
    col = pl.BlockSpec((tm, 1), lambda i: (i + base, 0))
    vec = pl.BlockSpec((1, d), lambda i: (0, 0))
    in_specs = [big, first, second, col, col, vec, vec]
    args = [x, y01_part, y01_part, w0, w1, gt, g_final]
    aliases = {}
    if out_prev is not None:
        in_specs.append(pl.BlockSpec(memory_space=pl.ANY))
        aliases = {len(args): 0}
        args.append(out_prev)
    return pl.pallas_call(
        functools.partial(_combine_kernel, final),
        grid=(part // tm,),
        in_specs=in_specs,
        out_specs=big,
        out_shape=jax.ShapeDtypeStruct((s, d), F32),
        input_output_aliases=aliases,
        compiler_params=_cparams(1),
    )(*args)


def _rope_tables(positions):
    half = HEAD_DIM // 2
    inv = ROPE_THETA ** (-jnp.arange(0, HEAD_DIM, 2, dtype=F32) / HEAD_DIM)
    ang = positions.astype(F32)[:, None] * inv
    cos, sin = jnp.cos(ang), jnp.sin(ang)
    cos_t = jnp.tile(cos, (1, LANES // half))
    sin_t = jnp.tile(jnp.concatenate([-sin, sin], axis=1), (1, LANES // HEAD_DIM))
    return cos_t, sin_t


def _block_means(ksum):
    nblk, width = ksum.shape
    km = (ksum / MOBA_BLOCK).reshape(nblk, width // HEAD_DIM, HEAD_DIM)
    km = jnp.pad(km, ((0, HEAD_DIM - nblk), (0, 0), (0, 0)))
    return jnp.transpose(km, (1, 0, 2))


def _moe(x1, h2, route, counts, w_gate, w_up, w_down, layer, gt, g_final, final):
    s, d = x1.shape
    e0, e1 = route[0].astype(I32), route[1].astype(I32)
    r0, r1 = route[2].astype(I32), route[3].astype(I32)
    cnt = counts[:, 0].astype(I32)
    cnt_pad = ((cnt + FFN_ROWS - 1) // FFN_ROWS) * FFN_ROWS
    off = jnp.concatenate([jnp.zeros((1,), I32), jnp.cumsum(cnt_pad).astype(I32)])
    d0 = off[e0] + r0
    d1 = off[e1] + r1
    n_rows = 2 * s + N_EXPERTS * FFN_ROWS
    tok = jnp.arange(s, dtype=I32)
    src = (jnp.arange(n_rows, dtype=I32) % s).at[jnp.concatenate([d0, d1])].set(jnp.concatenate([tok, tok]))
    n_tiles = n_rows // FFN_ROWS
    tile_start = jnp.arange(n_tiles, dtype=I32) * FFN_ROWS
    tile_expert = jnp.sum((off[None, 1:] <= tile_start[:, None]).astype(I32), axis=1)
    tile_expert = jnp.minimum(tile_expert, N_EXPERTS - 1).astype(I32)
    n_used = (off[N_EXPERTS] // FFN_ROWS).reshape(1).astype(I32)
    y = None
    for part in range(MOE_PARTS):
        tiles = n_tiles // MOE_PARTS
        rows = slice(part * tiles * FFN_ROWS, (part + 1) * tiles * FFN_ROWS)
        y = _expert_ffn(tile_expert, n_used, h2[src[rows]], w_gate, w_up, w_down, layer,
                        part * tiles, n_tiles, y)
    w0, w1 = route[4].reshape(s, 1), route[5].reshape(s, 1)
    out = None
    for part in range(MOE_PARTS):
        toks = slice(part * s // MOE_PARTS, (part + 1) * s // MOE_PARTS)
        y01 = y[jnp.concatenate([d0[toks], d1[toks]])]
        out = _combine(x1, y01, w0, w1, gt, g_final, final, part * s // MOE_PARTS, out)
    return out


def kernel(x, c, positions, w_ada, b_ada, g_attn, g_mlp, w_in, w_out, lam_q1, lam_k1, lam_q2, lam_k2,
           g_subln, w_gate, w_up, w_down, w_router, b_router, g_final):
    batch, s, d = x.shape
    assert batch == 1 and s % TQ == 0 and s // MOBA_BLOCK <= HEAD_DIM
    assert w_in.shape[2] == 3 * N_KIND_TILES * PROJ_TN
    depth = w_ada.shape[0]
    xs = x.reshape(s, d)
    cos_t, sin_t = _rope_tables(positions[0])
    mod = _modulation(c, w_ada, b_ada)
    tab = _dilated_bias_table()
    gf = g_final.reshape(1, d)
    row = lambda v: v.reshape(1, -1)
    for l in range(depth):
        lam_init = 0.8 - 0.6 * math.exp(-0.3 * l)
        sh_a, sc_a, gt_a, sh_m, sc_m, gt_m = [mod[l, :, k * d:(k + 1) * d] for k in range(6)]
        k, qt, qbt, vt, ksum = _in_projection(xs, row(g_attn[l]), sc_a, sh_a,
                                              _layer_to_bf16(w_in, l, PROJ_TN), cos_t, sin_t)
        o_a = _dilated_attention(qt, k, vt, tab, 0, 0, 0)
        bias = _moba_gate(qbt, _block_means(ksum.reshape(s // MOBA_BLOCK, -1)))
        o_b = _moba_attention(qt, bias, k, vt, 4, 4, 4)
        o_c = _diff_attention(qt, k, vt, row(lam_q1[l]), row(lam_k1[l]), row(lam_q2[l]), row(lam_k2[l]),
                              g_subln[l].reshape(-1, 1), lam_init, 8, 8, 8)
        x1 = _out_projection(o_a, o_b, o_c, w_out[l].T.astype(BF16), xs, gt_a)
        h2, route, counts = _norm_and_route(x1, row(g_mlp[l]), sc_m, sh_m, w_router, b_router)
        xs = _moe(x1, h2, route, counts, w_gate, w_up, w_down, l, gt_m, gf, l == depth - 1)
    return xs.reshape(batch, s, d)
```

```python
import functools
import math

import numpy as np
import jax
import jax.numpy as jnp
from jax import lax
from jax.experimental import pallas as pl
from jax.experimental.pallas import tpu as pltpu

F32 = jnp.float32
BF16 = jnp.bfloat16
I32 = jnp.int32

HEAD_DIM = 64
LANES = 128
A_PATTERNS = ((128, 1), (512, 4), (2048, 16))
MOBA_BLOCK = 256
MOBA_TOPK = 3
N_EXPERTS = 16
N_GROUPS = 4
EXPERTS_PER_GROUP = N_EXPERTS // N_GROUPS
ROPE_THETA = 10000.0
RMS_EPS = 1e-6
NEG = -1e30
VMEM_LIMIT = 56 * 1024 * 1024

TQ = 1024
TK = 512
KPQ = TQ // TK
assert KPQ == 2
ALL_QUERIES = slice(None)
LATE_QUERIES = slice(TQ // 2, TQ)
Q_CHUNK = 256
PROJ_TM = 512
PROJ_TN = 512
NT_DIMS = (((1,), (1,)), ((), ()))
Q_SCALE = HEAD_DIM ** -0.5 * math.log2(math.e)


def _cparams(n_axes):
    return pltpu.CompilerParams(dimension_semantics=("arbitrary",) * n_axes,
                                vmem_limit_bytes=VMEM_LIMIT)


def _split_bf16(x):
    hi = x.astype(BF16)
    lo = (x - hi.astype(F32)).astype(BF16)
    return hi, lo


def _dot3(a, b, dims=(((1,), (0,)), ((), ()))):
    a_hi, a_lo = _split_bf16(a)
    b_hi, b_lo = _split_bf16(b)
    dg = functools.partial(lax.dot_general, dimension_numbers=dims, preferred_element_type=F32)
    return dg(a_hi, b_hi) + dg(a_hi, b_lo) + dg(a_lo, b_hi)


def _sigmoid(x):
    return 1.0 / (1.0 + jnp.exp(-x))


CAST_STREAMS = 4
CAST_BLOCK_ELEMS = 512 * 1024
CAST_COLS = 1024


def _cast_kernel(*refs):
    o_ref = refs[-1]
    col_tile = o_ref.shape[2]
    for k, x_ref in enumerate(refs[:-1]):
        rows = x_ref.shape[1]
        x = x_ref[0].astype(o_ref.dtype)
        for t in range(o_ref.shape[0]):
            o_ref[t, k * rows:(k + 1) * rows, :] = x[:, t * col_tile:(t + 1) * col_tile]


def _layer_to_bf16(w, l, col_tile=None):
    cols = w.shape[-1]
    col_tile = col_tile or cols
    w3 = w.reshape(w.shape[0], -1, cols)
    rows = w3.shape[1]
    tc = max(min(cols, CAST_COLS), col_tile)
    sr = CAST_BLOCK_ELEMS // tc
    tr = CAST_STREAMS * sr
    stream = lambda k: pl.BlockSpec((1, sr, tc), lambda i, j: (l, CAST_STREAMS * i + k, j))
    return pl.pallas_call(
        _cast_kernel,
        grid=(rows // tr, cols // tc),
        in_specs=[stream(k) for k in range(CAST_STREAMS)],
        out_specs=pl.BlockSpec((tc // col_tile, tr, col_tile), lambda i, j: (j, i, 0)),
        out_shape=jax.ShapeDtypeStruct((cols // col_tile, rows, col_tile), BF16),
        compiler_params=_cparams(2),
    )(*([w3] * CAST_STREAMS))


def _mod_kernel(c_ref, w_ref, b_ref, o_ref):
    c = c_ref[...]
    sc = c * _sigmoid(c)
    d = c.shape[0]
    acc = b_ref[0]
    for r in range(0, d, 256):
        acc = acc + jnp.sum(w_ref[0, r:r + 256, :] * sc[r:r + 256, :], axis=0, keepdims=True)
    o_ref[0] = acc


def _modulation(c, w_ada, b_ada):
    depth, d, n = w_ada.shape
    tn = 512
    return pl.pallas_call(
        _mod_kernel,
        grid=(depth, n // tn),
        in_specs=[pl.BlockSpec((d, 1), lambda l, j: (0, 0)),
                  pl.BlockSpec((1, d, tn), lambda l, j: (l, 0, j)),
                  pl.BlockSpec((1, 1, tn), lambda l, j: (l, 0, j))],
        out_specs=pl.BlockSpec((1, 1, tn), lambda l, j: (l, 0, j)),
        out_shape=jax.ShapeDtypeStruct((depth, 1, n), F32),
        compiler_params=_cparams(2),
    )(c.reshape(d, 1), w_ada, b_ada.reshape(depth, 1, n))


def _modulated_norm(x, g, sc, sh):
    var = jnp.mean(x * x, axis=-1, keepdims=True)
    return (x * lax.rsqrt(var + RMS_EPS) * g) * (1.0 + sc) + sh


def _rope_tile(acc, cos, sin):
    lane = lax.broadcasted_iota(I32, (1, LANES), 1)
    first = (lane & (HEAD_DIM - 1)) < (HEAD_DIM // 2)
    outs = []
    for c in range(acc.shape[1] // LANES):
        xc = acc[:, c * LANES:(c + 1) * LANES]
        rot = jnp.where(first, pltpu.roll(xc, LANES - HEAD_DIM // 2, 1), pltpu.roll(xc, HEAD_DIM // 2, 1))
        outs.append(xc * cos + rot * sin)
    return jnp.concatenate(outs, axis=1)


N_KIND_TILES = 4
COLUMN_TILE_ORDER = (1, 4, 8, 9, 0, 3, 6, 7, 2, 5, 10, 11)
KB_TILE = 1
QB_TILE = N_KIND_TILES + 1


def _inproj_kernel(x_ref, g_ref, sc_ref, sh_ref, w_ref, cos_ref, sin_ref,
                   k_ref, qt_ref, qbt_ref, vt_ref, ksum_ref, h_scr):
    tm = x_ref.shape[0]
    tn = w_ref.shape[2]
    h_scr[...] = _modulated_norm(x_ref[...], g_ref[...], sc_ref[...], sh_ref[...]).astype(BF16)
    for pos, tile in enumerate(COLUMN_TILE_ORDER):
        acc = jnp.dot(h_scr[...], w_ref[tile], preferred_element_type=F32)
        kind, t = divmod(pos, N_KIND_TILES)
        cols = slice(t * tn, (t + 1) * tn)
        if kind == 0:
            r = _rope_tile(acc, cos_ref[...], sin_ref[...])
            k_ref[:, cols] = r.astype(BF16)
            if pos == KB_TILE:
                ksum_ref[0] = jnp.sum(r.reshape(tm // MOBA_BLOCK, MOBA_BLOCK, tn), axis=1)
        elif kind == 1:
            rt = _rope_tile(acc, cos_ref[...], sin_ref[...]).T
            qt_ref[cols, :] = (rt * Q_SCALE).astype(BF16)
            if pos == QB_TILE:
                qbt_ref[...] = rt
        else:
            vt = acc.T.astype(BF16)
            for b in range(vt_ref.shape[0]):
                vt_ref[b, cols, :] = vt[:, b * TK:(b + 1) * TK]


def _in_projection(x, g, sc, sh, w_bf16, cos_t, sin_t):
    s, d = x.shape
    tm, tn = PROJ_TM, w_bf16.shape[2]
    nb = tm // MOBA_BLOCK
    width = N_KIND_TILES * tn
    row = lambda i: (0, 0)
    return pl.pallas_call(
        _inproj_kernel,
        grid=(s // tm,),
        in_specs=[pl.BlockSpec((tm, d), lambda i: (i, 0)),
                  pl.BlockSpec((1, d), row), pl.BlockSpec((1, d), row), pl.BlockSpec((1, d), row),
                  pl.BlockSpec(w_bf16.shape, lambda i: (0, 0, 0),
                               pipeline_mode=pl.Buffered(1)),
                  pl.BlockSpec((tm, LANES), lambda i: (i, 0)),
                  pl.BlockSpec((tm, LANES), lambda i: (i, 0))],
        out_specs=[pl.BlockSpec((tm, width), lambda i: (i, 0)),
                   pl.BlockSpec((width, tm), lambda i: (0, i)),
                   pl.BlockSpec((tn, tm), lambda i: (0, i)),
                   pl.BlockSpec((tm // TK, width, TK), lambda i: (i, 0, 0)),
                   pl.BlockSpec((1, nb, tn), lambda i: (i, 0, 0))],
        out_shape=[jax.ShapeDtypeStruct((s, width), BF16),
                   jax.ShapeDtypeStruct((width, s), BF16),
                   jax.ShapeDtypeStruct((tn, s), F32),
                   jax.ShapeDtypeStruct((s // TK, width, TK), BF16),
                   jax.ShapeDtypeStruct((s // tm, nb, tn), F32)],
        scratch_shapes=[pltpu.VMEM((tm, d), BF16)],
        compiler_params=_cparams(1),
    )(x, g, sc, sh, w_bf16, cos_t, sin_t)


def _flash_step(s_t, v_t, state):
    m_prev, l_prev, acc_prev = state
    m_new = jnp.maximum(m_prev, jnp.max(s_t, axis=0, keepdims=True))
    alpha = jnp.exp2(m_prev - m_new)
    p = jnp.exp2(s_t - m_new)
    l_new = alpha * l_prev + jnp.sum(p, axis=0, keepdims=True)
    acc_new = alpha * acc_prev + jnp.dot(v_t, p.astype(BF16), preferred_element_type=F32)
    return m_new, l_new, acc_new


def _sweep(first, n_loop, scores, consume, bufs, tail_masks):
    halves = (bufs[0:2], bufs[2:4])

    def park_pair(a, dst):
        for d in range(2):
            for h, val in enumerate(scores(a + d)):
                dst[d][h] = val

    def pair(a, cur, nxt):
        park_pair(a + 2, nxt)
        consume(a, cur[0], None, ALL_QUERIES)
        consume(a + 1, cur[1], None, ALL_QUERIES)

    def tail(cur):
        t0 = first + n_loop
        consume(t0, cur[0], tail_masks[0], ALL_QUERIES)
        consume(t0 + 1, cur[1], tail_masks[1], LATE_QUERIES)

    park_pair(first, halves[0])

    def two_pairs(j, carry):
        pair(first + 4 * j, halves[0], halves[1])
        pair(first + 4 * j + 2, halves[1], halves[0])
        return carry

    n_pairs = n_loop // 2
    lax.fori_loop(0, n_pairs // 2, two_pairs, 0)

    @pl.when(n_pairs % 2 == 1)
    def _():
        pair(first + n_loop - 2, halves[0], halves[1])
        tail(halves[1])

    @pl.when(n_pairs % 2 == 0)
    def _():
        tail(halves[0])


def _softmax_pv(buf, v_ts, refs, mask, cols):
    start, stop, _ = cols.indices(TQ)
    for c0 in range(start, stop, Q_CHUNK):
        cs = slice(c0, c0 + Q_CHUNK)
        for h, v_t in enumerate(v_ts):
            s_t = buf[h, :, cs]
            if mask is not None:
                s_t = jnp.where(mask[:, cs], s_t, NEG)
            _store_state(refs[h], _flash_step(s_t, v_t, _load_state(refs[h], cs)), cs)


N_SCORE_BUFS = 4


def _score_scratch(n_streams):
    return [pltpu.VMEM((n_streams, TK, TQ), F32) for _ in range(N_SCORE_BUFS)]


def _qk_t(k, q_t):
    return jnp.dot(k, q_t, preferred_element_type=F32)


def _k_block(k_ref, n):
    return k_ref[pl.ds(pl.multiple_of(n * TK, TK), TK), :]


def _init_state(refs):
    m_ref, l_ref, acc_ref = refs
    m_ref[...] = jnp.full(m_ref.shape, NEG, F32)
    l_ref[...] = jnp.zeros(l_ref.shape, F32)
    acc_ref[...] = jnp.zeros(acc_ref.shape, F32)


def _load_state(refs, cols=ALL_QUERIES):
    return tuple(r[:, cols] for r in refs)


def _store_state(refs, state, cols=ALL_QUERIES):
    for r, val in zip(refs, state):
        r[:, cols] = val


def _flash_scratch(n_streams, dv):
    out = []
    for _ in range(n_streams):
        out += [pltpu.VMEM((1, TQ), F32), pltpu.VMEM((1, TQ), F32), pltpu.VMEM((dv, TQ), F32)]
    return out


def _causal_t(d):
    r = lax.broadcasted_iota(I32, (TK, TQ), 0) + d * TK
    return r <= lax.broadcasted_iota(I32, (TK, TQ), 1)


def _stack_rows(top, bottom):
    return jnp.concatenate([top, bottom], axis=0)


def _dilated_bias_table():
    n_off = A_PATTERNS[-1][0] // TK + KPQ
    r = np.arange(TK)[:, None]
    c = np.arange(TQ)[None, :]
    tabs = []
    for t in range(n_off):
        delta = (t - (KPQ - 1)) * TK + c - r
        mult = np.zeros_like(delta)
        for window, dil in A_PATTERNS:
            mult += ((delta >= 0) & (delta % dil == 0) & (delta <= window)).astype(delta.dtype)
        tabs.append(np.where(mult > 0, np.log2(np.maximum(mult, 1).astype(np.float64)), NEG))
    return jnp.asarray(np.stack(tabs), F32)


def _dilated_kernel(qt_ref, k_ref, vt_ref, tab_ref, o_ref, *scratch):
    bufs, scr = scratch[:N_SCORE_BUFS], scratch[N_SCORE_BUFS:]
    i = pl.program_id(1)
    refs = (scr[0:3], scr[3:6])
    q_t = qt_ref[...]
    zero = jnp.zeros((HEAD_DIM, TQ), BF16)
    qs = (_stack_rows(q_t[:HEAD_DIM], zero), _stack_rows(zero, q_t[HEAD_DIM:]))
    n_off = tab_ref.shape[0]
    last = KPQ * i + (KPQ - 1)
    for h in range(2):
        _init_state(refs[h])

    def scores(n):
        kb = _k_block(k_ref, n)
        bias = tab_ref[last - n]
        return [_qk_t(kb, qs[h]) + bias for h in range(2)]

    def consume(n, buf, mask, cols):
        vb = vt_ref[n]
        _softmax_pv(buf, [vb[:HEAD_DIM, :], vb[HEAD_DIM:, :]], refs, mask, cols)

    first = jnp.maximum(last - (n_off - 1), 0)
    _sweep(first, last - 1 - first, scores, consume, bufs, (None, None))
    for h in range(2):
        _, l, acc = _load_state(refs[h])
        o_ref[h * HEAD_DIM:(h + 1) * HEAD_DIM, :] = (acc / l).astype(o_ref.dtype)


def _dilated_attention(qt, k, vt, tab, qrow, kcol, vrow):
    s = k.shape[0]
    n_pairs = 4
    return pl.pallas_call(
        _dilated_kernel,
        grid=(n_pairs, s // TQ),
        in_specs=[pl.BlockSpec((LANES, TQ), lambda hp, i: (qrow + hp, i)),
                  pl.BlockSpec((s, LANES), lambda hp, i: (0, kcol + hp)),
                  pl.BlockSpec((s // TK, LANES, TK), lambda hp, i: (0, vrow + hp, 0)),
                  pl.BlockSpec(tab.shape, lambda hp, i: (0, 0, 0), pipeline_mode=pl.Buffered(1))],
        out_specs=pl.BlockSpec((LANES, TQ), lambda hp, i: (hp, i)),
        out_shape=jax.ShapeDtypeStruct((n_pairs * LANES, s), BF16),
        scratch_shapes=_score_scratch(2) + _flash_scratch(2, HEAD_DIM),
        compiler_params=_cparams(2),
    )(qt, k, vt, tab)


def _moba_gate_kernel(qt_ref, km_ref, bias_ref):
    i = pl.program_id(0)
    tg = qt_ref.shape[1]
    n_heads, nblk, _ = km_ref.shape
    blk = lax.broadcasted_iota(I32, (nblk, tg), 0)
    blk_f = blk.astype(F32)
    tok = i * tg + lax.broadcasted_iota(I32, (nblk, tg), 1)
    own = jnp.right_shift(tok, int(math.log2(MOBA_BLOCK)))
    past = blk < own
    for h in range(n_heads):
        rows = slice(h * HEAD_DIM, (h + 1) * HEAD_DIM)
        gate = _dot3(km_ref[h], qt_ref[rows, :])
        sel = blk == own
        for _ in range(MOBA_TOPK):
            cand = past & jnp.logical_not(sel)
            g = jnp.where(cand, gate, -jnp.inf)
            mx = jnp.max(g, axis=0, keepdims=True)
            first = jnp.min(jnp.where(cand & (g == mx), blk_f, 2.0 * nblk), axis=0, keepdims=True)
            sel = sel | (blk_f == first)
        bias_ref[rows, :] = jnp.where(sel, 0.0, NEG).astype(BF16)


def _moba_gate(qbt, kmean):
    width, s = qbt.shape
    tg = 512
    return pl.pallas_call(
        _moba_gate_kernel,
        grid=(s // tg,),
        in_specs=[pl.BlockSpec((width, tg), lambda i: (0, i)),
                  pl.BlockSpec(kmean.shape, lambda i: (0, 0, 0))],
        out_specs=pl.BlockSpec((width, tg), lambda i: (0, i)),
        out_shape=jax.ShapeDtypeStruct((width, s), BF16),
        compiler_params=_cparams(1),
    )(qbt, kmean)


def _moba_kernel(qt_ref, bias_ref, k_ref, vt_ref, o_ref, *scratch):
    i = pl.program_id(1)
    bufs, scr = scratch[:N_SCORE_BUFS], scratch[N_SCORE_BUFS:]
    refs = (scr[0:3], scr[3:6])
    q_t = qt_ref[...]
    b_t = bias_ref[...]
    qs = (_stack_rows(q_t[:HEAD_DIM], b_t[:HEAD_DIM]), _stack_rows(b_t[HEAD_DIM:], q_t[HEAD_DIM:]))
    lane = lax.broadcasted_iota(I32, (1, LANES), 1)
    lo = lane < HEAD_DIM
    mine = (lo, jnp.logical_not(lo))
    blk = lane & (HEAD_DIM - 1)
    sub_block = jnp.right_shift(lax.broadcasted_iota(I32, (TK, LANES), 0), int(math.log2(MOBA_BLOCK)))
    for h in range(2):
        _init_state(refs[h])

    def scores(n):
        kb = _k_block(k_ref, n)
        onehot = jnp.where(blk == n * (TK // MOBA_BLOCK) + sub_block, 1.0, 0.0).astype(BF16)
        return [_qk_t(jnp.where(mine[h], kb, onehot), qs[h]) for h in range(2)]

    def consume(n, buf, mask, cols):
        vb = vt_ref[n]
        _softmax_pv(buf, [vb[:HEAD_DIM, :], vb[HEAD_DIM:, :]], refs, mask, cols)

    _sweep(0, KPQ * i, scores, consume, bufs, (_causal_t(0), _causal_t(1)))
    for h in range(2):
        _, l, acc = _load_state(refs[h])
        o_ref[h * HEAD_DIM:(h + 1) * HEAD_DIM, :] = (acc / l).astype(o_ref.dtype)


def _moba_attention(qt, bias, k, vt, qrow, kcol, vrow):
    s = k.shape[0]
    n_pairs = 4
    return pl.pallas_call(
        _moba_kernel,
        grid=(n_pairs, s // TQ),
        in_specs=[pl.BlockSpec((LANES, TQ), lambda hp, i: (qrow + hp, i)),
                  pl.BlockSpec((LANES, TQ), lambda hp, i: (hp, i)),
                  pl.BlockSpec((s, LANES), lambda hp, i: (0, kcol + hp)),
                  pl.BlockSpec((s // TK, LANES, TK), lambda hp, i: (0, vrow + hp, 0))],
        out_specs=pl.BlockSpec((LANES, TQ), lambda hp, i: (hp, i)),
        out_shape=jax.ShapeDtypeStruct((n_pairs * LANES, s), BF16),
        scratch_shapes=_score_scratch(2) + _flash_scratch(2, HEAD_DIM),
        compiler_params=_cparams(2),
    )(qt, bias, k, vt)


def _diff_kernel(lam_init, qt_ref, k_ref, vt_ref, lq1_ref, lk1_ref, lq2_ref, lk2_ref, g_ref, o_ref,
                 *scratch):
    i = pl.program_id(1)
    bufs, scr = scratch[:N_SCORE_BUFS], scratch[N_SCORE_BUFS:]
    refs = (scr[0:3], scr[3:6])
    q_t = qt_ref[...]
    zero = jnp.zeros((HEAD_DIM, TQ), BF16)
    qs = (_stack_rows(q_t[:HEAD_DIM], zero), _stack_rows(zero, q_t[HEAD_DIM:]))
    for h in range(2):
        _init_state(refs[h])

    def scores(n):
        kb = _k_block(k_ref, n)
        return [_qk_t(kb, qs[h]) for h in range(2)]

    def consume(n, buf, mask, cols):
        vb = vt_ref[n]
        _softmax_pv(buf, [vb, vb], refs, mask, cols)

    _sweep(0, KPQ * i, scores, consume, bufs, (_causal_t(0), _causal_t(1)))
    lam = (jnp.exp(jnp.sum(lq1_ref[...] * lk1_ref[...], axis=1, keepdims=True))
           - jnp.exp(jnp.sum(lq2_ref[...] * lk2_ref[...], axis=1, keepdims=True)) + lam_init)
    _, l1, acc1 = _load_state(refs[0])
    _, l2, acc2 = _load_state(refs[1])
    o = acc1 / l1 - lam * (acc2 / l2)
    var = jnp.mean(o * o, axis=0, keepdims=True)
    o = (o * lax.rsqrt(var + RMS_EPS) * g_ref[...]) * (1.0 - lam_init)
    o_ref[...] = o.astype(o_ref.dtype)


def _diff_attention(qt, k, vt, lq1, lk1, lq2, lk2, g_sub, lam_init, qrow, kcol, vrow):
    s = k.shape[0]
    n_heads = 8
    vec = lambda n: pl.BlockSpec((1, n), lambda h, i: (0, 0))
    return pl.pallas_call(
        functools.partial(_diff_kernel, lam_init),
        grid=(n_heads, s // TQ),
        in_specs=[pl.BlockSpec((LANES, TQ), lambda h, i: (qrow + h, i)),
                  pl.BlockSpec((s, LANES), lambda h, i: (0, kcol + h)),
                  pl.BlockSpec((s // TK, LANES, TK), lambda h, i: (0, vrow + h, 0)),
                  vec(HEAD_DIM), vec(HEAD_DIM), vec(HEAD_DIM), vec(HEAD_DIM),
                  pl.BlockSpec((LANES, 1), lambda h, i: (0, 0))],
        out_specs=pl.BlockSpec((LANES, TQ), lambda h, i: (h, i)),
        out_shape=jax.ShapeDtypeStruct((n_heads * LANES, s), BF16),
        scratch_shapes=_score_scratch(2) + _flash_scratch(2, LANES),
        compiler_params=_cparams(2),
    )(qt, k, vt, lq1, lk1, lq2, lk2, g_sub)


def _outproj_kernel(oa_ref, ob_ref, oc_ref, wt_ref, x_ref, gt_ref, o_ref):
    wa = oa_ref.shape[0]
    wb = ob_ref.shape[0]
    y_t = (jnp.dot(wt_ref[:, 0:wa], oa_ref[...], preferred_element_type=F32)
           + jnp.dot(wt_ref[:, wa:wa + wb], ob_ref[...], preferred_element_type=F32)
           + jnp.dot(wt_ref[:, wa + wb:], oc_ref[...], preferred_element_type=F32))
    o_ref[...] = x_ref[...] + gt_ref[...] * y_t.T


def _out_projection(oa_t, ob_t, oc_t, wt_bf16, x, gt):
    s, d = x.shape
    tm = 256
    return pl.pallas_call(
        _outproj_kernel,
        grid=(s // tm,),
        in_specs=[pl.BlockSpec((oa_t.shape[0], tm), lambda i: (0, i)),
                  pl.BlockSpec((ob_t.shape[0], tm), lambda i: (0, i)),
                  pl.BlockSpec((oc_t.shape[0], tm), lambda i: (0, i)),
                  pl.BlockSpec(wt_bf16.shape, lambda i: (0, 0)),
                  pl.BlockSpec((tm, d), lambda i: (i, 0)),
                  pl.BlockSpec((1, d), lambda i: (0, 0))],
        out_specs=pl.BlockSpec((tm, d), lambda i: (i, 0)),
        out_shape=jax.ShapeDtypeStruct((s, d), F32),
        compiler_params=_cparams(1),
    )(oa_t, ob_t, oc_t, wt_bf16, x, gt)


def _route_kernel(x_ref, g_ref, sc_ref, sh_ref, wr_ref, br_ref, tri_ref,
                  h_ref, route_ref, count_ref, base_scr):
    i = pl.program_id(0)
    tm = x_ref.shape[0]

    @pl.when(i == 0)
    def _():
        base_scr[...] = jnp.zeros_like(base_scr)

    h = _modulated_norm(x_ref[...], g_ref[...], sc_ref[...], sh_ref[...])
    h_ref[...] = h.astype(BF16)
    logits = _dot3(wr_ref[...], h, NT_DIMS)
    s = _sigmoid(logits)
    sg = s + br_ref[...]
    rows = [sg[r:r + 1, :] for r in range(N_EXPERTS)]

    def top2_sum(vals):
        best = None
        for a in range(len(vals)):
            for b in range(a + 1, len(vals)):
                pair = vals[a] + vals[b]
                best = pair if best is None else jnp.maximum(best, pair)
        return best

    score = [top2_sum(rows[g * EXPERTS_PER_GROUP:(g + 1) * EXPERTS_PER_GROUP]) for g in range(N_GROUPS)]
    best, grp = score[0], jnp.zeros((1, tm), I32)
    for g in range(1, N_GROUPS):
        better = score[g] > best
        grp = jnp.where(better, g, grp)
        best = jnp.where(better, score[g], best)
    cand = []
    for j in range(EXPERTS_PER_GROUP):
        c = rows[j]
        for g in range(1, N_GROUPS):
            c = jnp.where(grp == g, rows[g * EXPERTS_PER_GROUP + j], c)
        cand.append(c)
    b0, l0 = cand[0], jnp.zeros((1, tm), I32)
    for j in range(1, EXPERTS_PER_GROUP):
        better = cand[j] > b0
        l0 = jnp.where(better, j, l0)
        b0 = jnp.where(better, cand[j], b0)
    b1, l1 = jnp.full((1, tm), -jnp.inf, F32), jnp.zeros((1, tm), I32)
    for j in range(EXPERTS_PER_GROUP):
        better = (l0 != j) & (cand[j] > b1)
        l1 = jnp.where(better, j, l1)
        b1 = jnp.where(better, cand[j], b1)
    e0 = grp * EXPERTS_PER_GROUP + l0
    e1 = grp * EXPERTS_PER_GROUP + l1
    erow = lax.broadcasted_iota(I32, (N_EXPERTS, tm), 0)
    sel0 = erow == e0
    sel1 = erow == e1
    s0 = jnp.sum(jnp.where(sel0, s, 0.0), axis=0, keepdims=True)
    s1 = jnp.sum(jnp.where(sel1, s, 0.0), axis=0, keepdims=True)
    tot = s0 + s1
    chosen = jnp.where(sel0 | sel1, 1.0, 0.0).astype(BF16)
    csum = jnp.dot(chosen, tri_ref[...], preferred_element_type=F32)
    pos = base_scr[...] + csum - 1.0
    r0 = jnp.sum(jnp.where(sel0, pos, 0.0), axis=0, keepdims=True)
    r1 = jnp.sum(jnp.where(sel1, pos, 0.0), axis=0, keepdims=True)
    base = base_scr[...] + csum[:, tm - 1:tm]
    base_scr[...] = base
    count_ref[...] = jnp.broadcast_to(base, count_ref.shape)
    zero = jnp.zeros((1, tm), F32)
    for r, val in enumerate((e0.astype(F32), e1.astype(F32), r0, r1, s0 / tot, s1 / tot, zero, zero)):
        route_ref[r:r + 1, :] = val


def _norm_and_route(x, g, sc, sh, w_router, b_router):
    s, d = x.shape
    tm = 512
    tri = jnp.asarray(np.triu(np.ones((tm, tm), np.float32)), BF16)
    row = lambda i: (0, 0)
    return pl.pallas_call(
        _route_kernel,
        grid=(s // tm,),
        in_specs=[pl.BlockSpec((tm, d), lambda i: (i, 0)),
                  pl.BlockSpec((1, d), row), pl.BlockSpec((1, d), row), pl.BlockSpec((1, d), row),
                  pl.BlockSpec((N_EXPERTS, d), row), pl.BlockSpec((N_EXPERTS, 1), row),
                  pl.BlockSpec((tm, tm), row)],
        out_specs=[pl.BlockSpec((tm, d), lambda i: (i, 0)),
                   pl.BlockSpec((8, tm), lambda i: (0, i)),
                   pl.BlockSpec((N_EXPERTS, LANES), row)],
        out_shape=[jax.ShapeDtypeStruct((s, d), BF16),
                   jax.ShapeDtypeStruct((8, s), F32),
                   jax.ShapeDtypeStruct((N_EXPERTS, LANES), F32)],
        scratch_shapes=[pltpu.VMEM((N_EXPERTS, 1), F32)],
        compiler_params=_cparams(1),
    )(x, g, sc, sh, w_router.T, b_router.reshape(N_EXPERTS, 1), tri)


FFN_ROWS = 512


def _ffn_kernel(tile_base, te_ref, nu_ref, xs_ref, wg_ref, wu_ref, wd_ref, *rest):
    y_ref = rest[-1]
    j = pl.program_id(0) + tile_base

    @pl.when(j < nu_ref[0])
    def _():
        x = xs_ref[...]
        de = wg_ref.shape[3]
        y = None
        for c in range(de // FFN_CHUNK):
            cs = slice(c * FFN_CHUNK, (c + 1) * FFN_CHUNK)
            a = jnp.dot(x, wg_ref[0, 0, :, cs].astype(BF16), preferred_element_type=F32)
            b = jnp.dot(x, wu_ref[0, 0, :, cs].astype(BF16), preferred_element_type=F32)
            act = (a * _sigmoid(a) * b).astype(BF16)
            part = jnp.dot(act, wd_ref[0, 0, cs, :].astype(BF16), preferred_element_type=F32)
            y = part if y is None else y + part
        y_ref[...] = y.astype(y_ref.dtype)

    @pl.when(j >= nu_ref[0])
    def _():
        y_ref[...] = jnp.zeros_like(y_ref)


FFN_CHUNK = 512
MOE_PARTS = 2


def _expert_ffn(tile_expert, n_used, xs_part, w_gate, w_up, w_down, layer, tile_base, n_tiles, y_prev):
    r, d = xs_part.shape
    de = w_gate.shape[3]
    expert = lambda j, te, nu: (layer, te[j + tile_base], 0, 0)
    in_specs = [pl.BlockSpec((FFN_ROWS, d), lambda j, te, nu: (j, 0)),
                pl.BlockSpec((1, 1, d, de), expert, pipeline_mode=pl.Buffered(1)),
                pl.BlockSpec((1, 1, d, de), expert, pipeline_mode=pl.Buffered(1)),
                pl.BlockSpec((1, 1, de, d), expert, pipeline_mode=pl.Buffered(1))]
    args = [tile_expert, n_used, xs_part, w_gate, w_up, w_down]
    aliases = {}
    if y_prev is not None:
        in_specs.append(pl.BlockSpec(memory_space=pl.ANY))
        aliases = {len(args): 0}
        args.append(y_prev)
    grid_spec = pltpu.PrefetchScalarGridSpec(
        num_scalar_prefetch=2,
        grid=(r // FFN_ROWS,),
        in_specs=in_specs,
        out_specs=pl.BlockSpec((FFN_ROWS, d), lambda j, te, nu: (j + tile_base, 0)),
    )
    return pl.pallas_call(
        functools.partial(_ffn_kernel, tile_base),
        grid_spec=grid_spec,
        out_shape=jax.ShapeDtypeStruct((n_tiles * FFN_ROWS, d), BF16),
        input_output_aliases=aliases,
        compiler_params=_cparams(1),
    )(*args)


def _combine_kernel(final, x_ref, y0_ref, y1_ref, w0_ref, w1_ref, gt_ref, gf_ref, *rest):
    o_ref = rest[-1]
    x = x_ref[...] + gt_ref[...] * (w0_ref[...] * y0_ref[...].astype(F32)
                                    + w1_ref[...] * y1_ref[...].astype(F32))
    if final:
        var = jnp.mean(x * x, axis=-1, keepdims=True)
        x = x * lax.rsqrt(var + RMS_EPS) * gf_ref[...]
    o_ref[...] = x


def _combine(x, y01_part, w0, w1, gt, g_final, final, row_base, out_prev):
    s, d = x.shape
    tm = 256
    part = y01_part.shape[0] // 2
    base = row_base // tm
    big = pl.BlockSpec((tm, d), lambda i: (i + base, 0))
    first = pl.BlockSpec((tm, d), lambda i: (i, 0))
    second = pl.BlockSpec((tm, d), lambda i: (i + part // tm, 0))
    col = pl.BlockSpec((tm, 1), lambda i: (i + base, 0))
    vec = pl.BlockSpec((1, d), lambda i: (0, 0))
    in_specs = [big, first, second, col, col, vec, vec]
    args = [x, y01_part, y01_part, w0, w1, gt, g_final]
    aliases = {}
    if out_prev is not None:
        in_specs.append(pl.BlockSpec(memory_space=pl.ANY))
        aliases = {len(args): 0}
        args.append(out_prev)
    return pl.pallas_call(
        functools.partial(_combine_kernel, final),
        grid=(part // tm,),
        in_specs=in_specs,
        out_specs=big,
        out_shape=jax.ShapeDtypeStruct((s, d), F32),
        input_output_aliases=aliases,
        compiler_params=_cparams(1),
    )(*args)


def _rope_tables(positions):
    half = HEAD_DIM // 2
    inv = ROPE_THETA ** (-jnp.arange(0, HEAD_DIM, 2, dtype=F32) / HEAD_DIM)
    ang = positions.astype(F32)[:, None] * inv
    cos, sin = jnp.cos(ang), jnp.sin(ang)
    cos_t = jnp.tile(cos, (1, LANES // half))
    sin_t = jnp.tile(jnp.concatenate([-sin, sin], axis=1), (1, LANES // HEAD_DIM))
    return cos_t, sin_t


def _block_means(ksum):
    nblk, width = ksum.shape
    km = (ksum / MOBA_BLOCK).reshape(nblk, width // HEAD_DIM, HEAD_DIM)
    km = jnp.pad(km, ((0, HEAD_DIM - nblk), (0, 0), (0, 0)))
    return jnp.transpose(km, (1, 0, 2))


def _moe(x1, h2, route, counts, w_gate, w_up, w_down, layer, gt, g_final, final):
    s, d = x1.shape
    e0, e1 = route[0].astype(I32), route[1].astype(I32)
    r0, r1 = route[2].astype(I32), route[3].astype(I32)
    cnt = counts[:, 0].astype(I32)
    cnt_pad = ((cnt + FFN_ROWS - 1) // FFN_ROWS) * FFN_ROWS
    off = jnp.concatenate([jnp.zeros((1,), I32), jnp.cumsum(cnt_pad).astype(I32)])
    d0 = off[e0] + r0
    d1 = off[e1] + r1
    n_rows = 2 * s + N_EXPERTS * FFN_ROWS
    tok = jnp.arange(s, dtype=I32)
    src = (jnp.arange(n_rows, dtype=I32) % s).at[jnp.concatenate([d0, d1])].set(jnp.concatenate([tok, tok]))
    n_tiles = n_rows // FFN_ROWS
    tile_start = jnp.arange(n_tiles, dtype=I32) * FFN_ROWS
    tile_expert = jnp.sum((off[None, 1:] <= tile_start[:, None]).astype(I32), axis=1)
    tile_expert = jnp.minimum(tile_expert, N_EXPERTS - 1).astype(I32)
    n_used = (off[N_EXPERTS] // FFN_ROWS).reshape(1).astype(I32)
    y = None
    for part in range(MOE_PARTS):
        tiles = n_tiles // MOE_PARTS
        rows = slice(part * tiles * FFN_ROWS, (part + 1) * tiles * FFN_ROWS)
        y = _expert_ffn(tile_expert, n_used, h2[src[rows]], w_gate, w_up, w_down, layer,
                        part * tiles, n_tiles, y)
    w0, w1 = route[4].reshape(s, 1), route[5].reshape(s, 1)
    out = None
    for part in range(MOE_PARTS):
        toks = slice(part * s // MOE_PARTS, (part + 1) * s // MOE_PARTS)
        y01 = y[jnp.concatenate([d0[toks], d1[toks]])]
        out = _combine(x1, y01, w0, w1, gt, g_final, final, part * s // MOE_PARTS, out)
    return out


def kernel(x, c, positions, w_ada, b_ada, g_attn, g_mlp, w_in, w_out, lam_q1, lam_k1, lam_q2, lam_k2,
           g_subln, w_gate, w_up, w_down, w_router, b_router, g_final):
    batch, s, d = x.shape
    assert batch == 1 and s % TQ == 0 and s // MOBA_BLOCK <= HEAD_DIM
    assert w_in.shape[2] == 3 * N_KIND_TILES * PROJ_TN
    depth = w_ada.shape[0]
    xs = x.reshape(s, d)
    cos_t, sin_t = _rope_tables(positions[0])
    mod = _modulation(c, w_ada, b_ada)
    tab = _dilated_bias_table()
    gf = g_final.reshape(1, d)
    row = lambda v: v.reshape(1, -1)
    for l in range(depth):
        lam_init = 0.8 - 0.6 * math.exp(-0.3 * l)
        sh_a, sc_a, gt_a, sh_m, sc_m, gt_m = [mod[l, :, k * d:(k + 1) * d] for k in range(6)]
        k, qt, qbt, vt, ksum = _in_projection(xs, row(g_attn[l]), sc_a, sh_a,
                                              _layer_to_bf16(w_in, l, PROJ_TN), cos_t, sin_t)
        o_a = _dilated_attention(qt, k, vt, tab, 0, 0, 0)
        bias = _moba_gate(qbt, _block_means(ksum.reshape(s // MOBA_BLOCK, -1)))
        o_b = _moba_attention(qt, bias, k, vt, 4, 4, 4)
        o_c = _diff_attention(qt, k, vt, row(lam_q1[l]), row(lam_k1[l]), row(lam_q2[l]), row(lam_k2[l]),
                              g_subln[l].reshape(-1, 1), lam_init, 8, 8, 8)
        x1 = _out_projection(o_a, o_b, o_c, w_out[l].T.astype(BF16), xs, gt_a)
        h2, route, counts = _norm_and_route(x1, row(g_mlp[l]), sc_m, sh_m, w_router, b_router)
        xs = _moe(x1, h2, route, counts, w_gate, w_up, w_down, l, gt_m, gf, l == depth - 1)
    return xs.reshape(batch, s, d)
```

```python
import functools
import math

import numpy as np
import jax
import jax.numpy as jnp
from jax import lax
from jax.experimental import pallas as pl
from jax.experimental.pallas import tpu as pltpu

F32 = jnp.float32
BF16 = jnp.bfloat16
I32 = jnp.int32

HEAD_DIM = 64
LANES = 128
A_PATTERNS = ((128, 1), (512, 4), (2048, 16))
MOBA_BLOCK = 256
MOBA_TOPK = 3
N_EXPERTS = 16
N_GROUPS = 4
EXPERTS_PER_GROUP = N_EXPERTS // N_GROUPS
ROPE_THETA = 10000.0
RMS_EPS = 1e-6
NEG = -1e30
VMEM_LIMIT = 56 * 1024 * 1024

TQ = 1024
TK = 512
KPQ = TQ // TK
assert KPQ == 2
ALL_QUERIES = slice(None)
LATE_QUERIES = slice(TQ // 2, TQ)
Q_CHUNK = 256
PROJ_TM = 512
PROJ_TN = 512
NT_DIMS = (((1,), (1,)), ((), ()))
Q_SCALE = HEAD_DIM ** -0.5 * math.log2(math.e)


def _cparams(n_axes):
    return pltpu.CompilerParams(dimension_semantics=("arbitrary",) * n_axes,
                                vmem_limit_bytes=VMEM_LIMIT)


def _split_bf16(x):
    hi = x.astype(BF16)
    lo = (x - hi.astype(F32)).astype(BF16)
    return hi, lo


def _dot3(a, b, dims=(((1,), (0,)), ((), ()))):
    a_hi, a_lo = _split_bf16(a)
    b_hi, b_lo = _split_bf16(b)
    dg = functools.partial(lax.dot_general, dimension_numbers=dims, preferred_element_type=F32)
    return dg(a_hi, b_hi) + dg(a_hi, b_lo) + dg(a_lo, b_hi)


def _sigmoid(x):
    return 1.0 / (1.0 + jnp.exp(-x))


CAST_STREAMS = 4
CAST_BLOCK_ELEMS = 512 * 1024
CAST_COLS = 1024


def _cast_kernel(*refs):
    o_ref = refs[-1]
    col_tile = o_ref.shape[2]
    for k, x_ref in enumerate(refs[:-1]):
        rows = x_ref.shape[1]
        x = x_ref[0].astype(o_ref.dtype)
        for t in range(o_ref.shape[0]):
            o_ref[t, k * rows:(k + 1) * rows, :] = x[:, t * col_tile:(t + 1) * col_tile]


def _layer_to_bf16(w, l, col_tile=None):
    cols = w.shape[-1]
    col_tile = col_tile or cols
    w3 = w.reshape(w.shape[0], -1, cols)
    rows = w3.shape[1]
    tc = max(min(cols, CAST_COLS), col_tile)
    sr = CAST_BLOCK_ELEMS // tc
    tr = CAST_STREAMS * sr
    stream = lambda k: pl.BlockSpec((1, sr, tc), lambda i, j: (l, CAST_STREAMS * i + k, j))
    return pl.pallas_call(
        _cast_kernel,
        grid=(rows // tr, cols // tc),
        in_specs=[stream(k) for k in range(CAST_STREAMS)],
        out_specs=pl.BlockSpec((tc // col_tile, tr, col_tile), lambda i, j: (j, i, 0)),
        out_shape=jax.ShapeDtypeStruct((cols // col_tile, rows, col_tile), BF16),
        compiler_params=_cparams(2),
    )(*([w3] * CAST_STREAMS))


def _mod_kernel(c_ref, w_ref, b_ref, o_ref):
    c = c_ref[...]
    sc = c * _sigmoid(c)
    d = c.shape[0]
    acc = b_ref[0]
    for r in range(0, d, 256):
        acc = acc + jnp.sum(w_ref[0, r:r + 256, :] * sc[r:r + 256, :], axis=0, keepdims=True)
    o_ref[0] = acc


def _modulation(c, w_ada, b_ada):
    depth, d, n = w_ada.shape
    tn = 512
    return pl.pallas_call(
        _mod_kernel,
        grid=(depth, n // tn),
        in_specs=[pl.BlockSpec((d, 1), lambda l, j: (0, 0)),
                  pl.BlockSpec((1, d, tn), lambda l, j: (l, 0, j)),
                  pl.BlockSpec((1, 1, tn), lambda l, j: (l, 0, j))],
        out_specs=pl.BlockSpec((1, 1, tn), lambda l, j: (l, 0, j)),
        out_shape=jax.ShapeDtypeStruct((depth, 1, n), F32),
        compiler_params=_cparams(2),
    )(c.reshape(d, 1), w_ada, b_ada.reshape(depth, 1, n))


def _modulated_norm(x, g, sc, sh):
    var = jnp.mean(x * x, axis=-1, keepdims=True)
    return (x * lax.rsqrt(var + RMS_EPS) * g) * (1.0 + sc) + sh


def _rope_tile(acc, cos, sin):
    lane = lax.broadcasted_iota(I32, (1, LANES), 1)
    first = (lane & (HEAD_DIM - 1)) < (HEAD_DIM // 2)
    outs = []
    for c in range(acc.shape[1] // LANES):
        xc = acc[:, c * LANES:(c + 1) * LANES]
        rot = jnp.where(first, pltpu.roll(xc, LANES - HEAD_DIM // 2, 1), pltpu.roll(xc, HEAD_DIM // 2, 1))
        outs.append(xc * cos + rot * sin)
    return jnp.concatenate(outs, axis=1)


N_KIND_TILES = 4
COLUMN_TILE_ORDER = (1, 4, 8, 9, 0, 3, 6, 7, 2, 5, 10, 11)
KB_TILE = 1
QB_TILE = N_KIND_TILES + 1


def _inproj_kernel(x_ref, g_ref, sc_ref, sh_ref, w_ref, cos_ref, sin_ref,
                   k_ref, qt_ref, qbt_ref, vt_ref, ksum_ref, h_scr):
    tm = x_ref.shape[0]
    tn = w_ref.shape[2]
    h_scr[...] = _modulated_norm(x_ref[...], g_ref[...], sc_ref[...], sh_ref[...]).astype(BF16)
    for pos, tile in enumerate(COLUMN_TILE_ORDER):
        acc = jnp.dot(h_scr[...], w_ref[tile], preferred_element_type=F32)
        kind, t = divmod(pos, N_KIND_TILES)
        cols = slice(t * tn, (t + 1) * tn)
        if kind == 0:
            r = _rope_tile(acc, cos_ref[...], sin_ref[...])
            k_ref[:, cols] = r.astype(BF16)
            if pos == KB_TILE:
                ksum_ref[0] = jnp.sum(r.reshape(tm // MOBA_BLOCK, MOBA_BLOCK, tn), axis=1)
        elif kind == 1:
            rt = _rope_tile(acc, cos_ref[...], sin_ref[...]).T
            qt_ref[cols, :] = (rt * Q_SCALE).astype(BF16)
            if pos == QB_TILE:
                qbt_ref[...] = rt
        else:
            vt = acc.T.astype(BF16)
            for b in range(vt_ref.shape[0]):
                vt_ref[b, cols, :] = vt[:, b * TK:(b + 1) * TK]


def _in_projection(x, g, sc, sh, w_bf16, cos_t, sin_t):
    s, d = x.shape
    tm, tn = PROJ_TM, w_bf16.shape[2]
    nb = tm // MOBA_BLOCK
    width = N_KIND_TILES * tn
    row = lambda i: (0, 0)
    return pl.pallas_call(
        _inproj_kernel,
        grid=(s // tm,),
        in_specs=[pl.BlockSpec((tm, d), lambda i: (i, 0)),
                  pl.BlockSpec((1, d), row), pl.BlockSpec((1, d), row), pl.BlockSpec((1, d), row),
                  pl.BlockSpec(w_bf16.shape, lambda i: (0, 0, 0),
                               pipeline_mode=pl.Buffered(1)),
                  pl.BlockSpec((tm, LANES), lambda i: (i, 0)),
                  pl.BlockSpec((tm, LANES), lambda i: (i, 0))],
        out_specs=[pl.BlockSpec((tm, width), lambda i: (i, 0)),
                   pl.BlockSpec((width, tm), lambda i: (0, i)),
                   pl.BlockSpec((tn, tm), lambda i: (0, i)),
                   pl.BlockSpec((tm // TK, width, TK), lambda i: (i, 0, 0)),
                   pl.BlockSpec((1, nb, tn), lambda i: (i, 0, 0))],
        out_shape=[jax.ShapeDtypeStruct((s, width), BF16),
                   jax.ShapeDtypeStruct((width, s), BF16),
                   jax.ShapeDtypeStruct((tn, s), F32),
                   jax.ShapeDtypeStruct((s // TK, width, TK), BF16),
                   jax.ShapeDtypeStruct((s // tm, nb, tn), F32)],
        scratch_shapes=[pltpu.VMEM((tm, d), BF16)],
        compiler_params=_cparams(1),
    )(x, g, sc, sh, w_bf16, cos_t, sin_t)


def _flash_step(s_t, v_t, state):
    m_prev, l_prev, acc_prev = state
    m_new = jnp.maximum(m_prev, jnp.max(s_t, axis=0, keepdims=True))
    alpha = jnp.exp2(m_prev - m_new)
    p = jnp.exp2(s_t - m_new)
    l_new = alpha * l_prev + jnp.sum(p, axis=0, keepdims=True)
    acc_new = alpha * acc_prev + jnp.dot(v_t, p.astype(BF16), preferred_element_type=F32)
    return m_new, l_new, acc_new


def _sweep(first, n_loop, scores, consume, bufs, tail_masks):
    halves = (bufs[0:2], bufs[2:4])

    def park_pair(a, dst):
        for d in range(2):
            for h, val in enumerate(scores(a + d)):
                dst[d][h] = val

    def pair(a, cur, nxt):
        park_pair(a + 2, nxt)
        consume(a, cur[0], None, ALL_QUERIES)
        consume(a + 1, cur[1], None, ALL_QUERIES)

    def tail(cur):
        t0 = first + n_loop
        consume(t0, cur[0], tail_masks[0], ALL_QUERIES)
        consume(t0 + 1, cur[1], tail_masks[1], LATE_QUERIES)

    park_pair(first, halves[0])

    def two_pairs(j, carry):
        pair(first + 4 * j, halves[0], halves[1])
        pair(first + 4 * j + 2, halves[1], halves[0])
        return carry

    n_pairs = n_loop // 2
    lax.fori_loop(0, n_pairs // 2, two_pairs, 0)

    @pl.when(n_pairs % 2 == 1)
    def _():
        pair(first + n_loop - 2, halves[0], halves[1])
        tail(halves[1])

    @pl.when(n_pairs % 2 == 0)
    def _():
        tail(halves[0])


def _softmax_pv(buf, v_ts, refs, mask, cols):
    start, stop, _ = cols.indices(TQ)
    for c0 in range(start, stop, Q_CHUNK):
        cs = slice(c0, c0 + Q_CHUNK)
        for h, v_t in enumerate(v_ts):
            s_t = buf[h, :, cs]
            if mask is not None:
                s_t = jnp.where(mask[:, cs], s_t, NEG)
            _store_state(refs[h], _flash_step(s_t, v_t, _load_state(refs[h], cs)), cs)


N_SCORE_BUFS = 4


def _score_scratch(n_streams):
    return [pltpu.VMEM((n_streams, TK, TQ), F32) for _ in range(N_SCORE_BUFS)]


def _qk_t(k, q_t):
    return jnp.dot(k, q_t, preferred_element_type=F32)


def _k_block(k_ref, n):
    return k_ref[pl.ds(pl.multiple_of(n * TK, TK), TK), :]


def _init_state(refs):
    m_ref, l_ref, acc_ref = refs
    m_ref[...] = jnp.full(m_ref.shape, NEG, F32)
    l_ref[...] = jnp.zeros(l_ref.shape, F32)
    acc_ref[...] = jnp.zeros(acc_ref.shape, F32)


def _load_state(refs, cols=ALL_QUERIES):
    return tuple(r[:, cols] for r in refs)


def _store_state(refs, state, cols=ALL_QUERIES):
    for r, val in zip(refs, state):
        r[:, cols] = val


def _flash_scratch(n_streams, dv):
    out = []
    for _ in range(n_streams):
        out += [pltpu.VMEM((1, TQ), F32), pltpu.VMEM((1, TQ), F32), pltpu.VMEM((dv, TQ), F32)]
    return out


def _causal_t(d):
    r = lax.broadcasted_iota(I32, (TK, TQ), 0) + d * TK
    return r <= lax.broadcasted_iota(I32, (TK, TQ), 1)


def _stack_rows(top, bottom):
    return jnp.concatenate([top, bottom], axis=0)


def _dilated_bias_table():
    n_off = A_PATTERNS[-1][0] // TK + KPQ
    r = np.arange(TK)[:, None]
    c = np.arange(TQ)[None, :]
    tabs = []
    for t in range(n_off):
        delta = (t - (KPQ - 1)) * TK + c - r
        mult = np.zeros_like(delta)
        for window, dil in A_PATTERNS:
            mult += ((delta >= 0) & (delta % dil == 0) & (delta <= window)).astype(delta.dtype)
        tabs.append(np.where(mult > 0, np.log2(np.maximum(mult, 1).astype(np.float64)), NEG))
    return jnp.asarray(np.stack(tabs), F32)


def _dilated_kernel(qt_ref, k_ref, vt_ref, tab_ref, o_ref, *scratch):
    bufs, scr = scratch[:N_SCORE_BUFS], scratch[N_SCORE_BUFS:]
    i = pl.program_id(1)
    refs = (scr[0:3], scr[3:6])
    q_t = qt_ref[...]
    zero = jnp.zeros((HEAD_DIM, TQ), BF16)
    qs = (_stack_rows(q_t[:HEAD_DIM], zero), _stack_rows(zero, q_t[HEAD_DIM:]))
    n_off = tab_ref.shape[0]
    last = KPQ * i + (KPQ - 1)
    for h in range(2):
        _init_state(refs[h])

    def scores(n):
        kb = _k_block(k_ref, n)
        bias = tab_ref[last - n]
        return [_qk_t(kb, qs[h]) + bias for h in range(2)]

    def consume(n, buf, mask, cols):
        vb = vt_ref[n]
        _softmax_pv(buf, [vb[:HEAD_DIM, :], vb[HEAD_DIM:, :]], refs, mask, cols)

    first = jnp.maximum(last - (n_off - 1), 0)
    _sweep(first, last - 1 - first, scores, consume, bufs, (None, None))
    for h in range(2):
        _, l, acc = _load_state(refs[h])
        o_ref[h * HEAD_DIM:(h + 1) * HEAD_DIM, :] = (acc / l).astype(o_ref.dtype)


def _dilated_attention(qt, k, vt, tab, qrow, kcol, vrow):
    s = k.shape[0]
    n_pairs = 4
    return pl.pallas_call(
        _dilated_kernel,
        grid=(n_pairs, s // TQ),
        in_specs=[pl.BlockSpec((LANES, TQ), lambda hp, i: (qrow + hp, i)),
                  pl.BlockSpec((s, LANES), lambda hp, i: (0, kcol + hp)),
                  pl.BlockSpec((s // TK, LANES, TK), lambda hp, i: (0, vrow + hp, 0)),
                  pl.BlockSpec(tab.shape, lambda hp, i: (0, 0, 0), pipeline_mode=pl.Buffered(1))],
        out_specs=pl.BlockSpec((LANES, TQ), lambda hp, i: (hp, i)),
        out_shape=jax.ShapeDtypeStruct((n_pairs * LANES, s), BF16),
        scratch_shapes=_score_scratch(2) + _flash_scratch(2, HEAD_DIM),
        compiler_params=_cparams(2),
    )(qt, k, vt, tab)


def _moba_gate_kernel(qt_ref, km_ref, bias_ref):
    i = pl.program_id(0)
    tg = qt_ref.shape[1]
    n_heads, nblk, _ = km_ref.shape
    blk = lax.broadcasted_iota(I32, (nblk, tg), 0)
    blk_f = blk.astype(F32)
    tok = i * tg + lax.broadcasted_iota(I32, (nblk, tg), 1)
    own = jnp.right_shift(tok, int(math.log2(MOBA_BLOCK)))
    past = blk < own
    for h in range(n_heads):
        rows = slice(h * HEAD_DIM, (h + 1) * HEAD_DIM)
        gate = _dot3(km_ref[h], qt_ref[rows, :])
        sel = blk == own
        for _ in range(MOBA_TOPK):
            cand = past & jnp.logical_not(sel)
            g = jnp.where(cand, gate, -jnp.inf)
            mx = jnp.max(g, axis=0, keepdims=True)
            first = jnp.min(jnp.where(cand & (g == mx), blk_f, 2.0 * nblk), axis=0, keepdims=True)
            sel = sel | (blk_f == first)
        bias_ref[rows, :] = jnp.where(sel, 0.0, NEG).astype(BF16)


def _moba_gate(qbt, kmean):
    width, s = qbt.shape
    tg = 512
    return pl.pallas_call(
        _moba_gate_kernel,
        grid=(s // tg,),
        in_specs=[pl.BlockSpec((width, tg), lambda i: (0, i)),
                  pl.BlockSpec(kmean.shape, lambda i: (0, 0, 0))],
        out_specs=pl.BlockSpec((width, tg), lambda i: (0, i)),
        out_shape=jax.ShapeDtypeStruct((width, s), BF16),
        compiler_params=_cparams(1),
    )(qbt, kmean)


def _moba_kernel(qt_ref, bias_ref, k_ref, vt_ref, o_ref, *scratch):
    i = pl.program_id(1)
    bufs, scr = scratch[:N_SCORE_BUFS], scratch[N_SCORE_BUFS:]
    refs = (scr[0:3], scr[3:6])
    q_t = qt_ref[...]
    b_t = bias_ref[...]
    qs = (_stack_rows(q_t[:HEAD_DIM], b_t[:HEAD_DIM]), _stack_rows(b_t[HEAD_DIM:], q_t[HEAD_DIM:]))
    lane = lax.broadcasted_iota(I32, (1, LANES), 1)
    lo = lane < HEAD_DIM
    mine = (lo, jnp.logical_not(lo))
    blk = lane & (HEAD_DIM - 1)
    sub_block = jnp.right_shift(lax.broadcasted_iota(I32, (TK, LANES), 0), int(math.log2(MOBA_BLOCK)))
    for h in range(2):
        _init_state(refs[h])

    def scores(n):
        kb = _k_block(k_ref, n)
        onehot = jnp.where(blk == n * (TK // MOBA_BLOCK) + sub_block, 1.0, 0.0).astype(BF16)
        return [_qk_t(jnp.where(mine[h], kb, onehot), qs[h]) for h in range(2)]

    def consume(n, buf, mask, cols):
        vb = vt_ref[n]
        _softmax_pv(buf, [vb[:HEAD_DIM, :], vb[HEAD_DIM:, :]], refs, mask, cols)

    _sweep(0, KPQ * i, scores, consume, bufs, (_causal_t(0), _causal_t(1)))
    for h in range(2):
        _, l, acc = _load_state(refs[h])
        o_ref[h * HEAD_DIM:(h + 1) * HEAD_DIM, :] = (acc / l).astype(o_ref.dtype)


def _moba_attention(qt, bias, k, vt, qrow, kcol, vrow):
    s = k.shape[0]
    n_pairs = 4
    return pl.pallas_call(
        _moba_kernel,
        grid=(n_pairs, s // TQ),
        in_specs=[pl.BlockSpec((LANES, TQ), lambda hp, i: (qrow + hp, i)),
                  pl.BlockSpec((LANES, TQ), lambda hp, i: (hp, i)),
                  pl.BlockSpec((s, LANES), lambda hp, i: (0, kcol + hp)),
                  pl.BlockSpec((s // TK, LANES, TK), lambda hp, i: (0, vrow + hp, 0))],
        out_specs=pl.BlockSpec((LANES, TQ), lambda hp, i: (hp, i)),
        out_shape=jax.ShapeDtypeStruct((n_pairs * LANES, s), BF16),
        scratch_shapes=_score_scratch(2) + _flash_scratch(2, HEAD_DIM),
        compiler_params=_cparams(2),
    )(qt, bias, k, vt)


def _diff_kernel(lam_init, qt_ref, k_ref, vt_ref, lq1_ref, lk1_ref, lq2_ref, lk2_ref, g_ref, o_ref,
                 *scratch):
    i = pl.program_id(1)
    bufs, scr = scratch[:N_SCORE_BUFS], scratch[N_SCORE_BUFS:]
    refs = (scr[0:3], scr[3:6])
    q_t = qt_ref[...]
    zero = jnp.zeros((HEAD_DIM, TQ), BF16)
    qs = (_stack_rows(q_t[:HEAD_DIM], zero), _stack_rows(zero, q_t[HEAD_DIM:]))
    for h in range(2):
        _init_state(refs[h])

    def scores(n):
        kb = _k_block(k_ref, n)
        return [_qk_t(kb, qs[h]) for h in range(2)]

    def consume(n, buf, mask, cols):
        vb = vt_ref[n]
        _softmax_pv(buf, [vb, vb], refs, mask, cols)

    _sweep(0, KPQ * i, scores, consume, bufs, (_causal_t(0), _causal_t(1)))
    lam = (jnp.exp(jnp.sum(lq1_ref[...] * lk1_ref[...], axis=1, keepdims=True))
           - jnp.exp(jnp.sum(lq2_ref[...] * lk2_ref[...], axis=1, keepdims=True)) + lam_init)
    _, l1, acc1 = _load_state(refs[0])
    _, l2, acc2 = _load_state(refs[1])
    o = acc1 / l1 - lam * (acc2 / l2)
    var = jnp.mean(o * o, axis=0, keepdims=True)
    o = (o * lax.rsqrt(var + RMS_EPS) * g_ref[...]) * (1.0 - lam_init)
    o_ref[...] = o.astype(o_ref.dtype)


def _diff_attention(qt, k, vt, lq1, lk1, lq2, lk2, g_sub, lam_init, qrow, kcol, vrow):
    s = k.shape[0]
    n_heads = 8
    vec = lambda n: pl.BlockSpec((1, n), lambda h, i: (0, 0))
    return pl.pallas_call(
        functools.partial(_diff_kernel, lam_init),
        grid=(n_heads, s // TQ),
        in_specs=[pl.BlockSpec((LANES, TQ), lambda h, i: (qrow + h, i)),
                  pl.BlockSpec((s, LANES), lambda h, i: (0, kcol + h)),
                  pl.BlockSpec((s // TK, LANES, TK), lambda h, i: (0, vrow + h, 0)),
                  vec(HEAD_DIM), vec(HEAD_DIM), vec(HEAD_DIM), vec(HEAD_DIM),
                  pl.BlockSpec((LANES, 1), lambda h, i: (0, 0))],
        out_specs=pl.BlockSpec((LANES, TQ), lambda h, i: (h, i)),
        out_shape=jax.ShapeDtypeStruct((n_heads * LANES, s), BF16),
        scratch_shapes=_score_scratch(2) + _flash_scratch(2, LANES),
        compiler_params=_cparams(2),
    )(qt, k, vt, lq1, lk1, lq2, lk2, g_sub)


def _outproj_kernel(oa_ref, ob_ref, oc_ref, wt_ref, x_ref, gt_ref, o_ref):
    wa = oa_ref.shape[0]
    wb = ob_ref.shape[0]
    y_t = (jnp.dot(wt_ref[:, 0:wa], oa_ref[...], preferred_element_type=F32)
           + jnp.dot(wt_ref[:, wa:wa + wb], ob_ref[...], preferred_element_type=F32)
           + jnp.dot(wt_ref[:, wa + wb:], oc_ref[...], preferred_element_type=F32))
    o_ref[...] = x_ref[...] + gt_ref[...] * y_t.T


def _out_projection(oa_t, ob_t, oc_t, wt_bf16, x, gt):
    s, d = x.shape
    tm = 256
    return pl.pallas_call(
        _outproj_kernel,
        grid=(s // tm,),
        in_specs=[pl.BlockSpec((oa_t.shape[0], tm), lambda i: (0, i)),
                  pl.BlockSpec((ob_t.shape[0], tm), lambda i: (0, i)),
                  pl.BlockSpec((oc_t.shape[0], tm), lambda i: (0, i)),
                  pl.BlockSpec(wt_bf16.shape, lambda i: (0, 0)),
                  pl.BlockSpec((tm, d), lambda i: (i, 0)),
                  pl.BlockSpec((1, d), lambda i: (0, 0))],
        out_specs=pl.BlockSpec((tm, d), lambda i: (i, 0)),
        out_shape=jax.ShapeDtypeStruct((s, d), F32),
        compiler_params=_cparams(1),
    )(oa_t, ob_t, oc_t, wt_bf16, x, gt)


def _route_kernel(x_ref, g_ref, sc_ref, sh_ref, wr_ref, br_ref, tri_ref,
                  h_ref, route_ref, count_ref, base_scr):
    i = pl.program_id(0)
    tm = x_ref.shape[0]

    @pl.when(i == 0)
    def _():
        base_scr[...] = jnp.zeros_like(base_scr)

    h = _modulated_norm(x_ref[...], g_ref[...], sc_ref[...], sh_ref[...])
    h_ref[...] = h.astype(BF16)
    logits = _dot3(wr_ref[...], h, NT_DIMS)
    s = _sigmoid(logits)
    sg = s + br_ref[...]
    rows = [sg[r:r + 1, :] for r in range(N_EXPERTS)]

    def top2_sum(vals):
        best = None
        for a in range(len(vals)):
            for b in range(a + 1, len(vals)):
                pair = vals[a] + vals[b]
                best = pair if best is None else jnp.maximum(best, pair)
        return best

    score = [top2_sum(rows[g * EXPERTS_PER_GROUP:(g + 1) * EXPERTS_PER_GROUP]) for g in range(N_GROUPS)]
    best, grp = score[0], jnp.zeros((1, tm), I32)
    for g in range(1, N_GROUPS):
        better = score[g] > best
        grp = jnp.where(better, g, grp)
        best = jnp.where(better, score[g], best)
    cand = []
    for j in range(EXPERTS_PER_GROUP):
        c = rows[j]
        for g in range(1, N_GROUPS):
            c = jnp.where(grp == g, rows[g * EXPERTS_PER_GROUP + j], c)
        cand.append(c)
    b0, l0 = cand[0], jnp.zeros((1, tm), I32)
    for j in range(1, EXPERTS_PER_GROUP):
        better = cand[j] > b0
        l0 = jnp.where(better, j, l0)
        b0 = jnp.where(better, cand[j], b0)
    b1, l1 = jnp.full((1, tm), -jnp.inf, F32), jnp.zeros((1, tm), I32)
    for j in range(EXPERTS_PER_GROUP):
        better = (l0 != j) & (cand[j] > b1)
        l1 = jnp.where(better, j, l1)
        b1 = jnp.where(better, cand[j], b1)
    e0 = grp * EXPERTS_PER_GROUP + l0
    e1 = grp * EXPERTS_PER_GROUP + l1
    erow = lax.broadcasted_iota(I32, (N_EXPERTS, tm), 0)
    sel0 = erow == e0
    sel1 = erow == e1
    s0 = jnp.sum(jnp.where(sel0, s, 0.0), axis=0, keepdims=True)
    s1 = jnp.sum(jnp.where(sel1, s, 0.0), axis=0, keepdims=True)
    tot = s0 + s1
    chosen = jnp.where(sel0 | sel1, 1.0, 0.0).astype(BF16)
    csum = jnp.dot(chosen, tri_ref[...], preferred_element_type=F32)
    pos = base_scr[...] + csum - 1.0
    r0 = jnp.sum(jnp.where(sel0, pos, 0.0), axis=0, keepdims=True)
    r1 = jnp.sum(jnp.where(sel1, pos, 0.0), axis=0, keepdims=True)
    base = base_scr[...] + csum[:, tm - 1:tm]
    base_scr[...] = base
    count_ref[...] = jnp.broadcast_to(base, count_ref.shape)
    zero = jnp.zeros((1, tm), F32)
    for r, val in enumerate((e0.astype(F32), e1.astype(F32), r0, r1, s0 / tot, s1 / tot, zero, zero)):
        route_ref[r:r + 1, :] = val


def _norm_and_route(x, g, sc, sh, w_router, b_router):
    s, d = x.shape
    tm = 512
    tri = jnp.asarray(np.triu(np.ones((tm, tm), np.float32)), BF16)
    row = lambda i: (0, 0)
    return pl.pallas_call(
        _route_kernel,
        grid=(s // tm,),
        in_specs=[pl.BlockSpec((tm, d), lambda i: (i, 0)),
                  pl.BlockSpec((1, d), row), pl.BlockSpec((1, d), row), pl.BlockSpec((1, d), row),
                  pl.BlockSpec((N_EXPERTS, d), row), pl.BlockSpec((N_EXPERTS, 1), row),
                  pl.BlockSpec((tm, tm), row)],
        out_specs=[pl.BlockSpec((tm, d), lambda i: (i, 0)),
                   pl.BlockSpec((8, tm), lambda i: (0, i)),
                   pl.BlockSpec((N_EXPERTS, LANES), row)],
        out_shape=[jax.ShapeDtypeStruct((s, d), BF16),
                   jax.ShapeDtypeStruct((8, s), F32),
                   jax.ShapeDtypeStruct((N_EXPERTS, LANES), F32)],
        scratch_shapes=[pltpu.VMEM((N_EXPERTS, 1), F32)],
        compiler_params=_cparams(1),
    )(x, g, sc, sh, w_router.T, b_router.reshape(N_EXPERTS, 1), tri)


FFN_ROWS = 512


def _ffn_kernel(tile_base, te_ref, nu_ref, xs_ref, wg_ref, wu_ref, wd_ref, *rest):
    y_ref = rest[-1]
    j = pl.program_id(0) + tile_base

    @pl.when(j < nu_ref[0])
    def _():
        x = xs_ref[...]
        de = wg_ref.shape[3]
        y = None
        for c in range(de // FFN_CHUNK):
            cs = slice(c * FFN_CHUNK, (c + 1) * FFN_CHUNK)
            a = jnp.dot(x, wg_ref[0, 0, :, cs].astype(BF16), preferred_element_type=F32)
            b = jnp.dot(x, wu_ref[0, 0, :, cs].astype(BF16), preferred_element_type=F32)
            act = (a * _sigmoid(a) * b).astype(BF16)
            part = jnp.dot(act, wd_ref[0, 0, cs, :].astype(BF16), preferred_element_type=F32)
            y = part if y is None else y + part
        y_ref[...] = y.astype(y_ref.dtype)

    @pl.when(j >= nu_ref[0])
    def _():
        y_ref[...] = jnp.zeros_like(y_ref)


FFN_CHUNK = 512
MOE_PARTS = 4


def _expert_ffn(tile_expert, n_used, xs_part, w_gate, w_up, w_down, layer, tile_base, n_tiles, y_prev):
    r, d = xs_part.shape
    de = w_gate.shape[3]
    expert = lambda j, te, nu: (layer, te[j + tile_base], 0, 0)
    in_specs = [pl.BlockSpec((FFN_ROWS, d), lambda j, te, nu: (j, 0)),
                pl.BlockSpec((1, 1, d, de), expert, pipeline_mode=pl.Buffered(1)),
                pl.BlockSpec((1, 1, d, de), expert, pipeline_mode=pl.Buffered(1)),
                pl.BlockSpec((1, 1, de, d), expert, pipeline_mode=pl.Buffered(1))]
    args = [tile_expert, n_used, xs_part, w_gate, w_up, w_down]
    aliases = {}
    if y_prev is not None:
        in_specs.append(pl.BlockSpec(memory_space=pl.ANY))
        aliases = {len(args): 0}
        args.append(y_prev)
    grid_spec = pltpu.PrefetchScalarGridSpec(
        num_scalar_prefetch=2,
        grid=(r // FFN_ROWS,),
        in_specs=in_specs,
        out_specs=pl.BlockSpec((FFN_ROWS, d), lambda j, te, nu: (j + tile_base, 0)),
    )
    return pl.pallas_call(
        functools.partial(_ffn_kernel, tile_base),
        grid_spec=grid_spec,
        out_shape=jax.ShapeDtypeStruct((n_tiles * FFN_ROWS, d), BF16),
        input_output_aliases=aliases,
        compiler_params=_cparams(1),
    )(*args)


def _combine_kernel(final, x_ref, y0_ref, y1_ref, w0_ref, w1_ref, gt_ref, gf_ref, *rest):
    o_ref = rest[-1]
    x = x_ref[...] + gt_ref[...] * (w0_ref[...] * y0_ref[...].astype(F32)
                                    + w1_ref[...] * y1_ref[...].astype(F32))
    if final:
        var = jnp.mean(x * x, axis=-1, keepdims=True)
        x = x * lax.rsqrt(var + RMS_EPS) * gf_ref[...]
    o_ref[...] = x


def _combine(x, y01_part, w0, w1, gt, g_final, final, row_base, out_prev):
    s, d = x.shape
    tm = 256
    part = y01_part.shape[0] // 2
    base = row_base // tm
    big = pl.BlockSpec((tm, d), lambda i: (i + base, 0))
    first = pl.BlockSpec((tm, d), lambda i: (i, 0))
    second = pl.BlockSpec((tm, d), lambda i: (i + part // tm, 0))
    col = pl.BlockSpec((tm, 1), lambda i: (i + base, 0))
    vec = pl.BlockSpec((1, d), lambda i: (0, 0))
    in_specs = [big, first, second, col, col, vec, vec]
    args = [x, y01_part, y01_part, w0, w1, gt, g_final]
    aliases = {}
    if out_prev is not None:
        in_specs.append(pl.BlockSpec(memory_space=pl.ANY))
        aliases = {len(args): 0}
        args.append(out_prev)
    return pl.pallas_call(
        functools.partial(_combine_kernel, final),
        grid=(part // tm,),
        in_specs=in_specs,
        out_specs=big,
        out_shape=jax.ShapeDtypeStruct((s, d), F32),
        input_output_aliases=aliases,
        compiler_params=_cparams(1),
    )(*args)


def _rope_tables(positions):
    half = HEAD_DIM // 2
    inv = ROPE_THETA ** (-jnp.arange(0, HEAD_DIM, 2, dtype=F32) / HEAD_DIM)
    ang = positions.astype(F32)[:, None] * inv
    cos, sin = jnp.cos(ang), jnp.sin(ang)
    cos_t = jnp.tile(cos, (1, LANES // half))
    sin_t = jnp.tile(jnp.concatenate([-sin, sin], axis=1), (1, LANES // HEAD_DIM))
    return cos_t, sin_t


def _block_means(ksum):
    nblk, width = ksum.shape
    km = (ksum / MOBA_BLOCK).reshape(nblk, width // HEAD_DIM, HEAD_DIM)
    km = jnp.pad(km, ((0, HEAD_DIM - nblk), (0, 0), (0, 0)))
    return jnp.transpose(km, (1, 0, 2))


def _moe(x1, h2, route, counts, w_gate, w_up, w_down, layer, gt, g_final, final):
    s, d = x1.shape
    e0, e1 = route[0].astype(I32), route[1].astype(I32)
    r0, r1 = route[2].astype(I32), route[3].astype(I32)
    cnt = counts[:, 0].astype(I32)
    cnt_pad = ((cnt + FFN_ROWS - 1) // FFN_ROWS) * FFN_ROWS
    off = jnp.concatenate([jnp.zeros((1,), I32), jnp.cumsum(cnt_pad).astype(I32)])
    d0 = off[e0] + r0
    d1 = off[e1] + r1
    n_rows = 2 * s + N_EXPERTS * FFN_ROWS
    tok = jnp.arange(s, dtype=I32)
    src = (jnp.arange(n_rows, dtype=I32) % s).at[jnp.concatenate([d0, d1])].set(jnp.concatenate([tok, tok]))
    n_tiles = n_rows // FFN_ROWS
    tile_start = jnp.arange(n_tiles, dtype=I32) * FFN_ROWS
    tile_expert = jnp.sum((off[None, 1:] <= tile_start[:, None]).astype(I32), axis=1)
    tile_expert = jnp.minimum(tile_expert, N_EXPERTS - 1).astype(I32)
    n_used = (off[N_EXPERTS] // FFN_ROWS).reshape(1).astype(I32)
    y = None
    for part in range(MOE_PARTS):
        tiles = n_tiles // MOE_PARTS
        rows = slice(part * tiles * FFN_ROWS, (part + 1) * tiles * FFN_ROWS)
        y = _expert_ffn(tile_expert, n_used, h2[src[rows]], w_gate, w_up, w_down, layer,
                        part * tiles, n_tiles, y)
    w0, w1 = route[4].reshape(s, 1), route[5].reshape(s, 1)
    out = None
    for part in range(MOE_PARTS):
        toks = slice(part * s // MOE_PARTS, (part + 1) * s // MOE_PARTS)
        y01 = y[jnp.concatenate([d0[toks], d1[toks]])]
        out = _combine(x1, y01, w0, w1, gt, g_final, final, part * s // MOE_PARTS, out)
    return out


def kernel(x, c, positions, w_ada, b_ada, g_attn, g_mlp, w_in, w_out, lam_q1, lam_k1, lam_q2, lam_k2,
           g_subln, w_gate, w_up, w_down, w_router, b_router, g_final):
    batch, s, d = x.shape
    assert batch == 1 and s % TQ == 0 and s // MOBA_BLOCK <= HEAD_DIM
    assert w_in.shape[2] == 3 * N_KIND_TILES * PROJ_TN
    depth = w_ada.shape[0]
    xs = x.reshape(s, d)
    cos_t, sin_t = _rope_tables(positions[0])
    mod = _modulation(c, w_ada, b_ada)
    tab = _dilated_bias_table()
    gf = g_final.reshape(1, d)
    row = lambda v: v.reshape(1, -1)
    for l in range(depth):
        lam_init = 0.8 - 0.6 * math.exp(-0.3 * l)
        sh_a, sc_a, gt_a, sh_m, sc_m, gt_m = [mod[l, :, k * d:(k + 1) * d] for k in range(6)]
        k, qt, qbt, vt, ksum = _in_projection(xs, row(g_attn[l]), sc_a, sh_a,
                                              _layer_to_bf16(w_in, l, PROJ_TN), cos_t, sin_t)
        o_a = _dilated_attention(qt, k, vt, tab, 0, 0, 0)
        bias = _moba_gate(qbt, _block_means(ksum.reshape(s // MOBA_BLOCK, -1)))
        o_b = _moba_attention(qt, bias, k, vt, 4, 4, 4)
        o_c = _diff_attention(qt, k, vt, row(lam_q1[l]), row(lam_k1[l]), row(lam_q2[l]), row(lam_k2[l]),
                              g_subln[l].reshape(-1, 1), lam_init, 8, 8, 8)
        x1 = _out_projection(o_a, o_b, o_c, w_out[l].T.astype(BF16), xs, gt_a)
        h2, route, counts = _norm_and_route(x1, row(g_mlp[l]), sc_m, sh_m, w_router, b_router)
        xs = _moe(x1, h2, route, counts, w_gate, w_up, w_down, l, gt_m, gf, l == depth - 1)
    return xs.reshape(batch, s, d)
```

```python
import functools
import math

import numpy as np
import jax
import jax.numpy as jnp
from jax import lax
from jax.experimental import pallas as pl
from jax.experimental.pallas import tpu as pltpu

F32 = jnp.float32
BF16 = jnp.bfloat16
I32 = jnp.int32

HEAD_DIM = 64
LANES = 128
A_PATTERNS = ((128, 1), (512, 4), (2048, 16))
MOBA_BLOCK = 256
MOBA_TOPK = 3
N_EXPERTS = 16
N_GROUPS = 4
EXPERTS_PER_GROUP = N_EXPERTS // N_GROUPS
ROPE_THETA = 10000.0
RMS_EPS = 1e-6
NEG = -1e30
VMEM_LIMIT = 56 * 1024 * 1024

TQ = 1024
TK = 512
KPQ = TQ // TK
assert KPQ == 2
ALL_QUERIES = slice(None)
LATE_QUERIES = slice(TQ // 2, TQ)
Q_CHUNK = 256
PROJ_TM = 512
PROJ_TN = 512
NT_DIMS = (((1,), (1,)), ((), ()))
Q_SCALE = HEAD_DIM ** -0.5 * math.log2(math.e)


def _cparams(n_axes):
    return pltpu.CompilerParams(dimension_semantics=("arbitrary",) * n_axes,
                                vmem_limit_bytes=VMEM_LIMIT)


def _split_bf16(x):
    hi = x.astype(BF16)
    lo = (x - hi.astype(F32)).astype(BF16)
    return hi, lo


def _dot3(a, b, dims=(((1,), (0,)), ((), ()))):
    a_hi, a_lo = _split_bf16(a)
    b_hi, b_lo = _split_bf16(b)
    dg = functools.partial(lax.dot_general, dimension_numbers=dims, preferred_element_type=F32)
    return dg(a_hi, b_hi) + dg(a_hi, b_lo) + dg(a_lo, b_hi)


def _sigmoid(x):
    return 1.0 / (1.0 + jnp.exp(-x))


CAST_STREAMS = 4
CAST_BLOCK_ELEMS = 512 * 1024
CAST_COLS = 1024


def _cast_kernel(*refs):
    o_ref = refs[-1]
    col_tile = o_ref.shape[2]
    for k, x_ref in enumerate(refs[:-1]):
        rows = x_ref.shape[1]
        x = x_ref[0].astype(o_ref.dtype)
        for t in range(o_ref.shape[0]):
            o_ref[t, k * rows:(k + 1) * rows, :] = x[:, t * col_tile:(t + 1) * col_tile]


def _layer_to_bf16(w, l, col_tile=None):
    cols = w.shape[-1]
    col_tile = col_tile or cols
    w3 = w.reshape(w.shape[0], -1, cols)
    rows = w3.shape[1]
    tc = max(min(cols, CAST_COLS), col_tile)
    sr = CAST_BLOCK_ELEMS // tc
    tr = CAST_STREAMS * sr
    stream = lambda k: pl.BlockSpec((1, sr, tc), lambda i, j: (l, CAST_STREAMS * i + k, j))
    return pl.pallas_call(
        _cast_kernel,
        grid=(rows // tr, cols // tc),
        in_specs=[stream(k) for k in range(CAST_STREAMS)],
        out_specs=pl.BlockSpec((tc // col_tile, tr, col_tile), lambda i, j: (j, i, 0)),
        out_shape=jax.ShapeDtypeStruct((cols // col_tile, rows, col_tile), BF16),
        compiler_params=_cparams(2),
    )(*([w3] * CAST_STREAMS))


def _mod_kernel(c_ref, w_ref, b_ref, o_ref):
    c = c_ref[...]
    sc = c * _sigmoid(c)
    d = c.shape[0]
    acc = b_ref[0]
    for r in range(0, d, 256):
        acc = acc + jnp.sum(w_ref[0, r:r + 256, :] * sc[r:r + 256, :], axis=0, keepdims=True)
    o_ref[0] = acc


def _modulation(c, w_ada, b_ada):
    depth, d, n = w_ada.shape
    tn = 512
    return pl.pallas_call(
        _mod_kernel,
        grid=(depth, n // tn),
        in_specs=[pl.BlockSpec((d, 1), lambda l, j: (0, 0)),
                  pl.BlockSpec((1, d, tn), lambda l, j: (l, 0, j)),
                  pl.BlockSpec((1, 1, tn), lambda l, j: (l, 0, j))],
        out_specs=pl.BlockSpec((1, 1, tn), lambda l, j: (l, 0, j)),
        out_shape=jax.ShapeDtypeStruct((depth, 1, n), F32),
        compiler_params=_cparams(2),
    )(c.reshape(d, 1), w_ada, b_ada.reshape(depth, 1, n))


def _modulated_norm(x, g, sc, sh):
    var = jnp.mean(x * x, axis=-1, keepdims=True)
    return (x * lax.rsqrt(var + RMS_EPS) * g) * (1.0 + sc) + sh


def _rope_tile(acc, cos, sin):
    lane = lax.broadcasted_iota(I32, (1, LANES), 1)
    first = (lane & (HEAD_DIM - 1)) < (HEAD_DIM // 2)
    outs = []
    for c in range(acc.shape[1] // LANES):
        xc = acc[:, c * LANES:(c + 1) * LANES]
        rot = jnp.where(first, pltpu.roll(xc, LANES - HEAD_DIM // 2, 1), pltpu.roll(xc, HEAD_DIM // 2, 1))
        outs.append(xc * cos + rot * sin)
    return jnp.concatenate(outs, axis=1)


N_KIND_TILES = 4
COLUMN_TILE_ORDER = (1, 4, 8, 9, 0, 3, 6, 7, 2, 5, 10, 11)
KB_TILE = 1
QB_TILE = N_KIND_TILES + 1


def _inproj_kernel(x_ref, g_ref, sc_ref, sh_ref, w_ref, cos_ref, sin_ref,
                   k_ref, qt_ref, qbt_ref, vt_ref, ksum_ref, h_scr):
    tm = x_ref.shape[0]
    tn = w_ref.shape[2]
    h_scr[...] = _modulated_norm(x_ref[...], g_ref[...], sc_ref[...], sh_ref[...]).astype(BF16)
    for pos, tile in enumerate(COLUMN_TILE_ORDER):
        acc = jnp.dot(h_scr[...], w_ref[tile], preferred_element_type=F32)
        kind, t = divmod(pos, N_KIND_TILES)
        cols = slice(t * tn, (t + 1) * tn)
        if kind == 0:
            r = _rope_tile(acc, cos_ref[...], sin_ref[...])
            k_ref[:, cols] = r.astype(BF16)
            if pos == KB_TILE:
                ksum_ref[0] = jnp.sum(r.reshape(tm // MOBA_BLOCK, MOBA_BLOCK, tn), axis=1)
        elif kind == 1:
            rt = _rope_tile(acc, cos_ref[...], sin_ref[...]).T
            qt_ref[0, cols, :] = (rt * Q_SCALE).astype(BF16)
            if pos == QB_TILE:
                qbt_ref[...] = rt
        else:
            vt = acc.T.astype(BF16)
            for b in range(vt_ref.shape[0]):
                vt_ref[b, cols, :] = vt[:, b * TK:(b + 1) * TK]


def _in_projection(x, g, sc, sh, w_bf16, cos_t, sin_t):
    s, d = x.shape
    tm, tn = PROJ_TM, w_bf16.shape[2]
    nb = tm // MOBA_BLOCK
    width = N_KIND_TILES * tn
    row = lambda i: (0, 0)
    return pl.pallas_call(
        _inproj_kernel,
        grid=(s // tm,),
        in_specs=[pl.BlockSpec((tm, d), lambda i: (i, 0)),
                  pl.BlockSpec((1, d), row), pl.BlockSpec((1, d), row), pl.BlockSpec((1, d), row),
                  pl.BlockSpec(w_bf16.shape, lambda i: (0, 0, 0),
                               pipeline_mode=pl.Buffered(1)),
                  pl.BlockSpec((tm, LANES), lambda i: (i, 0)),
                  pl.BlockSpec((tm, LANES), lambda i: (i, 0))],
        out_specs=[pl.BlockSpec((tm, width), lambda i: (i, 0)),
                   pl.BlockSpec((1, width, tm), lambda i: (i // (TQ // tm), 0, i % (TQ // tm))),
                   pl.BlockSpec((tn, tm), lambda i: (0, i)),
                   pl.BlockSpec((tm // TK, width, TK), lambda i: (i, 0, 0)),
                   pl.BlockSpec((1, nb, tn), lambda i: (i, 0, 0))],
        out_shape=[jax.ShapeDtypeStruct((s, width), BF16),
                   jax.ShapeDtypeStruct((s // TQ, width, TQ), BF16),
                   jax.ShapeDtypeStruct((tn, s), F32),
                   jax.ShapeDtypeStruct((s // TK, width, TK), BF16),
                   jax.ShapeDtypeStruct((s // tm, nb, tn), F32)],
        scratch_shapes=[pltpu.VMEM((tm, d), BF16)],
        compiler_params=_cparams(1),
    )(x, g, sc, sh, w_bf16, cos_t, sin_t)


def _flash_step(s_t, v_t, state):
    m_prev, l_prev, acc_prev = state
    m_new = jnp.maximum(m_prev, jnp.max(s_t, axis=0, keepdims=True))
    alpha = jnp.exp2(m_prev - m_new)
    p = jnp.exp2(s_t - m_new)
    l_new = alpha * l_prev + jnp.sum(p, axis=0, keepdims=True)
    acc_new = alpha * acc_prev + jnp.dot(v_t, p.astype(BF16), preferred_element_type=F32)
    return m_new, l_new, acc_new


def _sweep(first, n_loop, scores, consume, bufs, tail_masks):
    halves = (bufs[0:2], bufs[2:4])

    def park_pair(a, dst):
        for d in range(2):
            for h, val in enumerate(scores(a + d)):
                dst[d][h] = val

    def pair(a, cur, nxt):
        park_pair(a + 2, nxt)
        consume(a, cur[0], None, ALL_QUERIES)
        consume(a + 1, cur[1], None, ALL_QUERIES)

    def tail(cur):
        t0 = first + n_loop
        consume(t0, cur[0], tail_masks[0], ALL_QUERIES)
        consume(t0 + 1, cur[1], tail_masks[1], LATE_QUERIES)

    park_pair(first, halves[0])

    def two_pairs(j, carry):
        pair(first + 4 * j, halves[0], halves[1])
        pair(first + 4 * j + 2, halves[1], halves[0])
        return carry

    n_pairs = n_loop // 2
    lax.fori_loop(0, n_pairs // 2, two_pairs, 0)

    @pl.when(n_pairs % 2 == 1)
    def _():
        pair(first + n_loop - 2, halves[0], halves[1])
        tail(halves[1])

    @pl.when(n_pairs % 2 == 0)
    def _():
        tail(halves[0])


def _softmax_pv(buf, v_ts, refs, mask, cols):
    start, stop, _ = cols.indices(TQ)
    for c0 in range(start, stop, Q_CHUNK):
        cs = slice(c0, c0 + Q_CHUNK)
        for h, v_t in enumerate(v_ts):
            s_t = buf[h, :, cs]
            if mask is not None:
                s_t = jnp.where(mask[:, cs], s_t, NEG)
            _store_state(refs[h], _flash_step(s_t, v_t, _load_state(refs[h], cs)), cs)


N_SCORE_BUFS = 4
N_HEAD_SCORE_BUFS = 6


def _score_scratch(n_streams, n_bufs=N_SCORE_BUFS):
    return [pltpu.VMEM((n_streams, TK, TQ), F32) for _ in range(n_bufs)]


def _head_sweep(n_q, make_q, scores, consume, reset, finalize, bufs, q_scr, tail_masks):
    lead, half0, half1 = bufs[0:2], bufs[2:4], bufs[4:6]
    n_streams = q_scr.shape[0]

    def park_pair(a, dst, qs):
        for d in range(2):
            for h, val in enumerate(scores(a + d, qs)):
                dst[d][h] = val

    def current_q():
        return [q_scr[h] for h in range(n_streams)]

    def pair(a, cur, nxt):
        park_pair(a + 2, nxt, current_q())
        consume(a, cur[0], None, ALL_QUERIES)
        consume(a + 1, cur[1], None, ALL_QUERIES)

    def tail(i, cur, lead_is_free):
        qs_next = make_q(jnp.minimum(i + 1, n_q - 1))
        if lead_is_free:
            park_pair(0, lead, qs_next)
        consume(KPQ * i, cur[0], tail_masks[0], ALL_QUERIES)
        consume(KPQ * i + 1, cur[1], tail_masks[1], LATE_QUERIES)
        if not lead_is_free:
            park_pair(0, lead, qs_next)
        for h, q in enumerate(qs_next):
            q_scr[h] = q

    qs_first = make_q(0)
    for h, q in enumerate(qs_first):
        q_scr[h] = q
    park_pair(0, lead, qs_first)

    def query_tile(i, carry):
        reset()

        @pl.when(i == 0)
        def _():
            tail(i, lead, False)

        @pl.when(i > 0)
        def _():
            pair(0, lead, half0)
            rest = i - 1

            def two_pairs(j, c):
                pair(2 + 4 * j, half0, half1)
                pair(4 + 4 * j, half1, half0)
                return c

            lax.fori_loop(0, rest // 2, two_pairs, 0)

            @pl.when(rest % 2 == 1)
            def _():
                pair(KPQ * i - 2, half0, half1)
                tail(i, half1, True)

            @pl.when(rest % 2 == 0)
            def _():
                tail(i, half0, True)

        finalize(i)
        return carry

    lax.fori_loop(0, n_q, query_tile, 0)


def _qk_t(k, q_t):
    return jnp.dot(k, q_t, preferred_element_type=F32)


def _k_block(k_ref, n):
    return k_ref[pl.ds(pl.multiple_of(n * TK, TK), TK), :]


def _init_state(refs):
    m_ref, l_ref, acc_ref = refs
    m_ref[...] = jnp.full(m_ref.shape, NEG, F32)
    l_ref[...] = jnp.zeros(l_ref.shape, F32)
    acc_ref[...] = jnp.zeros(acc_ref.shape, F32)


def _load_state(refs, cols=ALL_QUERIES):
    return tuple(r[:, cols] for r in refs)


def _store_state(refs, state, cols=ALL_QUERIES):
    for r, val in zip(refs, state):
        r[:, cols] = val


def _flash_scratch(n_streams, dv):
    out = []
    for _ in range(n_streams):
        out += [pltpu.VMEM((1, TQ), F32), pltpu.VMEM((1, TQ), F32), pltpu.VMEM((dv, TQ), F32)]
    return out


def _causal_t(d):
    r = lax.broadcasted_iota(I32, (TK, TQ), 0) + d * TK
    return r <= lax.broadcasted_iota(I32, (TK, TQ), 1)


def _stack_rows(top, bottom):
    return jnp.concatenate([top, bottom], axis=0)


def _dilated_bias_table():
    n_off = A_PATTERNS[-1][0] // TK + KPQ
    r = np.arange(TK)[:, None]
    c = np.arange(TQ)[None, :]
    tabs = []
    for t in range(n_off):
        delta = (t - (KPQ - 1)) * TK + c - r
        mult = np.zeros_like(delta)
        for window, dil in A_PATTERNS:
            mult += ((delta >= 0) & (delta % dil == 0) & (delta <= window)).astype(delta.dtype)
        tabs.append(np.where(mult > 0, np.log2(np.maximum(mult, 1).astype(np.float64)), NEG))
    return jnp.asarray(np.stack(tabs), F32)


def _dilated_kernel(qt_ref, k_ref, vt_ref, tab_ref, o_ref, *scratch):
    bufs, scr = scratch[:N_SCORE_BUFS], scratch[N_SCORE_BUFS:]
    i = pl.program_id(1)
    refs = (scr[0:3], scr[3:6])
    q_t = qt_ref[0]
    zero = jnp.zeros((HEAD_DIM, TQ), BF16)
    qs = (_stack_rows(q_t[:HEAD_DIM], zero), _stack_rows(zero, q_t[HEAD_DIM:]))
    n_off = tab_ref.shape[0]
    last = KPQ * i + (KPQ - 1)
    for h in range(2):
        _init_state(refs[h])

    def scores(n):
        kb = _k_block(k_ref, n)
        bias = tab_ref[last - n]
        return [_qk_t(kb, qs[h]) + bias for h in range(2)]

    def consume(n, buf, mask, cols):
        vb = vt_ref[n]
        _softmax_pv(buf, [vb[:HEAD_DIM, :], vb[HEAD_DIM:, :]], refs, mask, cols)

    first = jnp.maximum(last - (n_off - 1), 0)
    _sweep(first, last - 1 - first, scores, consume, bufs, (None, None))
    for h in range(2):
        _, l, acc = _load_state(refs[h])
        o_ref[h * HEAD_DIM:(h + 1) * HEAD_DIM, :] = (acc / l).astype(o_ref.dtype)


def _dilated_attention(qt, k, vt, tab, qrow, kcol, vrow):
    s = k.shape[0]
    n_pairs = 4
    return pl.pallas_call(
        _dilated_kernel,
        grid=(n_pairs, s // TQ),
        in_specs=[pl.BlockSpec((1, LANES, TQ), lambda hp, i: (i, qrow + hp, 0)),
                  pl.BlockSpec((s, LANES), lambda hp, i: (0, kcol + hp)),
                  pl.BlockSpec((s // TK, LANES, TK), lambda hp, i: (0, vrow + hp, 0)),
                  pl.BlockSpec(tab.shape, lambda hp, i: (0, 0, 0), pipeline_mode=pl.Buffered(1))],
        out_specs=pl.BlockSpec((LANES, TQ), lambda hp, i: (hp, i)),
        out_shape=jax.ShapeDtypeStruct((n_pairs * LANES, s), BF16),
        scratch_shapes=_score_scratch(2) + _flash_scratch(2, HEAD_DIM),
        compiler_params=_cparams(2),
    )(qt, k, vt, tab)


def _moba_gate_kernel(qt_ref, km_ref, bias_ref):
    i = pl.program_id(0)
    tg = qt_ref.shape[1]
    n_heads, nblk, _ = km_ref.shape
    blk = lax.broadcasted_iota(I32, (nblk, tg), 0)
    blk_f = blk.astype(F32)
    tok = i * tg + lax.broadcasted_iota(I32, (nblk, tg), 1)
    own = jnp.right_shift(tok, int(math.log2(MOBA_BLOCK)))
    past = blk < own
    for h in range(n_heads):
        rows = slice(h * HEAD_DIM, (h + 1) * HEAD_DIM)
        gate = _dot3(km_ref[h], qt_ref[rows, :])
        sel = blk == own
        for _ in range(MOBA_TOPK):
            cand = past & jnp.logical_not(sel)
            g = jnp.where(cand, gate, -jnp.inf)
            mx = jnp.max(g, axis=0, keepdims=True)
            first = jnp.min(jnp.where(cand & (g == mx), blk_f, 2.0 * nblk), axis=0, keepdims=True)
            sel = sel | (blk_f == first)
        bias_ref[0, rows, :] = jnp.where(sel, 0.0, NEG).astype(BF16)


def _moba_gate(qbt, kmean):
    width, s = qbt.shape
    tg = 512
    return pl.pallas_call(
        _moba_gate_kernel,
        grid=(s // tg,),
        in_specs=[pl.BlockSpec((width, tg), lambda i: (0, i)),
                  pl.BlockSpec(kmean.shape, lambda i: (0, 0, 0))],
        out_specs=pl.BlockSpec((1, width, tg), lambda i: (i // (TQ // tg), 0, i % (TQ // tg))),
        out_shape=jax.ShapeDtypeStruct((s // TQ, width, TQ), BF16),
        compiler_params=_cparams(1),
    )(qbt, kmean)


def _moba_kernel(qt_ref, bias_ref, k_ref, vt_ref, o_ref, *scratch):
    bufs, q_scr, scr = (scratch[:N_HEAD_SCORE_BUFS], scratch[N_HEAD_SCORE_BUFS],
                        scratch[N_HEAD_SCORE_BUFS + 1:])
    refs = (scr[0:3], scr[3:6])
    lane = lax.broadcasted_iota(I32, (1, LANES), 1)
    lo = lane < HEAD_DIM
    mine = (lo, jnp.logical_not(lo))
    blk = lane & (HEAD_DIM - 1)
    sub_block = jnp.right_shift(lax.broadcasted_iota(I32, (TK, LANES), 0), int(math.log2(MOBA_BLOCK)))

    def make_q(i):
        q_t = qt_ref[i]
        b_t = bias_ref[i]
        return [_stack_rows(q_t[:HEAD_DIM], b_t[:HEAD_DIM]), _stack_rows(b_t[HEAD_DIM:], q_t[HEAD_DIM:])]

    def scores(n, qs):
        kb = _k_block(k_ref, n)
        onehot = jnp.where(blk == n * (TK // MOBA_BLOCK) + sub_block, 1.0, 0.0).astype(BF16)
        return [_qk_t(jnp.where(mine[h], kb, onehot), qs[h]) for h in range(2)]

    def consume(n, buf, mask, cols):
        vb = vt_ref[n]
        _softmax_pv(buf, [vb[:HEAD_DIM, :], vb[HEAD_DIM:, :]], refs, mask, cols)

    def reset():
        for h in range(2):
            _init_state(refs[h])

    def finalize(i):
        for h in range(2):
            _, l, acc = _load_state(refs[h])
            o_ref[i, h * HEAD_DIM:(h + 1) * HEAD_DIM, :] = (acc / l).astype(o_ref.dtype)

    _head_sweep(qt_ref.shape[0], make_q, scores, consume, reset, finalize, bufs, q_scr,
                (_causal_t(0), _causal_t(1)))


def _whole(shape, index_map):
    return pl.BlockSpec(shape, index_map, pipeline_mode=pl.Buffered(1))


def _moba_attention(qt, bias, k, vt, qrow, kcol, vrow):
    s = k.shape[0]
    n_pairs = 4
    n_q = s // TQ
    return pl.pallas_call(
        _moba_kernel,
        grid=(n_pairs,),
        in_specs=[_whole((n_q, LANES, TQ), lambda hp: (0, qrow + hp, 0)),
                  _whole((n_q, LANES, TQ), lambda hp: (0, hp, 0)),
                  _whole((s, LANES), lambda hp: (0, kcol + hp)),
                  _whole((s // TK, LANES, TK), lambda hp: (0, vrow + hp, 0))],
        out_specs=pl.BlockSpec((n_q, LANES, TQ), lambda hp: (0, hp, 0)),
        out_shape=jax.ShapeDtypeStruct((n_q, n_pairs * LANES, TQ), BF16),
        scratch_shapes=(_score_scratch(2, N_HEAD_SCORE_BUFS) + [pltpu.VMEM((2, LANES, TQ), BF16)]
                        + _flash_scratch(2, HEAD_DIM)),
        compiler_params=_cparams(1),
    )(qt, bias, k, vt)


def _diff_kernel(lam_init, qt_ref, k_ref, vt_ref, lq1_ref, lk1_ref, lq2_ref, lk2_ref, g_ref, o_ref,
                 *scratch):
    bufs, q_scr, scr = (scratch[:N_HEAD_SCORE_BUFS], scratch[N_HEAD_SCORE_BUFS],
                        scratch[N_HEAD_SCORE_BUFS + 1:])
    refs = (scr[0:3], scr[3:6])
    lam = (jnp.exp(jnp.sum(lq1_ref[...] * lk1_ref[...], axis=1, keepdims=True))
           - jnp.exp(jnp.sum(lq2_ref[...] * lk2_ref[...], axis=1, keepdims=True)) + lam_init)

    def make_q(i):
        q_t = qt_ref[i]
        zero = jnp.zeros((HEAD_DIM, TQ), BF16)
        return [_stack_rows(q_t[:HEAD_DIM], zero), _stack_rows(zero, q_t[HEAD_DIM:])]

    def scores(n, qs):
        kb = _k_block(k_ref, n)
        return [_qk_t(kb, qs[h]) for h in range(2)]

    def consume(n, buf, mask, cols):
        vb = vt_ref[n]
        _softmax_pv(buf, [vb, vb], refs, mask, cols)

    def reset():
        for h in range(2):
            _init_state(refs[h])

    def finalize(i):
        _, l1, acc1 = _load_state(refs[0])
        _, l2, acc2 = _load_state(refs[1])
        o = acc1 / l1 - lam * (acc2 / l2)
        var = jnp.mean(o * o, axis=0, keepdims=True)
        o = (o * lax.rsqrt(var + RMS_EPS) * g_ref[...]) * (1.0 - lam_init)
        o_ref[i] = o.astype(o_ref.dtype)

    _head_sweep(qt_ref.shape[0], make_q, scores, consume, reset, finalize, bufs, q_scr,
                (_causal_t(0), _causal_t(1)))


def _diff_attention(qt, k, vt, lq1, lk1, lq2, lk2, g_sub, lam_init, qrow, kcol, vrow):
    s = k.shape[0]
    n_heads = 8
    n_q = s // TQ
    vec = lambda n: pl.BlockSpec((1, n), lambda h: (0, 0))
    return pl.pallas_call(
        functools.partial(_diff_kernel, lam_init),
        grid=(n_heads,),
        in_specs=[_whole((n_q, LANES, TQ), lambda h: (0, qrow + h, 0)),
                  _whole((s, LANES), lambda h: (0, kcol + h)),
                  _whole((s // TK, LANES, TK), lambda h: (0, vrow + h, 0)),
                  vec(HEAD_DIM), vec(HEAD_DIM), vec(HEAD_DIM), vec(HEAD_DIM),
                  pl.BlockSpec((LANES, 1), lambda h: (0, 0))],
        out_specs=pl.BlockSpec((n_q, LANES, TQ), lambda h: (0, h, 0)),
        out_shape=jax.ShapeDtypeStruct((n_q, n_heads * LANES, TQ), BF16),
        scratch_shapes=(_score_scratch(2, N_HEAD_SCORE_BUFS) + [pltpu.VMEM((2, LANES, TQ), BF16)]
                        + _flash_scratch(2, LANES)),
        compiler_params=_cparams(1),
    )(qt, k, vt, lq1, lk1, lq2, lk2, g_sub)


def _outproj_kernel(oa_ref, ob_ref, oc_ref, wt_ref, x_ref, gt_ref, o_ref):
    wa = oa_ref.shape[0]
    wb = ob_ref.shape[1]
    y_t = (jnp.dot(wt_ref[:, 0:wa], oa_ref[...], preferred_element_type=F32)
           + jnp.dot(wt_ref[:, wa:wa + wb], ob_ref[0], preferred_element_type=F32)
           + jnp.dot(wt_ref[:, wa + wb:], oc_ref[0], preferred_element_type=F32))
    o_ref[...] = x_ref[...] + gt_ref[...] * y_t.T


def _out_projection(oa_t, ob_t, oc_t, wt_bf16, x, gt):
    s, d = x.shape
    tm = 256
    per_tile = TQ // tm
    tiled = lambda o: pl.BlockSpec((1, o.shape[1], tm), lambda i: (i // per_tile, 0, i % per_tile))
    return pl.pallas_call(
        _outproj_kernel,
        grid=(s // tm,),
        in_specs=[pl.BlockSpec((oa_t.shape[0], tm), lambda i: (0, i)),
                  tiled(ob_t), tiled(oc_t),
                  pl.BlockSpec(wt_bf16.shape, lambda i: (0, 0)),
                  pl.BlockSpec((tm, d), lambda i: (i, 0)),
                  pl.BlockSpec((1, d), lambda i: (0, 0))],
        out_specs=pl.BlockSpec((tm, d), lambda i: (i, 0)),
        out_shape=jax.ShapeDtypeStruct((s, d), F32),
        compiler_params=_cparams(1),
    )(oa_t, ob_t, oc_t, wt_bf16, x, gt)


def _route_kernel(x_ref, g_ref, sc_ref, sh_ref, wr_ref, br_ref, tri_ref,
                  h_ref, route_ref, count_ref, base_scr):
    i = pl.program_id(0)
    tm = x_ref.shape[0]

    @pl.when(i == 0)
    def _():
        base_scr[...] = jnp.zeros_like(base_scr)

    h = _modulated_norm(x_ref[...], g_ref[...], sc_ref[...], sh_ref[...])
    h_ref[...] = h.astype(BF16)
    logits = _dot3(wr_ref[...], h, NT_DIMS)
    s = _sigmoid(logits)
    sg = s + br_ref[...]
    rows = [sg[r:r + 1, :] for r in range(N_EXPERTS)]

    def top2_sum(vals):
        best = None
        for a in range(len(vals)):
            for b in range(a + 1, len(vals)):
                pair = vals[a] + vals[b]
                best = pair if best is None else jnp.maximum(best, pair)
        return best

    score = [top2_sum(rows[g * EXPERTS_PER_GROUP:(g + 1) * EXPERTS_PER_GROUP]) for g in range(N_GROUPS)]
    best, grp = score[0], jnp.zeros((1, tm), I32)
    for g in range(1, N_GROUPS):
        better = score[g] > best
        grp = jnp.where(better, g, grp)
        best = jnp.where(better, score[g], best)
    cand = []
    for j in range(EXPERTS_PER_GROUP):
        c = rows[j]
        for g in range(1, N_GROUPS):
            c = jnp.where(grp == g, rows[g * EXPERTS_PER_GROUP + j], c)
        cand.append(c)
    b0, l0 = cand[0], jnp.zeros((1, tm), I32)
    for j in range(1, EXPERTS_PER_GROUP):
        better = cand[j] > b0
        l0 = jnp.where(better, j, l0)
        b0 = jnp.where(better, cand[j], b0)
    b1, l1 = jnp.full((1, tm), -jnp.inf, F32), jnp.zeros((1, tm), I32)
    for j in range(EXPERTS_PER_GROUP):
        better = (l0 != j) & (cand[j] > b1)
        l1 = jnp.where(better, j, l1)
        b1 = jnp.where(better, cand[j], b1)
    e0 = grp * EXPERTS_PER_GROUP + l0
    e1 = grp * EXPERTS_PER_GROUP + l1
    erow = lax.broadcasted_iota(I32, (N_EXPERTS, tm), 0)
    sel0 = erow == e0
    sel1 = erow == e1
    s0 = jnp.sum(jnp.where(sel0, s, 0.0), axis=0, keepdims=True)
    s1 = jnp.sum(jnp.where(sel1, s, 0.0), axis=0, keepdims=True)
    tot = s0 + s1
    chosen = jnp.where(sel0 | sel1, 1.0, 0.0).astype(BF16)
    csum = jnp.dot(chosen, tri_ref[...], preferred_element_type=F32)
    pos = base_scr[...] + csum - 1.0
    r0 = jnp.sum(jnp.where(sel0, pos, 0.0), axis=0, keepdims=True)
    r1 = jnp.sum(jnp.where(sel1, pos, 0.0), axis=0, keepdims=True)
    base = base_scr[...] + csum[:, tm - 1:tm]
    base_scr[...] = base
    count_ref[...] = jnp.broadcast_to(base, count_ref.shape)
    zero = jnp.zeros((1, tm), F32)
    for r, val in enumerate((e0.astype(F32), e1.astype(F32), r0, r1, s0 / tot, s1 / tot, zero, zero)):
        route_ref[r:r + 1, :] = val


def _norm_and_route(x, g, sc, sh, w_router, b_router):
    s, d = x.shape
    tm = 512
    tri = jnp.asarray(np.triu(np.ones((tm, tm), np.float32)), BF16)
    row = lambda i: (0, 0)
    return pl.pallas_call(
        _route_kernel,
        grid=(s // tm,),
        in_specs=[pl.BlockSpec((tm, d), lambda i: (i, 0)),
                  pl.BlockSpec((1, d), row), pl.BlockSpec((1, d), row), pl.BlockSpec((1, d), row),
                  pl.BlockSpec((N_EXPERTS, d), row), pl.BlockSpec((N_EXPERTS, 1), row),
                  pl.BlockSpec((tm, tm), row)],
        out_specs=[pl.BlockSpec((tm, d), lambda i: (i, 0)),
                   pl.BlockSpec((8, tm), lambda i: (0, i)),
                   pl.BlockSpec((N_EXPERTS, LANES), row)],
        out_shape=[jax.ShapeDtypeStruct((s, d), BF16),
                   jax.ShapeDtypeStruct((8, s), F32),
                   jax.ShapeDtypeStruct((N_EXPERTS, LANES), F32)],
        scratch_shapes=[pltpu.VMEM((N_EXPERTS, 1), F32)],
        compiler_params=_cparams(1),
    )(x, g, sc, sh, w_router.T, b_router.reshape(N_EXPERTS, 1), tri)


FFN_ROWS = 512


def _ffn_kernel(tile_base, te_ref, nu_ref, xs_ref, wg_ref, wu_ref, wd_ref, *rest):
    y_ref = rest[-1]
    j = pl.program_id(0) + tile_base

    @pl.when(j < nu_ref[0])
    def _():
        x = xs_ref[...]
        de = wg_ref.shape[3]
        y = None
        for c in range(de // FFN_CHUNK):
            cs = slice(c * FFN_CHUNK, (c + 1) * FFN_CHUNK)
            a = jnp.dot(x, wg_ref[0, 0, :, cs].astype(BF16), preferred_element_type=F32)
            b = jnp.dot(x, wu_ref[0, 0, :, cs].astype(BF16), preferred_element_type=F32)
            act = (a * _sigmoid(a) * b).astype(BF16)
            part = jnp.dot(act, wd_ref[0, 0, cs, :].astype(BF16), preferred_element_type=F32)
            y = part if y is None else y + part
        y_ref[...] = y.astype(y_ref.dtype)

    @pl.when(j >= nu_ref[0])
    def _():
        y_ref[...] = jnp.zeros_like(y_ref)


FFN_CHUNK = 512
MOE_PARTS = 2


def _expert_ffn(tile_expert, n_used, xs_part, w_gate, w_up, w_down, layer, tile_base, n_tiles, y_prev):
    r, d = xs_part.shape
    de = w_gate.shape[3]
    expert = lambda j, te, nu: (layer, te[j + tile_base], 0, 0)
    in_specs = [pl.BlockSpec((FFN_ROWS, d), lambda j, te, nu: (j, 0)),
                pl.BlockSpec((1, 1, d, de), expert, pipeline_mode=pl.Buffered(1)),
                pl.BlockSpec((1, 1, d, de), expert, pipeline_mode=pl.Buffered(1)),
                pl.BlockSpec((1, 1, de, d), expert, pipeline_mode=pl.Buffered(1))]
    args = [tile_expert, n_used, xs_part, w_gate, w_up, w_down]
    aliases = {}
    if y_prev is not None:
        in_specs.append(pl.BlockSpec(memory_space=pl.ANY))
        aliases = {len(args): 0}
        args.append(y_prev)
    grid_spec = pltpu.PrefetchScalarGridSpec(
        num_scalar_prefetch=2,
        grid=(r // FFN_ROWS,),
        in_specs=in_specs,
        out_specs=pl.BlockSpec((FFN_ROWS, d), lambda j, te, nu: (j + tile_base, 0)),
    )
    return pl.pallas_call(
        functools.partial(_ffn_kernel, tile_base),
        grid_spec=grid_spec,
        out_shape=jax.ShapeDtypeStruct((n_tiles * FFN_ROWS, d), BF16),
        input_output_aliases=aliases,
        compiler_params=_cparams(1),
    )(*args)


def _combine_kernel(final, x_ref, y0_ref, y1_ref, w0_ref, w1_ref, gt_ref, gf_ref, *rest):
    o_ref = rest[-1]
    x = x_ref[...] + gt_ref[...] * (w0_ref[...] * y0_ref[...].astype(F32)
                                    + w1_ref[...] * y1_ref[...].astype(F32))
    if final:
        var = jnp.mean(x * x, axis=-1, keepdims=True)
        x = x * lax.rsqrt(var + RMS_EPS) * gf_ref[...]
    o_ref[...] = x


def _combine(x, y01_part, w0, w1, gt, g_final, final, row_base, out_prev):
    s, d = x.shape
    tm = 256
    part = y01_part.shape[0] // 2
    base = row_base // tm
    big = pl.BlockSpec((tm, d), lambda i: (i + base, 0))
    first = pl.BlockSpec((tm, d), lambda i: (i, 0))
    second = pl.BlockSpec((tm, d), lambda i: (i + part // tm, 0))
    col = pl.BlockSpec((tm, 1), lambda i: (i + base, 0))
    vec = pl.BlockSpec((1, d), lambda i: (0, 0))
    in_specs = [big, first, second, col, col, vec, vec]
    args = [x, y01_part, y01_part, w0, w1, gt, g_final]
    aliases = {}
    if out_prev is not None:
        in_specs.append(pl.BlockSpec(memory_space=pl.ANY))
        aliases = {len(args): 0}
        args.append(out_prev)
    return pl.pallas_call(
        functools.partial(_combine_kernel, final),
        grid=(part // tm,),
        in_specs=in_specs,
        out_specs=big,
        out_shape=jax.ShapeDtypeStruct((s, d), F32),
        input_output_aliases=aliases,
        compiler_params=_cparams(1),
    )(*args)


def _rope_tables(positions):
    half = HEAD_DIM // 2
    inv = ROPE_THETA ** (-jnp.arange(0, HEAD_DIM, 2, dtype=F32) / HEAD_DIM)
    ang = positions.astype(F32)[:, None] * inv
    cos, sin = jnp.cos(ang), jnp.sin(ang)
    cos_t = jnp.tile(cos, (1, LANES // half))
    sin_t = jnp.tile(jnp.concatenate([-sin, sin], axis=1), (1, LANES // HEAD_DIM))
    return cos_t, sin_t


def _block_means(ksum):
    nblk, width = ksum.shape
    km = (ksum / MOBA_BLOCK).reshape(nblk, width // HEAD_DIM, HEAD_DIM)
    km = jnp.pad(km, ((0, HEAD_DIM - nblk), (0, 0), (0, 0)))
    return jnp.transpose(km, (1, 0, 2))


def _moe(x1, h2, route, counts, w_gate, w_up, w_down, layer, gt, g_final, final):
    s, d = x1.shape
    e0, e1 = route[0].astype(I32), route[1].astype(I32)
    r0, r1 = route[2].astype(I32), route[3].astype(I32)
    cnt = counts[:, 0].astype(I32)
    cnt_pad = ((cnt + FFN_ROWS - 1) // FFN_ROWS) * FFN_ROWS
    off = jnp.concatenate([jnp.zeros((1,), I32), jnp.cumsum(cnt_pad).astype(I32)])
    d0 = off[e0] + r0
    d1 = off[e1] + r1
    n_rows = 2 * s + N_EXPERTS * FFN_ROWS
    tok = jnp.arange(s, dtype=I32)
    src = (jnp.arange(n_rows, dtype=I32) % s).at[jnp.concatenate([d0, d1])].set(jnp.concatenate([tok, tok]))
    n_tiles = n_rows // FFN_ROWS
    tile_start = jnp.arange(n_tiles, dtype=I32) * FFN_ROWS
    tile_expert = jnp.sum((off[None, 1:] <= tile_start[:, None]).astype(I32), axis=1)
    tile_expert = jnp.minimum(tile_expert, N_EXPERTS - 1).astype(I32)
    n_used = (off[N_EXPERTS] // FFN_ROWS).reshape(1).astype(I32)
    y = None
    for part in range(MOE_PARTS):
        tiles = n_tiles // MOE_PARTS
        rows = slice(part * tiles * FFN_ROWS, (part + 1) * tiles * FFN_ROWS)
        y = _expert_ffn(tile_expert, n_used, h2[src[rows]], w_gate, w_up, w_down, layer,
                        part * tiles, n_tiles, y)
    w0, w1 = route[4].reshape(s, 1), route[5].reshape(s, 1)
    out = None
    for part in range(MOE_PARTS):
        toks = slice(part * s // MOE_PARTS, (part + 1) * s // MOE_PARTS)
        y01 = y[jnp.concatenate([d0[toks], d1[toks]])]
        out = _combine(x1, y01, w0, w1, gt, g_final, final, part * s // MOE_PARTS, out)
    return out


def kernel(x, c, positions, w_ada, b_ada, g_attn, g_mlp, w_in, w_out, lam_q1, lam_k1, lam_q2, lam_k2,
           g_subln, w_gate, w_up, w_down, w_router, b_router, g_final):
    batch, s, d = x.shape
    assert batch == 1 and s % TQ == 0 and s // MOBA_BLOCK <= HEAD_DIM
    assert w_in.shape[2] == 3 * N_KIND_TILES * PROJ_TN
    depth = w_ada.shape[0]
    xs = x.reshape(s, d)
    cos_t, sin_t = _rope_tables(positions[0])
    mod = _modulation(c, w_ada, b_ada)
    tab = _dilated_bias_table()
    gf = g_final.reshape(1, d)
    row = lambda v: v.reshape(1, -1)
    for l in range(depth):
        lam_init = 0.8 - 0.6 * math.exp(-0.3 * l)
        sh_a, sc_a, gt_a, sh_m, sc_m, gt_m = [mod[l, :, k * d:(k + 1) * d] for k in range(6)]
        k, qt, qbt, vt, ksum = _in_projection(xs, row(g_attn[l]), sc_a, sh_a,
                                              _layer_to_bf16(w_in, l, PROJ_TN), cos_t, sin_t)
        o_a = _dilated_attention(qt, k, vt, tab, 0, 0, 0)
        bias = _moba_gate(qbt, _block_means(ksum.reshape(s // MOBA_BLOCK, -1)))
        o_b = _moba_attention(qt, bias, k, vt, 4, 4, 4)
        o_c = _diff_attention(qt, k, vt, row(lam_q1[l]), row(lam_k1[l]), row(lam_q2[l]), row(lam_k2[l]),
                              g_subln[l].reshape(-1, 1), lam_init, 8, 8, 8)
        x1 = _out_projection(o_a, o_b, o_c, w_out[l].T.astype(BF16), xs, gt_a)
        h2, route, counts = _norm_and_route(x1, row(g_mlp[l]), sc_m, sh_m, w_router, b_router)
        xs = _moe(x1, h2, route, counts, w_gate, w_up, w_down, l, gt_m, gf, l == depth - 1)
    return xs.reshape(batch, s, d)
```

```python
import functools
import math

import numpy as np
import jax
import jax.numpy as jnp
from jax import lax
from jax.experimental import pallas as pl
from jax.experimental.pallas import tpu as pltpu

F32 = jnp.float32
BF16 = jnp.bfloat16
I32 = jnp.int32

HEAD_DIM = 64
LANES = 128
A_PATTERNS = ((128, 1), (512, 4), (2048, 16))
MOBA_BLOCK = 256
MOBA_TOPK = 3
N_EXPERTS = 16
N_GROUPS = 4
EXPERTS_PER_GROUP = N_EXPERTS // N_GROUPS
ROPE_THETA = 10000.0
RMS_EPS = 1e-6
NEG = -1e30
VMEM_LIMIT = 56 * 1024 * 1024

TQ = 1024
TK = 512
KPQ = TQ // TK
assert KPQ == 2
ALL_QUERIES = slice(None)
LATE_QUERIES = slice(TQ // 2, TQ)
Q_CHUNK = 256
PROJ_TM = 512
PROJ_TN = 512
NT_DIMS = (((1,), (1,)), ((), ()))
Q_SCALE = HEAD_DIM ** -0.5 * math.log2(math.e)


def _cparams(n_axes):
    return pltpu.CompilerParams(dimension_semantics=("arbitrary",) * n_axes,
                                vmem_limit_bytes=VMEM_LIMIT)


def _split_bf16(x):
    hi = x.astype(BF16)
    lo = (x - hi.astype(F32)).astype(BF16)
    return hi, lo


def _dot3(a, b, dims=(((1,), (0,)), ((), ()))):
    a_hi, a_lo = _split_bf16(a)
    b_hi, b_lo = _split_bf16(b)
    dg = functools.partial(lax.dot_general, dimension_numbers=dims, preferred_element_type=F32)
    return dg(a_hi, b_hi) + dg(a_hi, b_lo) + dg(a_lo, b_hi)


def _sigmoid(x):
    return 1.0 / (1.0 + jnp.exp(-x))


CAST_STREAMS = 4
CAST_BLOCK_ELEMS = 512 * 1024
CAST_COLS = 1024


def _cast_kernel(*refs):
    o_ref = refs[-1]
    col_tile = o_ref.shape[2]
    for k, x_ref in enumerate(refs[:-1]):
        rows = x_ref.shape[1]
        x = x_ref[0].astype(o_ref.dtype)
        for t in range(o_ref.shape[0]):
            o_ref[t, k * rows:(k + 1) * rows, :] = x[:, t * col_tile:(t + 1) * col_tile]


def _layer_to_bf16(w, l, col_tile=None):
    cols = w.shape[-1]
    col_tile = col_tile or cols
    w3 = w.reshape(w.shape[0], -1, cols)
    rows = w3.shape[1]
    tc = max(min(cols, CAST_COLS), col_tile)
    sr = CAST_BLOCK_ELEMS // tc
    tr = CAST_STREAMS * sr
    stream = lambda k: pl.BlockSpec((1, sr, tc), lambda i, j: (l, CAST_STREAMS * i + k, j))
    return pl.pallas_call(
        _cast_kernel,
        grid=(rows // tr, cols // tc),
        in_specs=[stream(k) for k in range(CAST_STREAMS)],
        out_specs=pl.BlockSpec((tc // col_tile, tr, col_tile), lambda i, j: (j, i, 0)),
        out_shape=jax.ShapeDtypeStruct((cols // col_tile, rows, col_tile), BF16),
        compiler_params=_cparams(2),
    )(*([w3] * CAST_STREAMS))


def _mod_kernel(c_ref, w_ref, b_ref, o_ref):
    c = c_ref[...]
    sc = c * _sigmoid(c)
    d = c.shape[0]
    acc = b_ref[0]
    for r in range(0, d, 256):
        acc = acc + jnp.sum(w_ref[0, r:r + 256, :] * sc[r:r + 256, :], axis=0, keepdims=True)
    o_ref[0] = acc


def _modulation(c, w_ada, b_ada):
    depth, d, n = w_ada.shape
    tn = 512
    return pl.pallas_call(
        _mod_kernel,
        grid=(depth, n // tn),
        in_specs=[pl.BlockSpec((d, 1), lambda l, j: (0, 0)),
                  pl.BlockSpec((1, d, tn), lambda l, j: (l, 0, j)),
                  pl.BlockSpec((1, 1, tn), lambda l, j: (l, 0, j))],
        out_specs=pl.BlockSpec((1, 1, tn), lambda l, j: (l, 0, j)),
        out_shape=jax.ShapeDtypeStruct((depth, 1, n), F32),
        compiler_params=_cparams(2),
    )(c.reshape(d, 1), w_ada, b_ada.reshape(depth, 1, n))


def _modulated_norm(x, g, sc, sh):
    var = jnp.mean(x * x, axis=-1, keepdims=True)
    return (x * lax.rsqrt(var + RMS_EPS) * g) * (1.0 + sc) + sh


def _rope_tile(acc, cos, sin):
    lane = lax.broadcasted_iota(I32, (1, LANES), 1)
    first = (lane & (HEAD_DIM - 1)) < (HEAD_DIM // 2)
    outs = []
    for c in range(acc.shape[1] // LANES):
        xc = acc[:, c * LANES:(c + 1) * LANES]
        rot = jnp.where(first, pltpu.roll(xc, LANES - HEAD_DIM // 2, 1), pltpu.roll(xc, HEAD_DIM // 2, 1))
        outs.append(xc * cos + rot * sin)
    return jnp.concatenate(outs, axis=1)


N_KIND_TILES = 4
COLUMN_TILE_ORDER = (1, 4, 8, 9, 0, 3, 6, 7, 2, 5, 10, 11)
KB_TILE = 1
QB_TILE = N_KIND_TILES + 1


def _inproj_kernel(x_ref, g_ref, sc_ref, sh_ref, w_ref, cos_ref, sin_ref,
                   k_ref, qt_ref, qbt_ref, vt_ref, ksum_ref, h_scr):
    tm = x_ref.shape[0]
    tn = w_ref.shape[2]
    h_scr[...] = _modulated_norm(x_ref[...], g_ref[...], sc_ref[...], sh_ref[...]).astype(BF16)
    for pos, tile in enumerate(COLUMN_TILE_ORDER):
        acc = jnp.dot(h_scr[...], w_ref[tile], preferred_element_type=F32)
        kind, t = divmod(pos, N_KIND_TILES)
        cols = slice(t * tn, (t + 1) * tn)
        if kind == 0:
            r = _rope_tile(acc, cos_ref[...], sin_ref[...])
            k_ref[:, cols] = r.astype(BF16)
            if pos == KB_TILE:
                ksum_ref[0] = jnp.sum(r.reshape(tm // MOBA_BLOCK, MOBA_BLOCK, tn), axis=1)
        elif kind == 1:
            rt = _rope_tile(acc, cos_ref[...], sin_ref[...]).T
            qt_ref[cols, :] = (rt * Q_SCALE).astype(BF16)
            if pos == QB_TILE:
                qbt_ref[...] = rt
        else:
            vt = acc.T.astype(BF16)
            for b in range(vt_ref.shape[0]):
                vt_ref[b, cols, :] = vt[:, b * TK:(b + 1) * TK]


def _in_projection(x, g, sc, sh, w_bf16, cos_t, sin_t):
    s, d = x.shape
    tm, tn = PROJ_TM, w_bf16.shape[2]
    nb = tm // MOBA_BLOCK
    width = N_KIND_TILES * tn
    row = lambda i: (0, 0)
    return pl.pallas_call(
        _inproj_kernel,
        grid=(s // tm,),
        in_specs=[pl.BlockSpec((tm, d), lambda i: (i, 0)),
                  pl.BlockSpec((1, d), row), pl.BlockSpec((1, d), row), pl.BlockSpec((1, d), row),
                  pl.BlockSpec(w_bf16.shape, lambda i: (0, 0, 0),
                               pipeline_mode=pl.Buffered(1)),
                  pl.BlockSpec((tm, LANES), lambda i: (i, 0)),
                  pl.BlockSpec((tm, LANES), lambda i: (i, 0))],
        out_specs=[pl.BlockSpec((tm, width), lambda i: (i, 0)),
                   pl.BlockSpec((width, tm), lambda i: (0, i)),
                   pl.BlockSpec((tn, tm), lambda i: (0, i)),
                   pl.BlockSpec((tm // TK, width, TK), lambda i: (i, 0, 0)),
                   pl.BlockSpec((1, nb, tn), lambda i: (i, 0, 0))],
        out_shape=[jax.ShapeDtypeStruct((s, width), BF16),
                   jax.ShapeDtypeStruct((width, s), BF16),
                   jax.ShapeDtypeStruct((tn, s), F32),
                   jax.ShapeDtypeStruct((s // TK, width, TK), BF16),
                   jax.ShapeDtypeStruct((s // tm, nb, tn), F32)],
        scratch_shapes=[pltpu.VMEM((tm, d), BF16)],
        compiler_params=_cparams(1),
    )(x, g, sc, sh, w_bf16, cos_t, sin_t)


def _flash_step(s_t, v_t, state):
    m_prev, l_prev, acc_prev = state
    m_new = jnp.maximum(m_prev, jnp.max(s_t, axis=0, keepdims=True))
    alpha = jnp.exp2(m_prev - m_new)
    p = jnp.exp2(s_t - m_new)
    l_new = alpha * l_prev + jnp.sum(p, axis=0, keepdims=True)
    acc_new = alpha * acc_prev + jnp.dot(v_t, p.astype(BF16), preferred_element_type=F32)
    return m_new, l_new, acc_new


def _sweep(first, n_loop, scores, consume, bufs, tail_masks):
    halves = (bufs[0:2], bufs[2:4])

    def park_pair(a, dst):
        for d in range(2):
            for h, val in enumerate(scores(a + d)):
                dst[d][h] = val

    def pair(a, cur, nxt):
        park_pair(a + 2, nxt)
        consume(a, cur[0], None, ALL_QUERIES)
        consume(a + 1, cur[1], None, ALL_QUERIES)

    def tail(cur):
        t0 = first + n_loop
        consume(t0, cur[0], tail_masks[0], ALL_QUERIES)
        consume(t0 + 1, cur[1], tail_masks[1], LATE_QUERIES)

    park_pair(first, halves[0])

    def two_pairs(j, carry):
        pair(first + 4 * j, halves[0], halves[1])
        pair(first + 4 * j + 2, halves[1], halves[0])
        return carry

    n_pairs = n_loop // 2
    lax.fori_loop(0, n_pairs // 2, two_pairs, 0)

    @pl.when(n_pairs % 2 == 1)
    def _():
        pair(first + n_loop - 2, halves[0], halves[1])
        tail(halves[1])

    @pl.when(n_pairs % 2 == 0)
    def _():
        tail(halves[0])


def _softmax_pv(buf, v_ts, refs, mask, cols):
    start, stop, _ = cols.indices(TQ)
    for c0 in range(start, stop, Q_CHUNK):
        cs = slice(c0, c0 + Q_CHUNK)
        for h, v_t in enumerate(v_ts):
            s_t = buf[h, :, cs]
            if mask is not None:
                s_t = jnp.where(mask[:, cs], s_t, NEG)
            _store_state(refs[h], _flash_step(s_t, v_t, _load_state(refs[h], cs)), cs)


N_SCORE_BUFS = 4


def _score_scratch(n_streams):
    return [pltpu.VMEM((n_streams, TK, TQ), F32) for _ in range(N_SCORE_BUFS)]


def _qk_t(k, q_t):
    return jnp.dot(k, q_t, preferred_element_type=F32)


def _k_block(k_ref, n):
    return k_ref[pl.ds(pl.multiple_of(n * TK, TK), TK), :]


def _init_state(refs):
    m_ref, l_ref, acc_ref = refs
    m_ref[...] = jnp.full(m_ref.shape, NEG, F32)
    l_ref[...] = jnp.zeros(l_ref.shape, F32)
    acc_ref[...] = jnp.zeros(acc_ref.shape, F32)


def _load_state(refs, cols=ALL_QUERIES):
    return tuple(r[:, cols] for r in refs)


def _store_state(refs, state, cols=ALL_QUERIES):
    for r, val in zip(refs, state):
        r[:, cols] = val


def _flash_scratch(n_streams, dv):
    out = []
    for _ in range(n_streams):
        out += [pltpu.VMEM((1, TQ), F32), pltpu.VMEM((1, TQ), F32), pltpu.VMEM((dv, TQ), F32)]
    return out


def _causal_t(d):
    r = lax.broadcasted_iota(I32, (TK, TQ), 0) + d * TK
    return r <= lax.broadcasted_iota(I32, (TK, TQ), 1)


def _stack_rows(top, bottom):
    return jnp.concatenate([top, bottom], axis=0)


def _dilated_bias_table():
    n_off = A_PATTERNS[-1][0] // TK + KPQ
    r = np.arange(TK)[:, None]
    c = np.arange(TQ)[None, :]
    tabs = []
    for t in range(n_off):
        delta = (t - (KPQ - 1)) * TK + c - r
        mult = np.zeros_like(delta)
        for window, dil in A_PATTERNS:
            mult += ((delta >= 0) & (delta % dil == 0) & (delta <= window)).astype(delta.dtype)
        tabs.append(np.where(mult > 0, np.log2(np.maximum(mult, 1).astype(np.float64)), NEG))
    return jnp.asarray(np.stack(tabs), F32)


def _dilated_kernel(qt_ref, k_ref, vt_ref, tab_ref, o_ref, *scratch):
    bufs, scr = scratch[:N_SCORE_BUFS], scratch[N_SCORE_BUFS:]
    i = pl.program_id(1)
    refs = (scr[0:3], scr[3:6])
    q_t = qt_ref[...]
    zero = jnp.zeros((HEAD_DIM, TQ), BF16)
    qs = (_stack_rows(q_t[:HEAD_DIM], zero), _stack_rows(zero, q_t[HEAD_DIM:]))
    n_off = tab_ref.shape[0]
    last = KPQ * i + (KPQ - 1)
    for h in range(2):
        _init_state(refs[h])

    def scores(n):
        kb = _k_block(k_ref, n)
        bias = tab_ref[last - n]
        return [_qk_t(kb, qs[h]) + bias for h in range(2)]

    def consume(n, buf, mask, cols):
        vb = vt_ref[n]
        _softmax_pv(buf, [vb[:HEAD_DIM, :], vb[HEAD_DIM:, :]], refs, mask, cols)

    first = jnp.maximum(last - (n_off - 1), 0)
    _sweep(first, last - 1 - first, scores, consume, bufs, (None, None))
    for h in range(2):
        _, l, acc = _load_state(refs[h])
        o_ref[h * HEAD_DIM:(h + 1) * HEAD_DIM, :] = (acc / l).astype(o_ref.dtype)


def _dilated_attention(qt, k, vt, tab, qrow, kcol, vrow):
    s = k.shape[0]
    n_pairs = 4
    return pl.pallas_call(
        _dilated_kernel,
        grid=(n_pairs, s // TQ),
        in_specs=[pl.BlockSpec((LANES, TQ), lambda hp, i: (qrow + hp, i)),
                  pl.BlockSpec((s, LANES), lambda hp, i: (0, kcol + hp)),
                  pl.BlockSpec((s // TK, LANES, TK), lambda hp, i: (0, vrow + hp, 0)),
                  pl.BlockSpec(tab.shape, lambda hp, i: (0, 0, 0), pipeline_mode=pl.Buffered(1))],
        out_specs=pl.BlockSpec((LANES, TQ), lambda hp, i: (hp, i)),
        out_shape=jax.ShapeDtypeStruct((n_pairs * LANES, s), BF16),
        scratch_shapes=_score_scratch(2) + _flash_scratch(2, HEAD_DIM),
        compiler_params=_cparams(2),
    )(qt, k, vt, tab)


def _moba_gate_kernel(qt_ref, km_ref, bias_ref):
    i = pl.program_id(0)
    tg = qt_ref.shape[1]
    n_heads, nblk, _ = km_ref.shape
    blk = lax.broadcasted_iota(I32, (nblk, tg), 0)
    blk_f = blk.astype(F32)
    tok = i * tg + lax.broadcasted_iota(I32, (nblk, tg), 1)
    own = jnp.right_shift(tok, int(math.log2(MOBA_BLOCK)))
    past = blk < own
    for h in range(n_heads):
        rows = slice(h * HEAD_DIM, (h + 1) * HEAD_DIM)
        gate = _dot3(km_ref[h], qt_ref[rows, :])
        sel = blk == own
        for _ in range(MOBA_TOPK):
            cand = past & jnp.logical_not(sel)
            g = jnp.where(cand, gate, -jnp.inf)
            mx = jnp.max(g, axis=0, keepdims=True)
            first = jnp.min(jnp.where(cand & (g == mx), blk_f, 2.0 * nblk), axis=0, keepdims=True)
            sel = sel | (blk_f == first)
        bias_ref[rows, :] = jnp.where(sel, 0.0, NEG).astype(BF16)


def _moba_gate(qbt, kmean):
    width, s = qbt.shape
    tg = 512
    return pl.pallas_call(
        _moba_gate_kernel,
        grid=(s // tg,),
        in_specs=[pl.BlockSpec((width, tg), lambda i: (0, i)),
                  pl.BlockSpec(kmean.shape, lambda i: (0, 0, 0))],
        out_specs=pl.BlockSpec((width, tg), lambda i: (0, i)),
        out_shape=jax.ShapeDtypeStruct((width, s), BF16),
        compiler_params=_cparams(1),
    )(qbt, kmean)


def _moba_kernel(qt_ref, bias_ref, k_ref, vt_ref, o_ref, *scratch):
    i = pl.program_id(1)
    bufs, scr = scratch[:N_SCORE_BUFS], scratch[N_SCORE_BUFS:]
    refs = (scr[0:3], scr[3:6])
    q_t = qt_ref[...]
    b_t = bias_ref[...]
    qs = (_stack_rows(q_t[:HEAD_DIM], b_t[:HEAD_DIM]), _stack_rows(b_t[HEAD_DIM:], q_t[HEAD_DIM:]))
    lane = lax.broadcasted_iota(I32, (1, LANES), 1)
    lo = lane < HEAD_DIM
    mine = (lo, jnp.logical_not(lo))
    blk = lane & (HEAD_DIM - 1)
    sub_block = jnp.right_shift(lax.broadcasted_iota(I32, (TK, LANES), 0), int(math.log2(MOBA_BLOCK)))
    for h in range(2):
        _init_state(refs[h])

    def scores(n):
        kb = _k_block(k_ref, n)
        onehot = jnp.where(blk == n * (TK // MOBA_BLOCK) + sub_block, 1.0, 0.0).astype(BF16)
        return [_qk_t(jnp.where(mine[h], kb, onehot), qs[h]) for h in range(2)]

    def consume(n, buf, mask, cols):
        vb = vt_ref[n]
        _softmax_pv(buf, [vb[:HEAD_DIM, :], vb[HEAD_DIM:, :]], refs, mask, cols)

    _sweep(0, KPQ * i, scores, consume, bufs, (_causal_t(0), _causal_t(1)))
    for h in range(2):
        _, l, acc = _load_state(refs[h])
        o_ref[h * HEAD_DIM:(h + 1) * HEAD_DIM, :] = (acc / l).astype(o_ref.dtype)


def _moba_attention(qt, bias, k, vt, qrow, kcol, vrow):
    s = k.shape[0]
    n_pairs = 4
    return pl.pallas_call(
        _moba_kernel,
        grid=(n_pairs, s // TQ),
        in_specs=[pl.BlockSpec((LANES, TQ), lambda hp, i: (qrow + hp, i)),
                  pl.BlockSpec((LANES, TQ), lambda hp, i: (hp, i)),
                  pl.BlockSpec((s, LANES), lambda hp, i: (0, kcol + hp)),
                  pl.BlockSpec((s // TK, LANES, TK), lambda hp, i: (0, vrow + hp, 0))],
        out_specs=pl.BlockSpec((LANES, TQ), lambda hp, i: (hp, i)),
        out_shape=jax.ShapeDtypeStruct((n_pairs * LANES, s), BF16),
        scratch_shapes=_score_scratch(2) + _flash_scratch(2, HEAD_DIM),
        compiler_params=_cparams(2),
    )(qt, bias, k, vt)


def _diff_kernel(lam_init, qt_ref, k_ref, vt_ref, lq1_ref, lk1_ref, lq2_ref, lk2_ref, g_ref, o_ref,
                 *scratch):
    i = pl.program_id(1)
    bufs, scr = scratch[:N_SCORE_BUFS], scratch[N_SCORE_BUFS:]
    refs = (scr[0:3], scr[3:6])
    q_t = qt_ref[...]
    zero = jnp.zeros((HEAD_DIM, TQ), BF16)
    qs = (_stack_rows(q_t[:HEAD_DIM], zero), _stack_rows(zero, q_t[HEAD_DIM:]))
    for h in range(2):
        _init_state(refs[h])

    def scores(n):
        kb = _k_block(k_ref, n)
        return [_qk_t(kb, qs[h]) for h in range(2)]

    def consume(n, buf, mask, cols):
        vb = vt_ref[n]
        _softmax_pv(buf, [vb, vb], refs, mask, cols)

    _sweep(0, KPQ * i, scores, consume, bufs, (_causal_t(0), _causal_t(1)))
    lam = (jnp.exp(jnp.sum(lq1_ref[...] * lk1_ref[...], axis=1, keepdims=True))
           - jnp.exp(jnp.sum(lq2_ref[...] * lk2_ref[...], axis=1, keepdims=True)) + lam_init)
    _, l1, acc1 = _load_state(refs[0])
    _, l2, acc2 = _load_state(refs[1])
    o = acc1 / l1 - lam * (acc2 / l2)
    var = jnp.mean(o * o, axis=0, keepdims=True)
    o = (o * lax.rsqrt(var + RMS_EPS) * g_ref[...]) * (1.0 - lam_init)
    o_ref[...] = o.astype(o_ref.dtype)


def _diff_attention(qt, k, vt, lq1, lk1, lq2, lk2, g_sub, lam_init, qrow, kcol, vrow):
    s = k.shape[0]
    n_heads = 8
    vec = lambda n: pl.BlockSpec((1, n), lambda h, i: (0, 0))
    return pl.pallas_call(
        functools.partial(_diff_kernel, lam_init),
        grid=(n_heads, s // TQ),
        in_specs=[pl.BlockSpec((LANES, TQ), lambda h, i: (qrow + h, i)),
                  pl.BlockSpec((s, LANES), lambda h, i: (0, kcol + h)),
                  pl.BlockSpec((s // TK, LANES, TK), lambda h, i: (0, vrow + h, 0)),
                  vec(HEAD_DIM), vec(HEAD_DIM), vec(HEAD_DIM), vec(HEAD_DIM),
                  pl.BlockSpec((LANES, 1), lambda h, i: (0, 0))],
        out_specs=pl.BlockSpec((LANES, TQ), lambda h, i: (h, i)),
        out_shape=jax.ShapeDtypeStruct((n_heads * LANES, s), BF16),
        scratch_shapes=_score_scratch(2) + _flash_scratch(2, LANES),
        compiler_params=_cparams(2),
    )(qt, k, vt, lq1, lk1, lq2, lk2, g_sub)


def _outproj_kernel(oa_ref, ob_ref, oc_ref, wt_ref, x_ref, gt_ref, o_ref):
    wa = oa_ref.shape[0]
    wb = ob_ref.shape[0]
    y_t = (jnp.dot(wt_ref[:, 0:wa], oa_ref[...], preferred_element_type=F32)
           + jnp.dot(wt_ref[:, wa:wa + wb], ob_ref[...], preferred_element_type=F32)
           + jnp.dot(wt_ref[:, wa + wb:], oc_ref[...], preferred_element_type=F32))
    o_ref[...] = x_ref[...] + gt_ref[...] * y_t.T


def _out_projection(oa_t, ob_t, oc_t, wt_bf16, x, gt):
    s, d = x.shape
    tm = 256
    return pl.pallas_call(
        _outproj_kernel,
        grid=(s // tm,),
        in_specs=[pl.BlockSpec((oa_t.shape[0], tm), lambda i: (0, i)),
                  pl.BlockSpec((ob_t.shape[0], tm), lambda i: (0, i)),
                  pl.BlockSpec((oc_t.shape[0], tm), lambda i: (0, i)),
                  pl.BlockSpec(wt_bf16.shape, lambda i: (0, 0)),
                  pl.BlockSpec((tm, d), lambda i: (i, 0)),
                  pl.BlockSpec((1, d), lambda i: (0, 0))],
        out_specs=pl.BlockSpec((tm, d), lambda i: (i, 0)),
        out_shape=jax.ShapeDtypeStruct((s, d), F32),
        compiler_params=_cparams(1),
    )(oa_t, ob_t, oc_t, wt_bf16, x, gt)


def _route_kernel(x_ref, g_ref, sc_ref, sh_ref, wr_ref, br_ref, tri_ref,
                  h_ref, route_ref, count_ref, base_scr):
    i = pl.program_id(0)
    tm = x_ref.shape[0]

    @pl.when(i == 0)
    def _():
        base_scr[...] = jnp.zeros_like(base_scr)

    h = _modulated_norm(x_ref[...], g_ref[...], sc_ref[...], sh_ref[...])
    h_ref[...] = h.astype(BF16)
    logits = _dot3(wr_ref[...], h, NT_DIMS)
    s = _sigmoid(logits)
    sg = s + br_ref[...]
    rows = [sg[r:r + 1, :] for r in range(N_EXPERTS)]

    def top2_sum(vals):
        best = None
        for a in range(len(vals)):
            for b in range(a + 1, len(vals)):
                pair = vals[a] + vals[b]
                best = pair if best is None else jnp.maximum(best, pair)
        return best

    score = [top2_sum(rows[g * EXPERTS_PER_GROUP:(g + 1) * EXPERTS_PER_GROUP]) for g in range(N_GROUPS)]
    best, grp = score[0], jnp.zeros((1, tm), I32)
    for g in range(1, N_GROUPS):
        better = score[g] > best
        grp = jnp.where(better, g, grp)
        best = jnp.where(better, score[g], best)
    cand = []
    for j in range(EXPERTS_PER_GROUP):
        c = rows[j]
        for g in range(1, N_GROUPS):
            c = jnp.where(grp == g, rows[g * EXPERTS_PER_GROUP + j], c)
        cand.append(c)
    b0, l0 = cand[0], jnp.zeros((1, tm), I32)
    for j in range(1, EXPERTS_PER_GROUP):
        better = cand[j] > b0
        l0 = jnp.where(better, j, l0)
        b0 = jnp.where(better, cand[j], b0)
    b1, l1 = jnp.full((1, tm), -jnp.inf, F32), jnp.zeros((1, tm), I32)
    for j in range(EXPERTS_PER_GROUP):
        better = (l0 != j) & (cand[j] > b1)
        l1 = jnp.where(better, j, l1)
        b1 = jnp.where(better, cand[j], b1)
    e0 = grp * EXPERTS_PER_GROUP + l0
    e1 = grp * EXPERTS_PER_GROUP + l1
    erow = lax.broadcasted_iota(I32, (N_EXPERTS, tm), 0)
    sel0 = erow == e0
    sel1 = erow == e1
    s0 = jnp.sum(jnp.where(sel0, s, 0.0), axis=0, keepdims=True)
    s1 = jnp.sum(jnp.where(sel1, s, 0.0), axis=0, keepdims=True)
    tot = s0 + s1
    chosen = jnp.where(sel0 | sel1, 1.0, 0.0).astype(BF16)
    csum = jnp.dot(chosen, tri_ref[...], preferred_element_type=F32)
    pos = base_scr[...] + csum - 1.0
    r0 = jnp.sum(jnp.where(sel0, pos, 0.0), axis=0, keepdims=True)
    r1 = jnp.sum(jnp.where(sel1, pos, 0.0), axis=0, keepdims=True)
    base = base_scr[...] + csum[:, tm - 1:tm]
    base_scr[...] = base
    count_ref[...] = jnp.broadcast_to(base, count_ref.shape)
    zero = jnp.zeros((1, tm), F32)
    for r, val in enumerate((e0.astype(F32), e1.astype(F32), r0, r1, s0 / tot, s1 / tot, zero, zero)):
        route_ref[r:r + 1, :] = val


def _norm_and_route(x, g, sc, sh, w_router, b_router):
    s, d = x.shape
    tm = 512
    tri = jnp.asarray(np.triu(np.ones((tm, tm), np.float32)), BF16)
    row = lambda i: (0, 0)
    return pl.pallas_call(
        _route_kernel,
        grid=(s // tm,),
        in_specs=[pl.BlockSpec((tm, d), lambda i: (i, 0)),
                  pl.BlockSpec((1, d), row), pl.BlockSpec((1, d), row), pl.BlockSpec((1, d), row),
                  pl.BlockSpec((N_EXPERTS, d), row), pl.BlockSpec((N_EXPERTS, 1), row),
                  pl.BlockSpec((tm, tm), row)],
        out_specs=[pl.BlockSpec((tm, d), lambda i: (i, 0)),
                   pl.BlockSpec((8, tm), lambda i: (0, i)),
                   pl.BlockSpec((N_EXPERTS, LANES), row)],
        out_shape=[jax.ShapeDtypeStruct((s, d), BF16),
                   jax.ShapeDtypeStruct((8, s), F32),
                   jax.ShapeDtypeStruct((N_EXPERTS, LANES), F32)],
        scratch_shapes=[pltpu.VMEM((N_EXPERTS, 1), F32)],
        compiler_params=_cparams(1),
    )(x, g, sc, sh, w_router.T, b_router.reshape(N_EXPERTS, 1), tri)


FFN_ROWS = 512


def _ffn_kernel(tile_base, te_ref, nu_ref, xs_ref, wg_ref, wu_ref, wd_ref, *rest):
    y_ref = rest[-1]
    j = pl.program_id(0) + tile_base

    @pl.when(j < nu_ref[0])
    def _():
        x = xs_ref[...]
        de = wg_ref.shape[3]
        y = None
        for c in range(de // FFN_CHUNK):
            cs = slice(c * FFN_CHUNK, (c + 1) * FFN_CHUNK)
            a = jnp.dot(x, wg_ref[0, 0, :, cs].astype(BF16), preferred_element_type=F32)
            b = jnp.dot(x, wu_ref[0, 0, :, cs].astype(BF16), preferred_element_type=F32)
            act = (a * _sigmoid(a) * b).astype(BF16)
            part = jnp.dot(act, wd_ref[0, 0, cs, :].astype(BF16), preferred_element_type=F32)
            y = part if y is None else y + part
        y_ref[...] = y.astype(y_ref.dtype)

    @pl.when(j >= nu_ref[0])
    def _():
        y_ref[...] = jnp.zeros_like(y_ref)


FFN_CHUNK = 512
MOE_PARTS = 2


def _expert_ffn(tile_expert, n_used, xs_part, w_gate, w_up, w_down, layer, tile_base, n_tiles, y_prev):
    r, d = xs_part.shape
    de = w_gate.shape[3]
    expert = lambda j, te, nu: (layer, te[j + tile_base], 0, 0)
    in_specs = [pl.BlockSpec((FFN_ROWS, d), lambda j, te, nu: (j, 0)),
                pl.BlockSpec((1, 1, d, de), expert, pipeline_mode=pl.Buffered(1)),
                pl.BlockSpec((1, 1, d, de), expert, pipeline_mode=pl.Buffered(1)),
                pl.BlockSpec((1, 1, de, d), expert, pipeline_mode=pl.Buffered(1))]
    args = [tile_expert, n_used, xs_part, w_gate, w_up, w_down]
    aliases = {}
    if y_prev is not None:
        in_specs.append(pl.BlockSpec(memory_space=pl.ANY))
        aliases = {len(args): 0}
        args.append(y_prev)
    grid_spec = pltpu.PrefetchScalarGridSpec(
        num_scalar_prefetch=2,
        grid=(r // FFN_ROWS,),
        in_specs=in_specs,
        out_specs=pl.BlockSpec((FFN_ROWS, d), lambda j, te, nu: (j + tile_base, 0)),
    )
    return pl.pallas_call(
        functools.partial(_ffn_kernel, tile_base),
        grid_spec=grid_spec,
        out_shape=jax.ShapeDtypeStruct((n_tiles * FFN_ROWS, d), BF16),
        input_output_aliases=aliases,
        compiler_params=_cparams(1),
    )(*args)


def _combine_kernel(final, x_ref, y0_ref, y1_ref, w0_ref, w1_ref, gt_ref, gf_ref, *rest):
    o_ref = rest[-1]
    x = x_ref[...] + gt_ref[...] * (w0_ref[...] * y0_ref[...].astype(F32)
                                    + w1_ref[...] * y1_ref[...].astype(F32))
    if final:
        var = jnp.mean(x * x, axis=-1, keepdims=True)
        x = x * lax.rsqrt(var + RMS_EPS) * gf_ref[...]
    o_ref[...] = x


def _combine(x, y01_part, w0, w1, gt, g_final, final, row_base, out_prev):
    s, d = x.shape
    tm = 256
    part = y01_part.shape[0] // 2
    base = row_base // tm
    big = pl.BlockSpec((tm, d), lambda i: (i + base, 0))
    first = pl.BlockSpec((tm, d), lambda i: (i, 0))
    second = pl.BlockSpec((tm, d), lambda i: (i + part // tm, 0))
    col = pl.BlockSpec((tm, 1), lambda i: (i + base, 0))
    vec = pl.BlockSpec((1, d), lambda i: (0, 0))
    in_specs = [big, first, second, col, col, vec, vec]
    args = [x, y01_part, y01_part, w0, w1, gt, g_final]
    aliases = {}
    if out_prev is not None:
        in_specs.append(pl.BlockSpec(memory_space=pl.ANY))
        aliases = {len(args): 0}
        args.append(out_prev)
    return pl.pallas_call(
        functools.partial(_combine_kernel, final),
        grid=(part // tm,),
        in_specs=in_specs,
        out_specs=big,
        out_shape=jax.ShapeDtypeStruct((s, d), F32),
        input_output_aliases=aliases,
        compiler_params=_cparams(1),
    )(*args)


def _rope_tables(positions):
    half = HEAD_DIM // 2
    inv = ROPE_THETA ** (-jnp.arange(0, HEAD_DIM, 2, dtype=F32) / HEAD_DIM)
    ang = positions.astype(F32)[:, None] * inv
    cos, sin = jnp.cos(ang), jnp.sin(ang)
    cos_t = jnp.tile(cos, (1, LANES // half))
    sin_t = jnp.tile(jnp.concatenate([-sin, sin], axis=1), (1, LANES // HEAD_DIM))
    return cos_t, sin_t


def _block_means(ksum):
    nblk, width = ksum.shape
    km = (ksum / MOBA_BLOCK).reshape(nblk, width // HEAD_DIM, HEAD_DIM)
    km = jnp.pad(km, ((0, HEAD_DIM - nblk), (0, 0), (0, 0)))
    return jnp.transpose(km, (1, 0, 2))


def _row_source_kernel(d0_ref, d1_ref, src_ref):
    n_rows, s = src_ref.shape[0], d0_ref.shape[0]

    def put(t, carry):
        src_ref[d0_ref[t]] = t
        src_ref[d1_ref[t]] = t
        return carry

    for base in range(0, n_rows, s):

        def pad(r, carry, base=base):
            src_ref[r] = r - base
            return carry

        lax.fori_loop(base, min(base + s, n_rows), pad, 0, unroll=8)
    lax.fori_loop(0, s, put, 0, unroll=8)


def _row_source(d0, d1, n_rows):
    smem = pl.BlockSpec(memory_space=pltpu.SMEM)
    return pl.pallas_call(
        _row_source_kernel,
        in_specs=[smem, smem],
        out_specs=smem,
        out_shape=jax.ShapeDtypeStruct((n_rows,), I32),
    )(d0, d1)


def _moe(x1, h2, route, counts, w_gate, w_up, w_down, layer, gt, g_final, final):
    s, d = x1.shape
    e0, e1 = route[0].astype(I32), route[1].astype(I32)
    r0, r1 = route[2].astype(I32), route[3].astype(I32)
    cnt = counts[:, 0].astype(I32)
    cnt_pad = ((cnt + FFN_ROWS - 1) // FFN_ROWS) * FFN_ROWS
    off = jnp.concatenate([jnp.zeros((1,), I32), jnp.cumsum(cnt_pad).astype(I32)])
    d0 = off[e0] + r0
    d1 = off[e1] + r1
    n_rows = 2 * s + N_EXPERTS * FFN_ROWS
    src = _row_source(d0, d1, n_rows)
    n_tiles = n_rows // FFN_ROWS
    tile_start = jnp.arange(n_tiles, dtype=I32) * FFN_ROWS
    tile_expert = jnp.sum((off[None, 1:] <= tile_start[:, None]).astype(I32), axis=1)
    tile_expert = jnp.minimum(tile_expert, N_EXPERTS - 1).astype(I32)
    n_used = (off[N_EXPERTS] // FFN_ROWS).reshape(1).astype(I32)
    y = None
    for part in range(MOE_PARTS):
        tiles = n_tiles // MOE_PARTS
        rows = slice(part * tiles * FFN_ROWS, (part + 1) * tiles * FFN_ROWS)
        y = _expert_ffn(tile_expert, n_used, h2[src[rows]], w_gate, w_up, w_down, layer,
                        part * tiles, n_tiles, y)
    w0, w1 = route[4].reshape(s, 1), route[5].reshape(s, 1)
    out = None
    for part in range(MOE_PARTS):
        toks = slice(part * s // MOE_PARTS, (part + 1) * s // MOE_PARTS)
        y01 = y[jnp.concatenate([d0[toks], d1[toks]])]
        out = _combine(x1, y01, w0, w1, gt, g_final, final, part * s // MOE_PARTS, out)
    return out


def kernel(x, c, positions, w_ada, b_ada, g_attn, g_mlp, w_in, w_out, lam_q1, lam_k1, lam_q2, lam_k2,
           g_subln, w_gate, w_up, w_down, w_router, b_router, g_final):
    batch, s, d = x.shape
    assert batch == 1 and s % TQ == 0 and s // MOBA_BLOCK <= HEAD_DIM
    assert w_in.shape[2] == 3 * N_KIND_TILES * PROJ_TN
    depth = w_ada.shape[0]
    xs = x.reshape(s, d)
    cos_t, sin_t = _rope_tables(positions[0])
    mod = _modulation(c, w_ada, b_ada)
    tab = _dilated_bias_table()
    gf = g_final.reshape(1, d)
    row = lambda v: v.reshape(1, -1)
    for l in range(depth):
        lam_init = 0.8 - 0.6 * math.exp(-0.3 * l)
        sh_a, sc_a, gt_a, sh_m, sc_m, gt_m = [mod[l, :, k * d:(k + 1) * d] for k in range(6)]
        k, qt, qbt, vt, ksum = _in_projection(xs, row(g_attn[l]), sc_a, sh_a,
                                              _layer_to_bf16(w_in, l, PROJ_TN), cos_t, sin_t)
        o_a = _dilated_attention(qt, k, vt, tab, 0, 0, 0)
        bias = _moba_gate(qbt, _block_means(ksum.reshape(s // MOBA_BLOCK, -1)))
        o_b = _moba_attention(qt, bias, k, vt, 4, 4, 4)
        o_c = _diff_attention(qt, k, vt, row(lam_q1[l]), row(lam_k1[l]), row(lam_q2[l]), row(lam_k2[l]),
                              g_subln[l].reshape(-1, 1), lam_init, 8, 8, 8)
        x1 = _out_projection(o_a, o_b, o_c, w_out[l].T.astype(BF16), xs, gt_a)
        h2, route, counts = _norm_and_route(x1, row(g_mlp[l]), sc_m, sh_m, w_router, b_router)
        xs = _moe(x1, h2, route, counts, w_gate, w_up, w_down, l, gt_m, gf, l == depth - 1)
    return xs.reshape(batch, s, d)
```

```python
import functools
import math

import numpy as np
import jax
import jax.numpy as jnp
from jax import lax
from jax.experimental import pallas as pl
from jax.experimental.pallas import tpu as pltpu

F32 = jnp.float32
BF16 = jnp.bfloat16
I32 = jnp.int32

HEAD_DIM = 64
LANES = 128
A_PATTERNS = ((128, 1), (512, 4), (2048, 16))
MOBA_BLOCK = 256
MOBA_TOPK = 3
N_EXPERTS = 16
N_GROUPS = 4
EXPERTS_PER_GROUP = N_EXPERTS // N_GROUPS
ROPE_THETA = 10000.0
RMS_EPS = 1e-6
NEG = -1e30
VMEM_LIMIT = 56 * 1024 * 1024

TQ = 1024
TK = 512
KPQ = TQ // TK
assert KPQ == 2
ALL_QUERIES = slice(None)
LATE_QUERIES = slice(TQ // 2, TQ)
Q_CHUNK = 256
PROJ_TM = 512
PROJ_TN = 512
NT_DIMS = (((1,), (1,)), ((), ()))
Q_SCALE = HEAD_DIM ** -0.5 * math.log2(math.e)


def _cparams(n_axes):
    return pltpu.CompilerParams(dimension_semantics=("arbitrary",) * n_axes,
                                vmem_limit_bytes=VMEM_LIMIT)


def _split_bf16(x):
    hi = x.astype(BF16)
    lo = (x - hi.astype(F32)).astype(BF16)
    return hi, lo


def _dot3(a, b, dims=(((1,), (0,)), ((), ()))):
    a_hi, a_lo = _split_bf16(a)
    b_hi, b_lo = _split_bf16(b)
    dg = functools.partial(lax.dot_general, dimension_numbers=dims, preferred_element_type=F32)
    return dg(a_hi, b_hi) + dg(a_hi, b_lo) + dg(a_lo, b_hi)


def _sigmoid(x):
    return 1.0 / (1.0 + jnp.exp(-x))


CAST_STREAMS = 4
CAST_BLOCK_ELEMS = 512 * 1024
CAST_COLS = 1024


def _cast_kernel(*refs):
    o_ref = refs[-1]
    col_tile = o_ref.shape[2]
    for k, x_ref in enumerate(refs[:-1]):
        rows = x_ref.shape[1]
        x = x_ref[0].astype(o_ref.dtype)
        for t in range(o_ref.shape[0]):
            o_ref[t, k * rows:(k + 1) * rows, :] = x[:, t * col_tile:(t + 1) * col_tile]


def _layer_to_bf16(w, l, col_tile=None):
    cols = w.shape[-1]
    col_tile = col_tile or cols
    w3 = w.reshape(w.shape[0], -1, cols)
    rows = w3.shape[1]
    tc = max(min(cols, CAST_COLS), col_tile)
    sr = CAST_BLOCK_ELEMS // tc
    tr = CAST_STREAMS * sr
    stream = lambda k: pl.BlockSpec((1, sr, tc), lambda i, j: (l, CAST_STREAMS * i + k, j))
    return pl.pallas_call(
        _cast_kernel,
        grid=(rows // tr, cols // tc),
        in_specs=[stream(k) for k in range(CAST_STREAMS)],
        out_specs=pl.BlockSpec((tc // col_tile, tr, col_tile), lambda i, j: (j, i, 0)),
        out_shape=jax.ShapeDtypeStruct((cols // col_tile, rows, col_tile), BF16),
        compiler_params=_cparams(2),
    )(*([w3] * CAST_STREAMS))


def _mod_kernel(c_ref, w_ref, b_ref, o_ref):
    c = c_ref[...]
    sc = c * _sigmoid(c)
    d = c.shape[0]
    acc = b_ref[0]
    for r in range(0, d, 256):
        acc = acc + jnp.sum(w_ref[0, r:r + 256, :] * sc[r:r + 256, :], axis=0, keepdims=True)
    o_ref[0] = acc


def _modulation(c, w_ada, b_ada):
    depth, d, n = w_ada.shape
    tn = 512
    return pl.pallas_call(
        _mod_kernel,
        grid=(depth, n // tn),
        in_specs=[pl.BlockSpec((d, 1), lambda l, j: (0, 0)),
                  pl.BlockSpec((1, d, tn), lambda l, j: (l, 0, j)),
                  pl.BlockSpec((1, 1, tn), lambda l, j: (l, 0, j))],
        out_specs=pl.BlockSpec((1, 1, tn), lambda l, j: (l, 0, j)),
        out_shape=jax.ShapeDtypeStruct((depth, 1, n), F32),
        compiler_params=_cparams(2),
    )(c.reshape(d, 1), w_ada, b_ada.reshape(depth, 1, n))


def _modulated_norm(x, g, sc, sh):
    var = jnp.mean(x * x, axis=-1, keepdims=True)
    return (x * lax.rsqrt(var + RMS_EPS) * g) * (1.0 + sc) + sh


def _rope_tile(acc, cos, sin):
    lane = lax.broadcasted_iota(I32, (1, LANES), 1)
    first = (lane & (HEAD_DIM - 1)) < (HEAD_DIM // 2)
    outs = []
    for c in range(acc.shape[1] // LANES):
        xc = acc[:, c * LANES:(c + 1) * LANES]
        rot = jnp.where(first, pltpu.roll(xc, LANES - HEAD_DIM // 2, 1), pltpu.roll(xc, HEAD_DIM // 2, 1))
        outs.append(xc * cos + rot * sin)
    return jnp.concatenate(outs, axis=1)


N_KIND_TILES = 4
COLUMN_TILE_ORDER = (1, 4, 8, 9, 0, 3, 6, 7, 2, 5, 10, 11)
KB_TILE = 1
QB_TILE = N_KIND_TILES + 1


def _inproj_kernel(x_ref, g_ref, sc_ref, sh_ref, w_ref, cos_ref, sin_ref,
                   k_ref, qt_ref, qbt_ref, vt_ref, ksum_ref, h_scr):
    tm = x_ref.shape[0]
    tn = w_ref.shape[2]
    h_scr[...] = _modulated_norm(x_ref[...], g_ref[...], sc_ref[...], sh_ref[...]).astype(BF16)
    for pos, tile in enumerate(COLUMN_TILE_ORDER):
        acc = jnp.dot(h_scr[...], w_ref[tile], preferred_element_type=F32)
        kind, t = divmod(pos, N_KIND_TILES)
        cols = slice(t * tn, (t + 1) * tn)
        if kind == 0:
            r = _rope_tile(acc, cos_ref[...], sin_ref[...])
            k_ref[:, cols] = r.astype(BF16)
            if pos == KB_TILE:
                ksum_ref[0] = jnp.sum(r.reshape(tm // MOBA_BLOCK, MOBA_BLOCK, tn), axis=1)
        elif kind == 1:
            rt = _rope_tile(acc, cos_ref[...], sin_ref[...]).T
            qt_ref[cols, :] = (rt * Q_SCALE).astype(BF16)
            if pos == QB_TILE:
                qbt_ref[...] = rt
        else:
            vt = acc.T.astype(BF16)
            for b in range(vt_ref.shape[0]):
                vt_ref[b, cols, :] = vt[:, b * TK:(b + 1) * TK]


def _in_projection(x, g, sc, sh, w_bf16, cos_t, sin_t):
    s, d = x.shape
    tm, tn = PROJ_TM, w_bf16.shape[2]
    nb = tm // MOBA_BLOCK
    width = N_KIND_TILES * tn
    row = lambda i: (0, 0)
    return pl.pallas_call(
        _inproj_kernel,
        grid=(s // tm,),
        in_specs=[pl.BlockSpec((tm, d), lambda i: (i, 0)),
                  pl.BlockSpec((1, d), row), pl.BlockSpec((1, d), row), pl.BlockSpec((1, d), row),
                  pl.BlockSpec(w_bf16.shape, lambda i: (0, 0, 0),
                               pipeline_mode=pl.Buffered(1)),
                  pl.BlockSpec((tm, LANES), lambda i: (i, 0)),
                  pl.BlockSpec((tm, LANES), lambda i: (i, 0))],
        out_specs=[pl.BlockSpec((tm, width), lambda i: (i, 0)),
                   pl.BlockSpec((width, tm), lambda i: (0, i)),
                   pl.BlockSpec((tn, tm), lambda i: (0, i)),
                   pl.BlockSpec((tm // TK, width, TK), lambda i: (i, 0, 0)),
                   pl.BlockSpec((1, nb, tn), lambda i: (i, 0, 0))],
        out_shape=[jax.ShapeDtypeStruct((s, width), BF16),
                   jax.ShapeDtypeStruct((width, s), BF16),
                   jax.ShapeDtypeStruct((tn, s), F32),
                   jax.ShapeDtypeStruct((s // TK, width, TK), BF16),
                   jax.ShapeDtypeStruct((s // tm, nb, tn), F32)],
        scratch_shapes=[pltpu.VMEM((tm, d), BF16)],
        compiler_params=_cparams(1),
    )(x, g, sc, sh, w_bf16, cos_t, sin_t)


def _flash_step(s_t, v_t, state):
    m_prev, l_prev, acc_prev = state
    m_new = jnp.maximum(m_prev, jnp.max(s_t, axis=0, keepdims=True))
    alpha = jnp.exp2(m_prev - m_new)
    p = jnp.exp2(s_t - m_new)
    l_new = alpha * l_prev + jnp.sum(p, axis=0, keepdims=True)
    acc_new = alpha * acc_prev + jnp.dot(v_t, p.astype(BF16), preferred_element_type=F32)
    return m_new, l_new, acc_new


def _sweep(first, n_loop, scores, consume, bufs, tail_masks, fuse_first):
    halves = (bufs[0:2], bufs[2:4])

    def park_pair(a, dst):
        for d in range(2):
            for h, val in enumerate(scores(a + d)):
                dst[d][h] = val

    def pair(a, cur, nxt):
        park_pair(a + 2, nxt)
        consume(a, cur[0], None, ALL_QUERIES)
        consume(a + 1, cur[1], None, ALL_QUERIES)

    def tail(cur):
        t0 = first + n_loop
        consume(t0, cur[0], tail_masks[0], ALL_QUERIES)
        consume(t0 + 1, cur[1], tail_masks[1], LATE_QUERIES)

    def remaining(start, n_pairs, cur, nxt):
        def two_pairs(j, carry):
            pair(start + 4 * j, cur, nxt)
            pair(start + 4 * j + 2, nxt, cur)
            return carry

        lax.fori_loop(0, n_pairs // 2, two_pairs, 0)

        @pl.when(n_pairs % 2 == 1)
        def _():
            pair(start + 2 * n_pairs - 2, cur, nxt)
            tail(nxt)

        @pl.when(n_pairs % 2 == 0)
        def _():
            tail(cur)

    n_pairs = n_loop // 2
    if not fuse_first:
        park_pair(first, halves[0])
        remaining(first, n_pairs, halves[0], halves[1])
        return

    @pl.when(n_pairs == 0)
    def _():
        park_pair(first, halves[0])
        tail(halves[0])

    @pl.when(n_pairs > 0)
    def _():
        park_pair(first, halves[0])
        pair(first, halves[0], halves[1])
        remaining(first + 2, n_pairs - 1, halves[1], halves[0])


def _softmax_pv(buf, v_ts, refs, mask, cols):
    start, stop, _ = cols.indices(TQ)
    for c0 in range(start, stop, Q_CHUNK):
        cs = slice(c0, c0 + Q_CHUNK)
        for h, v_t in enumerate(v_ts):
            s_t = buf[h, :, cs]
            if mask is not None:
                s_t = jnp.where(mask[:, cs], s_t, NEG)
            _store_state(refs[h], _flash_step(s_t, v_t, _load_state(refs[h], cs)), cs)


N_SCORE_BUFS = 4


def _score_scratch(n_streams):
    return [pltpu.VMEM((n_streams, TK, TQ), F32) for _ in range(N_SCORE_BUFS)]


def _qk_t(k, q_t):
    return jnp.dot(k, q_t, preferred_element_type=F32)


def _k_block(k_ref, n):
    return k_ref[pl.ds(pl.multiple_of(n * TK, TK), TK), :]


def _init_state(refs):
    m_ref, l_ref, acc_ref = refs
    m_ref[...] = jnp.full(m_ref.shape, NEG, F32)
    l_ref[...] = jnp.zeros(l_ref.shape, F32)
    acc_ref[...] = jnp.zeros(acc_ref.shape, F32)


def _load_state(refs, cols=ALL_QUERIES):
    return tuple(r[:, cols] for r in refs)


def _store_state(refs, state, cols=ALL_QUERIES):
    for r, val in zip(refs, state):
        r[:, cols] = val


def _flash_scratch(n_streams, dv):
    out = []
    for _ in range(n_streams):
        out += [pltpu.VMEM((1, TQ), F32), pltpu.VMEM((1, TQ), F32), pltpu.VMEM((dv, TQ), F32)]
    return out


def _causal_t(d):
    r = lax.broadcasted_iota(I32, (TK, TQ), 0) + d * TK
    return r <= lax.broadcasted_iota(I32, (TK, TQ), 1)


def _stack_rows(top, bottom):
    return jnp.concatenate([top, bottom], axis=0)


def _dilated_bias_table():
    n_off = A_PATTERNS[-1][0] // TK + KPQ
    r = np.arange(TK)[:, None]
    c = np.arange(TQ)[None, :]
    tabs = []
    for t in range(n_off):
        delta = (t - (KPQ - 1)) * TK + c - r
        mult = np.zeros_like(delta)
        for window, dil in A_PATTERNS:
            mult += ((delta >= 0) & (delta % dil == 0) & (delta <= window)).astype(delta.dtype)
        tabs.append(np.where(mult > 0, np.log2(np.maximum(mult, 1).astype(np.float64)), NEG))
    return jnp.asarray(np.stack(tabs), F32)


def _dilated_kernel(qt_ref, k_ref, vt_ref, tab_ref, o_ref, *scratch):
    bufs, scr = scratch[:N_SCORE_BUFS], scratch[N_SCORE_BUFS:]
    i = pl.program_id(1)
    refs = (scr[0:3], scr[3:6])
    q_t = qt_ref[...]
    zero = jnp.zeros((HEAD_DIM, TQ), BF16)
    qs = (_stack_rows(q_t[:HEAD_DIM], zero), _stack_rows(zero, q_t[HEAD_DIM:]))
    n_off = tab_ref.shape[0]
    last = KPQ * i + (KPQ - 1)
    for h in range(2):
        _init_state(refs[h])

    def scores(n):
        kb = _k_block(k_ref, n)
        bias = tab_ref[last - n]
        return [_qk_t(kb, qs[h]) + bias for h in range(2)]

    def consume(n, buf, mask, cols):
        vb = vt_ref[n]
        _softmax_pv(buf, [vb[:HEAD_DIM, :], vb[HEAD_DIM:, :]], refs, mask, cols)

    first = jnp.maximum(last - (n_off - 1), 0)
    _sweep(first, last - 1 - first, scores, consume, bufs, (None, None), fuse_first=False)
    for h in range(2):
        _, l, acc = _load_state(refs[h])
        o_ref[h * HEAD_DIM:(h + 1) * HEAD_DIM, :] = (acc / l).astype(o_ref.dtype)


def _dilated_attention(qt, k, vt, tab, qrow, kcol, vrow):
    s = k.shape[0]
    n_pairs = 4
    return pl.pallas_call(
        _dilated_kernel,
        grid=(n_pairs, s // TQ),
        in_specs=[pl.BlockSpec((LANES, TQ), lambda hp, i: (qrow + hp, i)),
                  pl.BlockSpec((s, LANES), lambda hp, i: (0, kcol + hp)),
                  pl.BlockSpec((s // TK, LANES, TK), lambda hp, i: (0, vrow + hp, 0)),
                  pl.BlockSpec(tab.shape, lambda hp, i: (0, 0, 0), pipeline_mode=pl.Buffered(1))],
        out_specs=pl.BlockSpec((LANES, TQ), lambda hp, i: (hp, i)),
        out_shape=jax.ShapeDtypeStruct((n_pairs * LANES, s), BF16),
        scratch_shapes=_score_scratch(2) + _flash_scratch(2, HEAD_DIM),
        compiler_params=_cparams(2),
    )(qt, k, vt, tab)


def _moba_gate_kernel(qt_ref, km_ref, bias_ref):
    i = pl.program_id(0)
    tg = qt_ref.shape[1]
    n_heads, nblk, _ = km_ref.shape
    blk = lax.broadcasted_iota(I32, (nblk, tg), 0)
    blk_f = blk.astype(F32)
    tok = i * tg + lax.broadcasted_iota(I32, (nblk, tg), 1)
    own = jnp.right_shift(tok, int(math.log2(MOBA_BLOCK)))
    past = blk < own
    for h in range(n_heads):
        rows = slice(h * HEAD_DIM, (h + 1) * HEAD_DIM)
        gate = _dot3(km_ref[h], qt_ref[rows, :])
        sel = blk == own
        for _ in range(MOBA_TOPK):
            cand = past & jnp.logical_not(sel)
            g = jnp.where(cand, gate, -jnp.inf)
            mx = jnp.max(g, axis=0, keepdims=True)
            first = jnp.min(jnp.where(cand & (g == mx), blk_f, 2.0 * nblk), axis=0, keepdims=True)
            sel = sel | (blk_f == first)
        bias_ref[rows, :] = jnp.where(sel, 0.0, NEG).astype(BF16)


def _moba_gate(qbt, kmean):
    width, s = qbt.shape
    tg = 512
    return pl.pallas_call(
        _moba_gate_kernel,
        grid=(s // tg,),
        in_specs=[pl.BlockSpec((width, tg), lambda i: (0, i)),
                  pl.BlockSpec(kmean.shape, lambda i: (0, 0, 0))],
        out_specs=pl.BlockSpec((width, tg), lambda i: (0, i)),
        out_shape=jax.ShapeDtypeStruct((width, s), BF16),
        compiler_params=_cparams(1),
    )(qbt, kmean)


def _moba_kernel(qt_ref, bias_ref, k_ref, vt_ref, o_ref, *scratch):
    i = pl.program_id(1)
    bufs, scr = scratch[:N_SCORE_BUFS], scratch[N_SCORE_BUFS:]
    refs = (scr[0:3], scr[3:6])
    q_t = qt_ref[...]
    b_t = bias_ref[...]
    qs = (_stack_rows(q_t[:HEAD_DIM], b_t[:HEAD_DIM]), _stack_rows(b_t[HEAD_DIM:], q_t[HEAD_DIM:]))
    lane = lax.broadcasted_iota(I32, (1, LANES), 1)
    lo = lane < HEAD_DIM
    mine = (lo, jnp.logical_not(lo))
    blk = lane & (HEAD_DIM - 1)
    sub_block = jnp.right_shift(lax.broadcasted_iota(I32, (TK, LANES), 0), int(math.log2(MOBA_BLOCK)))
    for h in range(2):
        _init_state(refs[h])

    def scores(n):
        kb = _k_block(k_ref, n)
        onehot = jnp.where(blk == n * (TK // MOBA_BLOCK) + sub_block, 1.0, 0.0).astype(BF16)
        return [_qk_t(jnp.where(mine[h], kb, onehot), qs[h]) for h in range(2)]

    def consume(n, buf, mask, cols):
        vb = vt_ref[n]
        _softmax_pv(buf, [vb[:HEAD_DIM, :], vb[HEAD_DIM:, :]], refs, mask, cols)

    _sweep(0, KPQ * i, scores, consume, bufs, (_causal_t(0), _causal_t(1)), fuse_first=True)
    for h in range(2):
        _, l, acc = _load_state(refs[h])
        o_ref[h * HEAD_DIM:(h + 1) * HEAD_DIM, :] = (acc / l).astype(o_ref.dtype)


def _moba_attention(qt, bias, k, vt, qrow, kcol, vrow):
    s = k.shape[0]
    n_pairs = 4
    return pl.pallas_call(
        _moba_kernel,
        grid=(n_pairs, s // TQ),
        in_specs=[pl.BlockSpec((LANES, TQ), lambda hp, i: (qrow + hp, i)),
                  pl.BlockSpec((LANES, TQ), lambda hp, i: (hp, i)),
                  pl.BlockSpec((s, LANES), lambda hp, i: (0, kcol + hp)),
                  pl.BlockSpec((s // TK, LANES, TK), lambda hp, i: (0, vrow + hp, 0))],
        out_specs=pl.BlockSpec((LANES, TQ), lambda hp, i: (hp, i)),
        out_shape=jax.ShapeDtypeStruct((n_pairs * LANES, s), BF16),
        scratch_shapes=_score_scratch(2) + _flash_scratch(2, HEAD_DIM),
        compiler_params=_cparams(2),
    )(qt, bias, k, vt)


def _diff_kernel(lam_init, qt_ref, k_ref, vt_ref, lq1_ref, lk1_ref, lq2_ref, lk2_ref, g_ref, o_ref,
                 *scratch):
    i = pl.program_id(1)
    bufs, scr = scratch[:N_SCORE_BUFS], scratch[N_SCORE_BUFS:]
    refs = (scr[0:3], scr[3:6])
    q_t = qt_ref[...]
    zero = jnp.zeros((HEAD_DIM, TQ), BF16)
    qs = (_stack_rows(q_t[:HEAD_DIM], zero), _stack_rows(zero, q_t[HEAD_DIM:]))
    for h in range(2):
        _init_state(refs[h])

    def scores(n):
        kb = _k_block(k_ref, n)
        return [_qk_t(kb, qs[h]) for h in range(2)]

    def consume(n, buf, mask, cols):
        vb = vt_ref[n]
        _softmax_pv(buf, [vb, vb], refs, mask, cols)

    _sweep(0, KPQ * i, scores, consume, bufs, (_causal_t(0), _causal_t(1)), fuse_first=True)
    lam = (jnp.exp(jnp.sum(lq1_ref[...] * lk1_ref[...], axis=1, keepdims=True))
           - jnp.exp(jnp.sum(lq2_ref[...] * lk2_ref[...], axis=1, keepdims=True)) + lam_init)
    _, l1, acc1 = _load_state(refs[0])
    _, l2, acc2 = _load_state(refs[1])
    o = acc1 / l1 - lam * (acc2 / l2)
    var = jnp.mean(o * o, axis=0, keepdims=True)
    o = (o * lax.rsqrt(var + RMS_EPS) * g_ref[...]) * (1.0 - lam_init)
    o_ref[...] = o.astype(o_ref.dtype)


def _diff_attention(qt, k, vt, lq1, lk1, lq2, lk2, g_sub, lam_init, qrow, kcol, vrow):
    s = k.shape[0]
    n_heads = 8
    vec = lambda n: pl.BlockSpec((1, n), lambda h, i: (0, 0))
    return pl.pallas_call(
        functools.partial(_diff_kernel, lam_init),
        grid=(n_heads, s // TQ),
        in_specs=[pl.BlockSpec((LANES, TQ), lambda h, i: (qrow + h, i)),
                  pl.BlockSpec((s, LANES), lambda h, i: (0, kcol + h)),
                  pl.BlockSpec((s // TK, LANES, TK), lambda h, i: (0, vrow + h, 0)),
                  vec(HEAD_DIM), vec(HEAD_DIM), vec(HEAD_DIM), vec(HEAD_DIM),
                  pl.BlockSpec((LANES, 1), lambda h, i: (0, 0))],
        out_specs=pl.BlockSpec((LANES, TQ), lambda h, i: (h, i)),
        out_shape=jax.ShapeDtypeStruct((n_heads * LANES, s), BF16),
        scratch_shapes=_score_scratch(2) + _flash_scratch(2, LANES),
        compiler_params=_cparams(2),
    )(qt, k, vt, lq1, lk1, lq2, lk2, g_sub)


def _outproj_kernel(oa_ref, ob_ref, oc_ref, wt_ref, x_ref, gt_ref, o_ref):
    wa = oa_ref.shape[0]
    wb = ob_ref.shape[0]
    y_t = (jnp.dot(wt_ref[:, 0:wa], oa_ref[...], preferred_element_type=F32)
           + jnp.dot(wt_ref[:, wa:wa + wb], ob_ref[...], preferred_element_type=F32)
           + jnp.dot(wt_ref[:, wa + wb:], oc_ref[...], preferred_element_type=F32))
    o_ref[...] = x_ref[...] + gt_ref[...] * y_t.T


def _out_projection(oa_t, ob_t, oc_t, wt_bf16, x, gt):
    s, d = x.shape
    tm = 256
    return pl.pallas_call(
        _outproj_kernel,
        grid=(s // tm,),
        in_specs=[pl.BlockSpec((oa_t.shape[0], tm), lambda i: (0, i)),
                  pl.BlockSpec((ob_t.shape[0], tm), lambda i: (0, i)),
                  pl.BlockSpec((oc_t.shape[0], tm), lambda i: (0, i)),
                  pl.BlockSpec(wt_bf16.shape, lambda i: (0, 0)),
                  pl.BlockSpec((tm, d), lambda i: (i, 0)),
                  pl.BlockSpec((1, d), lambda i: (0, 0))],
        out_specs=pl.BlockSpec((tm, d), lambda i: (i, 0)),
        out_shape=jax.ShapeDtypeStruct((s, d), F32),
        compiler_params=_cparams(1),
    )(oa_t, ob_t, oc_t, wt_bf16, x, gt)


def _route_kernel(x_ref, g_ref, sc_ref, sh_ref, wr_ref, br_ref, tri_ref,
                  h_ref, route_ref, count_ref, base_scr):
    i = pl.program_id(0)
    tm = x_ref.shape[0]

    @pl.when(i == 0)
    def _():
        base_scr[...] = jnp.zeros_like(base_scr)

    h = _modulated_norm(x_ref[...], g_ref[...], sc_ref[...], sh_ref[...])
    h_ref[...] = h.astype(BF16)
    logits = _dot3(wr_ref[...], h, NT_DIMS)
    s = _sigmoid(logits)
    sg = s + br_ref[...]
    rows = [sg[r:r + 1, :] for r in range(N_EXPERTS)]

    def top2_sum(vals):
        best = None
        for a in range(len(vals)):
            for b in range(a + 1, len(vals)):
                pair = vals[a] + vals[b]
                best = pair if best is None else jnp.maximum(best, pair)
        return best

    score = [top2_sum(rows[g * EXPERTS_PER_GROUP:(g + 1) * EXPERTS_PER_GROUP]) for g in range(N_GROUPS)]
    best, grp = score[0], jnp.zeros((1, tm), I32)
    for g in range(1, N_GROUPS):
        better = score[g] > best
        grp = jnp.where(better, g, grp)
        best = jnp.where(better, score[g], best)
    cand = []
    for j in range(EXPERTS_PER_GROUP):
        c = rows[j]
        for g in range(1, N_GROUPS):
            c = jnp.where(grp == g, rows[g * EXPERTS_PER_GROUP + j], c)
        cand.append(c)
    b0, l0 = cand[0], jnp.zeros((1, tm), I32)
    for j in range(1, EXPERTS_PER_GROUP):
        better = cand[j] > b0
        l0 = jnp.where(better, j, l0)
        b0 = jnp.where(better, cand[j], b0)
    b1, l1 = jnp.full((1, tm), -jnp.inf, F32), jnp.zeros((1, tm), I32)
    for j in range(EXPERTS_PER_GROUP):
        better = (l0 != j) & (cand[j] > b1)
        l1 = jnp.where(better, j, l1)
        b1 = jnp.where(better, cand[j], b1)
    e0 = grp * EXPERTS_PER_GROUP + l0
    e1 = grp * EXPERTS_PER_GROUP + l1
    erow = lax.broadcasted_iota(I32, (N_EXPERTS, tm), 0)
    sel0 = erow == e0
    sel1 = erow == e1
    s0 = jnp.sum(jnp.where(sel0, s, 0.0), axis=0, keepdims=True)
    s1 = jnp.sum(jnp.where(sel1, s, 0.0), axis=0, keepdims=True)
    tot = s0 + s1
    chosen = jnp.where(sel0 | sel1, 1.0, 0.0).astype(BF16)
    csum = jnp.dot(chosen, tri_ref[...], preferred_element_type=F32)
    pos = base_scr[...] + csum - 1.0
    r0 = jnp.sum(jnp.where(sel0, pos, 0.0), axis=0, keepdims=True)
    r1 = jnp.sum(jnp.where(sel1, pos, 0.0), axis=0, keepdims=True)
    base = base_scr[...] + csum[:, tm - 1:tm]
    base_scr[...] = base
    count_ref[...] = jnp.broadcast_to(base, count_ref.shape)
    zero = jnp.zeros((1, tm), F32)
    for r, val in enumerate((e0.astype(F32), e1.astype(F32), r0, r1, s0 / tot, s1 / tot, zero, zero)):
        route_ref[r:r + 1, :] = val


def _norm_and_route(x, g, sc, sh, w_router, b_router):
    s, d = x.shape
    tm = 512
    tri = jnp.asarray(np.triu(np.ones((tm, tm), np.float32)), BF16)
    row = lambda i: (0, 0)
    return pl.pallas_call(
        _route_kernel,
        grid=(s // tm,),
        in_specs=[pl.BlockSpec((tm, d), lambda i: (i, 0)),
                  pl.BlockSpec((1, d), row), pl.BlockSpec((1, d), row), pl.BlockSpec((1, d), row),
                  pl.BlockSpec((N_EXPERTS, d), row), pl.BlockSpec((N_EXPERTS, 1), row),
                  pl.BlockSpec((tm, tm), row)],
        out_specs=[pl.BlockSpec((tm, d), lambda i: (i, 0)),
                   pl.BlockSpec((8, tm), lambda i: (0, i)),
                   pl.BlockSpec((N_EXPERTS, LANES), row)],
        out_shape=[jax.ShapeDtypeStruct((s, d), BF16),
                   jax.ShapeDtypeStruct((8, s), F32),
                   jax.ShapeDtypeStruct((N_EXPERTS, LANES), F32)],
        scratch_shapes=[pltpu.VMEM((N_EXPERTS, 1), F32)],
        compiler_params=_cparams(1),
    )(x, g, sc, sh, w_router.T, b_router.reshape(N_EXPERTS, 1), tri)


FFN_ROWS = 512


def _ffn_kernel(tile_base, te_ref, nu_ref, xs_ref, wg_ref, wu_ref, wd_ref, *rest):
    y_ref = rest[-1]
    j = pl.program_id(0) + tile_base

    @pl.when(j < nu_ref[0])
    def _():
        x = xs_ref[...]
        de = wg_ref.shape[3]
        y = None
        for c in range(de // FFN_CHUNK):
            cs = slice(c * FFN_CHUNK, (c + 1) * FFN_CHUNK)
            a = jnp.dot(x, wg_ref[0, 0, :, cs].astype(BF16), preferred_element_type=F32)
            b = jnp.dot(x, wu_ref[0, 0, :, cs].astype(BF16), preferred_element_type=F32)
            act = (a * _sigmoid(a) * b).astype(BF16)
            part = jnp.dot(act, wd_ref[0, 0, cs, :].astype(BF16), preferred_element_type=F32)
            y = part if y is None else y + part
        y_ref[...] = y.astype(y_ref.dtype)

    @pl.when(j >= nu_ref[0])
    def _():
        y_ref[...] = jnp.zeros_like(y_ref)


FFN_CHUNK = 512
MOE_PARTS = 2


def _expert_ffn(tile_expert, n_used, xs_part, w_gate, w_up, w_down, layer, tile_base, n_tiles, y_prev):
    r, d = xs_part.shape
    de = w_gate.shape[3]
    expert = lambda j, te, nu: (layer, te[j + tile_base], 0, 0)
    in_specs = [pl.BlockSpec((FFN_ROWS, d), lambda j, te, nu: (j, 0)),
                pl.BlockSpec((1, 1, d, de), expert, pipeline_mode=pl.Buffered(1)),
                pl.BlockSpec((1, 1, d, de), expert, pipeline_mode=pl.Buffered(1)),
                pl.BlockSpec((1, 1, de, d), expert, pipeline_mode=pl.Buffered(1))]
    args = [tile_expert, n_used, xs_part, w_gate, w_up, w_down]
    aliases = {}
    if y_prev is not None:
        in_specs.append(pl.BlockSpec(memory_space=pl.ANY))
        aliases = {len(args): 0}
        args.append(y_prev)
    grid_spec = pltpu.PrefetchScalarGridSpec(
        num_scalar_prefetch=2,
        grid=(r // FFN_ROWS,),
        in_specs=in_specs,
        out_specs=pl.BlockSpec((FFN_ROWS, d), lambda j, te, nu: (j + tile_base, 0)),
    )
    return pl.pallas_call(
        functools.partial(_ffn_kernel, tile_base),
        grid_spec=grid_spec,
        out_shape=jax.ShapeDtypeStruct((n_tiles * FFN_ROWS, d), BF16),
        input_output_aliases=aliases,
        compiler_params=_cparams(1),
    )(*args)


def _combine_kernel(final, x_ref, y0_ref, y1_ref, w0_ref, w1_ref, gt_ref, gf_ref, *rest):
    o_ref = rest[-1]
    x = x_ref[...] + gt_ref[...] * (w0_ref[...] * y0_ref[...].astype(F32)
                                    + w1_ref[...] * y1_ref[...].astype(F32))
    if final:
        var = jnp.mean(x * x, axis=-1, keepdims=True)
        x = x * lax.rsqrt(var + RMS_EPS) * gf_ref[...]
    o_ref[...] = x


def _combine(x, y01_part, w0, w1, gt, g_final, final, row_base, out_prev):
    s, d = x.shape
    tm = 256
    part = y01_part.shape[0] // 2
    base = row_base // tm
    big = pl.BlockSpec((tm, d), lambda i: (i + base, 0))
    first = pl.BlockSpec((tm, d), lambda i: (i, 0))
    second = pl.BlockSpec((tm, d), lambda i: (i + part // tm, 0))
    col = pl.BlockSpec((tm, 1), lambda i: (i + base, 0))
    vec = pl.BlockSpec((1, d), lambda i: (0, 0))
    in_specs = [big, first, second, col, col, vec, vec]
    args = [x, y01_part, y01_part, w0, w1, gt, g_final]
    aliases = {}
    if out_prev is not None:
        in_specs.append(pl.BlockSpec(memory_space=pl.ANY))
        aliases = {len(args): 0}
        args.append(out_prev)
    return pl.pallas_call(
        functools.partial(_combine_kernel, final),
        grid=(part // tm,),
        in_specs=in_specs,
        out_specs=big,
        out_shape=jax.ShapeDtypeStruct((s, d), F32),
        input_output_aliases=aliases,
        compiler_params=_cparams(1),
    )(*args)


def _rope_tables(positions):
    half = HEAD_DIM // 2
    inv = ROPE_THETA ** (-jnp.arange(0, HEAD_DIM, 2, dtype=F32) / HEAD_DIM)
    ang = positions.astype(F32)[:, None] * inv
    cos, sin = jnp.cos(ang), jnp.sin(ang)
    cos_t = jnp.tile(cos, (1, LANES // half))
    sin_t = jnp.tile(jnp.concatenate([-sin, sin], axis=1), (1, LANES // HEAD_DIM))
    return cos_t, sin_t


def _block_means(ksum):
    nblk, width = ksum.shape
    km = (ksum / MOBA_BLOCK).reshape(nblk, width // HEAD_DIM, HEAD_DIM)
    km = jnp.pad(km, ((0, HEAD_DIM - nblk), (0, 0), (0, 0)))
    return jnp.transpose(km, (1, 0, 2))


def _moe(x1, h2, route, counts, w_gate, w_up, w_down, layer, gt, g_final, final):
    s, d = x1.shape
    e0, e1 = route[0].astype(I32), route[1].astype(I32)
    r0, r1 = route[2].astype(I32), route[3].astype(I32)
    cnt = counts[:, 0].astype(I32)
    cnt_pad = ((cnt + FFN_ROWS - 1) // FFN_ROWS) * FFN_ROWS
    off = jnp.concatenate([jnp.zeros((1,), I32), jnp.cumsum(cnt_pad).astype(I32)])
    d0 = off[e0] + r0
    d1 = off[e1] + r1
    n_rows = 2 * s + N_EXPERTS * FFN_ROWS
    tok = jnp.arange(s, dtype=I32)
    src = (jnp.arange(n_rows, dtype=I32) % s).at[jnp.concatenate([d0, d1])].set(jnp.concatenate([tok, tok]))
    n_tiles = n_rows // FFN_ROWS
    tile_start = jnp.arange(n_tiles, dtype=I32) * FFN_ROWS
    tile_expert = jnp.sum((off[None, 1:] <= tile_start[:, None]).astype(I32), axis=1)
    tile_expert = jnp.minimum(tile_expert, N_EXPERTS - 1).astype(I32)
    n_used = (off[N_EXPERTS] // FFN_ROWS).reshape(1).astype(I32)
    y = None
    for part in range(MOE_PARTS):
        tiles = n_tiles // MOE_PARTS
        rows = slice(part * tiles * FFN_ROWS, (part + 1) * tiles * FFN_ROWS)
        y = _expert_ffn(tile_expert, n_used, h2[src[rows]], w_gate, w_up, w_down, layer,
                        part * tiles, n_tiles, y)
    w0, w1 = route[4].reshape(s, 1), route[5].reshape(s, 1)
    out = None
    for part in range(MOE_PARTS):
        toks = slice(part * s // MOE_PARTS, (part + 1) * s // MOE_PARTS)
        y01 = y[jnp.concatenate([d0[toks], d1[toks]])]
        out = _combine(x1, y01, w0, w1, gt, g_final, final, part * s // MOE_PARTS, out)
    return out


def kernel(x, c, positions, w_ada, b_ada, g_attn, g_mlp, w_in, w_out, lam_q1, lam_k1, lam_q2, lam_k2,
           g_subln, w_gate, w_up, w_down, w_router, b_router, g_final):
    batch, s, d = x.shape
    assert batch == 1 and s % TQ == 0 and s // MOBA_BLOCK <= HEAD_DIM
    assert w_in.shape[2] == 3 * N_KIND_TILES * PROJ_TN
    depth = w_ada.shape[0]
    xs = x.reshape(s, d)
    cos_t, sin_t = _rope_tables(positions[0])
    mod = _modulation(c, w_ada, b_ada)
    tab = _dilated_bias_table()
    gf = g_final.reshape(1, d)
    row = lambda v: v.reshape(1, -1)
    for l in range(depth):
        lam_init = 0.8 - 0.6 * math.exp(-0.3 * l)
        sh_a, sc_a, gt_a, sh_m, sc_m, gt_m = [mod[l, :, k * d:(k + 1) * d] for k in range(6)]
        k, qt, qbt, vt, ksum = _in_projection(xs, row(g_attn[l]), sc_a, sh_a,
                                              _layer_to_bf16(w_in, l, PROJ_TN), cos_t, sin_t)
        o_a = _dilated_attention(qt, k, vt, tab, 0, 0, 0)
        bias = _moba_gate(qbt, _block_means(ksum.reshape(s // MOBA_BLOCK, -1)))
        o_b = _moba_attention(qt, bias, k, vt, 4, 4, 4)
        o_c = _diff_attention(qt, k, vt, row(lam_q1[l]), row(lam_k1[l]), row(lam_q2[l]), row(lam_k2[l]),
                              g_subln[l].reshape(-1, 1), lam_init, 8, 8, 8)
        x1 = _out_projection(o_a, o_b, o_c, w_out[l].T.astype(BF16), xs, gt_a)
        h2, route, counts = _norm_and_route(x1, row(g_mlp[l]), sc_m, sh_m, w_router, b_router)
        xs = _moe(x1, h2, route, counts, w_gate, w_up, w_down, l, gt_m, gf, l == depth - 1)
    return xs.reshape(batch, s, d)
```

```python
import functools
import math

import numpy as np
import jax
import jax.numpy as jnp
from jax import lax
from jax.experimental import pallas as pl
from jax.experimental.pallas import tpu as pltpu

F32 = jnp.float32
BF16 = jnp.bfloat16
I32 = jnp.int32

HEAD_DIM = 64
LANES = 128
A_PATTERNS = ((128, 1), (512, 4), (2048, 16))
MOBA_BLOCK = 256
MOBA_TOPK = 3
N_EXPERTS = 16
N_GROUPS = 4
EXPERTS_PER_GROUP = N_EXPERTS // N_GROUPS
ROPE_THETA = 10000.0
RMS_EPS = 1e-6
NEG = -1e30
VMEM_LIMIT = 56 * 1024 * 1024

TQ = 1024
TK = 512
KPQ = TQ // TK
assert KPQ == 2
ALL_QUERIES = slice(None)
LATE_QUERIES = slice(TQ // 2, TQ)
Q_CHUNK = 256
PROJ_TM = 512
PROJ_TN = 512
NT_DIMS = (((1,), (1,)), ((), ()))
Q_SCALE = HEAD_DIM ** -0.5 * math.log2(math.e)


def _cparams(n_axes):
    return pltpu.CompilerParams(dimension_semantics=("arbitrary",) * n_axes,
                                vmem_limit_bytes=VMEM_LIMIT)


def _split_bf16(x):
    hi = x.astype(BF16)
    lo = (x - hi.astype(F32)).astype(BF16)
    return hi, lo


def _dot3(a, b, dims=(((1,), (0,)), ((), ()))):
    a_hi, a_lo = _split_bf16(a)
    b_hi, b_lo = _split_bf16(b)
    dg = functools.partial(lax.dot_general, dimension_numbers=dims, preferred_element_type=F32)
    return dg(a_hi, b_hi) + dg(a_hi, b_lo) + dg(a_lo, b_hi)


def _sigmoid(x):
    return 1.0 / (1.0 + jnp.exp(-x))


CAST_STREAMS = 4
CAST_BLOCK_ELEMS = 512 * 1024
CAST_COLS = 1024


def _cast_kernel(*refs):
    o_ref = refs[-1]
    col_tile = o_ref.shape[2]
    for k, x_ref in enumerate(refs[:-1]):
        rows = x_ref.shape[1]
        x = x_ref[0].astype(o_ref.dtype)
        for t in range(o_ref.shape[0]):
            o_ref[t, k * rows:(k + 1) * rows, :] = x[:, t * col_tile:(t + 1) * col_tile]


def _layer_to_bf16(w, l, col_tile=None):
    cols = w.shape[-1]
    col_tile = col_tile or cols
    w3 = w.reshape(w.shape[0], -1, cols)
    rows = w3.shape[1]
    tc = max(min(cols, CAST_COLS), col_tile)
    sr = CAST_BLOCK_ELEMS // tc
    tr = CAST_STREAMS * sr
    stream = lambda k: pl.BlockSpec((1, sr, tc), lambda i, j: (l, CAST_STREAMS * i + k, j))
    return pl.pallas_call(
        _cast_kernel,
        grid=(rows // tr, cols // tc),
        in_specs=[stream(k) for k in range(CAST_STREAMS)],
        out_specs=pl.BlockSpec((tc // col_tile, tr, col_tile), lambda i, j: (j, i, 0)),
        out_shape=jax.ShapeDtypeStruct((cols // col_tile, rows, col_tile), BF16),
        compiler_params=_cparams(2),
    )(*([w3] * CAST_STREAMS))


def _mod_kernel(c_ref, w_ref, b_ref, o_ref):
    c = c_ref[...]
    sc = c * _sigmoid(c)
    d = c.shape[0]
    acc = b_ref[0]
    for r in range(0, d, 256):
        acc = acc + jnp.sum(w_ref[0, r:r + 256, :] * sc[r:r + 256, :], axis=0, keepdims=True)
    o_ref[0] = acc


def _modulation(c, w_ada, b_ada):
    depth, d, n = w_ada.shape
    tn = 512
    return pl.pallas_call(
        _mod_kernel,
        grid=(depth, n // tn),
        in_specs=[pl.BlockSpec((d, 1), lambda l, j: (0, 0)),
                  pl.BlockSpec((1, d, tn), lambda l, j: (l, 0, j)),
                  pl.BlockSpec((1, 1, tn), lambda l, j: (l, 0, j))],
        out_specs=pl.BlockSpec((1, 1, tn), lambda l, j: (l, 0, j)),
        out_shape=jax.ShapeDtypeStruct((depth, 1, n), F32),
        compiler_params=_cparams(2),
    )(c.reshape(d, 1), w_ada, b_ada.reshape(depth, 1, n))


def _modulated_norm(x, g, sc, sh):
    var = jnp.mean(x * x, axis=-1, keepdims=True)
    return (x * lax.rsqrt(var + RMS_EPS) * g) * (1.0 + sc) + sh


def _rope_tile(acc, cos, sin):
    lane = lax.broadcasted_iota(I32, (1, LANES), 1)
    first = (lane & (HEAD_DIM - 1)) < (HEAD_DIM // 2)
    outs = []
    for c in range(acc.shape[1] // LANES):
        xc = acc[:, c * LANES:(c + 1) * LANES]
        rot = jnp.where(first, pltpu.roll(xc, LANES - HEAD_DIM // 2, 1), pltpu.roll(xc, HEAD_DIM // 2, 1))
        outs.append(xc * cos + rot * sin)
    return jnp.concatenate(outs, axis=1)


N_KIND_TILES = 4
COLUMN_TILE_ORDER = (1, 4, 8, 9, 0, 3, 6, 7, 2, 5, 10, 11)
KB_TILE = 1
QB_TILE = N_KIND_TILES + 1


def _inproj_kernel(x_ref, g_ref, sc_ref, sh_ref, w_ref, cos_ref, sin_ref,
                   k_ref, qt_ref, qbt_ref, vt_ref, ksum_ref, h_scr):
    tm = x_ref.shape[0]
    tn = w_ref.shape[2]
    h_scr[...] = _modulated_norm(x_ref[...], g_ref[...], sc_ref[...], sh_ref[...]).astype(BF16)
    for pos, tile in enumerate(COLUMN_TILE_ORDER):
        acc = jnp.dot(h_scr[...], w_ref[tile], preferred_element_type=F32)
        kind, t = divmod(pos, N_KIND_TILES)
        cols = slice(t * tn, (t + 1) * tn)
        if kind == 0:
            r = _rope_tile(acc, cos_ref[...], sin_ref[...])
            k_ref[:, cols] = r.astype(BF16)
            if pos == KB_TILE:
                ksum_ref[0] = jnp.sum(r.reshape(tm // MOBA_BLOCK, MOBA_BLOCK, tn), axis=1)
        elif kind == 1:
            rt = _rope_tile(acc, cos_ref[...], sin_ref[...]).T
            qt_ref[cols, :] = (rt * Q_SCALE).astype(BF16)
            if pos == QB_TILE:
                qbt_ref[...] = rt
        else:
            vt = acc.T.astype(BF16)
            for b in range(vt_ref.shape[0]):
                vt_ref[b, cols, :] = vt[:, b * TK:(b + 1) * TK]


def _in_projection(x, g, sc, sh, w_bf16, cos_t, sin_t):
    s, d = x.shape
    tm, tn = PROJ_TM, w_bf16.shape[2]
    nb = tm // MOBA_BLOCK
    width = N_KIND_TILES * tn
    row = lambda i: (0, 0)
    return pl.pallas_call(
        _inproj_kernel,
        grid=(s // tm,),
        in_specs=[pl.BlockSpec((tm, d), lambda i: (i, 0)),
                  pl.BlockSpec((1, d), row), pl.BlockSpec((1, d), row), pl.BlockSpec((1, d), row),
                  pl.BlockSpec(w_bf16.shape, lambda i: (0, 0, 0),
                               pipeline_mode=pl.Buffered(1)),
                  pl.BlockSpec((tm, LANES), lambda i: (i, 0)),
                  pl.BlockSpec((tm, LANES), lambda i: (i, 0))],
        out_specs=[pl.BlockSpec((tm, width), lambda i: (i, 0)),
                   pl.BlockSpec((width, tm), lambda i: (0, i)),
                   pl.BlockSpec((tn, tm), lambda i: (0, i)),
                   pl.BlockSpec((tm // TK, width, TK), lambda i: (i, 0, 0)),
                   pl.BlockSpec((1, nb, tn), lambda i: (i, 0, 0))],
        out_shape=[jax.ShapeDtypeStruct((s, width), BF16),
                   jax.ShapeDtypeStruct((width, s), BF16),
                   jax.ShapeDtypeStruct((tn, s), F32),
                   jax.ShapeDtypeStruct((s // TK, width, TK), BF16),
                   jax.ShapeDtypeStruct((s // tm, nb, tn), F32)],
        scratch_shapes=[pltpu.VMEM((tm, d), BF16)],
        compiler_params=_cparams(1),
    )(x, g, sc, sh, w_bf16, cos_t, sin_t)


def _flash_step(s_t, v_t, state):
    m_prev, l_prev, acc_prev = state
    m_new = jnp.maximum(m_prev, jnp.max(s_t, axis=0, keepdims=True))
    alpha = jnp.exp2(m_prev - m_new)
    p = jnp.exp2(s_t - m_new)
    l_new = alpha * l_prev + jnp.sum(p, axis=0, keepdims=True)
    acc_new = alpha * acc_prev + jnp.dot(v_t, p.astype(BF16), preferred_element_type=F32)
    return m_new, l_new, acc_new


def _sweep(first, n_loop, scores, consume, bufs, tail_masks, fuse_first):
    halves = (bufs[0:2], bufs[2:4])

    def park_pair(a, dst):
        for d in range(2):
            for h, val in enumerate(scores(a + d)):
                dst[d][h] = val

    def pair(a, cur, nxt):
        park_pair(a + 2, nxt)
        consume(a, cur[0], None, ALL_QUERIES)
        consume(a + 1, cur[1], None, ALL_QUERIES)

    def tail(cur):
        t0 = first + n_loop
        consume(t0, cur[0], tail_masks[0], ALL_QUERIES)
        consume(t0 + 1, cur[1], tail_masks[1], LATE_QUERIES)

    def remaining(start, n_pairs, cur, nxt):
        def two_pairs(j, carry):
            pair(start + 4 * j, cur, nxt)
            pair(start + 4 * j + 2, nxt, cur)
            return carry

        lax.fori_loop(0, n_pairs // 2, two_pairs, 0)

        @pl.when(n_pairs % 2 == 1)
        def _():
            pair(start + 2 * n_pairs - 2, cur, nxt)
            tail(nxt)

        @pl.when(n_pairs % 2 == 0)
        def _():
            tail(cur)

    n_pairs = n_loop // 2
    if not fuse_first:
        park_pair(first, halves[0])
        remaining(first, n_pairs, halves[0], halves[1])
        return

    @pl.when(n_pairs == 0)
    def _():
        park_pair(first, halves[0])
        tail(halves[0])

    @pl.when(n_pairs > 0)
    def _():
        park_pair(first, halves[0])
        pair(first, halves[0], halves[1])
        remaining(first + 2, n_pairs - 1, halves[1], halves[0])


def _softmax_pv(buf, v_ts, refs, mask, cols):
    start, stop, _ = cols.indices(TQ)
    for c0 in range(start, stop, Q_CHUNK):
        cs = slice(c0, c0 + Q_CHUNK)
        for h, v_t in enumerate(v_ts):
            s_t = buf[h, :, cs]
            if mask is not None:
                s_t = jnp.where(mask(cs), s_t, NEG)
            _store_state(refs[h], _flash_step(s_t, v_t, _load_state(refs[h], cs)), cs)


N_SCORE_BUFS = 4


def _score_scratch(n_streams):
    return [pltpu.VMEM((n_streams, TK, TQ), F32) for _ in range(N_SCORE_BUFS)]


def _qk_t(k, q_t):
    return jnp.dot(k, q_t, preferred_element_type=F32)


def _k_block(k_ref, n):
    return k_ref[pl.ds(pl.multiple_of(n * TK, TK), TK), :]


def _init_state(refs):
    m_ref, l_ref, acc_ref = refs
    m_ref[...] = jnp.full(m_ref.shape, NEG, F32)
    l_ref[...] = jnp.zeros(l_ref.shape, F32)
    acc_ref[...] = jnp.zeros(acc_ref.shape, F32)


def _load_state(refs, cols=ALL_QUERIES):
    return tuple(r[:, cols] for r in refs)


def _store_state(refs, state, cols=ALL_QUERIES):
    for r, val in zip(refs, state):
        r[:, cols] = val


def _flash_scratch(n_streams, dv):
    out = []
    for _ in range(n_streams):
        out += [pltpu.VMEM((1, TQ), F32), pltpu.VMEM((1, TQ), F32), pltpu.VMEM((dv, TQ), F32)]
    return out


def _causal_t(d):
    def chunk_mask(cs):
        r = lax.broadcasted_iota(I32, (TK, cs.stop - cs.start), 0) + d * TK
        return r <= lax.broadcasted_iota(I32, (TK, cs.stop - cs.start), 1) + cs.start
    return chunk_mask


def _stack_rows(top, bottom):
    return jnp.concatenate([top, bottom], axis=0)


def _dilated_bias_table():
    n_off = A_PATTERNS[-1][0] // TK + KPQ
    r = np.arange(TK)[:, None]
    c = np.arange(TQ)[None, :]
    tabs = []
    for t in range(n_off):
        delta = (t - (KPQ - 1)) * TK + c - r
        mult = np.zeros_like(delta)
        for window, dil in A_PATTERNS:
            mult += ((delta >= 0) & (delta % dil == 0) & (delta <= window)).astype(delta.dtype)
        tabs.append(np.where(mult > 0, np.log2(np.maximum(mult, 1).astype(np.float64)), NEG))
    return jnp.asarray(np.stack(tabs), F32)


def _dilated_kernel(qt_ref, k_ref, vt_ref, tab_ref, o_ref, *scratch):
    bufs, scr = scratch[:N_SCORE_BUFS], scratch[N_SCORE_BUFS:]
    i = pl.program_id(1)
    refs = (scr[0:3], scr[3:6])
    q_t = qt_ref[...]
    zero = jnp.zeros((HEAD_DIM, TQ), BF16)
    qs = (_stack_rows(q_t[:HEAD_DIM], zero), _stack_rows(zero, q_t[HEAD_DIM:]))
    n_off = tab_ref.shape[0]
    last = KPQ * i + (KPQ - 1)
    for h in range(2):
        _init_state(refs[h])

    def scores(n):
        kb = _k_block(k_ref, n)
        bias = tab_ref[last - n]
        return [_qk_t(kb, qs[h]) + bias for h in range(2)]

    def consume(n, buf, mask, cols):
        vb = vt_ref[n]
        _softmax_pv(buf, [vb[:HEAD_DIM, :], vb[HEAD_DIM:, :]], refs, mask, cols)

    first = jnp.maximum(last - (n_off - 1), 0)
    _sweep(first, last - 1 - first, scores, consume, bufs, (None, None), fuse_first=False)
    for h in range(2):
        _, l, acc = _load_state(refs[h])
        o_ref[h * HEAD_DIM:(h + 1) * HEAD_DIM, :] = (acc / l).astype(o_ref.dtype)


def _dilated_attention(qt, k, vt, tab, qrow, kcol, vrow):
    s = k.shape[0]
    n_pairs = 4
    return pl.pallas_call(
        _dilated_kernel,
        grid=(n_pairs, s // TQ),
        in_specs=[pl.BlockSpec((LANES, TQ), lambda hp, i: (qrow + hp, i)),
                  pl.BlockSpec((s, LANES), lambda hp, i: (0, kcol + hp)),
                  pl.BlockSpec((s // TK, LANES, TK), lambda hp, i: (0, vrow + hp, 0)),
                  pl.BlockSpec(tab.shape, lambda hp, i: (0, 0, 0), pipeline_mode=pl.Buffered(1))],
        out_specs=pl.BlockSpec((LANES, TQ), lambda hp, i: (hp, i)),
        out_shape=jax.ShapeDtypeStruct((n_pairs * LANES, s), BF16),
        scratch_shapes=_score_scratch(2) + _flash_scratch(2, HEAD_DIM),
        compiler_params=_cparams(2),
    )(qt, k, vt, tab)


def _moba_gate_kernel(qt_ref, km_ref, bias_ref):
    i = pl.program_id(0)
    tg = qt_ref.shape[1]
    n_heads, nblk, _ = km_ref.shape
    blk = lax.broadcasted_iota(I32, (nblk, tg), 0)
    blk_f = blk.astype(F32)
    tok = i * tg + lax.broadcasted_iota(I32, (nblk, tg), 1)
    own = jnp.right_shift(tok, int(math.log2(MOBA_BLOCK)))
    past = blk < own
    for h in range(n_heads):
        rows = slice(h * HEAD_DIM, (h + 1) * HEAD_DIM)
        gate = _dot3(km_ref[h], qt_ref[rows, :])
        sel = blk == own
        for _ in range(MOBA_TOPK):
            cand = past & jnp.logical_not(sel)
            g = jnp.where(cand, gate, -jnp.inf)
            mx = jnp.max(g, axis=0, keepdims=True)
            first = jnp.min(jnp.where(cand & (g == mx), blk_f, 2.0 * nblk), axis=0, keepdims=True)
            sel = sel | (blk_f == first)
        bias_ref[rows, :] = jnp.where(sel, 0.0, NEG).astype(BF16)


def _moba_gate(qbt, kmean):
    width, s = qbt.shape
    tg = 512
    return pl.pallas_call(
        _moba_gate_kernel,
        grid=(s // tg,),
        in_specs=[pl.BlockSpec((width, tg), lambda i: (0, i)),
                  pl.BlockSpec(kmean.shape, lambda i: (0, 0, 0))],
        out_specs=pl.BlockSpec((width, tg), lambda i: (0, i)),
        out_shape=jax.ShapeDtypeStruct((width, s), BF16),
        compiler_params=_cparams(1),
    )(qbt, kmean)


def _moba_kernel(qt_ref, bias_ref, k_ref, vt_ref, o_ref, *scratch):
    i = pl.program_id(1)
    bufs, scr = scratch[:N_SCORE_BUFS], scratch[N_SCORE_BUFS:]
    refs = (scr[0:3], scr[3:6])
    q_t = qt_ref[...]
    b_t = bias_ref[...]
    qs = (_stack_rows(q_t[:HEAD_DIM], b_t[:HEAD_DIM]), _stack_rows(b_t[HEAD_DIM:], q_t[HEAD_DIM:]))
    lane = lax.broadcasted_iota(I32, (1, LANES), 1)
    lo = lane < HEAD_DIM
    mine = (lo, jnp.logical_not(lo))
    blk = lane & (HEAD_DIM - 1)
    sub_block = jnp.right_shift(lax.broadcasted_iota(I32, (TK, LANES), 0), int(math.log2(MOBA_BLOCK)))
    for h in range(2):
        _init_state(refs[h])

    def scores(n):
        kb = _k_block(k_ref, n)
        onehot = jnp.where(blk == n * (TK // MOBA_BLOCK) + sub_block, 1.0, 0.0).astype(BF16)
        return [_qk_t(jnp.where(mine[h], kb, onehot), qs[h]) for h in range(2)]

    def consume(n, buf, mask, cols):
        vb = vt_ref[n]
        _softmax_pv(buf, [vb[:HEAD_DIM, :], vb[HEAD_DIM:, :]], refs, mask, cols)

    _sweep(0, KPQ * i, scores, consume, bufs, (_causal_t(0), _causal_t(1)), fuse_first=True)
    for h in range(2):
        _, l, acc = _load_state(refs[h])
        o_ref[h * HEAD_DIM:(h + 1) * HEAD_DIM, :] = (acc / l).astype(o_ref.dtype)


def _moba_attention(qt, bias, k, vt, qrow, kcol, vrow):
    s = k.shape[0]
    n_pairs = 4
    return pl.pallas_call(
        _moba_kernel,
        grid=(n_pairs, s // TQ),
        in_specs=[pl.BlockSpec((LANES, TQ), lambda hp, i: (qrow + hp, i)),
                  pl.BlockSpec((LANES, TQ), lambda hp, i: (hp, i)),
                  pl.BlockSpec((s, LANES), lambda hp, i: (0, kcol + hp)),
                  pl.BlockSpec((s // TK, LANES, TK), lambda hp, i: (0, vrow + hp, 0))],
        out_specs=pl.BlockSpec((LANES, TQ), lambda hp, i: (hp, i)),
        out_shape=jax.ShapeDtypeStruct((n_pairs * LANES, s), BF16),
        scratch_shapes=_score_scratch(2) + _flash_scratch(2, HEAD_DIM),
        compiler_params=_cparams(2),
    )(qt, bias, k, vt)


def _diff_kernel(lam_init, qt_ref, k_ref, vt_ref, lq1_ref, lk1_ref, lq2_ref, lk2_ref, g_ref, o_ref,
                 *scratch):
    i = pl.program_id(1)
    bufs, scr = scratch[:N_SCORE_BUFS], scratch[N_SCORE_BUFS:]
    refs = (scr[0:3], scr[3:6])
    q_t = qt_ref[...]
    zero = jnp.zeros((HEAD_DIM, TQ), BF16)
    qs = (_stack_rows(q_t[:HEAD_DIM], zero), _stack_rows(zero, q_t[HEAD_DIM:]))
    for h in range(2):
        _init_state(refs[h])

    def scores(n):
        kb = _k_block(k_ref, n)
        return [_qk_t(kb, qs[h]) for h in range(2)]

    def consume(n, buf, mask, cols):
        vb = vt_ref[n]
        _softmax_pv(buf, [vb, vb], refs, mask, cols)

    _sweep(0, KPQ * i, scores, consume, bufs, (_causal_t(0), _causal_t(1)), fuse_first=True)
    lam = (jnp.exp(jnp.sum(lq1_ref[...] * lk1_ref[...], axis=1, keepdims=True))
           - jnp.exp(jnp.sum(lq2_ref[...] * lk2_ref[...], axis=1, keepdims=True)) + lam_init)
    _, l1, acc1 = _load_state(refs[0])
    _, l2, acc2 = _load_state(refs[1])
    o = acc1 / l1 - lam * (acc2 / l2)
    var = jnp.mean(o * o, axis=0, keepdims=True)
    o = (o * lax.rsqrt(var + RMS_EPS) * g_ref[...]) * (1.0 - lam_init)
    o_ref[...] = o.astype(o_ref.dtype)


def _diff_attention(qt, k, vt, lq1, lk1, lq2, lk2, g_sub, lam_init, qrow, kcol, vrow):
    s = k.shape[0]
    n_heads = 8
    vec = lambda n: pl.BlockSpec((1, n), lambda h, i: (0, 0))
    return pl.pallas_call(
        functools.partial(_diff_kernel, lam_init),
        grid=(n_heads, s // TQ),
        in_specs=[pl.BlockSpec((LANES, TQ), lambda h, i: (qrow + h, i)),
                  pl.BlockSpec((s, LANES), lambda h, i: (0, kcol + h)),
                  pl.BlockSpec((s // TK, LANES, TK), lambda h, i: (0, vrow + h, 0)),
                  vec(HEAD_DIM), vec(HEAD_DIM), vec(HEAD_DIM), vec(HEAD_DIM),
                  pl.BlockSpec((LANES, 1), lambda h, i: (0, 0))],
        out_specs=pl.BlockSpec((LANES, TQ), lambda h, i: (h, i)),
        out_shape=jax.ShapeDtypeStruct((n_heads * LANES, s), BF16),
        scratch_shapes=_score_scratch(2) + _flash_scratch(2, LANES),
        compiler_params=_cparams(2),
    )(qt, k, vt, lq1, lk1, lq2, lk2, g_sub)


def _outproj_kernel(oa_ref, ob_ref, oc_ref, wt_ref, x_ref, gt_ref, o_ref):
    wa = oa_ref.shape[0]
    wb = ob_ref.shape[0]
    y_t = (jnp.dot(wt_ref[:, 0:wa], oa_ref[...], preferred_element_type=F32)
           + jnp.dot(wt_ref[:, wa:wa + wb], ob_ref[...], preferred_element_type=F32)
           + jnp.dot(wt_ref[:, wa + wb:], oc_ref[...], preferred_element_type=F32))
    o_ref[...] = x_ref[...] + gt_ref[...] * y_t.T


def _out_projection(oa_t, ob_t, oc_t, wt_bf16, x, gt):
    s, d = x.shape
    tm = 256
    return pl.pallas_call(
        _outproj_kernel,
        grid=(s // tm,),
        in_specs=[pl.BlockSpec((oa_t.shape[0], tm), lambda i: (0, i)),
                  pl.BlockSpec((ob_t.shape[0], tm), lambda i: (0, i)),
                  pl.BlockSpec((oc_t.shape[0], tm), lambda i: (0, i)),
                  pl.BlockSpec(wt_bf16.shape, lambda i: (0, 0)),
                  pl.BlockSpec((tm, d), lambda i: (i, 0)),
                  pl.BlockSpec((1, d), lambda i: (0, 0))],
        out_specs=pl.BlockSpec((tm, d), lambda i: (i, 0)),
        out_shape=jax.ShapeDtypeStruct((s, d), F32),
        compiler_params=_cparams(1),
    )(oa_t, ob_t, oc_t, wt_bf16, x, gt)


def _route_kernel(x_ref, g_ref, sc_ref, sh_ref, wr_ref, br_ref, tri_ref,
                  h_ref, route_ref, count_ref, base_scr):
    i = pl.program_id(0)
    tm = x_ref.shape[0]

    @pl.when(i == 0)
    def _():
        base_scr[...] = jnp.zeros_like(base_scr)

    h = _modulated_norm(x_ref[...], g_ref[...], sc_ref[...], sh_ref[...])
    h_ref[...] = h.astype(BF16)
    logits = _dot3(wr_ref[...], h, NT_DIMS)
    s = _sigmoid(logits)
    sg = s + br_ref[...]
    rows = [sg[r:r + 1, :] for r in range(N_EXPERTS)]

    def top2_sum(vals):
        best = None
        for a in range(len(vals)):
            for b in range(a + 1, len(vals)):
                pair = vals[a] + vals[b]
                best = pair if best is None else jnp.maximum(best, pair)
        return best

    score = [top2_sum(rows[g * EXPERTS_PER_GROUP:(g + 1) * EXPERTS_PER_GROUP]) for g in range(N_GROUPS)]
    best, grp = score[0], jnp.zeros((1, tm), I32)
    for g in range(1, N_GROUPS):
        better = score[g] > best
        grp = jnp.where(better, g, grp)
        best = jnp.where(better, score[g], best)
    cand = []
    for j in range(EXPERTS_PER_GROUP):
        c = rows[j]
        for g in range(1, N_GROUPS):
            c = jnp.where(grp == g, rows[g * EXPERTS_PER_GROUP + j], c)
        cand.append(c)
    b0, l0 = cand[0], jnp.zeros((1, tm), I32)
    for j in range(1, EXPERTS_PER_GROUP):
        better = cand[j] > b0
        l0 = jnp.where(better, j, l0)
        b0 = jnp.where(better, cand[j], b0)
    b1, l1 = jnp.full((1, tm), -jnp.inf, F32), jnp.zeros((1, tm), I32)
    for j in range(EXPERTS_PER_GROUP):
        better = (l0 != j) & (cand[j] > b1)
        l1 = jnp.where(better, j, l1)
        b1 = jnp.where(better, cand[j], b1)
    e0 = grp * EXPERTS_PER_GROUP + l0
    e1 = grp * EXPERTS_PER_GROUP + l1
    erow = lax.broadcasted_iota(I32, (N_EXPERTS, tm), 0)
    sel0 = erow == e0
    sel1 = erow == e1
    s0 = jnp.sum(jnp.where(sel0, s, 0.0), axis=0, keepdims=True)
    s1 = jnp.sum(jnp.where(sel1, s, 0.0), axis=0, keepdims=True)
    tot = s0 + s1
    chosen = jnp.where(sel0 | sel1, 1.0, 0.0).astype(BF16)
    csum = jnp.dot(chosen, tri_ref[...], preferred_element_type=F32)
    pos = base_scr[...] + csum - 1.0
    r0 = jnp.sum(jnp.where(sel0, pos, 0.0), axis=0, keepdims=True)
    r1 = jnp.sum(jnp.where(sel1, pos, 0.0), axis=0, keepdims=True)
    base = base_scr[...] + csum[:, tm - 1:tm]
    base_scr[...] = base
    count_ref[...] = jnp.broadcast_to(base, count_ref.shape)
    zero = jnp.zeros((1, tm), F32)
    for r, val in enumerate((e0.astype(F32), e1.astype(F32), r0, r1, s0 / tot, s1 / tot, zero, zero)):
        route_ref[r:r + 1, :] = val


def _norm_and_route(x, g, sc, sh, w_router, b_router):
    s, d = x.shape
    tm = 512
    tri = jnp.asarray(np.triu(np.ones((tm, tm), np.float32)), BF16)
    row = lambda i: (0, 0)
    return pl.pallas_call(
        _route_kernel,
        grid=(s // tm,),
        in_specs=[pl.BlockSpec((tm, d), lambda i: (i, 0)),
                  pl.BlockSpec((1, d), row), pl.BlockSpec((1, d), row), pl.BlockSpec((1, d), row),
                  pl.BlockSpec((N_EXPERTS, d), row), pl.BlockSpec((N_EXPERTS, 1), row),
                  pl.BlockSpec((tm, tm), row)],
        out_specs=[pl.BlockSpec((tm, d), lambda i: (i, 0)),
                   pl.BlockSpec((8, tm), lambda i: (0, i)),
                   pl.BlockSpec((N_EXPERTS, LANES), row)],
        out_shape=[jax.ShapeDtypeStruct((s, d), BF16),
                   jax.ShapeDtypeStruct((8, s), F32),
                   jax.ShapeDtypeStruct((N_EXPERTS, LANES), F32)],
        scratch_shapes=[pltpu.VMEM((N_EXPERTS, 1), F32)],
        compiler_params=_cparams(1),
    )(x, g, sc, sh, w_router.T, b_router.reshape(N_EXPERTS, 1), tri)


FFN_ROWS = 512


def _ffn_kernel(tile_base, te_ref, nu_ref, xs_ref, wg_ref, wu_ref, wd_ref, *rest):
    y_ref = rest[-1]
    j = pl.program_id(0) + tile_base

    @pl.when(j < nu_ref[0])
    def _():
        x = xs_ref[...]
        de = wg_ref.shape[3]
        y = None
        for c in range(de // FFN_CHUNK):
            cs = slice(c * FFN_CHUNK, (c + 1) * FFN_CHUNK)
            a = jnp.dot(x, wg_ref[0, 0, :, cs].astype(BF16), preferred_element_type=F32)
            b = jnp.dot(x, wu_ref[0, 0, :, cs].astype(BF16), preferred_element_type=F32)
            act = (a * _sigmoid(a) * b).astype(BF16)
            part = jnp.dot(act, wd_ref[0, 0, cs, :].astype(BF16), preferred_element_type=F32)
            y = part if y is None else y + part
        y_ref[...] = y.astype(y_ref.dtype)

    @pl.when(j >= nu_ref[0])
    def _():
        y_ref[...] = jnp.zeros_like(y_ref)


FFN_CHUNK = 512
MOE_PARTS = 2


def _expert_ffn(tile_expert, n_used, xs_part, w_gate, w_up, w_down, layer, tile_base, n_tiles, y_prev):
    r, d = xs_part.shape
    de = w_gate.shape[3]
    expert = lambda j, te, nu: (layer, te[j + tile_base], 0, 0)
    in_specs = [pl.BlockSpec((FFN_ROWS, d), lambda j, te, nu: (j, 0)),
                pl.BlockSpec((1, 1, d, de), expert, pipeline_mode=pl.Buffered(1)),
                pl.BlockSpec((1, 1, d, de), expert, pipeline_mode=pl.Buffered(1)),
                pl.BlockSpec((1, 1, de, d), expert, pipeline_mode=pl.Buffered(1))]
    args = [tile_expert, n_used, xs_part, w_gate, w_up, w_down]
    aliases = {}
    if y_prev is not None:
        in_specs.append(pl.BlockSpec(memory_space=pl.ANY))
        aliases = {len(args): 0}
        args.append(y_prev)
    grid_spec = pltpu.PrefetchScalarGridSpec(
        num_scalar_prefetch=2,
        grid=(r // FFN_ROWS,),
        in_specs=in_specs,
        out_specs=pl.BlockSpec((FFN_ROWS, d), lambda j, te, nu: (j + tile_base, 0)),
    )
    return pl.pallas_call(
        functools.partial(_ffn_kernel, tile_base),
        grid_spec=grid_spec,
        out_shape=jax.ShapeDtypeStruct((n_tiles * FFN_ROWS, d), BF16),
        input_output_aliases=aliases,
        compiler_params=_cparams(1),
    )(*args)


def _combine_kernel(final, x_ref, y0_ref, y1_ref, w0_ref, w1_ref, gt_ref, gf_ref, *rest):
    o_ref = rest[-1]
    x = x_ref[...] + gt_ref[...] * (w0_ref[...] * y0_ref[...].astype(F32)
                                    + w1_ref[...] * y1_ref[...].astype(F32))
    if final:
        var = jnp.mean(x * x, axis=-1, keepdims=True)
        x = x * lax.rsqrt(var + RMS_EPS) * gf_ref[...]
    o_ref[...] = x


def _combine(x, y01_part, w0, w1, gt, g_final, final, row_base, out_prev):
    s, d = x.shape
    tm = 256
    part = y01_part.shape[0] // 2
    base = row_base // tm
    big = pl.BlockSpec((tm, d), lambda i: (i + base, 0))
    first = pl.BlockSpec((tm, d), lambda i: (i, 0))
    second = pl.BlockSpec((tm, d), lambda i: (i + part // tm, 0))
    col = pl.BlockSpec((tm, 1), lambda i: (i + base, 0))
    vec = pl.BlockSpec((1, d), lambda i: (0, 0))
    in_specs = [big, first, second, col, col, vec, vec]
    args = [x, y01_part, y01_part, w0, w1, gt, g_final]
    aliases = {}
    if out_prev is not None:
        in_specs.append(pl.BlockSpec(memory_space=pl.ANY))
        aliases = {len(args): 0}
        args.append(out_prev)
    return pl.pallas_call(
        functools.partial(_combine_kernel, final),
        grid=(part // tm,),
        in_specs=in_specs,
        out_specs=big,
        out_shape=jax.ShapeDtypeStruct((s, d), F32),
        input_output_aliases=aliases,
        compiler_params=_cparams(1),
    )(*args)


def _rope_tables(positions):
    half = HEAD_DIM // 2
    inv = ROPE_THETA ** (-jnp.arange(0, HEAD_DIM, 2, dtype=F32) / HEAD_DIM)
    ang = positions.astype(F32)[:, None] * inv
    cos, sin = jnp.cos(ang), jnp.sin(ang)
    cos_t = jnp.tile(cos, (1, LANES // half))
    sin_t = jnp.tile(jnp.concatenate([-sin, sin], axis=1), (1, LANES // HEAD_DIM))
    return cos_t, sin_t


def _block_means(ksum):
    nblk, width = ksum.shape
    km = (ksum / MOBA_BLOCK).reshape(nblk, width // HEAD_DIM, HEAD_DIM)
    km = jnp.pad(km, ((0, HEAD_DIM - nblk), (0, 0), (0, 0)))
    return jnp.transpose(km, (1, 0, 2))


def _moe(x1, h2, route, counts, w_gate, w_up, w_down, layer, gt, g_final, final):
    s, d = x1.shape
    e0, e1 = route[0].astype(I32), route[1].astype(I32)
    r0, r1 = route[2].astype(I32), route[3].astype(I32)
    cnt = counts[:, 0].astype(I32)
    cnt_pad = ((cnt + FFN_ROWS - 1) // FFN_ROWS) * FFN_ROWS
    off = jnp.concatenate([jnp.zeros((1,), I32), jnp.cumsum(cnt_pad).astype(I32)])
    d0 = off[e0] + r0
    d1 = off[e1] + r1
    n_rows = 2 * s + N_EXPERTS * FFN_ROWS
    tok = jnp.arange(s, dtype=I32)
    src = (jnp.arange(n_rows, dtype=I32) % s).at[jnp.concatenate([d0, d1])].set(jnp.concatenate([tok, tok]))
    n_tiles = n_rows // FFN_ROWS
    tile_start = jnp.arange(n_tiles, dtype=I32) * FFN_ROWS
    tile_expert = jnp.sum((off[None, 1:] <= tile_start[:, None]).astype(I32), axis=1)
    tile_expert = jnp.minimum(tile_expert, N_EXPERTS - 1).astype(I32)
    n_used = (off[N_EXPERTS] // FFN_ROWS).reshape(1).astype(I32)
    y = None
    for part in range(MOE_PARTS):
        tiles = n_tiles // MOE_PARTS
        rows = slice(part * tiles * FFN_ROWS, (part + 1) * tiles * FFN_ROWS)
        y = _expert_ffn(tile_expert, n_used, h2[src[rows]], w_gate, w_up, w_down, layer,
                        part * tiles, n_tiles, y)
    w0, w1 = route[4].reshape(s, 1), route[5].reshape(s, 1)
    out = None
    for part in range(MOE_PARTS):
        toks = slice(part * s // MOE_PARTS, (part + 1) * s // MOE_PARTS)
        y01 = y[jnp.concatenate([d0[toks], d1[toks]])]
        out = _combine(x1, y01, w0, w1, gt, g_final, final, part * s // MOE_PARTS, out)
    return out


def kernel(x, c, positions, w_ada, b_ada, g_attn, g_mlp, w_in, w_out, lam_q1, lam_k1, lam_q2, lam_k2,
           g_subln, w_gate, w_up, w_down, w_router, b_router, g_final):
    batch, s, d = x.shape
    assert batch == 1 and s % TQ == 0 and s // MOBA_BLOCK <= HEAD_DIM
    assert w_in.shape[2] == 3 * N_KIND_TILES * PROJ_TN
    depth = w_ada.shape[0]
    xs = x.reshape(s, d)
    cos_t, sin_t = _rope_tables(positions[0])
    mod = _modulation(c, w_ada, b_ada)
    tab = _dilated_bias_table()
    gf = g_final.reshape(1, d)
    row = lambda v: v.reshape(1, -1)
    for l in range(depth):
        lam_init = 0.8 - 0.6 * math.exp(-0.3 * l)
        sh_a, sc_a, gt_a, sh_m, sc_m, gt_m = [mod[l, :, k * d:(k + 1) * d] for k in range(6)]
        k, qt, qbt, vt, ksum = _in_projection(xs, row(g_attn[l]), sc_a, sh_a,
                                              _layer_to_bf16(w_in, l, PROJ_TN), cos_t, sin_t)
        o_a = _dilated_attention(qt, k, vt, tab, 0, 0, 0)
        bias = _moba_gate(qbt, _block_means(ksum.reshape(s // MOBA_BLOCK, -1)))
        o_b = _moba_attention(qt, bias, k, vt, 4, 4, 4)
        o_c = _diff_attention(qt, k, vt, row(lam_q1[l]), row(lam_k1[l]), row(lam_q2[l]), row(lam_k2[l]),
                              g_subln[l].reshape(-1, 1), lam_init, 8, 8, 8)
        x1 = _out_projection(o_a, o_b, o_c, w_out[l].T.astype(BF16), xs, gt_a)
        h2, route, counts = _norm_and_route(x1, row(g_mlp[l]), sc_m, sh_m, w_router, b_router)
        xs = _moe(x1, h2, route, counts, w_gate, w_up, w_down, l, gt_m, gf, l == depth - 1)
    return xs.reshape(batch, s, d)
```
